```python
import math
import jax
import jax.numpy as jnp
from jax import lax
import numpy as np

D_MODEL = 2048
BATCH = 8
SEQ = 8192
DEPTH = 2

GRID_W = 64
N_MEM = 256
EPS = 1e-6

SSM_GROUP = 16
SSM_STATE = 64
SSM_GROUPS = 48
SSM_WIDTH = SSM_GROUPS * SSM_GROUP

DN_HEADS = 6
DN_HEAD_DIM = 128
DN_WIDTH = DN_HEADS * DN_HEAD_DIM
DN_CONV = 5
DN_CHUNK = 64

ATT_HEADS = 8
ATT_KV_HEADS = 2
ATT_HEAD_DIM = 128
ATT_WIDTH = ATT_HEADS * ATT_HEAD_DIM
ATT_KV_WIDTH = ATT_KV_HEADS * ATT_HEAD_DIM
ATT_BLOCK = 128
ROPE_THETA = 10000.0

MEM_HEADS = 4
MEM_HEAD_DIM = 128
MEM_WIDTH = MEM_HEADS * MEM_HEAD_DIM

N_BRANCH = 4
BRANCH_WIDTHS = (SSM_WIDTH, DN_WIDTH, ATT_WIDTH, MEM_WIDTH)
BRANCH_OFFSETS = (0, SSM_WIDTH, SSM_WIDTH + DN_WIDTH, SSM_WIDTH + DN_WIDTH + ATT_WIDTH)
BRANCH_TOTAL = SSM_WIDTH + DN_WIDTH + ATT_WIDTH + MEM_WIDTH

IN_SPLITS = (
    SSM_WIDTH, SSM_WIDTH,
    DN_WIDTH, DN_WIDTH, DN_WIDTH, 2 * DN_HEADS, 2 * DN_HEADS, DN_WIDTH,
    ATT_WIDTH, ATT_KV_WIDTH, ATT_KV_WIDTH, ATT_WIDTH,
    MEM_WIDTH, MEM_WIDTH,
    N_BRANCH * D_MODEL,
)
IN_WIDTH = (2 * SSM_WIDTH + 4 * DN_WIDTH + 4 * DN_HEADS + 2 * ATT_WIDTH
            + 2 * ATT_KV_WIDTH + 2 * MEM_WIDTH + N_BRANCH * D_MODEL)

kernel_name = "hybrid_gated_s5_deltanet_gridattn_encoder"


def rmsnorm(x, g):
    xf = x.astype(jnp.float32)
    y = xf * lax.rsqrt(jnp.mean(xf * xf, axis=-1, keepdims=True) + EPS)
    return (y * g.astype(jnp.float32)).astype(x.dtype)


def l2norm(x):
    return x * lax.rsqrt(jnp.sum(x * x, axis=-1, keepdims=True) + EPS)


def _cmul(ar, ai, br, bi):
    return ar * br - ai * bi, ar * bi + ai * br


def _ssm_combine(e1, e2):
    a1r, a1i, b1r, b1i = e1
    a2r, a2i, b2r, b2i = e2
    ar, ai = _cmul(a2r, a2i, a1r, a1i)
    br, bi = _cmul(a2r, a2i, b1r, b1i)
    return ar, ai, br + b2r, bi + b2i


def s5_direction(u, a_re, a_im, log_step, b_re, b_im, c_re, c_im, reverse):
    step = jnp.exp(log_step)[:, None]
    mag = jnp.exp(a_re * step)
    lam_re = mag * jnp.cos(a_im * step)
    lam_im = mag * jnp.sin(a_im * step)
    den = a_re * a_re + a_im * a_im
    nr = lam_re - 1.0
    ni = lam_im
    coef_re = (nr * a_re + ni * a_im) / den
    coef_im = (ni * a_re - nr * a_im) / den
    bb_re = coef_re[..., None] * b_re - coef_im[..., None] * b_im
    bb_im = coef_re[..., None] * b_im + coef_im[..., None] * b_re
    bu_re = jnp.einsum("blgp,gnp->blgn", u, bb_re)
    bu_im = jnp.einsum("blgp,gnp->blgn", u, bb_im)
    lr = jnp.broadcast_to(lam_re, bu_re.shape)
    li = jnp.broadcast_to(lam_im, bu_re.shape)
    _, _, s_re, s_im = lax.associative_scan(
        _ssm_combine, (lr, li, bu_re, bu_im), reverse=reverse, axis=1)
    return (jnp.einsum("blgn,gpn->blgp", s_re, c_re)
            - jnp.einsum("blgn,gpn->blgp", s_im, c_im))


def s5_mixer(u, a_re, a_im, log_step, b_re, b_im, c_re, c_im, d, w_glu, b_glu):
    dtype = u.dtype
    bsz, seq, _ = u.shape
    f = lambda t: t.astype(jnp.float32)
    ug = f(u).reshape(bsz, seq, SSM_GROUPS, SSM_GROUP)
    y = s5_direction(ug, f(a_re[0]), f(a_im[0]), f(log_step[0]), f(b_re[0]), f(b_im[0]),
                     f(c_re[0]), f(c_im[0]), reverse=False)
    y = y + s5_direction(ug, f(a_re[1]), f(a_im[1]), f(log_step[1]), f(b_re[1]), f(b_im[1]),
                         f(c_re[1]), f(c_im[1]), reverse=True)
    y = y + f(d).reshape(SSM_GROUPS, SSM_GROUP) * ug
    y = jax.nn.gelu(y.reshape(bsz, seq, SSM_WIDTH))
    y = y * jax.nn.sigmoid(y @ f(w_glu) + f(b_glu))
    return y.astype(dtype)


def short_conv(x, w):
    ch = x.shape[-1]
    rhs = jnp.transpose(w)[:, None, :].astype(x.dtype)
    return lax.conv_general_dilated(
        x, rhs, window_strides=(1,), padding=[(DN_CONV // 2, DN_CONV // 2)],
        dimension_numbers=("NWC", "WIO", "NWC"), feature_group_count=ch)


def gated_delta_rule(q, k, v, beta, g):
    b, h, l, dk = q.shape
    dv = v.shape[-1]
    c = DN_CHUNK
    n = l // c
    q = q.reshape(b, h, n, c, dk)
    k = k.reshape(b, h, n, c, dk)
    v = v.reshape(b, h, n, c, dv)
    beta = beta.reshape(b, h, n, c)
    g = jnp.cumsum(g.reshape(b, h, n, c), axis=-1)
    idx = jnp.arange(c)
    incl = idx[:, None] >= idx[None, :]
    strict = idx[:, None] > idx[None, :]
    diff = g[..., :, None] - g[..., None, :]
    decay = jnp.where(incl, jnp.exp(jnp.where(incl, diff, 0.0)), 0.0)
    k_beta = k * beta[..., None]
    lower = jnp.where(strict, jnp.einsum("bhncd,bhnsd->bhncs", k_beta, k) * decay, 0.0)
    eye = jnp.eye(c, dtype=q.dtype)
    rhs = jnp.concatenate([v * beta[..., None], k_beta * jnp.exp(g)[..., None]], axis=-1)
    sol = lax.linalg.triangular_solve(eye + lower, rhs, left_side=True, lower=True,
                                      unit_diagonal=True)
    u_c = sol[..., :dv]
    w_c = sol[..., dv:]
    intra = jnp.einsum("bhncd,bhnsd->bhncs", q, k) * decay
    q_dec = q * jnp.exp(g)[..., None]
    k_dec = k * jnp.exp(g[..., -1:] - g)[..., None]
    g_last = jnp.exp(g[..., -1])
    xs = (jnp.moveaxis(u_c, 2, 0), jnp.moveaxis(w_c, 2, 0), jnp.moveaxis(q_dec, 2, 0),
          jnp.moveaxis(k_dec, 2, 0), jnp.moveaxis(intra, 2, 0), jnp.moveaxis(g_last, 2, 0))

    def step(state, inp):
        u_i, w_i, qd_i, kd_i, a_i, gl_i = inp
        v_new = u_i - jnp.einsum("bhck,bhkv->bhcv", w_i, state)
        o = (jnp.einsum("bhck,bhkv->bhcv", qd_i, state)
             + jnp.einsum("bhcs,bhsv->bhcv", a_i, v_new))
        state = state * gl_i[..., None, None] + jnp.einsum("bhck,bhcv->bhkv", kd_i, v_new)
        return state, o

    s0 = jnp.zeros((b, h, dk, dv), q.dtype)
    _, o = lax.scan(step, s0, xs)
    return jnp.moveaxis(o, 0, 2).reshape(b, h, l, dv)


def deltanet_mixer(q, k, v, a_logit, b_logit, conv_w, a_log, dt_bias, norm_g):
    dtype = q.dtype
    bsz, seq, _ = q.shape
    qkv = jax.nn.silu(short_conv(jnp.concatenate([q, k, v], axis=-1), conv_w))
    qkv = qkv.astype(jnp.float32).reshape(bsz, seq, 3, DN_HEADS, DN_HEAD_DIM)
    qh = jnp.transpose(l2norm(qkv[:, :, 0]) * (DN_HEAD_DIM ** -0.5), (0, 2, 1, 3))
    kh = jnp.transpose(l2norm(qkv[:, :, 1]), (0, 2, 1, 3))
    vh = jnp.transpose(qkv[:, :, 2], (0, 2, 1, 3))
    a4 = a_logit.astype(jnp.float32).reshape(bsz, seq, 2, DN_HEADS)
    b4 = b_logit.astype(jnp.float32).reshape(bsz, seq, 2, DN_HEADS)
    beta = jax.nn.sigmoid(b4)
    g = -jnp.exp(a_log.astype(jnp.float32)) * jax.nn.softplus(a4 + dt_bias.astype(jnp.float32))
    beta_f = jnp.transpose(beta[:, :, 0], (0, 2, 1))
    beta_b = jnp.transpose(beta[:, :, 1], (0, 2, 1))
    g_f = jnp.transpose(g[:, :, 0], (0, 2, 1))
    g_b = jnp.transpose(g[:, :, 1], (0, 2, 1))
    o_f = gated_delta_rule(qh, kh, vh, beta_f, g_f)
    o_b = jnp.flip(gated_delta_rule(jnp.flip(qh, 2), jnp.flip(kh, 2), jnp.flip(vh, 2),
                                    jnp.flip(beta_b, 2), jnp.flip(g_b, 2)), 2)
    o = jnp.transpose(o_f + o_b, (0, 2, 1, 3))
    o = rmsnorm(o, norm_g)
    return o.reshape(bsz, seq, DN_WIDTH).astype(dtype)


def axial_rope(rows):
    row = jnp.repeat(jnp.arange(rows), GRID_W).astype(jnp.float32)
    col = jnp.tile(jnp.arange(GRID_W), rows).astype(jnp.float32)
    axis_dim = ATT_HEAD_DIM // 2
    freqs = ROPE_THETA ** (-jnp.arange(0, axis_dim, 2, dtype=jnp.float32) / axis_dim)
    ang = jnp.concatenate([row[:, None] * freqs, col[:, None] * freqs], axis=-1)
    return jnp.cos(ang), jnp.sin(ang)


def apply_rope(x, cos, sin):
    xp = x.reshape(x.shape[:-1] + (x.shape[-1] // 2, 2))
    x0, x1 = xp[..., 0], xp[..., 1]
    c = cos[None, :, None, :]
    s = sin[None, :, None, :]
    return jnp.stack([x0 * c - x1 * s, x0 * s + x1 * c], axis=-1).reshape(x.shape)


def grid_attention(q, k, v, qn_g, kn_g, cos, sin):
    dtype = q.dtype
    bsz, seq, _ = q.shape
    grp = ATT_HEADS // ATT_KV_HEADS
    qh = rmsnorm(q.reshape(bsz, seq, ATT_HEADS, ATT_HEAD_DIM), qn_g).astype(jnp.float32)
    kh = rmsnorm(k.reshape(bsz, seq, ATT_KV_HEADS, ATT_HEAD_DIM), kn_g).astype(jnp.float32)
    vh = v.reshape(bsz, seq, ATT_KV_HEADS, ATT_HEAD_DIM).astype(jnp.float32)
    qh = apply_rope(qh, cos, sin) * (ATT_HEAD_DIM ** -0.5)
    kh = apply_rope(kh, cos, sin)
    nblk = seq // ATT_BLOCK
    qb = qh.reshape(bsz, nblk, ATT_BLOCK, ATT_KV_HEADS, grp, ATT_HEAD_DIM)
    qb = jnp.transpose(qb, (1, 0, 2, 3, 4, 5))

    def block(qi):
        s = jnp.einsum("bqhgd,bkhd->bhgqk", qi, kh)
        p = jax.nn.softmax(s, axis=-1)
        return jnp.einsum("bhgqk,bkhd->bqhgd", p, vh)

    o = lax.map(block, qb)
    o = jnp.transpose(o, (1, 0, 2, 3, 4, 5)).reshape(bsz, seq, ATT_WIDTH)
    return o.astype(dtype)


def memory_attention(q, mem_n, w_kv):
    dtype = q.dtype
    bsz, seq, _ = q.shape
    kv = mem_n @ w_kv
    km = kv[..., :MEM_WIDTH].reshape(bsz, -1, MEM_HEADS, MEM_HEAD_DIM).astype(jnp.float32)
    vm = kv[..., MEM_WIDTH:].reshape(bsz, -1, MEM_HEADS, MEM_HEAD_DIM).astype(jnp.float32)
    qh = q.reshape(bsz, seq, MEM_HEADS, MEM_HEAD_DIM).astype(jnp.float32)
    s = jnp.einsum("bqhd,bkhd->bhqk", qh, km) * (MEM_HEAD_DIM ** -0.5)
    p = jax.nn.softmax(s, axis=-1)
    o = jnp.einsum("bhqk,bkhd->bqhd", p, vm).reshape(bsz, seq, MEM_WIDTH)
    return o.astype(dtype)


def _fwd_setup_inputs(seed: int = 0) -> dict:
    key = jax.random.key(seed)
    ks = jax.random.split(key, 32)
    f32 = jnp.float32

    def nrm(k, shape, scale):
        return jax.random.normal(k, shape, f32) * scale

    x = nrm(ks[0], (BATCH, SEQ, D_MODEL), 1.0)
    mem = nrm(ks[1], (BATCH, N_MEM, D_MODEL), 1.0)
    norm_g = 1.0 + nrm(ks[2], (DEPTH, D_MODEL), 0.02)
    w_in = nrm(ks[3], (DEPTH, D_MODEL, IN_WIDTH), D_MODEL ** -0.5)
    ssm_shape = (DEPTH, 2, SSM_GROUPS, SSM_STATE)
    ssm_a_re = -0.5 + nrm(ks[4], ssm_shape, 0.01)
    ssm_a_im = jnp.pi * jnp.arange(SSM_STATE, dtype=f32) + nrm(ks[5], ssm_shape, 0.01)
    ssm_log_step = jax.random.uniform(ks[6], (DEPTH, 2, SSM_GROUPS), f32,
                                      math.log(1e-3), math.log(1e-1))
    ssm_b_re = nrm(ks[7], (DEPTH, 2, SSM_GROUPS, SSM_STATE, SSM_GROUP), (2 * SSM_GROUP) ** -0.5)
    ssm_b_im = nrm(ks[8], (DEPTH, 2, SSM_GROUPS, SSM_STATE, SSM_GROUP), (2 * SSM_GROUP) ** -0.5)
    ssm_c_re = nrm(ks[9], (DEPTH, 2, SSM_GROUPS, SSM_GROUP, SSM_STATE), SSM_STATE ** -0.5)
    ssm_c_im = nrm(ks[10], (DEPTH, 2, SSM_GROUPS, SSM_GROUP, SSM_STATE), SSM_STATE ** -0.5)
    ssm_d = nrm(ks[11], (DEPTH, SSM_WIDTH), 1.0)
    ssm_w_glu = nrm(ks[12], (DEPTH, SSM_WIDTH, SSM_WIDTH), SSM_WIDTH ** -0.5)
    ssm_b_glu = nrm(ks[13], (DEPTH, SSM_WIDTH), 0.02)
    dn_conv = nrm(ks[14], (DEPTH, 3 * DN_WIDTH, DN_CONV), DN_CONV ** -0.5)
    dn_a_log = jnp.log(jax.random.uniform(ks[15], (DEPTH, 2, DN_HEADS), f32, 1.0, 16.0))
    dt = jnp.exp(jax.random.uniform(ks[16], (DEPTH, 2, DN_HEADS), f32,
                                    math.log(1e-3), math.log(1e-1)))
    dn_dt_bias = dt + jnp.log(-jnp.expm1(-dt))
    dn_norm_g = 1.0 + nrm(ks[17], (DEPTH, DN_HEAD_DIM), 0.02)
    attn_q_norm = 1.0 + nrm(ks[18], (DEPTH, ATT_HEAD_DIM), 0.02)
    attn_k_norm = 1.0 + nrm(ks[19], (DEPTH, ATT_HEAD_DIM), 0.02)
    mem_norm_g = 1.0 + nrm(ks[20], (DEPTH, D_MODEL), 0.02)
    w_mem_kv = nrm(ks[21], (DEPTH, D_MODEL, 2 * MEM_WIDTH), D_MODEL ** -0.5)
    bks = jax.random.split(ks[22], N_BRANCH)
    w_branch = jnp.concatenate(
        [nrm(bks[i], (DEPTH, BRANCH_WIDTHS[i], D_MODEL), BRANCH_WIDTHS[i] ** -0.5)
         for i in range(N_BRANCH)], axis=1)
    w_out = nrm(ks[23], (DEPTH, D_MODEL, D_MODEL), D_MODEL ** -0.5)
    final_norm_g = 1.0 + nrm(ks[24], (D_MODEL,), 0.02)
    return {
        "x": x, "mem": mem, "norm_g": norm_g, "w_in": w_in,
        "ssm_a_re": ssm_a_re, "ssm_a_im": ssm_a_im, "ssm_log_step": ssm_log_step,
        "ssm_b_re": ssm_b_re, "ssm_b_im": ssm_b_im, "ssm_c_re": ssm_c_re,
        "ssm_c_im": ssm_c_im, "ssm_d": ssm_d, "ssm_w_glu": ssm_w_glu,
        "ssm_b_glu": ssm_b_glu, "dn_conv": dn_conv, "dn_a_log": dn_a_log,
        "dn_dt_bias": dn_dt_bias, "dn_norm_g": dn_norm_g, "attn_q_norm": attn_q_norm,
        "attn_k_norm": attn_k_norm, "mem_norm_g": mem_norm_g, "w_mem_kv": w_mem_kv,
        "w_branch": w_branch, "w_out": w_out, "final_norm_g": final_norm_g,
    }


def _fwd_reference(x, mem, norm_g, w_in, ssm_a_re, ssm_a_im, ssm_log_step, ssm_b_re, ssm_b_im,
              ssm_c_re, ssm_c_im, ssm_d, ssm_w_glu, ssm_b_glu, dn_conv, dn_a_log,
              dn_dt_bias, dn_norm_g, attn_q_norm, attn_k_norm, mem_norm_g, w_mem_kv,
              w_branch, w_out, final_norm_g):
    bsz, seq, _ = x.shape
    rows = seq // GRID_W
    cos, sin = axial_rope(rows)
    split_at = [int(i) for i in np.cumsum(IN_SPLITS)[:-1]]
    for layer in range(DEPTH):
        xn = rmsnorm(x, norm_g[layer])
        h = xn @ w_in[layer]
        (u_a, z_a, dq, dk, dv, da, db, z_b, aq, ak, av, z_c, mq, z_m,
         gate_logits) = jnp.split(h, split_at, axis=-1)

        y_a = s5_mixer(u_a, ssm_a_re[layer], ssm_a_im[layer], ssm_log_step[layer],
                       ssm_b_re[layer], ssm_b_im[layer], ssm_c_re[layer], ssm_c_im[layer],
                       ssm_d[layer], ssm_w_glu[layer], ssm_b_glu[layer]) * jax.nn.silu(z_a)
        y_b = deltanet_mixer(dq, dk, dv, da, db, dn_conv[layer], dn_a_log[layer],
                             dn_dt_bias[layer], dn_norm_g[layer]) * jax.nn.silu(z_b)
        y_c = grid_attention(aq, ak, av, attn_q_norm[layer], attn_k_norm[layer],
                             cos, sin) * jax.nn.silu(z_c)
        y_m = memory_attention(mq, rmsnorm(mem, mem_norm_g[layer]),
                               w_mem_kv[layer]) * jax.nn.silu(z_m)

        gates = jax.nn.sigmoid(gate_logits.reshape(bsz, seq, N_BRANCH, D_MODEL))
        merged = jnp.zeros_like(x)
        for bi, y_br in enumerate((y_a, y_b, y_c, y_m)):
            off = BRANCH_OFFSETS[bi]
            w_b = w_branch[layer, off:off + BRANCH_WIDTHS[bi]]
            merged = merged + gates[:, :, bi] * (y_br @ w_b)
        x = x + merged @ w_out[layer]
    return rmsnorm(x, final_norm_g)


import jax as _jax
import jax.numpy as _jnp

TWIN_FORMAT = 'train_step'
FWD_PARAMS = ['x', 'mem', 'norm_g', 'w_in', 'ssm_a_re', 'ssm_a_im', 'ssm_log_step', 'ssm_b_re', 'ssm_b_im', 'ssm_c_re', 'ssm_c_im', 'ssm_d', 'ssm_w_glu', 'ssm_b_glu', 'dn_conv', 'dn_a_log', 'dn_dt_bias', 'dn_norm_g', 'attn_q_norm', 'attn_k_norm', 'mem_norm_g', 'w_mem_kv', 'w_branch', 'w_out', 'final_norm_g']
TWIN_WEIGHTS = ['norm_g', 'w_in', 'ssm_a_re', 'ssm_a_im', 'ssm_log_step', 'ssm_b_re', 'ssm_b_im', 'ssm_c_re', 'ssm_c_im', 'ssm_d', 'ssm_w_glu', 'ssm_b_glu', 'dn_conv', 'dn_a_log', 'dn_dt_bias', 'dn_norm_g', 'attn_q_norm', 'attn_k_norm', 'mem_norm_g', 'w_mem_kv', 'w_branch', 'w_out', 'final_norm_g']
TWIN_DIFF_INPUT = 'x'
TWIN_INPUTS = ['x', 'mem', 'norm_g', 'w_in', 'ssm_a_re', 'ssm_a_im', 'ssm_log_step', 'ssm_b_re', 'ssm_b_im', 'ssm_c_re', 'ssm_c_im', 'ssm_d', 'ssm_w_glu', 'ssm_b_glu', 'dn_conv', 'dn_a_log', 'dn_dt_bias', 'dn_norm_g', 'attn_q_norm', 'attn_k_norm', 'mem_norm_g', 'w_mem_kv', 'w_branch', 'w_out', 'final_norm_g', 'loss_target', 'm_norm_g', 'm_w_in', 'm_ssm_a_re', 'm_ssm_a_im', 'm_ssm_log_step', 'm_ssm_b_re', 'm_ssm_b_im', 'm_ssm_c_re', 'm_ssm_c_im', 'm_ssm_d', 'm_ssm_w_glu', 'm_ssm_b_glu', 'm_dn_conv', 'm_dn_a_log', 'm_dn_dt_bias', 'm_dn_norm_g', 'm_attn_q_norm', 'm_attn_k_norm', 'm_mem_norm_g', 'm_w_mem_kv', 'm_w_branch', 'm_w_out', 'm_final_norm_g', 'v_norm_g', 'v_w_in', 'v_ssm_a_re', 'v_ssm_a_im', 'v_ssm_log_step', 'v_ssm_b_re', 'v_ssm_b_im', 'v_ssm_c_re', 'v_ssm_c_im', 'v_ssm_d', 'v_ssm_w_glu', 'v_ssm_b_glu', 'v_dn_conv', 'v_dn_a_log', 'v_dn_dt_bias', 'v_dn_norm_g', 'v_attn_q_norm', 'v_attn_k_norm', 'v_mem_norm_g', 'v_w_mem_kv', 'v_w_branch', 'v_w_out', 'v_final_norm_g']
TWIN_OUTPUTS = ['loss', 'grad_x', 'grad_norm_g', 'grad_w_in', 'grad_ssm_a_re', 'grad_ssm_a_im', 'grad_ssm_log_step', 'grad_ssm_b_re', 'grad_ssm_b_im', 'grad_ssm_c_re', 'grad_ssm_c_im', 'grad_ssm_d', 'grad_ssm_w_glu', 'grad_ssm_b_glu', 'grad_dn_conv', 'grad_dn_a_log', 'grad_dn_dt_bias', 'grad_dn_norm_g', 'grad_attn_q_norm', 'grad_attn_k_norm', 'grad_mem_norm_g', 'grad_w_mem_kv', 'grad_w_branch', 'grad_w_out', 'grad_final_norm_g', 'delta_norm_g', 'delta_w_in', 'delta_ssm_a_re', 'delta_ssm_a_im', 'delta_ssm_log_step', 'delta_ssm_b_re', 'delta_ssm_b_im', 'delta_ssm_c_re', 'delta_ssm_c_im', 'delta_ssm_d', 'delta_ssm_w_glu', 'delta_ssm_b_glu', 'delta_dn_conv', 'delta_dn_a_log', 'delta_dn_dt_bias', 'delta_dn_norm_g', 'delta_attn_q_norm', 'delta_attn_k_norm', 'delta_mem_norm_g', 'delta_w_mem_kv', 'delta_w_branch', 'delta_w_out', 'delta_final_norm_g', 'new_m_norm_g', 'new_m_w_in', 'new_m_ssm_a_re', 'new_m_ssm_a_im', 'new_m_ssm_log_step', 'new_m_ssm_b_re', 'new_m_ssm_b_im', 'new_m_ssm_c_re', 'new_m_ssm_c_im', 'new_m_ssm_d', 'new_m_ssm_w_glu', 'new_m_ssm_b_glu', 'new_m_dn_conv', 'new_m_dn_a_log', 'new_m_dn_dt_bias', 'new_m_dn_norm_g', 'new_m_attn_q_norm', 'new_m_attn_k_norm', 'new_m_mem_norm_g', 'new_m_w_mem_kv', 'new_m_w_branch', 'new_m_w_out', 'new_m_final_norm_g', 'new_v_norm_g', 'new_v_w_in', 'new_v_ssm_a_re', 'new_v_ssm_a_im', 'new_v_ssm_log_step', 'new_v_ssm_b_re', 'new_v_ssm_b_im', 'new_v_ssm_c_re', 'new_v_ssm_c_im', 'new_v_ssm_d', 'new_v_ssm_w_glu', 'new_v_ssm_b_glu', 'new_v_dn_conv', 'new_v_dn_a_log', 'new_v_dn_dt_bias', 'new_v_dn_norm_g', 'new_v_attn_q_norm', 'new_v_attn_k_norm', 'new_v_mem_norm_g', 'new_v_w_mem_kv', 'new_v_w_branch', 'new_v_w_out', 'new_v_final_norm_g']
TWIN_LEAF_KINDS = {'loss': 'loss', 'grad_x': 'grad_x', 'grad_norm_g': 'grad_w', 'grad_w_in': 'grad_w', 'grad_ssm_a_re': 'grad_w', 'grad_ssm_a_im': 'grad_w', 'grad_ssm_log_step': 'grad_w', 'grad_ssm_b_re': 'grad_w', 'grad_ssm_b_im': 'grad_w', 'grad_ssm_c_re': 'grad_w', 'grad_ssm_c_im': 'grad_w', 'grad_ssm_d': 'grad_w', 'grad_ssm_w_glu': 'grad_w', 'grad_ssm_b_glu': 'grad_w', 'grad_dn_conv': 'grad_w', 'grad_dn_a_log': 'grad_w', 'grad_dn_dt_bias': 'grad_w', 'grad_dn_norm_g': 'grad_w', 'grad_attn_q_norm': 'grad_w', 'grad_attn_k_norm': 'grad_w', 'grad_mem_norm_g': 'grad_w', 'grad_w_mem_kv': 'grad_w', 'grad_w_branch': 'grad_w', 'grad_w_out': 'grad_w', 'grad_final_norm_g': 'grad_w', 'delta_norm_g': 'delta_w', 'delta_w_in': 'delta_w', 'delta_ssm_a_re': 'delta_w', 'delta_ssm_a_im': 'delta_w', 'delta_ssm_log_step': 'delta_w', 'delta_ssm_b_re': 'delta_w', 'delta_ssm_b_im': 'delta_w', 'delta_ssm_c_re': 'delta_w', 'delta_ssm_c_im': 'delta_w', 'delta_ssm_d': 'delta_w', 'delta_ssm_w_glu': 'delta_w', 'delta_ssm_b_glu': 'delta_w', 'delta_dn_conv': 'delta_w', 'delta_dn_a_log': 'delta_w', 'delta_dn_dt_bias': 'delta_w', 'delta_dn_norm_g': 'delta_w', 'delta_attn_q_norm': 'delta_w', 'delta_attn_k_norm': 'delta_w', 'delta_mem_norm_g': 'delta_w', 'delta_w_mem_kv': 'delta_w', 'delta_w_branch': 'delta_w', 'delta_w_out': 'delta_w', 'delta_final_norm_g': 'delta_w', 'new_m_norm_g': 'new_m', 'new_m_w_in': 'new_m', 'new_m_ssm_a_re': 'new_m', 'new_m_ssm_a_im': 'new_m', 'new_m_ssm_log_step': 'new_m', 'new_m_ssm_b_re': 'new_m', 'new_m_ssm_b_im': 'new_m', 'new_m_ssm_c_re': 'new_m', 'new_m_ssm_c_im': 'new_m', 'new_m_ssm_d': 'new_m', 'new_m_ssm_w_glu': 'new_m', 'new_m_ssm_b_glu': 'new_m', 'new_m_dn_conv': 'new_m', 'new_m_dn_a_log': 'new_m', 'new_m_dn_dt_bias': 'new_m', 'new_m_dn_norm_g': 'new_m', 'new_m_attn_q_norm': 'new_m', 'new_m_attn_k_norm': 'new_m', 'new_m_mem_norm_g': 'new_m', 'new_m_w_mem_kv': 'new_m', 'new_m_w_branch': 'new_m', 'new_m_w_out': 'new_m', 'new_m_final_norm_g': 'new_m', 'new_v_norm_g': 'new_v', 'new_v_w_in': 'new_v', 'new_v_ssm_a_re': 'new_v', 'new_v_ssm_a_im': 'new_v', 'new_v_ssm_log_step': 'new_v', 'new_v_ssm_b_re': 'new_v', 'new_v_ssm_b_im': 'new_v', 'new_v_ssm_c_re': 'new_v', 'new_v_ssm_c_im': 'new_v', 'new_v_ssm_d': 'new_v', 'new_v_ssm_w_glu': 'new_v', 'new_v_ssm_b_glu': 'new_v', 'new_v_dn_conv': 'new_v', 'new_v_dn_a_log': 'new_v', 'new_v_dn_dt_bias': 'new_v', 'new_v_dn_norm_g': 'new_v', 'new_v_attn_q_norm': 'new_v', 'new_v_attn_k_norm': 'new_v', 'new_v_mem_norm_g': 'new_v', 'new_v_w_mem_kv': 'new_v', 'new_v_w_branch': 'new_v', 'new_v_w_out': 'new_v', 'new_v_final_norm_g': 'new_v'}


def _forward(args):
    return _fwd_reference(*[args[k] for k in FWD_PARAMS])


def _output_shape():
    def fwd():
        inp = _fwd_setup_inputs(0)
        return _fwd_reference(*[inp[k] for k in FWD_PARAMS])
    out = _jax.eval_shape(fwd)
    return out.shape, out.dtype

N_MICROBATCH = 1
ADAM_LR = 0.001
ADAM_B1 = 0.9
ADAM_B2 = 0.999
ADAM_EPS = 1e-08
ADAM_WD = 0.01
ADAM_STEP = 10
PER_EXAMPLE_BATCH_AXIS = {'x': 0, 'mem': 0, 'loss_target': 0}
SHARED_INPUTS = []
_WEIGHT_DTYPES = {'norm_g': _jnp.float32, 'w_in': _jnp.float32, 'ssm_a_re': _jnp.float32, 'ssm_a_im': _jnp.float32, 'ssm_log_step': _jnp.float32, 'ssm_b_re': _jnp.float32, 'ssm_b_im': _jnp.float32, 'ssm_c_re': _jnp.float32, 'ssm_c_im': _jnp.float32, 'ssm_d': _jnp.float32, 'ssm_w_glu': _jnp.float32, 'ssm_b_glu': _jnp.float32, 'dn_conv': _jnp.float32, 'dn_a_log': _jnp.float32, 'dn_dt_bias': _jnp.float32, 'dn_norm_g': _jnp.float32, 'attn_q_norm': _jnp.float32, 'attn_k_norm': _jnp.float32, 'mem_norm_g': _jnp.float32, 'w_mem_kv': _jnp.float32, 'w_branch': _jnp.float32, 'w_out': _jnp.float32, 'final_norm_g': _jnp.float32}
MOMENT_SCALE = {'norm_g': 7.132965e-02, 'w_in': 2.408294e-02, 'ssm_a_re': 2.014254e-03, 'ssm_a_im': 2.027094e-03, 'ssm_log_step': 1.963797e+00, 'ssm_b_re': 1.153280e-03, 'ssm_b_im': 1.158665e-03, 'ssm_c_re': 1.652072e-03, 'ssm_c_im': 1.629421e-03, 'ssm_d': 2.430100e-02, 'ssm_w_glu': 6.530240e-03, 'ssm_b_glu': 9.540068e-03, 'dn_conv': 4.890497e-02, 'dn_a_log': 1.415193e-01, 'dn_dt_bias': 1.366402e-01, 'dn_norm_g': 1.550863e-01, 'attn_q_norm': 1.535014e-02, 'attn_k_norm': 1.508424e-02, 'mem_norm_g': 5.983480e-03, 'w_mem_kv': 8.108411e-03, 'w_branch': 2.088526e-02, 'w_out': 4.173895e-02, 'final_norm_g': 3.198425e+01}


def _to_microbatches(a, axis):
    t = _jnp.moveaxis(a, axis, 0)
    t = t.reshape((N_MICROBATCH, t.shape[0] // N_MICROBATCH) + t.shape[1:])
    return _jnp.moveaxis(t, 1, axis + 1)


def setup_inputs(seed: int = 0) -> dict:
    inp = _fwd_setup_inputs(seed)
    key = _jax.random.fold_in(_jax.random.key(seed), 7919)
    shape, _ = _output_shape()
    out = dict(inp)
    out["loss_target"] = _jax.random.normal(_jax.random.fold_in(key, 0), shape, _jnp.float32)
    for i, name in enumerate(TWIN_WEIGHTS):
        w = inp[name].astype(_jnp.float32)
        if MOMENT_SCALE is None:
            s = _jnp.sqrt(_jnp.mean(_jnp.square(w)) + 1e-30)
        else:
            s = MOMENT_SCALE[name]
        km, kv = _jax.random.split(_jax.random.fold_in(key, i + 1))
        out[name] = w
        out["m_" + name] = s * _jax.random.normal(km, w.shape, _jnp.float32)
        out["v_" + name] = (s * s) * _jax.random.uniform(kv, w.shape, _jnp.float32, 0.5, 1.5)
    if N_MICROBATCH > 1:
        for name, axis in PER_EXAMPLE_BATCH_AXIS.items():
            out[name] = _to_microbatches(out[name], axis)
    return {'x': out['x'], 'mem': out['mem'], 'norm_g': out['norm_g'], 'w_in': out['w_in'], 'ssm_a_re': out['ssm_a_re'], 'ssm_a_im': out['ssm_a_im'], 'ssm_log_step': out['ssm_log_step'], 'ssm_b_re': out['ssm_b_re'], 'ssm_b_im': out['ssm_b_im'], 'ssm_c_re': out['ssm_c_re'], 'ssm_c_im': out['ssm_c_im'], 'ssm_d': out['ssm_d'], 'ssm_w_glu': out['ssm_w_glu'], 'ssm_b_glu': out['ssm_b_glu'], 'dn_conv': out['dn_conv'], 'dn_a_log': out['dn_a_log'], 'dn_dt_bias': out['dn_dt_bias'], 'dn_norm_g': out['dn_norm_g'], 'attn_q_norm': out['attn_q_norm'], 'attn_k_norm': out['attn_k_norm'], 'mem_norm_g': out['mem_norm_g'], 'w_mem_kv': out['w_mem_kv'], 'w_branch': out['w_branch'], 'w_out': out['w_out'], 'final_norm_g': out['final_norm_g'], 'loss_target': out['loss_target'], 'm_norm_g': out['m_norm_g'], 'm_w_in': out['m_w_in'], 'm_ssm_a_re': out['m_ssm_a_re'], 'm_ssm_a_im': out['m_ssm_a_im'], 'm_ssm_log_step': out['m_ssm_log_step'], 'm_ssm_b_re': out['m_ssm_b_re'], 'm_ssm_b_im': out['m_ssm_b_im'], 'm_ssm_c_re': out['m_ssm_c_re'], 'm_ssm_c_im': out['m_ssm_c_im'], 'm_ssm_d': out['m_ssm_d'], 'm_ssm_w_glu': out['m_ssm_w_glu'], 'm_ssm_b_glu': out['m_ssm_b_glu'], 'm_dn_conv': out['m_dn_conv'], 'm_dn_a_log': out['m_dn_a_log'], 'm_dn_dt_bias': out['m_dn_dt_bias'], 'm_dn_norm_g': out['m_dn_norm_g'], 'm_attn_q_norm': out['m_attn_q_norm'], 'm_attn_k_norm': out['m_attn_k_norm'], 'm_mem_norm_g': out['m_mem_norm_g'], 'm_w_mem_kv': out['m_w_mem_kv'], 'm_w_branch': out['m_w_branch'], 'm_w_out': out['m_w_out'], 'm_final_norm_g': out['m_final_norm_g'], 'v_norm_g': out['v_norm_g'], 'v_w_in': out['v_w_in'], 'v_ssm_a_re': out['v_ssm_a_re'], 'v_ssm_a_im': out['v_ssm_a_im'], 'v_ssm_log_step': out['v_ssm_log_step'], 'v_ssm_b_re': out['v_ssm_b_re'], 'v_ssm_b_im': out['v_ssm_b_im'], 'v_ssm_c_re': out['v_ssm_c_re'], 'v_ssm_c_im': out['v_ssm_c_im'], 'v_ssm_d': out['v_ssm_d'], 'v_ssm_w_glu': out['v_ssm_w_glu'], 'v_ssm_b_glu': out['v_ssm_b_glu'], 'v_dn_conv': out['v_dn_conv'], 'v_dn_a_log': out['v_dn_a_log'], 'v_dn_dt_bias': out['v_dn_dt_bias'], 'v_dn_norm_g': out['v_dn_norm_g'], 'v_attn_q_norm': out['v_attn_q_norm'], 'v_attn_k_norm': out['v_attn_k_norm'], 'v_mem_norm_g': out['v_mem_norm_g'], 'v_w_mem_kv': out['v_w_mem_kv'], 'v_w_branch': out['v_w_branch'], 'v_w_out': out['v_w_out'], 'v_final_norm_g': out['v_final_norm_g']}


def _loss(weights, diff, rest, loss_target):
    with _jax.named_scope("forward"):
        args = {**rest, TWIN_DIFF_INPUT: diff, **{k: w.astype(_WEIGHT_DTYPES[k]) for k, w in weights.items()}}
        y = _forward(args)
    with _jax.named_scope("loss_head"):
        err = _jnp.square(y.astype(_jnp.float32) - loss_target)
        return 0.5 * _jnp.sum(_jnp.mean(err, axis=-1)) if err.ndim else 0.5 * err


def _adamw(w, g, m, v):
    m = ADAM_B1 * m + (1.0 - ADAM_B1) * g
    v = ADAM_B2 * v + (1.0 - ADAM_B2) * _jnp.square(g)
    m_hat = m / (1.0 - ADAM_B1 ** ADAM_STEP)
    v_hat = v / (1.0 - ADAM_B2 ** ADAM_STEP)
    delta = -ADAM_LR * (m_hat / (_jnp.sqrt(v_hat) + ADAM_EPS) + ADAM_WD * w)
    return delta, m, v


def reference(x, mem, norm_g, w_in, ssm_a_re, ssm_a_im, ssm_log_step, ssm_b_re, ssm_b_im, ssm_c_re, ssm_c_im, ssm_d, ssm_w_glu, ssm_b_glu, dn_conv, dn_a_log, dn_dt_bias, dn_norm_g, attn_q_norm, attn_k_norm, mem_norm_g, w_mem_kv, w_branch, w_out, final_norm_g, loss_target, m_norm_g, m_w_in, m_ssm_a_re, m_ssm_a_im, m_ssm_log_step, m_ssm_b_re, m_ssm_b_im, m_ssm_c_re, m_ssm_c_im, m_ssm_d, m_ssm_w_glu, m_ssm_b_glu, m_dn_conv, m_dn_a_log, m_dn_dt_bias, m_dn_norm_g, m_attn_q_norm, m_attn_k_norm, m_mem_norm_g, m_w_mem_kv, m_w_branch, m_w_out, m_final_norm_g, v_norm_g, v_w_in, v_ssm_a_re, v_ssm_a_im, v_ssm_log_step, v_ssm_b_re, v_ssm_b_im, v_ssm_c_re, v_ssm_c_im, v_ssm_d, v_ssm_w_glu, v_ssm_b_glu, v_dn_conv, v_dn_a_log, v_dn_dt_bias, v_dn_norm_g, v_attn_q_norm, v_attn_k_norm, v_mem_norm_g, v_w_mem_kv, v_w_branch, v_w_out, v_final_norm_g):
    given = dict(x=x, mem=mem, norm_g=norm_g, w_in=w_in, ssm_a_re=ssm_a_re, ssm_a_im=ssm_a_im, ssm_log_step=ssm_log_step, ssm_b_re=ssm_b_re, ssm_b_im=ssm_b_im, ssm_c_re=ssm_c_re, ssm_c_im=ssm_c_im, ssm_d=ssm_d, ssm_w_glu=ssm_w_glu, ssm_b_glu=ssm_b_glu, dn_conv=dn_conv, dn_a_log=dn_a_log, dn_dt_bias=dn_dt_bias, dn_norm_g=dn_norm_g, attn_q_norm=attn_q_norm, attn_k_norm=attn_k_norm, mem_norm_g=mem_norm_g, w_mem_kv=w_mem_kv, w_branch=w_branch, w_out=w_out, final_norm_g=final_norm_g, loss_target=loss_target, m_norm_g=m_norm_g, m_w_in=m_w_in, m_ssm_a_re=m_ssm_a_re, m_ssm_a_im=m_ssm_a_im, m_ssm_log_step=m_ssm_log_step, m_ssm_b_re=m_ssm_b_re, m_ssm_b_im=m_ssm_b_im, m_ssm_c_re=m_ssm_c_re, m_ssm_c_im=m_ssm_c_im, m_ssm_d=m_ssm_d, m_ssm_w_glu=m_ssm_w_glu, m_ssm_b_glu=m_ssm_b_glu, m_dn_conv=m_dn_conv, m_dn_a_log=m_dn_a_log, m_dn_dt_bias=m_dn_dt_bias, m_dn_norm_g=m_dn_norm_g, m_attn_q_norm=m_attn_q_norm, m_attn_k_norm=m_attn_k_norm, m_mem_norm_g=m_mem_norm_g, m_w_mem_kv=m_w_mem_kv, m_w_branch=m_w_branch, m_w_out=m_w_out, m_final_norm_g=m_final_norm_g, v_norm_g=v_norm_g, v_w_in=v_w_in, v_ssm_a_re=v_ssm_a_re, v_ssm_a_im=v_ssm_a_im, v_ssm_log_step=v_ssm_log_step, v_ssm_b_re=v_ssm_b_re, v_ssm_b_im=v_ssm_b_im, v_ssm_c_re=v_ssm_c_re, v_ssm_c_im=v_ssm_c_im, v_ssm_d=v_ssm_d, v_ssm_w_glu=v_ssm_w_glu, v_ssm_b_glu=v_ssm_b_glu, v_dn_conv=v_dn_conv, v_dn_a_log=v_dn_a_log, v_dn_dt_bias=v_dn_dt_bias, v_dn_norm_g=v_dn_norm_g, v_attn_q_norm=v_attn_q_norm, v_attn_k_norm=v_attn_k_norm, v_mem_norm_g=v_mem_norm_g, v_w_mem_kv=v_w_mem_kv, v_w_branch=v_w_branch, v_w_out=v_w_out, v_final_norm_g=v_final_norm_g)
    weights = {n: given[n] for n in TWIN_WEIGHTS}
    shared = {n: given[n] for n in SHARED_INPUTS}
    per_example = {n: given[n] for n in ['x', 'mem']}
    grad_fn = _jax.value_and_grad(_loss, argnums=(0, 1))

    def one_microbatch(ex, loss_target):
        ex = dict(ex)
        diff = ex.pop(TWIN_DIFF_INPUT)
        return grad_fn(weights, diff, {**shared, **ex}, loss_target)

    if N_MICROBATCH == 1:
        loss, (grad_w, grad_x) = one_microbatch(per_example, given["loss_target"])
    else:
        def body(carry, xs):
            loss_sum, grad_sum = carry
            l_k, (gw_k, gx_k) = one_microbatch(xs[0], xs[1])
            with _jax.named_scope("update"):
                return (loss_sum + l_k, _jax.tree.map(_jnp.add, grad_sum, gw_k)), gx_k

        init = (_jnp.zeros((), _jnp.float32), _jax.tree.map(_jnp.zeros_like, weights))
        (loss, grad_w), grad_x = _jax.lax.scan(body, init, (per_example, given["loss_target"]))
    with _jax.named_scope("update"):
        delta_w, new_m, new_v = {}, {}, {}
        for n in TWIN_WEIGHTS:
            delta_w[n], new_m[n], new_v[n] = _adamw(weights[n], grad_w[n], given["m_" + n], given["v_" + n])
    return (loss, grad_x, *[grad_w[n] for n in TWIN_WEIGHTS], *[delta_w[n] for n in TWIN_WEIGHTS],
            *[new_m[n] for n in TWIN_WEIGHTS], *[new_v[n] for n in TWIN_WEIGHTS])
```

```python
import functools
import math

import jax
import jax.numpy as jnp
import numpy as np
from jax import lax
from jax.experimental import pallas as pl
from jax.experimental.pallas import tpu as pltpu

F32 = jnp.float32
BF16 = jnp.bfloat16

GRID_W = 64
EPS = 1e-6
SSM_GROUP = 16
SSM_STATE = 64
SSM_GROUPS = 48
DN_HEADS = 6
DN_HEAD_DIM = 128
DN_CONV = 5
DN_CHUNK = 64
ATT_HEADS = 8
ATT_KV_HEADS = 2
ATT_HEAD_DIM = 128
ROPE_THETA = 10000.0
MEM_HEADS = 4
MEM_HEAD_DIM = 128
N_BRANCH = 4

ADAM_LR = 0.001
ADAM_B1 = 0.9
ADAM_B2 = 0.999
ADAM_EPS = 1e-08
ADAM_WD = 0.01
ADAM_STEP = 10

N_CHIPS = 4
N_DEV = 8
LANES = 128
VMEM_LIMIT = 48 * 1024 * 1024
IN_PAD_UNIT = 512

WEIGHTS = ['norm_g', 'w_in', 'ssm_a_re', 'ssm_a_im', 'ssm_log_step', 'ssm_b_re', 'ssm_b_im', 'ssm_c_re',
           'ssm_c_im', 'ssm_d', 'ssm_w_glu', 'ssm_b_glu', 'dn_conv', 'dn_a_log', 'dn_dt_bias', 'dn_norm_g',
           'attn_q_norm', 'attn_k_norm', 'mem_norm_g', 'w_mem_kv', 'w_branch', 'w_out', 'final_norm_g']
BIG = ['w_in', 'ssm_w_glu', 'w_mem_kv', 'w_branch', 'w_out']
SMALL = [n for n in WEIGHTS if n not in BIG]


def _pick(dim, cands):
    for c in cands:
        if dim % c == 0:
            return c
    return dim


def _widths(d_model):
    ssm_w = SSM_GROUPS * SSM_GROUP
    dn_w = DN_HEADS * DN_HEAD_DIM
    att_w = ATT_HEADS * ATT_HEAD_DIM
    kv_w = ATT_KV_HEADS * ATT_HEAD_DIM
    mem_w = MEM_HEADS * MEM_HEAD_DIM
    return [("u", ssm_w), ("z_a", ssm_w), ("dq", dn_w), ("dk", dn_w), ("dv", dn_w), ("dab", 4 * DN_HEADS),
            ("z_b", dn_w), ("aq", att_w), ("ak", kv_w), ("av", kv_w), ("z_c", att_w), ("mq", mem_w),
            ("z_m", mem_w), ("gates", N_BRANCH * d_model)]


def _layout(d_model):
    out, src, dst = {}, 0, 0
    for name, w in _widths(d_model):
        pw = -(-w // IN_PAD_UNIT) * IN_PAD_UNIT if w % LANES else w
        out[name] = (src, dst, w, pw)
        src += w
        dst += pw
    return out, src, dst


def _pad_w_in(full, d_model):
    lay, _, _ = _layout(d_model)
    parts = []
    for name, _ in _widths(d_model):
        s, _, w, pw = lay[name]
        seg = full[..., s:s + w]
        if pw != w:
            seg = jnp.pad(seg, [(0, 0)] * (full.ndim - 1) + [(0, pw - w)])
        parts.append(seg)
    return jnp.concatenate(parts, axis=-1)


def _unpad_w_in(padded, d_model):
    lay, _, _ = _layout(d_model)
    return jnp.concatenate([padded[..., lay[n][1]:lay[n][1] + lay[n][2]] for n, _ in _widths(d_model)], axis=-1)


def _mm(a, b, mode="nn", name="mm"):
    if mode == "nn":
        (m, k), (k2, n) = a.shape, b.shape
    elif mode == "nt":
        (m, k), (n, k2) = a.shape, b.shape
    else:
        (k, m), (k2, n) = a.shape, b.shape
    assert k == k2, (a.shape, b.shape, mode)
    tm = _pick(m, (1024, 512, 256, 128))
    tn = _pick(n, (1024, 512, 384, 256, 128))
    tk = _pick(k, (1024, 512, 256, 128))
    nk = k // tk
    dims = {"nn": (((1,), (0,)), ((), ())), "nt": (((1,), (1,)), ((), ())), "tn": (((0,), (0,)), ((), ()))}[mode]

    def body(a_ref, b_ref, o_ref, acc_ref):
        kk = pl.program_id(2)

        @pl.when(kk == 0)
        def _():
            acc_ref[...] = jnp.zeros_like(acc_ref)

        acc_ref[...] += lax.dot_general(a_ref[...].astype(BF16), b_ref[...].astype(BF16), dims,
                                        preferred_element_type=F32)

        @pl.when(kk == nk - 1)
        def _():
            o_ref[...] = acc_ref[...]

    a_spec = pl.BlockSpec((tk, tm), lambda i, j, kk: (kk, i)) if mode == "tn" else pl.BlockSpec((tm, tk), lambda i, j, kk: (i, kk))
    b_spec = pl.BlockSpec((tn, tk), lambda i, j, kk: (j, kk)) if mode == "nt" else pl.BlockSpec((tk, tn), lambda i, j, kk: (kk, j))
    return pl.pallas_call(
        body, name=name, out_shape=jax.ShapeDtypeStruct((m, n), F32), grid=(m // tm, n // tn, nk),
        in_specs=[a_spec, b_spec], out_specs=pl.BlockSpec((tm, tn), lambda i, j, kk: (i, j)),
        scratch_shapes=[pltpu.VMEM((tm, tn), F32)],
        compiler_params=pltpu.CompilerParams(dimension_semantics=("parallel", "parallel", "arbitrary"),
                                             vmem_limit_bytes=VMEM_LIMIT),
    )(a, b)


@jax.custom_vjp
def pmm(a, b):
    return _mm(a, b, "nn", "pmm_fwd")


def _pmm_fwd(a, b):
    return _mm(a, b, "nn", "pmm_fwd"), (a, b)


def _pmm_bwd(res, dc):
    a, b = res
    return _mm(dc, b, "nt", "pmm_da"), _mm(a, dc, "tn", "pmm_db")


pmm.defvjp(_pmm_fwd, _pmm_bwd)


_NT = (((1,), (1,)), ((), ()))
_NN = (((1,), (0,)), ((), ()))
_TN = (((0,), (0,)), ((), ()))


def _att_blocks(lq, lk):
    return _pick(lq, (512, 256, 128)), _pick(lk, (512, 256, 128))


def _flash_fwd(q, k, v, dh, group, scale):
    lq, lk = q.shape[0], k.shape[0]
    hq = q.shape[1] // dh
    bq, bk = _att_blocks(lq, lk)
    nkv = lk // bk

    def body(q_ref, k_ref, v_ref, o_ref, lse_ref, m_s, l_s, acc_s):
        j = pl.program_id(2)

        @pl.when(j == 0)
        def _():
            m_s[...] = jnp.full_like(m_s, -jnp.inf)
            l_s[...] = jnp.zeros_like(l_s)
            acc_s[...] = jnp.zeros_like(acc_s)

        s = lax.dot_general(q_ref[...].astype(BF16), k_ref[...].astype(BF16), _NT, preferred_element_type=F32) * scale
        m_prev = m_s[...]
        m_cur = jnp.maximum(m_prev, jnp.max(s, axis=1, keepdims=True))
        alpha = jnp.exp(m_prev - m_cur)
        p = jnp.exp(s - m_cur[:, :1])
        l_s[...] = alpha * l_s[...] + jnp.sum(p, axis=1, keepdims=True)
        acc_s[...] = acc_s[...] * alpha[:, :1] + lax.dot_general(p.astype(BF16), v_ref[...].astype(BF16), _NN,
                                                               preferred_element_type=F32)
        m_s[...] = m_cur

        @pl.when(j == nkv - 1)
        def _():
            o_ref[...] = acc_s[...] / l_s[...][:, :1]
            lse_ref[...] = m_s[...] + jnp.log(l_s[...])

    return pl.pallas_call(
        body, name="flash_fwd",
        out_shape=(jax.ShapeDtypeStruct((lq, hq * dh), F32), jax.ShapeDtypeStruct((lq, hq * LANES), F32)),
        grid=(hq, lq // bq, nkv),
        in_specs=[pl.BlockSpec((bq, dh), lambda h, i, j: (i, h)),
                  pl.BlockSpec((bk, dh), lambda h, i, j: (j, h // group)),
                  pl.BlockSpec((bk, dh), lambda h, i, j: (j, h // group))],
        out_specs=(pl.BlockSpec((bq, dh), lambda h, i, j: (i, h)), pl.BlockSpec((bq, LANES), lambda h, i, j: (i, h))),
        scratch_shapes=[pltpu.VMEM((bq, LANES), F32), pltpu.VMEM((bq, LANES), F32), pltpu.VMEM((bq, dh), F32)],
        compiler_params=pltpu.CompilerParams(dimension_semantics=("parallel", "parallel", "arbitrary"),
                                             vmem_limit_bytes=VMEM_LIMIT),
    )(q, k, v)


def _flash_dq(q, k, v, o, lse, do, dh, group, scale):
    lq, lk = q.shape[0], k.shape[0]
    hq = q.shape[1] // dh
    bq, bk = _att_blocks(lq, lk)
    nkv = lk // bk

    def body(q_ref, k_ref, v_ref, o_ref, lse_ref, do_ref, dq_ref, acc_s, dl_s):
        j = pl.program_id(2)

        @pl.when(j == 0)
        def _():
            acc_s[...] = jnp.zeros_like(acc_s)
            dl_s[...] = jnp.broadcast_to(jnp.sum(do_ref[...] * o_ref[...], axis=1, keepdims=True), dl_s.shape)

        kb = k_ref[...].astype(BF16)
        s = lax.dot_general(q_ref[...].astype(BF16), kb, _NT, preferred_element_type=F32) * scale
        p = jnp.exp(s - lse_ref[...][:, :1])
        dp = lax.dot_general(do_ref[...].astype(BF16), v_ref[...].astype(BF16), _NT, preferred_element_type=F32)
        ds = p * (dp - dl_s[...][:, :1]) * scale
        acc_s[...] += lax.dot_general(ds.astype(BF16), kb, _NN, preferred_element_type=F32)

        @pl.when(j == nkv - 1)
        def _():
            dq_ref[...] = acc_s[...]

    qs = pl.BlockSpec((bq, dh), lambda h, i, j: (i, h))
    ks = pl.BlockSpec((bk, dh), lambda h, i, j: (j, h // group))
    return pl.pallas_call(
        body, name="flash_dq", out_shape=jax.ShapeDtypeStruct((lq, hq * dh), F32), grid=(hq, lq // bq, nkv),
        in_specs=[qs, ks, ks, qs, pl.BlockSpec((bq, LANES), lambda h, i, j: (i, h)), qs], out_specs=qs,
        scratch_shapes=[pltpu.VMEM((bq, dh), F32), pltpu.VMEM((bq, LANES), F32)],
        compiler_params=pltpu.CompilerParams(dimension_semantics=("parallel", "parallel", "arbitrary"),
                                             vmem_limit_bytes=VMEM_LIMIT),
    )(q, k, v, o, lse, do)


def _flash_dkv(q, k, v, o, lse, do, dh, group, scale):
    lq, lk = q.shape[0], k.shape[0]
    hkv = k.shape[1] // dh
    bq, bk = _att_blocks(lq, lk)
    nq = lq // bq
    nt = group * nq

    def body(q_ref, k_ref, v_ref, o_ref, lse_ref, do_ref, dk_ref, dv_ref, dk_s, dv_s):
        t = pl.program_id(2)

        @pl.when(t == 0)
        def _():
            dk_s[...] = jnp.zeros_like(dk_s)
            dv_s[...] = jnp.zeros_like(dv_s)

        qb = q_ref[...].astype(BF16)
        dob = do_ref[...].astype(BF16)
        s = lax.dot_general(qb, k_ref[...].astype(BF16), _NT, preferred_element_type=F32) * scale
        p = jnp.exp(s - lse_ref[...][:, :1])
        delta = jnp.sum(do_ref[...] * o_ref[...], axis=1, keepdims=True)
        dp = lax.dot_general(dob, v_ref[...].astype(BF16), _NT, preferred_element_type=F32)
        ds = p * (dp - delta) * scale
        dv_s[...] += lax.dot_general(p.astype(BF16), dob, _TN, preferred_element_type=F32)
        dk_s[...] += lax.dot_general(ds.astype(BF16), qb, _TN, preferred_element_type=F32)

        @pl.when(t == nt - 1)
        def _():
            dk_ref[...] = dk_s[...]
            dv_ref[...] = dv_s[...]

    qs = pl.BlockSpec((bq, dh), lambda h, j, t: (t % nq, h * group + t // nq))
    ks = pl.BlockSpec((bk, dh), lambda h, j, t: (j, h))
    return pl.pallas_call(
        body, name="flash_dkv",
        out_shape=(jax.ShapeDtypeStruct(k.shape, F32), jax.ShapeDtypeStruct(v.shape, F32)),
        grid=(hkv, lk // bk, nt),
        in_specs=[qs, ks, ks, qs, pl.BlockSpec((bq, LANES), lambda h, j, t: (t % nq, h * group + t // nq)), qs],
        out_specs=(ks, ks),
        scratch_shapes=[pltpu.VMEM((bk, dh), F32), pltpu.VMEM((bk, dh), F32)],
        compiler_params=pltpu.CompilerParams(dimension_semantics=("parallel", "parallel", "arbitrary"),
                                             vmem_limit_bytes=VMEM_LIMIT),
    )(q, k, v, o, lse, do)


@functools.partial(jax.custom_vjp, nondiff_argnums=(3, 4, 5))
def flash(q, k, v, dh, group, scale):
    return _flash_fwd(q, k, v, dh, group, scale)[0]


def _flash_vfwd(q, k, v, dh, group, scale):
    o, lse = _flash_fwd(q, k, v, dh, group, scale)
    return o, (q, k, v, o, lse)


def _flash_vbwd(dh, group, scale, res, do):
    q, k, v, o, lse = res
    dq = _flash_dq(q, k, v, o, lse, do, dh, group, scale)
    dk, dv = _flash_dkv(q, k, v, o, lse, do, dh, group, scale)
    return dq, dk, dv


flash.defvjp(_flash_vfwd, _flash_vbwd)


def _scan_call(lr, li, br, bi, prev_r, prev_i, reverse, name):
    rows, width = br.shape
    ts = _pick(width, (1024, 512, 256, 128))
    tr = _pick(rows, (256, 128, 64, 32, 16, 8))
    nb = rows // tr
    with_prev = prev_r is not None

    def body(*refs):
        if with_prev:
            lr_ref, li_ref, br_ref, bi_ref, pr_ref, pi_ref, sr_ref, si_ref, dr_ref, di_ref, cr_s, ci_s = refs
        else:
            lr_ref, li_ref, br_ref, bi_ref, sr_ref, si_ref, pr_ref, pi_ref, cr_s, ci_s = refs
        i = pl.program_id(1)

        @pl.when(i == 0)
        def _():
            cr_s[...] = jnp.zeros_like(cr_s)
            ci_s[...] = jnp.zeros_like(ci_s)
            if with_prev:
                dr_ref[...] = jnp.zeros_like(dr_ref)
                di_ref[...] = jnp.zeros_like(di_ref)

        a_re = lr_ref[...]
        a_im = li_ref[...]

        def step(r, carry):
            if with_prev:
                c_re, c_im, d_re, d_im = carry
            else:
                c_re, c_im = carry
            t = tr - 1 - r if reverse else r
            row = pl.ds(t, 1)
            n_re = a_re * c_re - a_im * c_im + br_ref[row, :]
            n_im = a_re * c_im + a_im * c_re + bi_ref[row, :]
            sr_ref[row, :] = n_re
            si_ref[row, :] = n_im
            if with_prev:
                p_re = pr_ref[row, :]
                p_im = pi_ref[row, :]
                return n_re, n_im, d_re + n_re * p_re + n_im * p_im, d_im + n_im * p_re - n_re * p_im
            pr_ref[row, :] = c_re
            pi_ref[row, :] = c_im
            return n_re, n_im

        zero = jnp.zeros((1, ts), F32)
        init = (cr_s[...], ci_s[...]) + ((zero, zero) if with_prev else ())
        out = lax.fori_loop(0, tr, step, init, unroll=8)
        cr_s[...] = out[0]
        ci_s[...] = out[1]
        if with_prev:
            dr_ref[...] += out[2]
            di_ref[...] += out[3]

    blk = (lambda j, i: (nb - 1 - i, j)) if reverse else (lambda j, i: (i, j))
    row_spec = pl.BlockSpec((tr, ts), blk)
    vec_spec = pl.BlockSpec((1, ts), lambda j, i: (0, j))
    full = jax.ShapeDtypeStruct((rows, width), F32)
    vec = jax.ShapeDtypeStruct((1, width), F32)
    ins = [lr, li, br, bi] + ([prev_r, prev_i] if with_prev else [])
    return pl.pallas_call(
        body, name=name, grid=(width // ts, nb),
        in_specs=[vec_spec, vec_spec] + [row_spec] * (len(ins) - 2),
        out_shape=(full, full, vec, vec) if with_prev else (full, full, full, full),
        out_specs=(row_spec, row_spec, vec_spec, vec_spec) if with_prev else (row_spec,) * 4,
        scratch_shapes=[pltpu.VMEM((1, ts), F32), pltpu.VMEM((1, ts), F32)],
        compiler_params=pltpu.CompilerParams(dimension_semantics=("parallel", "arbitrary"), vmem_limit_bytes=VMEM_LIMIT),
    )(*ins)


@functools.partial(jax.custom_vjp, nondiff_argnums=(4,))
def cscan(lr, li, br, bi, reverse):
    out = _scan_call(lr, li, br, bi, None, None, reverse, "s5_scan")
    return out[0], out[1]


def _cscan_fwd(lr, li, br, bi, reverse):
    sr, si, pr, pi = _scan_call(lr, li, br, bi, None, None, reverse, "s5_scan")
    return (sr, si), (lr, li, pr, pi)


def _cscan_bwd(reverse, res, g):
    lr, li, pr, pi = res
    ar, ai, dlr, dli = _scan_call(lr, -li, g[0], g[1], pr, pi, not reverse, "s5_scan_bwd")
    return dlr, dli, ar, ai


cscan.defvjp(_cscan_fwd, _cscan_bwd)


def _rmsnorm(x, g):
    return x * lax.rsqrt(jnp.mean(x * x, axis=-1, keepdims=True) + EPS) * g


def _l2norm(x):
    return x * lax.rsqrt(jnp.sum(x * x, axis=-1, keepdims=True) + EPS)


def _block_diag(w):
    g, a, b = w.shape
    eye = jnp.eye(g, dtype=w.dtype)
    return (w[:, :, None, :] * eye[:, None, :, None]).reshape(g * a, g * b)


def _s5_mixer(u, a_re, a_im, log_step, b_re, b_im, c_re, c_im, d, w_glu, b_glu):
    y = u * d
    for direction in range(2):
        are, aim = a_re[direction], a_im[direction]
        step = jnp.exp(log_step[direction])[:, None]
        mag = jnp.exp(are * step)
        lam_re = mag * jnp.cos(aim * step)
        lam_im = mag * jnp.sin(aim * step)
        den = are * are + aim * aim
        nr = lam_re - 1.0
        ni = lam_im
        coef_re = (nr * are + ni * aim) / den
        coef_im = (ni * are - nr * aim) / den
        bb_re = coef_re[..., None] * b_re[direction] - coef_im[..., None] * b_im[direction]
        bb_im = coef_re[..., None] * b_im[direction] + coef_im[..., None] * b_re[direction]
        w_bu = jnp.concatenate([_block_diag(jnp.swapaxes(bb_re, 1, 2)), _block_diag(jnp.swapaxes(bb_im, 1, 2))], axis=1)
        bu = pmm(u, w_bu)
        width = bu.shape[1] // 2
        s_re, s_im = cscan(lam_re.reshape(1, width), lam_im.reshape(1, width), bu[:, :width], bu[:, width:],
                           direction == 1)
        y = y + pmm(s_re, _block_diag(jnp.swapaxes(c_re[direction], 1, 2)))
        y = y - pmm(s_im, _block_diag(jnp.swapaxes(c_im[direction], 1, 2)))
    y = jax.nn.gelu(y)
    return y * jax.nn.sigmoid(pmm(y, w_glu) + b_glu)


def _short_conv(x, w):
    ch = x.shape[-1]
    rhs = jnp.transpose(w)[:, None, :]
    return lax.conv_general_dilated(x[None], rhs, window_strides=(1,), padding=[(DN_CONV // 2, DN_CONV // 2)],
                                    dimension_numbers=("NWC", "WIO", "NWC"), feature_group_count=ch)[0]


def _gated_delta_rule(q, k, v, beta, g):
    h, l, dk = q.shape
    dv = v.shape[-1]
    c = DN_CHUNK
    n = l // c
    q = q.reshape(h, n, c, dk)
    k = k.reshape(h, n, c, dk)
    v = v.reshape(h, n, c, dv)
    beta = beta.reshape(h, n, c)
    g = jnp.cumsum(g.reshape(h, n, c), axis=-1)
    idx = jnp.arange(c)
    incl = idx[:, None] >= idx[None, :]
    strict = idx[:, None] > idx[None, :]
    diff = g[..., :, None] - g[..., None, :]
    decay = jnp.where(incl, jnp.exp(jnp.where(incl, diff, 0.0)), 0.0)
    k_beta = k * beta[..., None]
    lower = jnp.where(strict, jnp.einsum("hncd,hnsd->hncs", k_beta, k) * decay, 0.0)
    eye = jnp.eye(c, dtype=q.dtype)
    rhs = jnp.concatenate([v * beta[..., None], k_beta * jnp.exp(g)[..., None]], axis=-1)
    sol = lax.linalg.triangular_solve(eye + lower, rhs, left_side=True, lower=True, unit_diagonal=True)
    u_c = sol[..., :dv]
    w_c = sol[..., dv:]
    intra = jnp.einsum("hncd,hnsd->hncs", q, k) * decay
    q_dec = q * jnp.exp(g)[..., None]
    k_dec = k * jnp.exp(g[..., -1:] - g)[..., None]
    g_last = jnp.exp(g[..., -1])
    xs = (jnp.moveaxis(u_c, 1, 0), jnp.moveaxis(w_c, 1, 0), jnp.moveaxis(q_dec, 1, 0),
          jnp.moveaxis(k_dec, 1, 0), jnp.moveaxis(intra, 1, 0), jnp.moveaxis(g_last, 1, 0))

    def step(state, inp):
        u_i, w_i, qd_i, kd_i, a_i, gl_i = inp
        v_new = u_i - jnp.einsum("hck,hkv->hcv", w_i, state)
        o = jnp.einsum("hck,hkv->hcv", qd_i, state) + jnp.einsum("hcs,hsv->hcv", a_i, v_new)
        state = state * gl_i[..., None, None] + jnp.einsum("hck,hcv->hkv", kd_i, v_new)
        return state, o

    s0 = jnp.zeros((h, dk, dv), q.dtype)
    _, o = lax.scan(step, s0, xs)
    return jnp.moveaxis(o, 0, 1).reshape(h, l, dv)


def _deltanet_mixer(q, k, v, ab, conv_w, a_log, dt_bias, norm_g):
    seq = q.shape[0]
    qkv = jax.nn.silu(_short_conv(jnp.concatenate([q, k, v], axis=-1), conv_w))
    qkv = qkv.reshape(seq, 3, DN_HEADS, DN_HEAD_DIM)
    qh = jnp.transpose(_l2norm(qkv[:, 0]) * (DN_HEAD_DIM ** -0.5), (1, 0, 2))
    kh = jnp.transpose(_l2norm(qkv[:, 1]), (1, 0, 2))
    vh = jnp.transpose(qkv[:, 2], (1, 0, 2))
    a4 = ab[:, :2 * DN_HEADS].reshape(seq, 2, DN_HEADS)
    b4 = ab[:, 2 * DN_HEADS:4 * DN_HEADS].reshape(seq, 2, DN_HEADS)
    beta = jax.nn.sigmoid(b4)
    g = -jnp.exp(a_log) * jax.nn.softplus(a4 + dt_bias)
    o_f = _gated_delta_rule(qh, kh, vh, jnp.transpose(beta[:, 0]), jnp.transpose(g[:, 0]))
    o_b = jnp.flip(_gated_delta_rule(jnp.flip(qh, 1), jnp.flip(kh, 1), jnp.flip(vh, 1),
                                     jnp.flip(jnp.transpose(beta[:, 1]), 1), jnp.flip(jnp.transpose(g[:, 1]), 1)), 1)
    o = jnp.transpose(o_f + o_b, (1, 0, 2))
    return _rmsnorm(o, norm_g).reshape(seq, DN_HEADS * DN_HEAD_DIM)


def _axial_rope(seq):
    rows = seq // GRID_W
    row = jnp.repeat(jnp.arange(rows), GRID_W).astype(F32)
    col = jnp.tile(jnp.arange(GRID_W), rows).astype(F32)
    axis_dim = ATT_HEAD_DIM // 2
    freqs = ROPE_THETA ** (-jnp.arange(0, axis_dim, 2, dtype=F32) / axis_dim)
    ang = jnp.concatenate([row[:, None] * freqs, col[:, None] * freqs], axis=-1)
    return jnp.cos(ang), jnp.sin(ang)


def _apply_rope(x, cos, sin):
    xp = x.reshape(x.shape[:-1] + (x.shape[-1] // 2, 2))
    x0, x1 = xp[..., 0], xp[..., 1]
    c = cos[:, None, :]
    s = sin[:, None, :]
    return jnp.stack([x0 * c - x1 * s, x0 * s + x1 * c], axis=-1).reshape(x.shape)


def _grid_attention(q, k, v, qn_g, kn_g, cos, sin):
    seq = q.shape[0]
    qh = _rmsnorm(q.reshape(seq, ATT_HEADS, ATT_HEAD_DIM), qn_g)
    kh = _rmsnorm(k.reshape(seq, ATT_KV_HEADS, ATT_HEAD_DIM), kn_g)
    qh = _apply_rope(qh, cos, sin) * (ATT_HEAD_DIM ** -0.5)
    kh = _apply_rope(kh, cos, sin)
    return flash(qh.reshape(seq, -1), kh.reshape(seq, -1), v, ATT_HEAD_DIM, ATT_HEADS // ATT_KV_HEADS, 1.0)


def _memory_attention(q, mem_n, w_kv):
    kv = pmm(mem_n, w_kv)
    mem_w = MEM_HEADS * MEM_HEAD_DIM
    return flash(q, kv[:, :mem_w], kv[:, mem_w:], MEM_HEAD_DIM, 1, MEM_HEAD_DIM ** -0.5)


def _middle(h, x, mem, p):
    d_model = x.shape[1]
    lay, _, _ = _layout(d_model)

    def seg(name):
        _, dst, w, _ = lay[name]
        return h[:, dst:dst + w]

    cos, sin = _axial_rope(x.shape[0])
    y_a = _s5_mixer(seg("u"), p["ssm_a_re"], p["ssm_a_im"], p["ssm_log_step"], p["ssm_b_re"], p["ssm_b_im"],
                    p["ssm_c_re"], p["ssm_c_im"], p["ssm_d"], p["ssm_w_glu"], p["ssm_b_glu"]) * jax.nn.silu(seg("z_a"))
    y_b = _deltanet_mixer(seg("dq"), seg("dk"), seg("dv"), seg("dab"), p["dn_conv"], p["dn_a_log"],
                          p["dn_dt_bias"], p["dn_norm_g"]) * jax.nn.silu(seg("z_b"))
    y_c = _grid_attention(seg("aq"), seg("ak"), seg("av"), p["attn_q_norm"], p["attn_k_norm"], cos, sin) * jax.nn.silu(seg("z_c"))
    y_m = _memory_attention(seg("mq"), _rmsnorm(mem, p["mem_norm_g"]), p["w_mem_kv"]) * jax.nn.silu(seg("z_m"))
    gates = jax.nn.sigmoid(seg("gates"))
    merged = jnp.zeros_like(x)
    off = 0
    for bi, y_br in enumerate((y_a, y_b, y_c, y_m)):
        w = y_br.shape[1]
        merged = merged + gates[:, bi * d_model:(bi + 1) * d_model] * pmm(y_br, p["w_branch"][off:off + w])
        off += w
    return x + pmm(merged, p["w_out"])


def _final_loss(x, g, target):
    y = _rmsnorm(x, g)
    return 0.5 * jnp.sum(jnp.mean(jnp.square(y - target), axis=-1))


def _local_step(x, mem, target, w):
    depth = w["norm_g"].shape[0]
    d_model = x.shape[1]
    mid_names = [n for n in WEIGHTS if n not in ("norm_g", "w_in", "final_norm_g")]
    saved = []
    cur = x
    for layer in range(depth):
        xn, vjp_norm = jax.vjp(_rmsnorm, cur, w["norm_g"][layer])
        h = _mm(xn, w["w_in"][layer], "nn", "in_proj")
        p = {n: w[n][layer] for n in mid_names}
        nxt, vjp_mid = jax.vjp(lambda h_, x_, p_: _middle(h_, x_, mem, p_), h, cur, p)
        saved.append((xn, vjp_norm, vjp_mid))
        cur = nxt
    loss, vjp_fin = jax.vjp(lambda x_, g_: _final_loss(x_, g_, target), cur, w["final_norm_g"])
    dx, d_final = vjp_fin(jnp.ones((), F32))
    grads = {n: [None] * depth for n in WEIGHTS if n != "final_norm_g"}
    for layer in reversed(range(depth)):
        xn, vjp_norm, vjp_mid = saved[layer]
        dh, dx_skip, dp = vjp_mid(dx)
        grads["w_in"][layer] = _unpad_w_in(_mm(xn, dh, "tn", "in_proj_dw"), d_model)
        dxn = _mm(dh, w["w_in"][layer], "nt", "in_proj_dx")
        dx_n, dg = vjp_norm(dxn)
        dx = dx_skip + dx_n
        grads["norm_g"][layer] = dg
        for n in mid_names:
            grads[n][layer] = dp[n]
    out = {n: jnp.stack(v) for n, v in grads.items()}
    out["final_norm_g"] = d_final
    return loss, dx, out


_ANY = pl.BlockSpec(memory_space=pl.ANY)


def _remote(src, dst, send_sem, recv_sem, dev):
    return pltpu.make_async_remote_copy(src_ref=src, dst_ref=dst, send_sem=send_sem, recv_sem=recv_sem,
                                        device_id=dev, device_id_type=pl.DeviceIdType.MESH)


def _flip(v, bit):
    return 1 - v if bit else v


def _gather_chips(shards):
    n = len(shards)

    def body(*refs):
        ins, outs = refs[:n], refs[n:2 * n]
        send_sems, recv_sems, local_sems = refs[2 * n:]
        x, y, c = lax.axis_index("x"), lax.axis_index("y"), lax.axis_index("c")
        flips = [(1, 0), (0, 1), (1, 1)]
        locals_, sends = [], []
        for a in range(n):
            cp = pltpu.make_async_copy(ins[a], outs[a].at[2 * x + y], local_sems.at[a])
            cp.start()
            locals_.append(cp)

        def half(a, px, py, pc):
            rh = shards[a].shape[1] // 2
            return outs[a].at[2 * px + py, :, pl.ds(pc * rh, rh), :]

        for a in range(n):
            rh = shards[a].shape[1] // 2
            for j, (fx, fy) in enumerate(flips):
                cp = _remote(ins[a].at[:, pl.ds(c * rh, rh), :], half(a, x, y, c), send_sems.at[6 * a + j],
                             recv_sems.at[6 * a + j], (_flip(x, fx), _flip(y, fy), c))
                cp.start()
                sends.append(cp)
        for a in range(n):
            for j, (fx, fy) in enumerate(flips):
                blk = half(a, _flip(x, fx), _flip(y, fy), c)
                _remote(blk, blk, send_sems.at[6 * a + j], recv_sems.at[6 * a + j], (x, y, c)).wait_recv()
                cp = _remote(blk, blk, send_sems.at[6 * a + 3 + j], recv_sems.at[6 * a + 3 + j], (x, y, 1 - c))
                cp.start()
                sends.append(cp)
        for a in range(n):
            for j, (fx, fy) in enumerate(flips):
                blk = half(a, _flip(x, fx), _flip(y, fy), 1 - c)
                _remote(blk, blk, send_sems.at[6 * a + 3 + j], recv_sems.at[6 * a + 3 + j], (x, y, c)).wait_recv()
        for cp in sends:
            cp.wait_send()
        for cp in locals_:
            cp.wait()

    return pl.pallas_call(
        body, name="gather_weights",
        out_shape=tuple(jax.ShapeDtypeStruct((N_CHIPS,) + s.shape, s.dtype) for s in shards),
        in_specs=[_ANY] * n, out_specs=tuple([_ANY] * n),
        scratch_shapes=[pltpu.SemaphoreType.DMA((6 * n,)), pltpu.SemaphoreType.DMA((6 * n,)),
                        pltpu.SemaphoreType.DMA((n,))],
    )(*shards)


def _exchange_partials(big, small):
    n = len(big)

    def body(*refs):
        ins, small_in = refs[:n], refs[n]
        outs, small_out = refs[n + 1:2 * n + 1], refs[2 * n + 1]
        send_sems, recv_sems, local_sems = refs[2 * n + 2:]
        x, y, c = lax.axis_index("x"), lax.axis_index("y"), lax.axis_index("c")
        me = 4 * x + 2 * y + c
        started, locals_ = [], []
        for a in range(n):
            rh = big[a].shape[2] // 2
            cp = pltpu.make_async_copy(ins[a].at[2 * x + y, :, pl.ds(c * rh, rh), :], outs[a].at[0], local_sems.at[a])
            cp.start()
            locals_.append(cp)
        cp = pltpu.make_async_copy(small_in, small_out.at[me], local_sems.at[n])
        cp.start()
        locals_.append(cp)
        for k in range(1, N_DEV):
            px, py, pc = _flip(x, k >> 2 & 1), _flip(y, k >> 1 & 1), _flip(c, k & 1)
            for a in range(n):
                rh = big[a].shape[2] // 2
                cp = _remote(ins[a].at[2 * px + py, :, pl.ds(pc * rh, rh), :], outs[a].at[k],
                             send_sems.at[(n + 1) * (k - 1) + a], recv_sems.at[(n + 1) * (k - 1) + a], (px, py, pc))
                cp.start()
                started.append(cp)
            cp = _remote(small_in, small_out.at[me], send_sems.at[(n + 1) * (k - 1) + n],
                         recv_sems.at[(n + 1) * (k - 1) + n], (px, py, pc))
            cp.start()
            started.append(cp)
        for k in range(1, N_DEV):
            px, py, pc = _flip(x, k >> 2 & 1), _flip(y, k >> 1 & 1), _flip(c, k & 1)
            for a in range(n):
                _remote(outs[a].at[k], outs[a].at[k], send_sems.at[(n + 1) * (k - 1) + a],
                        recv_sems.at[(n + 1) * (k - 1) + a], (x, y, c)).wait_recv()
            peer = 4 * px + 2 * py + pc
            _remote(small_out.at[peer], small_out.at[peer], send_sems.at[(n + 1) * (k - 1) + n],
                    recv_sems.at[(n + 1) * (k - 1) + n], (x, y, c)).wait_recv()
        for cp in started:
            cp.wait_send()
        for cp in locals_:
            cp.wait()

    shapes = tuple(jax.ShapeDtypeStruct((N_DEV, b.shape[1], b.shape[2] // 2, b.shape[3]), b.dtype) for b in big)
    shapes += (jax.ShapeDtypeStruct((N_DEV,) + small.shape, small.dtype),)
    nsem = (n + 1) * (N_DEV - 1)
    return pl.pallas_call(
        body, name="exchange_partials", out_shape=shapes, in_specs=[_ANY] * (n + 1), out_specs=tuple([_ANY] * (n + 1)),
        scratch_shapes=[pltpu.SemaphoreType.DMA((nsem,)), pltpu.SemaphoreType.DMA((nsem,)),
                        pltpu.SemaphoreType.DMA((n + 1,))],
    )(*big, small)


def _swap_halves(halves):
    n = len(halves)

    def body(*refs):
        ins, outs = refs[:n], refs[n:2 * n]
        send_sems, recv_sems, local_sems = refs[2 * n:]
        x, y, c = lax.axis_index("x"), lax.axis_index("y"), lax.axis_index("c")
        started, locals_ = [], []
        for a in range(n):
            rh = halves[a].shape[1]
            mine = outs[a].at[:, pl.ds(c * rh, rh), :]
            cp = pltpu.make_async_copy(ins[a], mine, local_sems.at[a])
            cp.start()
            locals_.append(cp)
            cp = _remote(ins[a], mine, send_sems.at[a], recv_sems.at[a], (x, y, 1 - c))
            cp.start()
            started.append(cp)
        for a in range(n):
            rh = halves[a].shape[1]
            theirs = outs[a].at[:, pl.ds((1 - c) * rh, rh), :]
            _remote(theirs, theirs, send_sems.at[a], recv_sems.at[a], (x, y, c)).wait_recv()
        for cp in started:
            cp.wait_send()
        for cp in locals_:
            cp.wait()

    return pl.pallas_call(
        body, name="swap_halves",
        out_shape=tuple(jax.ShapeDtypeStruct((h.shape[0], 2 * h.shape[1], h.shape[2]), h.dtype) for h in halves),
        in_specs=[_ANY] * n, out_specs=tuple([_ANY] * n),
        scratch_shapes=[pltpu.SemaphoreType.DMA((n,)), pltpu.SemaphoreType.DMA((n,)), pltpu.SemaphoreType.DMA((n,))],
    )(*halves)


def _adamw_math(g, w, m, v):
    m = ADAM_B1 * m + (1.0 - ADAM_B1) * g
    v = ADAM_B2 * v + (1.0 - ADAM_B2) * jnp.square(g)
    m_hat = m / (1.0 - ADAM_B1 ** ADAM_STEP)
    v_hat = v / (1.0 - ADAM_B2 ** ADAM_STEP)
    delta = -ADAM_LR * (m_hat / (jnp.sqrt(v_hat) + ADAM_EPS) + ADAM_WD * w)
    return delta, m, v


def _row_tile(rows, row_bytes, budget):
    best = 8
    for t in range(8, rows + 1, 8):
        if rows % t == 0 and t * row_bytes <= budget:
            best = t
    return best


def _sum_adamw_big(parts, w, m, v, core):
    _, a, rh, c = parts.shape
    tr = _row_tile(rh, 4 * (-(-c // LANES) * LANES), 256 * 1024)
    w4, m4, v4 = (t.reshape(a, 2, rh, c) for t in (w, m, v))

    def body(core_ref, p_ref, w_ref, m_ref, v_ref, g_out, d_out, m_out, v_out):
        g = p_ref[0, 0]
        for k in range(1, N_DEV):
            g = g + p_ref[k, 0]
        delta, mm_, vv_ = _adamw_math(g, w_ref[0, 0], m_ref[0, 0], v_ref[0, 0])
        g_out[0] = g
        d_out[0] = delta
        m_out[0] = mm_
        v_out[0] = vv_

    wspec = pl.BlockSpec((1, 1, tr, c), lambda l, i, core_ref: (l, core_ref[0], i, 0))
    ospec = pl.BlockSpec((1, tr, c), lambda l, i, core_ref: (l, i, 0))
    out = jax.ShapeDtypeStruct((a, rh, c), F32)
    return pl.pallas_call(
        body, name="sum_adamw_big", out_shape=(out,) * 4,
        grid_spec=pltpu.PrefetchScalarGridSpec(
            num_scalar_prefetch=1, grid=(a, rh // tr),
            in_specs=[pl.BlockSpec((N_DEV, 1, tr, c), lambda l, i, core_ref: (0, l, i, 0)), wspec, wspec, wspec],
            out_specs=(ospec,) * 4),
        compiler_params=pltpu.CompilerParams(dimension_semantics=("parallel", "parallel"), vmem_limit_bytes=VMEM_LIMIT),
    )(core, parts, w4, m4, v4)


def _sum_small(parts):
    _, rows, _ = parts.shape
    tr = _row_tile(rows, 4 * LANES, 256 * 1024)

    def body(p_ref, o_ref):
        g = p_ref[0]
        for k in range(1, N_DEV):
            g = g + p_ref[k]
        o_ref[...] = g

    return pl.pallas_call(
        body, name="sum_small", out_shape=jax.ShapeDtypeStruct((rows, LANES), F32), grid=(rows // tr,),
        in_specs=[pl.BlockSpec((N_DEV, tr, LANES), lambda i: (0, i, 0))], out_specs=pl.BlockSpec((tr, LANES), lambda i: (i, 0)),
        compiler_params=pltpu.CompilerParams(dimension_semantics=("parallel",)),
    )(parts)


def _adamw_small(g, w, m, v):
    rows, _ = g.shape
    tr = _row_tile(rows, 4 * LANES, 256 * 1024)

    def body(g_ref, w_ref, m_ref, v_ref, d_out, m_out, v_out):
        d_out[...], m_out[...], v_out[...] = _adamw_math(g_ref[...], w_ref[...], m_ref[...], v_ref[...])

    spec = pl.BlockSpec((tr, LANES), lambda i: (i, 0))
    out = jax.ShapeDtypeStruct((rows, LANES), F32)
    return pl.pallas_call(
        body, name="adamw_small", out_shape=(out,) * 3, grid=(rows // tr,), in_specs=[spec] * 4, out_specs=(spec,) * 3,
        compiler_params=pltpu.CompilerParams(dimension_semantics=("parallel",)),
    )(g, w, m, v)


def _pack(arrays):
    flat = jnp.concatenate([a.reshape(-1) for a in arrays])
    rows = -(-flat.shape[0] // (8 * LANES)) * 8
    return jnp.pad(flat, (0, rows * LANES - flat.shape[0])).reshape(rows, LANES)


def _unpack(packed, shapes):
    flat, out, off = packed.reshape(-1), [], 0
    for s in shapes:
        size = math.prod(s)
        out.append(flat[off:off + size].reshape(s))
        off += size
    return out


def kernel(x, mem, norm_g, w_in, ssm_a_re, ssm_a_im, ssm_log_step, ssm_b_re, ssm_b_im, ssm_c_re, ssm_c_im, ssm_d, ssm_w_glu, ssm_b_glu, dn_conv, dn_a_log, dn_dt_bias, dn_norm_g, attn_q_norm, attn_k_norm, mem_norm_g, w_mem_kv, w_branch, w_out, final_norm_g, loss_target, m_norm_g, m_w_in, m_ssm_a_re, m_ssm_a_im, m_ssm_log_step, m_ssm_b_re, m_ssm_b_im, m_ssm_c_re, m_ssm_c_im, m_ssm_d, m_ssm_w_glu, m_ssm_b_glu, m_dn_conv, m_dn_a_log, m_dn_dt_bias, m_dn_norm_g, m_attn_q_norm, m_attn_k_norm, m_mem_norm_g, m_w_mem_kv, m_w_branch, m_w_out, m_final_norm_g, v_norm_g, v_w_in, v_ssm_a_re, v_ssm_a_im, v_ssm_log_step, v_ssm_b_re, v_ssm_b_im, v_ssm_c_re, v_ssm_c_im, v_ssm_d, v_ssm_w_glu, v_ssm_b_glu, v_dn_conv, v_dn_a_log, v_dn_dt_bias, v_dn_norm_g, v_attn_q_norm, v_attn_k_norm, v_mem_norm_g, v_w_mem_kv, v_w_branch, v_w_out, v_final_norm_g):
    args = locals()
    wts = {n: args[n] for n in WEIGHTS}
    mom = {n: args["m_" + n] for n in WEIGHTS}
    var = {n: args["v_" + n] for n in WEIGHTS}
    d_model = x.shape[-1]
    depth = norm_g.shape[0]
    chip = 2 * lax.axis_index("x") + lax.axis_index("y")
    core = lax.axis_index("c")

    conv_rows = dn_conv.shape[1]
    conv_flat = _pack([dn_conv])
    gathered = _gather_chips([wts[n].astype(BF16) for n in BIG] + [conv_flat[None]])
    full = {n: wts[n] for n in SMALL}
    g_in = gathered[0]
    full["w_in"] = _pad_w_in(jnp.concatenate([g_in[j] for j in range(N_CHIPS)], axis=-1), d_model)
    for n, g in zip(BIG[1:], gathered[1:]):
        full[n] = jnp.concatenate([g[j] for j in range(N_CHIPS)], axis=1).astype(F32)
    conv_all = gathered[-1][:, 0].reshape(N_CHIPS, -1)[:, :dn_conv.size].reshape((N_CHIPS,) + dn_conv.shape)
    full["dn_conv"] = jnp.concatenate([conv_all[j] for j in range(N_CHIPS)], axis=1)

    loss, grad_x, grads = _local_step(x[0], mem[0], loss_target[0], full)
    loss = lax.psum(loss, ("x", "y", "c"))

    big_parts = []
    for n in BIG:
        g = grads[n]
        if n == "w_in":
            cols = g.shape[-1] // N_CHIPS
            big_parts.append(jnp.stack([g[..., j * cols:(j + 1) * cols] for j in range(N_CHIPS)]))
        else:
            rows = g.shape[1] // N_CHIPS
            big_parts.append(jnp.stack([g[:, j * rows:(j + 1) * rows] for j in range(N_CHIPS)]))
    small_shapes = [grads[n].shape for n in SMALL]
    landed = _exchange_partials(big_parts, _pack([grads[n] for n in SMALL]))

    core_arr = jnp.reshape(core, (1,)).astype(jnp.int32)
    halves = []
    for n, parts in zip(BIG, landed[:-1]):
        halves.extend(_sum_adamw_big(parts, wts[n], mom[n], var[n], core_arr))
    swapped = _swap_halves(halves)
    res = {}
    for i, n in enumerate(BIG):
        res[n] = swapped[4 * i:4 * i + 4]
    small_g = _unpack(_sum_small(landed[-1]), small_shapes)
    small_g = dict(zip(SMALL, small_g))
    small_g["dn_conv"] = lax.dynamic_slice_in_dim(small_g["dn_conv"], chip * conv_rows, conv_rows, axis=1)
    shapes = [small_g[n].shape for n in SMALL]
    upd = _adamw_small(_pack([small_g[n] for n in SMALL]), _pack([wts[n] for n in SMALL]),
                       _pack([mom[n] for n in SMALL]), _pack([var[n] for n in SMALL]))
    upd = [_unpack(u, shapes) for u in upd]
    for i, n in enumerate(SMALL):
        res[n] = (small_g[n], upd[0][i], upd[1][i], upd[2][i])

    out = [loss, grad_x[None]]
    for kind in range(4):
        out.extend(res[n][kind] for n in WEIGHTS)
    return tuple(out)
```

```python
import functools
import math

import jax
import jax.numpy as jnp
from jax import lax
from jax.experimental import pallas as pl
from jax.experimental.pallas import tpu as pltpu

F32 = jnp.float32
BF16 = jnp.bfloat16

GRID_W = 64
EPS = 1e-6
SSM_GROUP = 16
SSM_STATE = 64
SSM_GROUPS = 48
DN_HEADS = 6
DN_HEAD_DIM = 128
DN_CONV = 5
DN_CHUNK = 64
ATT_HEADS = 8
ATT_KV_HEADS = 2
ATT_HEAD_DIM = 128
ROPE_THETA = 10000.0
MEM_HEADS = 4
MEM_HEAD_DIM = 128
N_BRANCH = 4

ADAM_LR = 0.001
ADAM_B1 = 0.9
ADAM_B2 = 0.999
ADAM_EPS = 1e-08
ADAM_WD = 0.01
ADAM_STEP = 10

N_CHIPS = 4
N_DEV = 8
LANES = 128
VMEM_LIMIT = 48 * 1024 * 1024
IN_PAD_UNIT = 512
ATT_HEADS_PER_STEP = 4

WEIGHTS = ['norm_g', 'w_in', 'ssm_a_re', 'ssm_a_im', 'ssm_log_step', 'ssm_b_re', 'ssm_b_im', 'ssm_c_re',
           'ssm_c_im', 'ssm_d', 'ssm_w_glu', 'ssm_b_glu', 'dn_conv', 'dn_a_log', 'dn_dt_bias', 'dn_norm_g',
           'attn_q_norm', 'attn_k_norm', 'mem_norm_g', 'w_mem_kv', 'w_branch', 'w_out', 'final_norm_g']
BIG = ['w_in', 'ssm_w_glu', 'w_mem_kv', 'w_branch', 'w_out']
SMALL = [n for n in WEIGHTS if n not in BIG]


def _pick(dim, cands):
    for c in cands:
        if dim % c == 0:
            return c
    return dim


def _widths(d_model):
    ssm_w = SSM_GROUPS * SSM_GROUP
    dn_w = DN_HEADS * DN_HEAD_DIM
    att_w = ATT_HEADS * ATT_HEAD_DIM
    kv_w = ATT_KV_HEADS * ATT_HEAD_DIM
    mem_w = MEM_HEADS * MEM_HEAD_DIM
    return [("u", ssm_w), ("z_a", ssm_w), ("dq", dn_w), ("dk", dn_w), ("dv", dn_w), ("dab", 4 * DN_HEADS),
            ("z_b", dn_w), ("aq", att_w), ("ak", kv_w), ("av", kv_w), ("z_c", att_w), ("mq", mem_w),
            ("z_m", mem_w), ("gates", N_BRANCH * d_model)]


def _layout(d_model):
    out, src, dst = {}, 0, 0
    for name, w in _widths(d_model):
        pw = -(-w // IN_PAD_UNIT) * IN_PAD_UNIT if w % LANES else w
        out[name] = (src, dst, w, pw)
        src += w
        dst += pw
    return out, src, dst


def _pad_w_in(full, d_model):
    lay, _, _ = _layout(d_model)
    parts = []
    for name, _ in _widths(d_model):
        s, _, w, pw = lay[name]
        seg = full[..., s:s + w]
        if pw != w:
            seg = jnp.pad(seg, [(0, 0)] * (full.ndim - 1) + [(0, pw - w)])
        parts.append(seg)
    return jnp.concatenate(parts, axis=-1)


def _unpad_w_in(padded, d_model):
    lay, _, _ = _layout(d_model)
    return jnp.concatenate([padded[..., lay[n][1]:lay[n][1] + lay[n][2]] for n, _ in _widths(d_model)], axis=-1)


def _mm(a, b, mode="nn", name="mm"):
    if mode == "nn":
        (m, k), (k2, n) = a.shape, b.shape
    elif mode == "nt":
        (m, k), (n, k2) = a.shape, b.shape
    else:
        (k, m), (k2, n) = a.shape, b.shape
    assert k == k2, (a.shape, b.shape, mode)
    tm = _pick(m, (1024, 512, 256, 128))
    tn = _pick(n, (1024, 512, 384, 256, 128))
    tk = _pick(k, (1024, 512, 256, 128))
    nk = k // tk
    dims = {"nn": (((1,), (0,)), ((), ())), "nt": (((1,), (1,)), ((), ())), "tn": (((0,), (0,)), ((), ()))}[mode]

    def body(a_ref, b_ref, o_ref, acc_ref):
        kk = pl.program_id(2)

        @pl.when(kk == 0)
        def _():
            acc_ref[...] = jnp.zeros_like(acc_ref)

        acc_ref[...] += lax.dot_general(a_ref[...].astype(BF16), b_ref[...].astype(BF16), dims,
                                        preferred_element_type=F32)

        @pl.when(kk == nk - 1)
        def _():
            o_ref[...] = acc_ref[...]

    a_spec = pl.BlockSpec((tk, tm), lambda i, j, kk: (kk, i)) if mode == "tn" else pl.BlockSpec((tm, tk), lambda i, j, kk: (i, kk))
    b_spec = pl.BlockSpec((tn, tk), lambda i, j, kk: (j, kk)) if mode == "nt" else pl.BlockSpec((tk, tn), lambda i, j, kk: (kk, j))
    return pl.pallas_call(
        body, name=name, out_shape=jax.ShapeDtypeStruct((m, n), F32), grid=(m // tm, n // tn, nk),
        in_specs=[a_spec, b_spec], out_specs=pl.BlockSpec((tm, tn), lambda i, j, kk: (i, j)),
        scratch_shapes=[pltpu.VMEM((tm, tn), F32)],
        compiler_params=pltpu.CompilerParams(dimension_semantics=("parallel", "parallel", "arbitrary"),
                                             vmem_limit_bytes=VMEM_LIMIT),
    )(a, b)


@jax.custom_vjp
def pmm(a, b):
    return _mm(a, b, "nn", "pmm_fwd")


def _pmm_fwd(a, b):
    return _mm(a, b, "nn", "pmm_fwd"), (a, b)


def _pmm_bwd(res, dc):
    a, b = res
    return _mm(dc, b, "nt", "pmm_da"), _mm(a, dc, "tn", "pmm_db")


pmm.defvjp(_pmm_fwd, _pmm_bwd)


_NT = (((1,), (1,)), ((), ()))
_NN = (((1,), (0,)), ((), ()))
_TN = (((0,), (0,)), ((), ()))


def _att_plan(lq, lk, hq, group):
    hp = min(ATT_HEADS_PER_STEP, hq)
    assert hq % hp == 0 and (hp % group == 0 or group % hp == 0)
    kvp = max(1, hp // group)
    return _pick(lq, (256, 128)), _pick(lk, (512, 256, 128)), hp, kvp


def _att_specs(bq, bk, dh, hp, kvp, group, q_map, k_map):
    qs = pl.BlockSpec((bq, hp * dh), lambda *g: q_map(*g))
    ls = pl.BlockSpec((bq, hp * LANES), lambda *g: q_map(*g))
    ks = pl.BlockSpec((bk, kvp * dh), lambda *g: (k_map(*g)[0], k_map(*g)[1] * hp // (group * kvp)))
    return qs, ls, ks


def _flash_fwd(q, k, v, dh, group, scale):
    lq, lk = q.shape[0], k.shape[0]
    hq = q.shape[1] // dh
    bq, bk, hp, kvp = _att_plan(lq, lk, hq, group)
    nkv = lk // bk

    def body(q_ref, k_ref, v_ref, o_ref, lse_ref, m_s, l_s, acc_s):
        j = pl.program_id(2)

        @pl.when(j == 0)
        def _():
            m_s[...] = jnp.full_like(m_s, -jnp.inf)
            l_s[...] = jnp.zeros_like(l_s)
            acc_s[...] = jnp.zeros_like(acc_s)

        for h in range(hp):
            hk = h // group if kvp > 1 else 0
            cols, lcols, kcols = slice(h * dh, (h + 1) * dh), slice(h * LANES, (h + 1) * LANES), slice(hk * dh, (hk + 1) * dh)
            s = lax.dot_general(q_ref[:, cols], k_ref[:, kcols], _NT, preferred_element_type=F32)
            if scale != 1.0:
                s = s * scale
            m_prev = m_s[:, lcols]
            m_cur = jnp.maximum(m_prev, jnp.max(s, axis=1, keepdims=True))
            alpha = jnp.exp(m_prev - m_cur)
            p = jnp.exp(s - m_cur[:, :1])
            l_s[:, lcols] = alpha * l_s[:, lcols] + jnp.sum(p, axis=1, keepdims=True)
            acc_s[:, cols] = acc_s[:, cols] * alpha[:, :1] + lax.dot_general(p.astype(BF16), v_ref[:, kcols], _NN,
                                                                           preferred_element_type=F32)
            m_s[:, lcols] = m_cur

        @pl.when(j == nkv - 1)
        def _():
            for h in range(hp):
                cols, lcols = slice(h * dh, (h + 1) * dh), slice(h * LANES, (h + 1) * LANES)
                o_ref[:, cols] = acc_s[:, cols] / l_s[:, lcols][:, :1]
                lse_ref[:, lcols] = m_s[:, lcols] + jnp.log(l_s[:, lcols])

    qs, ls, ks = _att_specs(bq, bk, dh, hp, kvp, group, lambda h, i, j: (i, h), lambda h, i, j: (j, h))
    return pl.pallas_call(
        body, name="flash_fwd",
        out_shape=(jax.ShapeDtypeStruct((lq, hq * dh), F32), jax.ShapeDtypeStruct((lq, hq * LANES), F32)),
        grid=(hq // hp, lq // bq, nkv), in_specs=[qs, ks, ks], out_specs=(qs, ls),
        scratch_shapes=[pltpu.VMEM((bq, hp * LANES), F32), pltpu.VMEM((bq, hp * LANES), F32), pltpu.VMEM((bq, hp * dh), F32)],
        compiler_params=pltpu.CompilerParams(dimension_semantics=("parallel", "parallel", "arbitrary"),
                                             vmem_limit_bytes=VMEM_LIMIT),
    )(q, k, v)


def _flash_dq(q, k, v, o, lse, do, dh, group, scale):
    lq, lk = q.shape[0], k.shape[0]
    hq = q.shape[1] // dh
    bq, bk, hp, kvp = _att_plan(lq, lk, hq, group)
    nkv = lk // bk

    def body(q_ref, k_ref, v_ref, o_ref, lse_ref, do_ref, dq_ref, dob_ref, dl_ref, acc_s):
        j = pl.program_id(2)

        @pl.when(j == 0)
        def _():
            acc_s[...] = jnp.zeros_like(acc_s)
            dob_ref[...] = do_ref[...].astype(BF16)
            for h in range(hp):
                cols, lcols = slice(h * dh, (h + 1) * dh), slice(h * LANES, (h + 1) * LANES)
                dl_ref[:, lcols] = jnp.broadcast_to(jnp.sum(do_ref[:, cols] * o_ref[:, cols], axis=1, keepdims=True),
                                                    (bq, LANES))

        for h in range(hp):
            hk = h // group if kvp > 1 else 0
            cols, lcols, kcols = slice(h * dh, (h + 1) * dh), slice(h * LANES, (h + 1) * LANES), slice(hk * dh, (hk + 1) * dh)
            kb = k_ref[:, kcols]
            s = lax.dot_general(q_ref[:, cols], kb, _NT, preferred_element_type=F32)
            if scale != 1.0:
                s = s * scale
            p = jnp.exp(s - lse_ref[:, lcols][:, :1])
            dp = lax.dot_general(dob_ref[:, cols], v_ref[:, kcols], _NT, preferred_element_type=F32)
            ds = p * (dp - dl_ref[:, lcols][:, :1])
            if scale != 1.0:
                ds = ds * scale
            acc_s[:, cols] += lax.dot_general(ds.astype(BF16), kb, _NN, preferred_element_type=F32)

        @pl.when(j == nkv - 1)
        def _():
            dq_ref[...] = acc_s[...]

    qs, ls, ks = _att_specs(bq, bk, dh, hp, kvp, group, lambda h, i, j: (i, h), lambda h, i, j: (j, h))
    return pl.pallas_call(
        body, name="flash_dq",
        out_shape=(jax.ShapeDtypeStruct((lq, hq * dh), F32), jax.ShapeDtypeStruct((lq, hq * dh), BF16),
                   jax.ShapeDtypeStruct((lq, hq * LANES), F32)),
        grid=(hq // hp, lq // bq, nkv), in_specs=[qs, ks, ks, qs, ls, qs], out_specs=(qs, qs, ls),
        scratch_shapes=[pltpu.VMEM((bq, hp * dh), F32)],
        compiler_params=pltpu.CompilerParams(dimension_semantics=("parallel", "parallel", "arbitrary"),
                                             vmem_limit_bytes=VMEM_LIMIT),
    )(q, k, v, o, lse, do)


def _flash_dkv(q, k, v, lse, dob, delta, dh, group, scale):
    lq, lk = q.shape[0], k.shape[0]
    hq = q.shape[1] // dh
    bq, bk, hp, kvp = _att_plan(lq, lk, hq, group)
    nq = lq // bq
    reps = max(1, group // hp)
    nt = reps * nq

    def body(q_ref, k_ref, v_ref, lse_ref, dob_ref, dl_ref, dk_ref, dv_ref, dk_s, dv_s):
        t = pl.program_id(2)

        @pl.when(t == 0)
        def _():
            dk_s[...] = jnp.zeros_like(dk_s)
            dv_s[...] = jnp.zeros_like(dv_s)

        for h in range(hp):
            hk = h // group if kvp > 1 else 0
            cols, lcols, kcols = slice(h * dh, (h + 1) * dh), slice(h * LANES, (h + 1) * LANES), slice(hk * dh, (hk + 1) * dh)
            qb = q_ref[:, cols]
            dob = dob_ref[:, cols]
            s = lax.dot_general(qb, k_ref[:, kcols], _NT, preferred_element_type=F32)
            if scale != 1.0:
                s = s * scale
            p = jnp.exp(s - lse_ref[:, lcols][:, :1])
            dp = lax.dot_general(dob, v_ref[:, kcols], _NT, preferred_element_type=F32)
            ds = p * (dp - dl_ref[:, lcols][:, :1])
            if scale != 1.0:
                ds = ds * scale
            dv_s[:, kcols] += lax.dot_general(p.astype(BF16), dob, _TN, preferred_element_type=F32)
            dk_s[:, kcols] += lax.dot_general(ds.astype(BF16), qb, _TN, preferred_element_type=F32)

        @pl.when(t == nt - 1)
        def _():
            dk_ref[...] = dk_s[...]
            dv_ref[...] = dv_s[...]

    hkv_blocks = (hq // group) // kvp
    qs = pl.BlockSpec((bq, hp * dh), lambda h, j, t: (t % nq, h * reps + t // nq))
    ls = pl.BlockSpec((bq, hp * LANES), lambda h, j, t: (t % nq, h * reps + t // nq))
    ks = pl.BlockSpec((bk, kvp * dh), lambda h, j, t: (j, h))
    return pl.pallas_call(
        body, name="flash_dkv",
        out_shape=(jax.ShapeDtypeStruct(k.shape, F32), jax.ShapeDtypeStruct(v.shape, F32)),
        grid=(hkv_blocks, lk // bk, nt), in_specs=[qs, ks, ks, ls, qs, ls], out_specs=(ks, ks),
        scratch_shapes=[pltpu.VMEM((bk, kvp * dh), F32), pltpu.VMEM((bk, kvp * dh), F32)],
        compiler_params=pltpu.CompilerParams(dimension_semantics=("parallel", "parallel", "arbitrary"),
                                             vmem_limit_bytes=VMEM_LIMIT),
    )(q, k, v, lse, dob, delta)


@functools.partial(jax.custom_vjp, nondiff_argnums=(3, 4, 5))
def flash(q, k, v, dh, group, scale):
    return _flash_fwd(q.astype(BF16), k.astype(BF16), v.astype(BF16), dh, group, scale)[0]


def _flash_vfwd(q, k, v, dh, group, scale):
    qb, kb, vb = q.astype(BF16), k.astype(BF16), v.astype(BF16)
    o, lse = _flash_fwd(qb, kb, vb, dh, group, scale)
    return o, (qb, kb, vb, o, lse)


def _flash_vbwd(dh, group, scale, res, do):
    q, k, v, o, lse = res
    dq, dob, delta = _flash_dq(q, k, v, o, lse, do, dh, group, scale)
    dk, dv = _flash_dkv(q, k, v, lse, dob, delta, dh, group, scale)
    return dq, dk, dv


flash.defvjp(_flash_vfwd, _flash_vbwd)


def _scan_call(lr, li, br, bi, prev_r, prev_i, reverse, name):
    rows, width = br.shape
    ts = _pick(width, (1024, 512, 256, 128))
    tr = _pick(rows, (256, 128, 64, 32, 16, 8))
    nb = rows // tr
    with_prev = prev_r is not None

    def body(*refs):
        if with_prev:
            lr_ref, li_ref, br_ref, bi_ref, pr_ref, pi_ref, sr_ref, si_ref, dr_ref, di_ref, cr_s, ci_s = refs
        else:
            lr_ref, li_ref, br_ref, bi_ref, sr_ref, si_ref, pr_ref, pi_ref, cr_s, ci_s = refs
        i = pl.program_id(1)

        @pl.when(i == 0)
        def _():
            cr_s[...] = jnp.zeros_like(cr_s)
            ci_s[...] = jnp.zeros_like(ci_s)
            if with_prev:
                dr_ref[...] = jnp.zeros_like(dr_ref)
                di_ref[...] = jnp.zeros_like(di_ref)

        a_re = lr_ref[...]
        a_im = li_ref[...]

        def step(r, carry):
            if with_prev:
                c_re, c_im, d_re, d_im = carry
            else:
                c_re, c_im = carry
            t = tr - 1 - r if reverse else r
            row = pl.ds(t, 1)
            n_re = a_re * c_re - a_im * c_im + br_ref[row, :]
            n_im = a_re * c_im + a_im * c_re + bi_ref[row, :]
            sr_ref[row, :] = n_re
            si_ref[row, :] = n_im
            if with_prev:
                p_re = pr_ref[row, :]
                p_im = pi_ref[row, :]
                return n_re, n_im, d_re + n_re * p_re + n_im * p_im, d_im + n_im * p_re - n_re * p_im
            pr_ref[row, :] = c_re
            pi_ref[row, :] = c_im
            return n_re, n_im

        zero = jnp.zeros((1, ts), F32)
        init = (cr_s[...], ci_s[...]) + ((zero, zero) if with_prev else ())
        out = lax.fori_loop(0, tr, step, init, unroll=8)
        cr_s[...] = out[0]
        ci_s[...] = out[1]
        if with_prev:
            dr_ref[...] += out[2]
            di_ref[...] += out[3]

    blk = (lambda j, i: (nb - 1 - i, j)) if reverse else (lambda j, i: (i, j))
    row_spec = pl.BlockSpec((tr, ts), blk)
    vec_spec = pl.BlockSpec((1, ts), lambda j, i: (0, j))
    full = jax.ShapeDtypeStruct((rows, width), F32)
    vec = jax.ShapeDtypeStruct((1, width), F32)
    ins = [lr, li, br, bi] + ([prev_r, prev_i] if with_prev else [])
    return pl.pallas_call(
        body, name=name, grid=(width // ts, nb),
        in_specs=[vec_spec, vec_spec] + [row_spec] * (len(ins) - 2),
        out_shape=(full, full, vec, vec) if with_prev else (full, full, full, full),
        out_specs=(row_spec, row_spec, vec_spec, vec_spec) if with_prev else (row_spec,) * 4,
        scratch_shapes=[pltpu.VMEM((1, ts), F32), pltpu.VMEM((1, ts), F32)],
        compiler_params=pltpu.CompilerParams(dimension_semantics=("parallel", "arbitrary"), vmem_limit_bytes=VMEM_LIMIT),
    )(*ins)


@functools.partial(jax.custom_vjp, nondiff_argnums=(4,))
def cscan(lr, li, br, bi, reverse):
    out = _scan_call(lr, li, br, bi, None, None, reverse, "s5_scan")
    return out[0], out[1]


def _cscan_fwd(lr, li, br, bi, reverse):
    sr, si, pr, pi = _scan_call(lr, li, br, bi, None, None, reverse, "s5_scan")
    return (sr, si), (lr, li, pr, pi)


def _cscan_bwd(reverse, res, g):
    lr, li, pr, pi = res
    ar, ai, dlr, dli = _scan_call(lr, -li, g[0], g[1], pr, pi, not reverse, "s5_scan_bwd")
    return dlr, dli, ar, ai


cscan.defvjp(_cscan_fwd, _cscan_bwd)


_BDIMS = {"nn": (((2,), (1,)), ((0,), (0,))), "nt": (((2,), (2,)), ((0,), (0,))), "tn": (((1,), (1,)), ((0,), (0,)))}


def _bdot_raw(a, b, mode):
    return lax.dot_general(a.astype(BF16), b.astype(BF16), _BDIMS[mode], preferred_element_type=F32)


@functools.partial(jax.custom_vjp, nondiff_argnums=(2,))
def _bdot(a, b, mode):
    return _bdot_raw(a, b, mode)


def _bdot_fwd(a, b, mode):
    return _bdot_raw(a, b, mode), (a, b)


def _bdot_bwd(mode, res, dc):
    a, b = res
    if mode == "nn":
        return _bdot_raw(dc, b, "nt"), _bdot_raw(a, dc, "tn")
    if mode == "nt":
        return _bdot_raw(dc, b, "nn"), _bdot_raw(dc, a, "tn")
    return _bdot_raw(b, dc, "nt"), _bdot_raw(a, dc, "nn")


_bdot.defvjp(_bdot_fwd, _bdot_bwd)


def _fdot(a, b):
    return lax.dot_general(a, b, _BDIMS["nn"], precision=lax.Precision.HIGHEST, preferred_element_type=F32)


def _dn_chunk(state, q, k, v, bb, gc, gc64, gr, direction):
    c = q.shape[1]
    row = lax.broadcasted_iota(jnp.int32, (c, c), 0)
    col = lax.broadcasted_iota(jnp.int32, (c, c), 1)
    sgn = 1 - 2 * direction
    after = ((row - col) * sgn > 0)[None]
    incl = ((row - col) * sgn >= 0)[None]
    eye = (row == col).astype(F32)[None]
    kb = k * bb
    decay = jnp.where(incl, jnp.exp(jnp.where(incl, gc64 - gr, 0.0)), 0.0)
    lower = jnp.where(after, _bdot(kb, k, "nt") * decay, 0.0)
    inv = eye - lower
    power = _fdot(lower, lower)
    span = 2
    while span < c:
        inv = inv + _fdot(inv, power)
        span *= 2
        if span < c:
            power = _fdot(power, power)
    eg = jnp.exp(gc)
    u = _fdot(inv, v * bb)
    w = _fdot(inv, kb * eg)
    intra = _bdot(q, k, "nt") * decay
    last = (lax.broadcasted_iota(jnp.int32, (c, 1), 0) == (c - 1) * (1 - direction)).astype(F32)[None]
    g_last = jnp.sum(gc * last, axis=1, keepdims=True)
    k_dec = k * jnp.exp(g_last - gc)
    v_new = u - _bdot(w, state, "nn")
    o = _bdot(q * eg, state, "nn") + _bdot(intra, v_new, "nn")
    new_state = state * jnp.exp(g_last) + _bdot(k_dec, v_new, "tn")
    return new_state, o


def _dn_rows(lq):
    return _pick(lq, (256, 128, 64))


def _dn_fwd_call(q, k, v, bb, gc, gc64, gr):
    h, lq, d = q.shape
    c = DN_CHUNK
    t = _dn_rows(lq)
    nb, nc = lq // t, t // c

    def body(q_ref, k_ref, v_ref, bb_ref, gc_ref, g64_ref, gr_ref, o_ref, st_ref, s_s):
        direction = pl.program_id(0)

        @pl.when(pl.program_id(1) == 0)
        def _():
            s_s[...] = jnp.zeros_like(s_s)

        def step(j, state):
            ci = j + direction * (nc - 1 - 2 * j)
            rows = pl.ds(pl.multiple_of(ci * c, c), c)
            st_ref[0, ci] = state
            state, o = _dn_chunk(state, q_ref[:, rows, :], k_ref[:, rows, :], v_ref[:, rows, :], bb_ref[0, :, rows, :],
                                 gc_ref[0, :, rows, :], g64_ref[0, :, rows, :], gr_ref[0, :, ci], direction)
            o_ref[0, :, rows, :] = o
            return state

        s_s[...] = lax.fori_loop(0, nc, step, s_s[...])

    blk = lambda dd, i: i + dd * (nb - 1 - 2 * i)
    qs = pl.BlockSpec((h, t, d), lambda dd, i: (0, blk(dd, i), 0))
    ds = pl.BlockSpec((1, h, t, d), lambda dd, i: (dd, 0, blk(dd, i), 0))
    return pl.pallas_call(
        body, name="deltanet_fwd", grid=(2, nb),
        in_specs=[qs, qs, qs, ds, ds, pl.BlockSpec((1, h, t, c), lambda dd, i: (dd, 0, blk(dd, i), 0)),
                  pl.BlockSpec((1, h, nc, 1, c), lambda dd, i: (dd, 0, blk(dd, i), 0, 0))],
        out_shape=(jax.ShapeDtypeStruct((2, h, lq, d), F32), jax.ShapeDtypeStruct((2, lq // c, h, d, d), F32)),
        out_specs=(ds, pl.BlockSpec((1, nc, h, d, d), lambda dd, i: (dd, blk(dd, i), 0, 0, 0))),
        scratch_shapes=[pltpu.VMEM((h, d, d), F32)],
        compiler_params=pltpu.CompilerParams(dimension_semantics=("parallel", "arbitrary"), vmem_limit_bytes=VMEM_LIMIT),
    )(q, k, v, bb, gc, gc64, gr)


def _dn_bwd_call(q, k, v, bb, gc, gc64, gr, states, do):
    h, lq, d = q.shape
    c = DN_CHUNK
    t = _dn_rows(lq)
    nb, nc = lq // t, t // c

    def body(q_ref, k_ref, v_ref, bb_ref, gc_ref, g64_ref, gr_ref, st_ref, do_ref,
             dq_ref, dk_ref, dv_ref, dbb_ref, dgc_ref, dg64_ref, dgr_ref, ds_s):
        direction = pl.program_id(0)

        @pl.when(pl.program_id(1) == 0)
        def _():
            ds_s[...] = jnp.zeros_like(ds_s)

        def step(j, dstate):
            ci = (nc - 1 - j) + direction * (2 * j - (nc - 1))
            rows = pl.ds(pl.multiple_of(ci * c, c), c)
            args = (st_ref[0, ci], q_ref[:, rows, :], k_ref[:, rows, :], v_ref[:, rows, :], bb_ref[0, :, rows, :],
                    gc_ref[0, :, rows, :], g64_ref[0, :, rows, :], gr_ref[0, :, ci])
            _, pull = jax.vjp(lambda *a: _dn_chunk(*a, direction), *args)
            dstate, dq, dk, dv, dbb, dgc, dg64, dgr = pull((dstate, do_ref[0, :, rows, :]))
            dq_ref[0, :, rows, :] = dq
            dk_ref[0, :, rows, :] = dk
            dv_ref[0, :, rows, :] = dv
            dbb_ref[0, :, rows, :] = dbb
            dgc_ref[0, :, rows, :] = dgc
            dg64_ref[0, :, rows, :] = dg64
            dgr_ref[0, :, ci] = dgr
            return dstate

        ds_s[...] = lax.fori_loop(0, nc, step, ds_s[...])

    blk = lambda dd, i: (nb - 1 - i) + dd * (2 * i - (nb - 1))
    qs = pl.BlockSpec((h, t, d), lambda dd, i: (0, blk(dd, i), 0))
    ds = pl.BlockSpec((1, h, t, d), lambda dd, i: (dd, 0, blk(dd, i), 0))
    g64s = pl.BlockSpec((1, h, t, c), lambda dd, i: (dd, 0, blk(dd, i), 0))
    grs = pl.BlockSpec((1, h, nc, 1, c), lambda dd, i: (dd, 0, blk(dd, i), 0, 0))
    big = jax.ShapeDtypeStruct((2, h, lq, d), F32)
    return pl.pallas_call(
        body, name="deltanet_bwd", grid=(2, nb),
        in_specs=[qs, qs, qs, ds, ds, g64s, grs, pl.BlockSpec((1, nc, h, d, d), lambda dd, i: (dd, blk(dd, i), 0, 0, 0)), ds],
        out_shape=(big, big, big, big, big, jax.ShapeDtypeStruct((2, h, lq, c), F32),
                   jax.ShapeDtypeStruct((2, h, lq // c, 1, c), F32)),
        out_specs=(ds, ds, ds, ds, ds, g64s, grs),
        scratch_shapes=[pltpu.VMEM((h, d, d), F32)],
        compiler_params=pltpu.CompilerParams(dimension_semantics=("parallel", "arbitrary"), vmem_limit_bytes=VMEM_LIMIT),
    )(q, k, v, bb, gc, gc64, gr, states, do)


@jax.custom_vjp
def dn_rule(q, k, v, bb, gc, gc64, gr):
    return _dn_fwd_call(q, k, v, bb, gc, gc64, gr)[0]


def _dn_rule_fwd(q, k, v, bb, gc, gc64, gr):
    o, states = _dn_fwd_call(q, k, v, bb, gc, gc64, gr)
    return o, (q, k, v, bb, gc, gc64, gr, states)


def _dn_rule_bwd(res, do):
    dq, dk, dv, dbb, dgc, dg64, dgr = _dn_bwd_call(*res, do)
    return dq[0] + dq[1], dk[0] + dk[1], dv[0] + dv[1], dbb, dgc, dg64, dgr


dn_rule.defvjp(_dn_rule_fwd, _dn_rule_bwd)


def gated_delta_both(q, k, v, beta2, g2):
    h, lq, d = q.shape
    c = DN_CHUNK
    gch = g2.reshape(2, h, lq // c, c)
    cum = jnp.stack([jnp.cumsum(gch[0], axis=-1), lax.cumsum(gch[1], axis=2, reverse=True)])
    flat = cum.reshape(2, h, lq)
    o2 = dn_rule(q, k, v, jnp.broadcast_to(beta2[..., None], (2, h, lq, d)), jnp.broadcast_to(flat[..., None], (2, h, lq, d)),
                 jnp.broadcast_to(flat[..., None], (2, h, lq, c)), cum.reshape(2, h, lq // c, 1, c))
    return o2[0] + o2[1]


def _rmsnorm(x, g):
    return x * lax.rsqrt(jnp.mean(x * x, axis=-1, keepdims=True) + EPS) * g


def _l2norm(x):
    return x * lax.rsqrt(jnp.sum(x * x, axis=-1, keepdims=True) + EPS)


def _block_diag(w):
    g, a, b = w.shape
    eye = jnp.eye(g, dtype=w.dtype)
    return (w[:, :, None, :] * eye[:, None, :, None]).reshape(g * a, g * b)


def _s5_mixer(u, a_re, a_im, log_step, b_re, b_im, c_re, c_im, d, w_glu, b_glu):
    y = u * d
    for direction in range(2):
        are, aim = a_re[direction], a_im[direction]
        step = jnp.exp(log_step[direction])[:, None]
        mag = jnp.exp(are * step)
        lam_re = mag * jnp.cos(aim * step)
        lam_im = mag * jnp.sin(aim * step)
        den = are * are + aim * aim
        nr = lam_re - 1.0
        ni = lam_im
        coef_re = (nr * are + ni * aim) / den
        coef_im = (ni * are - nr * aim) / den
        bb_re = coef_re[..., None] * b_re[direction] - coef_im[..., None] * b_im[direction]
        bb_im = coef_re[..., None] * b_im[direction] + coef_im[..., None] * b_re[direction]
        w_bu = jnp.concatenate([_block_diag(jnp.swapaxes(bb_re, 1, 2)), _block_diag(jnp.swapaxes(bb_im, 1, 2))], axis=1)
        bu = pmm(u, w_bu)
        width = bu.shape[1] // 2
        s_re, s_im = cscan(lam_re.reshape(1, width), lam_im.reshape(1, width), bu[:, :width], bu[:, width:],
                           direction == 1)
        y = y + pmm(s_re, _block_diag(jnp.swapaxes(c_re[direction], 1, 2)))
        y = y - pmm(s_im, _block_diag(jnp.swapaxes(c_im[direction], 1, 2)))
    y = jax.nn.gelu(y)
    return y * jax.nn.sigmoid(pmm(y, w_glu) + b_glu)


def _short_conv(x, w):
    ch = x.shape[-1]
    rhs = jnp.transpose(w)[:, None, :]
    return lax.conv_general_dilated(x[None], rhs, window_strides=(1,), padding=[(DN_CONV // 2, DN_CONV // 2)],
                                    dimension_numbers=("NWC", "WIO", "NWC"), feature_group_count=ch)[0]


def _deltanet_mixer(q, k, v, ab, conv_w, a_log, dt_bias, norm_g):
    seq = q.shape[0]
    qkv = jax.nn.silu(_short_conv(jnp.concatenate([q, k, v], axis=-1), conv_w))
    qkv = qkv.reshape(seq, 3, DN_HEADS, DN_HEAD_DIM)
    qh = jnp.transpose(_l2norm(qkv[:, 0]) * (DN_HEAD_DIM ** -0.5), (1, 0, 2))
    kh = jnp.transpose(_l2norm(qkv[:, 1]), (1, 0, 2))
    vh = jnp.transpose(qkv[:, 2], (1, 0, 2))
    a4 = ab[:, :2 * DN_HEADS].reshape(seq, 2, DN_HEADS)
    b4 = ab[:, 2 * DN_HEADS:4 * DN_HEADS].reshape(seq, 2, DN_HEADS)
    beta = jax.nn.sigmoid(b4)
    g = -jnp.exp(a_log) * jax.nn.softplus(a4 + dt_bias)
    o = jnp.transpose(gated_delta_both(qh, kh, vh, jnp.transpose(beta, (1, 2, 0)), jnp.transpose(g, (1, 2, 0))), (1, 0, 2))
    return _rmsnorm(o, norm_g).reshape(seq, DN_HEADS * DN_HEAD_DIM)


def _axial_rope(seq):
    rows = seq // GRID_W
    row = jnp.repeat(jnp.arange(rows), GRID_W).astype(F32)
    col = jnp.tile(jnp.arange(GRID_W), rows).astype(F32)
    axis_dim = ATT_HEAD_DIM // 2
    freqs = ROPE_THETA ** (-jnp.arange(0, axis_dim, 2, dtype=F32) / axis_dim)
    ang = jnp.concatenate([row[:, None] * freqs, col[:, None] * freqs], axis=-1)
    return jnp.cos(ang), jnp.sin(ang)


def _apply_rope(x, cos, sin):
    xp = x.reshape(x.shape[:-1] + (x.shape[-1] // 2, 2))
    x0, x1 = xp[..., 0], xp[..., 1]
    c = cos[:, None, :]
    s = sin[:, None, :]
    return jnp.stack([x0 * c - x1 * s, x0 * s + x1 * c], axis=-1).reshape(x.shape)


def _grid_attention(q, k, v, qn_g, kn_g, cos, sin):
    seq = q.shape[0]
    qh = _rmsnorm(q.reshape(seq, ATT_HEADS, ATT_HEAD_DIM), qn_g)
    kh = _rmsnorm(k.reshape(seq, ATT_KV_HEADS, ATT_HEAD_DIM), kn_g)
    qh = _apply_rope(qh, cos, sin) * (ATT_HEAD_DIM ** -0.5)
    kh = _apply_rope(kh, cos, sin)
    return flash(qh.reshape(seq, -1), kh.reshape(seq, -1), v, ATT_HEAD_DIM, ATT_HEADS // ATT_KV_HEADS, 1.0)


def _memory_attention(q, mem_n, w_kv):
    kv = pmm(mem_n, w_kv)
    mem_w = MEM_HEADS * MEM_HEAD_DIM
    return flash(q, kv[:, :mem_w], kv[:, mem_w:], MEM_HEAD_DIM, 1, MEM_HEAD_DIM ** -0.5)


def _middle(h, x, mem, p):
    d_model = x.shape[1]
    lay, _, _ = _layout(d_model)

    def seg(name):
        _, dst, w, _ = lay[name]
        return h[:, dst:dst + w]

    cos, sin = _axial_rope(x.shape[0])
    y_a = _s5_mixer(seg("u"), p["ssm_a_re"], p["ssm_a_im"], p["ssm_log_step"], p["ssm_b_re"], p["ssm_b_im"],
                    p["ssm_c_re"], p["ssm_c_im"], p["ssm_d"], p["ssm_w_glu"], p["ssm_b_glu"]) * jax.nn.silu(seg("z_a"))
    y_b = _deltanet_mixer(seg("dq"), seg("dk"), seg("dv"), seg("dab"), p["dn_conv"], p["dn_a_log"],
                          p["dn_dt_bias"], p["dn_norm_g"]) * jax.nn.silu(seg("z_b"))
    y_c = _grid_attention(seg("aq"), seg("ak"), seg("av"), p["attn_q_norm"], p["attn_k_norm"], cos, sin) * jax.nn.silu(seg("z_c"))
    y_m = _memory_attention(seg("mq"), _rmsnorm(mem, p["mem_norm_g"]), p["w_mem_kv"]) * jax.nn.silu(seg("z_m"))
    gates = jax.nn.sigmoid(seg("gates"))
    merged = jnp.zeros_like(x)
    off = 0
    for bi, y_br in enumerate((y_a, y_b, y_c, y_m)):
        w = y_br.shape[1]
        merged = merged + gates[:, bi * d_model:(bi + 1) * d_model] * pmm(y_br, p["w_branch"][off:off + w])
        off += w
    return x + pmm(merged, p["w_out"])


def _final_loss(x, g, target):
    y = _rmsnorm(x, g)
    return 0.5 * jnp.sum(jnp.mean(jnp.square(y - target), axis=-1))


def _local_step(x, mem, target, w):
    depth = w["norm_g"].shape[0]
    d_model = x.shape[1]
    mid_names = [n for n in WEIGHTS if n not in ("norm_g", "w_in", "final_norm_g")]
    saved = []
    cur = x
    for layer in range(depth):
        xn, vjp_norm = jax.vjp(_rmsnorm, cur, w["norm_g"][layer])
        h = _mm(xn, w["w_in"][layer], "nn", "in_proj")
        p = {n: w[n][layer] for n in mid_names}
        nxt, vjp_mid = jax.vjp(lambda h_, x_, p_: _middle(h_, x_, mem, p_), h, cur, p)
        saved.append((xn, vjp_norm, vjp_mid))
        cur = nxt
    loss, vjp_fin = jax.vjp(lambda x_, g_: _final_loss(x_, g_, target), cur, w["final_norm_g"])
    dx, d_final = vjp_fin(jnp.ones((), F32))
    grads = {n: [None] * depth for n in WEIGHTS if n != "final_norm_g"}
    for layer in reversed(range(depth)):
        xn, vjp_norm, vjp_mid = saved[layer]
        dh, dx_skip, dp = vjp_mid(dx)
        grads["w_in"][layer] = _unpad_w_in(_mm(xn, dh, "tn", "in_proj_dw"), d_model)
        dxn = _mm(dh, w["w_in"][layer], "nt", "in_proj_dx")
        dx_n, dg = vjp_norm(dxn)
        dx = dx_skip + dx_n
        grads["norm_g"][layer] = dg
        for n in mid_names:
            grads[n][layer] = dp[n]
    out = {n: jnp.stack(v) for n, v in grads.items()}
    out["final_norm_g"] = d_final
    return loss, dx, out


_ANY = pl.BlockSpec(memory_space=pl.ANY)
_CHIP_FLIPS = [(1, 0), (0, 1), (1, 1)]


def _remote(src, dst, send_sem, recv_sem, dev):
    return pltpu.make_async_remote_copy(src_ref=src, dst_ref=dst, send_sem=send_sem, recv_sem=recv_sem,
                                        device_id=dev, device_id_type=pl.DeviceIdType.MESH)


def _flip(v, bit):
    return 1 - v if bit else v


def _gather_chips(shards):
    n = len(shards)

    def body(*refs):
        ins, outs = refs[:n], refs[n:2 * n]
        send_sems, recv_sems, local_sems = refs[2 * n:]
        x, y, c = lax.axis_index("x"), lax.axis_index("y"), lax.axis_index("c")
        locals_, sends = [], []
        for a in range(n):
            cp = pltpu.make_async_copy(ins[a], outs[a].at[2 * x + y], local_sems.at[a])
            cp.start()
            locals_.append(cp)

        def half(a, px, py, pc):
            rh = shards[a].shape[1] // 2
            return outs[a].at[2 * px + py, :, pl.ds(pc * rh, rh), :]

        for a in range(n):
            rh = shards[a].shape[1] // 2
            for j, (fx, fy) in enumerate(_CHIP_FLIPS):
                cp = _remote(ins[a].at[:, pl.ds(c * rh, rh), :], half(a, x, y, c), send_sems.at[6 * a + j],
                             recv_sems.at[6 * a + j], (_flip(x, fx), _flip(y, fy), c))
                cp.start()
                sends.append(cp)
        for a in range(n):
            for j, (fx, fy) in enumerate(_CHIP_FLIPS):
                blk = half(a, _flip(x, fx), _flip(y, fy), c)
                _remote(blk, blk, send_sems.at[6 * a + j], recv_sems.at[6 * a + j], (x, y, c)).wait_recv()
                cp = _remote(blk, blk, send_sems.at[6 * a + 3 + j], recv_sems.at[6 * a + 3 + j], (x, y, 1 - c))
                cp.start()
                sends.append(cp)
        for a in range(n):
            for j, (fx, fy) in enumerate(_CHIP_FLIPS):
                blk = half(a, _flip(x, fx), _flip(y, fy), 1 - c)
                _remote(blk, blk, send_sems.at[6 * a + 3 + j], recv_sems.at[6 * a + 3 + j], (x, y, c)).wait_recv()
        for cp in sends:
            cp.wait_send()
        for cp in locals_:
            cp.wait()

    return pl.pallas_call(
        body, name="gather_weights",
        out_shape=tuple(jax.ShapeDtypeStruct((N_CHIPS,) + s.shape, s.dtype) for s in shards),
        in_specs=[_ANY] * n, out_specs=tuple([_ANY] * n),
        scratch_shapes=[pltpu.SemaphoreType.DMA((6 * n,)), pltpu.SemaphoreType.DMA((6 * n,)),
                        pltpu.SemaphoreType.DMA((n,))],
    )(*shards)


def _exchange_pair(parts):
    n = len(parts)

    def body(*refs):
        ins, outs = refs[:n], refs[n:2 * n]
        send_sems, recv_sems = refs[2 * n:]
        x, y, c = lax.axis_index("x"), lax.axis_index("y"), lax.axis_index("c")
        started = []
        for a in range(n):
            rh = parts[a].shape[2] // 2
            cp = _remote(ins[a].at[:, :, pl.ds((1 - c) * rh, rh), :], outs[a], send_sems.at[a], recv_sems.at[a], (x, y, 1 - c))
            cp.start()
            started.append(cp)
        for a in range(n):
            _remote(outs[a], outs[a], send_sems.at[a], recv_sems.at[a], (x, y, c)).wait_recv()
        for cp in started:
            cp.wait_send()

    return pl.pallas_call(
        body, name="exchange_pair",
        out_shape=tuple(jax.ShapeDtypeStruct((p.shape[0], p.shape[1], p.shape[2] // 2, p.shape[3]), p.dtype) for p in parts),
        in_specs=[_ANY] * n, out_specs=tuple([_ANY] * n),
        scratch_shapes=[pltpu.SemaphoreType.DMA((n,)), pltpu.SemaphoreType.DMA((n,))],
    )(*parts)


def _exchange_chips(pair_sums, small):
    n = len(pair_sums)

    def body(*refs):
        ins, small_in = refs[:n], refs[n]
        outs, small_out = refs[n + 1:2 * n + 1], refs[2 * n + 1]
        send_sems, recv_sems, small_send, small_recv, local_sem = refs[2 * n + 2:]
        x, y, c = lax.axis_index("x"), lax.axis_index("y"), lax.axis_index("c")
        me = 4 * x + 2 * y + c
        started = []
        own = pltpu.make_async_copy(small_in, small_out.at[me], local_sem)
        own.start()
        for f, (fx, fy) in enumerate(_CHIP_FLIPS):
            px, py = _flip(x, fx), _flip(y, fy)
            for a in range(n):
                cp = _remote(ins[a].at[2 * px + py], outs[a].at[f], send_sems.at[3 * a + f], recv_sems.at[3 * a + f], (px, py, c))
                cp.start()
                started.append(cp)
        for k in range(1, N_DEV):
            peer = (_flip(x, k >> 2 & 1), _flip(y, k >> 1 & 1), _flip(c, k & 1))
            cp = _remote(small_in, small_out.at[me], small_send.at[k - 1], small_recv.at[k - 1], peer)
            cp.start()
            started.append(cp)
        for f in range(len(_CHIP_FLIPS)):
            for a in range(n):
                _remote(outs[a].at[f], outs[a].at[f], send_sems.at[3 * a + f], recv_sems.at[3 * a + f], (x, y, c)).wait_recv()
        for k in range(1, N_DEV):
            peer = 4 * _flip(x, k >> 2 & 1) + 2 * _flip(y, k >> 1 & 1) + _flip(c, k & 1)
            _remote(small_out.at[peer], small_out.at[peer], small_send.at[k - 1], small_recv.at[k - 1], (x, y, c)).wait_recv()
        for cp in started:
            cp.wait_send()
        own.wait()

    shapes = tuple(jax.ShapeDtypeStruct((3,) + p.shape[1:], p.dtype) for p in pair_sums)
    shapes += (jax.ShapeDtypeStruct((N_DEV,) + small.shape, small.dtype),)
    return pl.pallas_call(
        body, name="exchange_chips", out_shape=shapes, in_specs=[_ANY] * (n + 1), out_specs=tuple([_ANY] * (n + 1)),
        scratch_shapes=[pltpu.SemaphoreType.DMA((3 * n,)), pltpu.SemaphoreType.DMA((3 * n,)),
                        pltpu.SemaphoreType.DMA((N_DEV - 1,)), pltpu.SemaphoreType.DMA((N_DEV - 1,)),
                        pltpu.SemaphoreType.DMA(())],
    )(*pair_sums, small)


def _swap_halves(arrays):
    n = len(arrays)

    def body(*refs):
        outs = refs[n:2 * n]
        send_sems, recv_sems = refs[2 * n:]
        x, y, c = lax.axis_index("x"), lax.axis_index("y"), lax.axis_index("c")
        started = []
        for a in range(n):
            rh = arrays[a].shape[1] // 2
            mine = outs[a].at[:, pl.ds(c * rh, rh), :]
            cp = _remote(mine, mine, send_sems.at[a], recv_sems.at[a], (x, y, 1 - c))
            cp.start()
            started.append(cp)
        for a in range(n):
            rh = arrays[a].shape[1] // 2
            theirs = outs[a].at[:, pl.ds((1 - c) * rh, rh), :]
            _remote(theirs, theirs, send_sems.at[a], recv_sems.at[a], (x, y, c)).wait_recv()
        for cp in started:
            cp.wait_send()

    return pl.pallas_call(
        body, name="swap_halves", out_shape=tuple(jax.ShapeDtypeStruct(t.shape, t.dtype) for t in arrays),
        in_specs=[_ANY] * n, out_specs=tuple([_ANY] * n), input_output_aliases={a: a for a in range(n)},
        scratch_shapes=[pltpu.SemaphoreType.DMA((n,)), pltpu.SemaphoreType.DMA((n,))],
    )(*arrays)


def _adamw_math(g, w, m, v):
    m = ADAM_B1 * m + (1.0 - ADAM_B1) * g
    v = ADAM_B2 * v + (1.0 - ADAM_B2) * jnp.square(g)
    m_hat = m / (1.0 - ADAM_B1 ** ADAM_STEP)
    v_hat = v / (1.0 - ADAM_B2 ** ADAM_STEP)
    delta = -ADAM_LR * (m_hat / (jnp.sqrt(v_hat) + ADAM_EPS) + ADAM_WD * w)
    return delta, m, v


def _row_tile(rows, row_bytes, budget, step=8):
    best = step
    for t in range(step, rows + 1, step):
        if rows % t == 0 and t * row_bytes <= budget:
            best = t
    return best


def _pair_sum(part, other, core):
    nj, a, r, c = part.shape
    rh = r // 2
    tr = _row_tile(rh, 4 * (-(-c // LANES) * LANES), 512 * 1024, 16)

    def body(core_ref, p_ref, o_ref, s32_ref, s16_ref):
        s = p_ref[0, 0, 0] + o_ref[0, 0]
        s32_ref[0, 0] = s
        s16_ref[0, 0] = s.astype(BF16)

    spec = pl.BlockSpec((1, 1, tr, c), lambda j, l, i, core_ref: (j, l, i, 0))
    return pl.pallas_call(
        body, name="pair_sum",
        out_shape=(jax.ShapeDtypeStruct(other.shape, F32), jax.ShapeDtypeStruct(other.shape, BF16)),
        grid_spec=pltpu.PrefetchScalarGridSpec(
            num_scalar_prefetch=1, grid=(nj, a, rh // tr),
            in_specs=[pl.BlockSpec((1, 1, 1, tr, c), lambda j, l, i, core_ref: (j, l, core_ref[0], i, 0)), spec],
            out_specs=(spec, spec)),
        compiler_params=pltpu.CompilerParams(dimension_semantics=("parallel", "parallel", "parallel"),
                                             vmem_limit_bytes=VMEM_LIMIT),
    )(core, part.reshape(nj, a, 2, rh, c), other)


def _sum_adamw_big(own, landed, w, m, v, place):
    _, a, rh, c = own.shape
    tr = _row_tile(rh, 4 * (-(-c // LANES) * LANES), 256 * 1024, 16)
    w4, m4, v4 = (t.reshape(a, 2, rh, c) for t in (w, m, v))

    def body(place_ref, q_ref, p_ref, w_ref, m_ref, v_ref, g_out, d_out, m_out, v_out):
        g = q_ref[0, 0]
        for f in range(3):
            g = g + p_ref[f, 0].astype(F32)
        delta, mm_, vv_ = _adamw_math(g, w_ref[0, 0], m_ref[0, 0], v_ref[0, 0])
        g_out[0, 0] = g
        d_out[0, 0] = delta
        m_out[0, 0] = mm_
        v_out[0, 0] = vv_

    wspec = pl.BlockSpec((1, 1, tr, c), lambda l, i, place_ref: (l, place_ref[0], i, 0))
    out = jax.ShapeDtypeStruct((a, 2, rh, c), F32)
    res = pl.pallas_call(
        body, name="sum_adamw_big", out_shape=(out,) * 4,
        grid_spec=pltpu.PrefetchScalarGridSpec(
            num_scalar_prefetch=1, grid=(a, rh // tr),
            in_specs=[pl.BlockSpec((1, 1, tr, c), lambda l, i, place_ref: (place_ref[1], l, i, 0)),
                      pl.BlockSpec((3, 1, tr, c), lambda l, i, place_ref: (0, l, i, 0)), wspec, wspec, wspec],
            out_specs=(wspec,) * 4),
        compiler_params=pltpu.CompilerParams(dimension_semantics=("parallel", "parallel"), vmem_limit_bytes=VMEM_LIMIT),
    )(place, own, landed, w4, m4, v4)
    return [t.reshape(a, 2 * rh, c) for t in res]


def _sum_small(parts):
    _, rows, _ = parts.shape
    tr = _row_tile(rows, 4 * LANES, 256 * 1024)

    def body(p_ref, o_ref):
        g = p_ref[0]
        for k in range(1, N_DEV):
            g = g + p_ref[k]
        o_ref[...] = g

    return pl.pallas_call(
        body, name="sum_small", out_shape=jax.ShapeDtypeStruct((rows, LANES), F32), grid=(rows // tr,),
        in_specs=[pl.BlockSpec((N_DEV, tr, LANES), lambda i: (0, i, 0))], out_specs=pl.BlockSpec((tr, LANES), lambda i: (i, 0)),
        compiler_params=pltpu.CompilerParams(dimension_semantics=("parallel",)),
    )(parts)


def _adamw_small(g, w, m, v):
    rows, _ = g.shape
    tr = _row_tile(rows, 4 * LANES, 256 * 1024)

    def body(g_ref, w_ref, m_ref, v_ref, d_out, m_out, v_out):
        d_out[...], m_out[...], v_out[...] = _adamw_math(g_ref[...], w_ref[...], m_ref[...], v_ref[...])

    spec = pl.BlockSpec((tr, LANES), lambda i: (i, 0))
    out = jax.ShapeDtypeStruct((rows, LANES), F32)
    return pl.pallas_call(
        body, name="adamw_small", out_shape=(out,) * 3, grid=(rows // tr,), in_specs=[spec] * 4, out_specs=(spec,) * 3,
        compiler_params=pltpu.CompilerParams(dimension_semantics=("parallel",)),
    )(g, w, m, v)


def _pack(arrays):
    flat = jnp.concatenate([a.reshape(-1) for a in arrays])
    rows = -(-flat.shape[0] // (8 * LANES)) * 8
    return jnp.pad(flat, (0, rows * LANES - flat.shape[0])).reshape(rows, LANES)


def _unpack(packed, shapes):
    flat, out, off = packed.reshape(-1), [], 0
    for s in shapes:
        size = math.prod(s)
        out.append(flat[off:off + size].reshape(s))
        off += size
    return out


def kernel(x, mem, norm_g, w_in, ssm_a_re, ssm_a_im, ssm_log_step, ssm_b_re, ssm_b_im, ssm_c_re, ssm_c_im, ssm_d, ssm_w_glu, ssm_b_glu, dn_conv, dn_a_log, dn_dt_bias, dn_norm_g, attn_q_norm, attn_k_norm, mem_norm_g, w_mem_kv, w_branch, w_out, final_norm_g, loss_target, m_norm_g, m_w_in, m_ssm_a_re, m_ssm_a_im, m_ssm_log_step, m_ssm_b_re, m_ssm_b_im, m_ssm_c_re, m_ssm_c_im, m_ssm_d, m_ssm_w_glu, m_ssm_b_glu, m_dn_conv, m_dn_a_log, m_dn_dt_bias, m_dn_norm_g, m_attn_q_norm, m_attn_k_norm, m_mem_norm_g, m_w_mem_kv, m_w_branch, m_w_out, m_final_norm_g, v_norm_g, v_w_in, v_ssm_a_re, v_ssm_a_im, v_ssm_log_step, v_ssm_b_re, v_ssm_b_im, v_ssm_c_re, v_ssm_c_im, v_ssm_d, v_ssm_w_glu, v_ssm_b_glu, v_dn_conv, v_dn_a_log, v_dn_dt_bias, v_dn_norm_g, v_attn_q_norm, v_attn_k_norm, v_mem_norm_g, v_w_mem_kv, v_w_branch, v_w_out, v_final_norm_g):
    args = locals()
    wts = {n: args[n] for n in WEIGHTS}
    mom = {n: args["m_" + n] for n in WEIGHTS}
    var = {n: args["v_" + n] for n in WEIGHTS}
    d_model = x.shape[-1]
    chip = 2 * lax.axis_index("x") + lax.axis_index("y")
    core = lax.axis_index("c")

    conv_rows = dn_conv.shape[1]
    conv_flat = _pack([dn_conv])
    gathered = _gather_chips([wts[n].astype(BF16) for n in BIG] + [conv_flat[None]])
    full = {n: wts[n] for n in SMALL}
    g_in = gathered[0]
    full["w_in"] = _pad_w_in(jnp.concatenate([g_in[j] for j in range(N_CHIPS)], axis=-1), d_model)
    for n, g in zip(BIG[1:], gathered[1:]):
        full[n] = jnp.concatenate([g[j] for j in range(N_CHIPS)], axis=1).astype(F32)
    conv_all = gathered[-1][:, 0].reshape(N_CHIPS, -1)[:, :dn_conv.size].reshape((N_CHIPS,) + dn_conv.shape)
    full["dn_conv"] = jnp.concatenate([conv_all[j] for j in range(N_CHIPS)], axis=1)

    loss, grad_x, grads = _local_step(x[0], mem[0], loss_target[0], full)
    loss = lax.psum(loss, ("x", "y", "c"))

    big_parts = []
    for n in BIG:
        g = grads[n]
        if n == "w_in":
            cols = g.shape[-1] // N_CHIPS
            big_parts.append(jnp.stack([g[..., j * cols:(j + 1) * cols] for j in range(N_CHIPS)]))
        else:
            rows = g.shape[1] // N_CHIPS
            big_parts.append(jnp.stack([g[:, j * rows:(j + 1) * rows] for j in range(N_CHIPS)]))
    core_arr = jnp.reshape(core, (1,)).astype(jnp.int32)
    place = jnp.stack([core, chip]).astype(jnp.int32)
    from_pair = _exchange_pair(big_parts)
    sums = [_pair_sum(p, o, core_arr) for p, o in zip(big_parts, from_pair)]
    small_shapes = [grads[n].shape for n in SMALL]
    landed = _exchange_chips([s[1] for s in sums], _pack([grads[n] for n in SMALL]))

    halves = []
    for n, s, parts in zip(BIG, sums, landed[:-1]):
        halves.extend(_sum_adamw_big(s[0], parts, wts[n], mom[n], var[n], place))
    swapped = _swap_halves(halves)
    res = {}
    for i, n in enumerate(BIG):
        res[n] = swapped[4 * i:4 * i + 4]
    small_g = _unpack(_sum_small(landed[-1]), small_shapes)
    small_g = dict(zip(SMALL, small_g))
    small_g["dn_conv"] = lax.dynamic_slice_in_dim(small_g["dn_conv"], chip * conv_rows, conv_rows, axis=1)
    shapes = [small_g[n].shape for n in SMALL]
    upd = _adamw_small(_pack([small_g[n] for n in SMALL]), _pack([wts[n] for n in SMALL]),
                       _pack([mom[n] for n in SMALL]), _pack([var[n] for n in SMALL]))
    upd = [_unpack(u, shapes) for u in upd]
    for i, n in enumerate(SMALL):
        res[n] = (small_g[n], upd[0][i], upd[1][i], upd[2][i])

    out = [loss, grad_x[None]]
    for kind in range(4):
        out.extend(res[n][kind] for n in WEIGHTS)
    return tuple(out)
```

```python
import functools
import math

import jax
import jax.numpy as jnp
from jax import lax
from jax.experimental import pallas as pl
from jax.experimental.pallas import tpu as pltpu

F32 = jnp.float32
BF16 = jnp.bfloat16

GRID_W = 64
EPS = 1e-6
SSM_GROUP = 16
SSM_STATE = 64
SSM_GROUPS = 48
DN_HEADS = 6
DN_HEAD_DIM = 128
DN_CONV = 5
DN_CHUNK = 64
ATT_HEADS = 8
ATT_KV_HEADS = 2
ATT_HEAD_DIM = 128
ROPE_THETA = 10000.0
MEM_HEADS = 4
MEM_HEAD_DIM = 128
N_BRANCH = 4

ADAM_LR = 0.001
ADAM_B1 = 0.9
ADAM_B2 = 0.999
ADAM_EPS = 1e-08
ADAM_WD = 0.01
ADAM_STEP = 10

N_CHIPS = 4
N_DEV = 8
LANES = 128
VMEM_LIMIT = 48 * 1024 * 1024
IN_PAD_UNIT = 512
MM_FULL_K = 2048
ATT_HEADS_PER_STEP = 4

WEIGHTS = ['norm_g', 'w_in', 'ssm_a_re', 'ssm_a_im', 'ssm_log_step', 'ssm_b_re', 'ssm_b_im', 'ssm_c_re',
           'ssm_c_im', 'ssm_d', 'ssm_w_glu', 'ssm_b_glu', 'dn_conv', 'dn_a_log', 'dn_dt_bias', 'dn_norm_g',
           'attn_q_norm', 'attn_k_norm', 'mem_norm_g', 'w_mem_kv', 'w_branch', 'w_out', 'final_norm_g']
BIG = ['w_in', 'ssm_w_glu', 'w_mem_kv', 'w_branch', 'w_out']
SMALL = [n for n in WEIGHTS if n not in BIG]


def _pick(dim, cands):
    for c in cands:
        if dim % c == 0:
            return c
    return dim


def _widths(d_model):
    ssm_w = SSM_GROUPS * SSM_GROUP
    dn_w = DN_HEADS * DN_HEAD_DIM
    att_w = ATT_HEADS * ATT_HEAD_DIM
    kv_w = ATT_KV_HEADS * ATT_HEAD_DIM
    mem_w = MEM_HEADS * MEM_HEAD_DIM
    return [("u", ssm_w), ("z_a", ssm_w), ("dq", dn_w), ("dk", dn_w), ("dv", dn_w), ("dab", 4 * DN_HEADS),
            ("z_b", dn_w), ("aq", att_w), ("ak", kv_w), ("av", kv_w), ("z_c", att_w), ("mq", mem_w),
            ("z_m", mem_w), ("gates", N_BRANCH * d_model)]


def _layout(d_model):
    out, src, dst = {}, 0, 0
    for name, w in _widths(d_model):
        pw = -(-w // IN_PAD_UNIT) * IN_PAD_UNIT if w % LANES else w
        out[name] = (src, dst, w, pw)
        src += w
        dst += pw
    return out, src, dst


def _pad_w_in(full, d_model):
    lay, _, _ = _layout(d_model)
    parts = []
    for name, _ in _widths(d_model):
        s, _, w, pw = lay[name]
        seg = full[..., s:s + w]
        if pw != w:
            seg = jnp.pad(seg, [(0, 0)] * (full.ndim - 1) + [(0, pw - w)])
        parts.append(seg)
    return jnp.concatenate(parts, axis=-1)


def _unpad_w_in(padded, d_model):
    lay, _, _ = _layout(d_model)
    return jnp.concatenate([padded[..., lay[n][1]:lay[n][1] + lay[n][2]] for n, _ in _widths(d_model)], axis=-1)


def _mm(a, b, mode="nn", name="mm"):
    if mode == "nn":
        (m, k), (k2, n) = a.shape, b.shape
    elif mode == "nt":
        (m, k), (n, k2) = a.shape, b.shape
    else:
        (k, m), (k2, n) = a.shape, b.shape
    assert k == k2, (a.shape, b.shape, mode)
    tk = k if k <= MM_FULL_K else _pick(k, (1024, 512, 256, 128))
    tm = _pick(m, (2048, 1024, 512, 256, 128) if mode == "tn" else (1024, 512, 256, 128))
    tn = _pick(n, (512, 384, 256, 128) if tk > 1024 or tm > 1024 else (1024, 512, 384, 256, 128))
    nk = k // tk
    dims = {"nn": (((1,), (0,)), ((), ())), "nt": (((1,), (1,)), ((), ())), "tn": (((0,), (0,)), ((), ()))}[mode]

    def body(a_ref, b_ref, o_ref, *scratch):
        prod = lax.dot_general(a_ref[...].astype(BF16), b_ref[...].astype(BF16), dims, preferred_element_type=F32)
        if nk == 1:
            o_ref[...] = prod
            return
        acc_ref, = scratch
        kk = pl.program_id(2)

        @pl.when(kk == 0)
        def _():
            acc_ref[...] = prod

        @pl.when(kk > 0)
        def _():
            acc_ref[...] += prod

        @pl.when(kk == nk - 1)
        def _():
            o_ref[...] = acc_ref[...]

    a_spec = pl.BlockSpec((tk, tm), lambda i, j, kk: (kk, i)) if mode == "tn" else pl.BlockSpec((tm, tk), lambda i, j, kk: (i, kk))
    b_spec = pl.BlockSpec((tn, tk), lambda i, j, kk: (j, kk)) if mode == "nt" else pl.BlockSpec((tk, tn), lambda i, j, kk: (kk, j))
    return pl.pallas_call(
        body, name=name, out_shape=jax.ShapeDtypeStruct((m, n), F32), grid=(m // tm, n // tn, nk),
        in_specs=[a_spec, b_spec], out_specs=pl.BlockSpec((tm, tn), lambda i, j, kk: (i, j)),
        scratch_shapes=[pltpu.VMEM((tm, tn), F32)] if nk > 1 else [],
        compiler_params=pltpu.CompilerParams(dimension_semantics=("parallel", "parallel", "arbitrary"),
                                             vmem_limit_bytes=VMEM_LIMIT),
    )(a, b)


@jax.custom_vjp
def pmm(a, b):
    return _mm(a, b, "nn", "pmm_fwd")


def _pmm_fwd(a, b):
    return _mm(a, b, "nn", "pmm_fwd"), (a, b)


def _pmm_bwd(res, dc):
    a, b = res
    return _mm(dc, b, "nt", "pmm_da"), _mm(a, dc, "tn", "pmm_db")


pmm.defvjp(_pmm_fwd, _pmm_bwd)


def _cmm_call(a, w):
    m = a.shape[0]
    nc, kc, nn = w.shape
    tm = _pick(m, (1024, 512, 256, 128))

    def body(a_ref, w_ref, o_ref):
        o_ref[...] = lax.dot_general(a_ref[...].astype(BF16), w_ref[0].astype(BF16), (((1,), (0,)), ((), ())),
                                     preferred_element_type=F32)

    return pl.pallas_call(
        body, name="cmm", out_shape=jax.ShapeDtypeStruct((m, nc * nn), F32), grid=(m // tm, nc),
        in_specs=[pl.BlockSpec((tm, kc), lambda i, c: (i, c)), pl.BlockSpec((1, kc, nn), lambda i, c: (c, 0, 0))],
        out_specs=pl.BlockSpec((tm, nn), lambda i, c: (i, c)),
        compiler_params=pltpu.CompilerParams(dimension_semantics=("parallel", "parallel"), vmem_limit_bytes=VMEM_LIMIT),
    )(a, w)


def _cmm_dw_call(a, dc, kc, nn):
    m = a.shape[0]
    nc = a.shape[1] // kc
    tm = _pick(m, (1024, 512, 256, 128))
    nm = m // tm

    def body(a_ref, d_ref, o_ref, acc_s):
        i = pl.program_id(1)

        @pl.when(i == 0)
        def _():
            acc_s[...] = jnp.zeros_like(acc_s)

        acc_s[...] += lax.dot_general(a_ref[...].astype(BF16), d_ref[...].astype(BF16), (((0,), (0,)), ((), ())),
                                      preferred_element_type=F32)

        @pl.when(i == nm - 1)
        def _():
            o_ref[0] = acc_s[...]

    return pl.pallas_call(
        body, name="cmm_dw", out_shape=jax.ShapeDtypeStruct((nc, kc, nn), F32), grid=(nc, nm),
        in_specs=[pl.BlockSpec((tm, kc), lambda c, i: (i, c)), pl.BlockSpec((tm, nn), lambda c, i: (i, c))],
        out_specs=pl.BlockSpec((1, kc, nn), lambda c, i: (c, 0, 0)),
        scratch_shapes=[pltpu.VMEM((kc, nn), F32)],
        compiler_params=pltpu.CompilerParams(dimension_semantics=("parallel", "arbitrary"), vmem_limit_bytes=VMEM_LIMIT),
    )(a, dc)


@jax.custom_vjp
def cmm(a, w):
    return _cmm_call(a, w)


def _cmm_fwd(a, w):
    return _cmm_call(a, w), (a, w)


def _cmm_bwd(res, dc):
    a, w = res
    return _cmm_call(dc, jnp.swapaxes(w, 1, 2)), _cmm_dw_call(a, dc, w.shape[1], w.shape[2])


cmm.defvjp(_cmm_fwd, _cmm_bwd)


def _cluster_diag(w, per):
    g, a, b = w.shape
    eye = jnp.eye(per, dtype=w.dtype)
    wc = w.reshape(g // per, per, a, b)
    return (wc[:, :, :, None, :] * eye[None, :, None, :, None]).reshape(g // per, per * a, per * b)


def _norm_rows(rows, d):
    return _row_tile(rows, 4 * d, 1024 * 1024)


def _rms_fwd_call(x, g, out_dtype=F32):
    rows, d = x.shape
    tr = _norm_rows(rows, d)

    def body(x_ref, g_ref, o_ref):
        xv = x_ref[...]
        o_ref[...] = (xv * lax.rsqrt(jnp.mean(xv * xv, axis=1, keepdims=True) + EPS) * g_ref[...]).astype(out_dtype)

    return pl.pallas_call(
        body, name="rms_fwd", out_shape=jax.ShapeDtypeStruct((rows, d), out_dtype), grid=(rows // tr,),
        in_specs=[pl.BlockSpec((tr, d), lambda i: (i, 0)), pl.BlockSpec((1, d), lambda i: (0, 0))],
        out_specs=pl.BlockSpec((tr, d), lambda i: (i, 0)),
        compiler_params=pltpu.CompilerParams(dimension_semantics=("parallel",), vmem_limit_bytes=VMEM_LIMIT),
    )(x, g.reshape(1, d))


def _rms_bwd_call(x, g, dy):
    rows, d = x.shape
    tr = _norm_rows(rows, d)

    def body(x_ref, g_ref, dy_ref, dx_ref, dg_ref):
        @pl.when(pl.program_id(0) == 0)
        def _():
            dg_ref[...] = jnp.zeros_like(dg_ref)

        xv = x_ref[...]
        r = lax.rsqrt(jnp.mean(xv * xv, axis=1, keepdims=True) + EPS)
        xh = xv * r
        dyv = dy_ref[...]
        dxh = dyv * g_ref[...]
        dx_ref[...] = r * (dxh - xh * jnp.mean(dxh * xh, axis=1, keepdims=True))
        dg_ref[...] += jnp.sum(dyv * xh, axis=0, keepdims=True)

    return pl.pallas_call(
        body, name="rms_bwd", out_shape=(jax.ShapeDtypeStruct((rows, d), F32), jax.ShapeDtypeStruct((1, d), F32)),
        grid=(rows // tr,),
        in_specs=[pl.BlockSpec((tr, d), lambda i: (i, 0)), pl.BlockSpec((1, d), lambda i: (0, 0)),
                  pl.BlockSpec((tr, d), lambda i: (i, 0))],
        out_specs=(pl.BlockSpec((tr, d), lambda i: (i, 0)), pl.BlockSpec((1, d), lambda i: (0, 0))),
        compiler_params=pltpu.CompilerParams(dimension_semantics=("arbitrary",), vmem_limit_bytes=VMEM_LIMIT),
    )(x, g.reshape(1, d), dy)


@jax.custom_vjp
def prms(x, g):
    return _rms_fwd_call(x, g)


def _prms_fwd(x, g):
    return _rms_fwd_call(x, g), (x, g)


def _prms_bwd(res, dy):
    x, g = res
    dx, dg = _rms_bwd_call(x, g, dy)
    return dx, dg.reshape(g.shape)


prms.defvjp(_prms_fwd, _prms_bwd)


def _loss_call(x, g, target):
    rows, d = x.shape
    tr = _norm_rows(rows, d)
    nb = rows // tr

    def body(x_ref, g_ref, t_ref, l_ref, dx_ref, dg_ref):
        @pl.when(pl.program_id(0) == 0)
        def _():
            dg_ref[...] = jnp.zeros_like(dg_ref)

        xv = x_ref[...]
        r = lax.rsqrt(jnp.mean(xv * xv, axis=1, keepdims=True) + EPS)
        xh = xv * r
        err = xh * g_ref[...] - t_ref[...]
        l_ref[...] = jnp.broadcast_to(0.5 * jnp.sum(jnp.mean(err * err, axis=1, keepdims=True), axis=0, keepdims=True)[None],
                                      l_ref.shape)
        dyv = err * (1.0 / d)
        dxh = dyv * g_ref[...]
        dx_ref[...] = r * (dxh - xh * jnp.mean(dxh * xh, axis=1, keepdims=True))
        dg_ref[...] += jnp.sum(dyv * xh, axis=0, keepdims=True)

    return pl.pallas_call(
        body, name="final_loss",
        out_shape=(jax.ShapeDtypeStruct((nb, 1, LANES), F32), jax.ShapeDtypeStruct((rows, d), F32),
                   jax.ShapeDtypeStruct((1, d), F32)),
        grid=(nb,),
        in_specs=[pl.BlockSpec((tr, d), lambda i: (i, 0)), pl.BlockSpec((1, d), lambda i: (0, 0)),
                  pl.BlockSpec((tr, d), lambda i: (i, 0))],
        out_specs=(pl.BlockSpec((1, 1, LANES), lambda i: (i, 0, 0)), pl.BlockSpec((tr, d), lambda i: (i, 0)),
                   pl.BlockSpec((1, d), lambda i: (0, 0))),
        compiler_params=pltpu.CompilerParams(dimension_semantics=("arbitrary",), vmem_limit_bytes=VMEM_LIMIT),
    )(x, g.reshape(1, d), target)


_NT = (((1,), (1,)), ((), ()))
_NN = (((1,), (0,)), ((), ()))
_TN = (((0,), (0,)), ((), ()))


def _att_plan(lq, lk, hq, group):
    hp = min(ATT_HEADS_PER_STEP, hq)
    assert hq % hp == 0 and (hp % group == 0 or group % hp == 0)
    kvp = max(1, hp // group)
    return _pick(lq, (256, 128)), _pick(lk, (1024, 512, 256, 128)), hp, kvp


def _att_specs(bq, bk, dh, hp, kvp, group, q_map, k_map):
    qs = pl.BlockSpec((bq, hp * dh), lambda *g: q_map(*g))
    ls = pl.BlockSpec((bq, hp * LANES), lambda *g: q_map(*g))
    ks = pl.BlockSpec((bk, kvp * dh), lambda *g: (k_map(*g)[0], k_map(*g)[1] * hp // (group * kvp)))
    return qs, ls, ks


def _flash_fwd(q, k, v, dh, group, scale):
    lq, lk = q.shape[0], k.shape[0]
    hq = q.shape[1] // dh
    bq, bk, hp, kvp = _att_plan(lq, lk, hq, group)
    nkv = lk // bk

    def body(q_ref, k_ref, v_ref, o_ref, lse_ref, m_s, l_s, acc_s):
        j = pl.program_id(2)

        @pl.when(j == 0)
        def _():
            m_s[...] = jnp.full_like(m_s, -jnp.inf)
            l_s[...] = jnp.zeros_like(l_s)
            acc_s[...] = jnp.zeros_like(acc_s)

        for h in range(hp):
            hk = h // group if kvp > 1 else 0
            cols, lcols, kcols = slice(h * dh, (h + 1) * dh), slice(h * LANES, (h + 1) * LANES), slice(hk * dh, (hk + 1) * dh)
            s = lax.dot_general(q_ref[:, cols], k_ref[:, kcols], _NT, preferred_element_type=F32)
            if scale != 1.0:
                s = s * scale
            m_prev = m_s[:, lcols]
            m_cur = jnp.maximum(m_prev, jnp.max(s, axis=1, keepdims=True))
            alpha = jnp.exp(m_prev - m_cur)
            p = jnp.exp(s - m_cur[:, :1])
            l_s[:, lcols] = alpha * l_s[:, lcols] + jnp.sum(p, axis=1, keepdims=True)
            acc_s[:, cols] = acc_s[:, cols] * alpha[:, :1] + lax.dot_general(p.astype(BF16), v_ref[:, kcols], _NN,
                                                                           preferred_element_type=F32)
            m_s[:, lcols] = m_cur

        @pl.when(j == nkv - 1)
        def _():
            for h in range(hp):
                cols, lcols = slice(h * dh, (h + 1) * dh), slice(h * LANES, (h + 1) * LANES)
                o_ref[:, cols] = acc_s[:, cols] / l_s[:, lcols][:, :1]
                lse_ref[:, lcols] = m_s[:, lcols] + jnp.log(l_s[:, lcols])

    qs, ls, ks = _att_specs(bq, bk, dh, hp, kvp, group, lambda h, i, j: (i, h), lambda h, i, j: (j, h))
    return pl.pallas_call(
        body, name="flash_fwd",
        out_shape=(jax.ShapeDtypeStruct((lq, hq * dh), F32), jax.ShapeDtypeStruct((lq, hq * LANES), F32)),
        grid=(hq // hp, lq // bq, nkv), in_specs=[qs, ks, ks], out_specs=(qs, ls),
        scratch_shapes=[pltpu.VMEM((bq, hp * LANES), F32), pltpu.VMEM((bq, hp * LANES), F32), pltpu.VMEM((bq, hp * dh), F32)],
        compiler_params=pltpu.CompilerParams(dimension_semantics=("parallel", "parallel", "arbitrary"),
                                             vmem_limit_bytes=VMEM_LIMIT),
    )(q, k, v)


def _flash_dq(q, k, v, o, lse, do, dh, group, scale):
    lq, lk = q.shape[0], k.shape[0]
    hq = q.shape[1] // dh
    bq, bk, hp, kvp = _att_plan(lq, lk, hq, group)
    nkv = lk // bk

    def body(q_ref, k_ref, v_ref, o_ref, lse_ref, do_ref, dq_ref, dob_ref, dl_ref, acc_s):
        j = pl.program_id(2)

        @pl.when(j == 0)
        def _():
            acc_s[...] = jnp.zeros_like(acc_s)
            dob_ref[...] = do_ref[...].astype(BF16)
            for h in range(hp):
                cols, lcols = slice(h * dh, (h + 1) * dh), slice(h * LANES, (h + 1) * LANES)
                dl_ref[:, lcols] = jnp.broadcast_to(jnp.sum(do_ref[:, cols] * o_ref[:, cols], axis=1, keepdims=True),
                                                    (bq, LANES))

        for h in range(hp):
            hk = h // group if kvp > 1 else 0
            cols, lcols, kcols = slice(h * dh, (h + 1) * dh), slice(h * LANES, (h + 1) * LANES), slice(hk * dh, (hk + 1) * dh)
            kb = k_ref[:, kcols]
            s = lax.dot_general(q_ref[:, cols], kb, _NT, preferred_element_type=F32)
            if scale != 1.0:
                s = s * scale
            p = jnp.exp(s - lse_ref[:, lcols][:, :1])
            dp = lax.dot_general(dob_ref[:, cols], v_ref[:, kcols], _NT, preferred_element_type=F32)
            ds = p * (dp - dl_ref[:, lcols][:, :1])
            if scale != 1.0:
                ds = ds * scale
            acc_s[:, cols] += lax.dot_general(ds.astype(BF16), kb, _NN, preferred_element_type=F32)

        @pl.when(j == nkv - 1)
        def _():
            dq_ref[...] = acc_s[...]

    qs, ls, ks = _att_specs(bq, bk, dh, hp, kvp, group, lambda h, i, j: (i, h), lambda h, i, j: (j, h))
    return pl.pallas_call(
        body, name="flash_dq",
        out_shape=(jax.ShapeDtypeStruct((lq, hq * dh), F32), jax.ShapeDtypeStruct((lq, hq * dh), BF16),
                   jax.ShapeDtypeStruct((lq, hq * LANES), F32)),
        grid=(hq // hp, lq // bq, nkv), in_specs=[qs, ks, ks, qs, ls, qs], out_specs=(qs, qs, ls),
        scratch_shapes=[pltpu.VMEM((bq, hp * dh), F32)],
        compiler_params=pltpu.CompilerParams(dimension_semantics=("parallel", "parallel", "arbitrary"),
                                             vmem_limit_bytes=VMEM_LIMIT),
    )(q, k, v, o, lse, do)


def _flash_dkv(q, k, v, lse, dob, delta, dh, group, scale):
    lq, lk = q.shape[0], k.shape[0]
    hq = q.shape[1] // dh
    bq, bk, hp, kvp = _att_plan(lq, lk, hq, group)
    nq = lq // bq
    reps = max(1, group // hp)
    nt = reps * nq

    def body(q_ref, k_ref, v_ref, lse_ref, dob_ref, dl_ref, dk_ref, dv_ref, dk_s, dv_s):
        t = pl.program_id(2)

        @pl.when(t == 0)
        def _():
            dk_s[...] = jnp.zeros_like(dk_s)
            dv_s[...] = jnp.zeros_like(dv_s)

        for h in range(hp):
            hk = h // group if kvp > 1 else 0
            cols, lcols, kcols = slice(h * dh, (h + 1) * dh), slice(h * LANES, (h + 1) * LANES), slice(hk * dh, (hk + 1) * dh)
            qb = q_ref[:, cols]
            dob = dob_ref[:, cols]
            s = lax.dot_general(qb, k_ref[:, kcols], _NT, preferred_element_type=F32)
            if scale != 1.0:
                s = s * scale
            p = jnp.exp(s - lse_ref[:, lcols][:, :1])
            dp = lax.dot_general(dob, v_ref[:, kcols], _NT, preferred_element_type=F32)
            ds = p * (dp - dl_ref[:, lcols][:, :1])
            if scale != 1.0:
                ds = ds * scale
            dv_s[:, kcols] += lax.dot_general(p.astype(BF16), dob, _TN, preferred_element_type=F32)
            dk_s[:, kcols] += lax.dot_general(ds.astype(BF16), qb, _TN, preferred_element_type=F32)

        @pl.when(t == nt - 1)
        def _():
            dk_ref[...] = dk_s[...]
            dv_ref[...] = dv_s[...]

    hkv_blocks = (hq // group) // kvp
    qs = pl.BlockSpec((bq, hp * dh), lambda h, j, t: (t % nq, h * reps + t // nq))
    ls = pl.BlockSpec((bq, hp * LANES), lambda h, j, t: (t % nq, h * reps + t // nq))
    ks = pl.BlockSpec((bk, kvp * dh), lambda h, j, t: (j, h))
    return pl.pallas_call(
        body, name="flash_dkv",
        out_shape=(jax.ShapeDtypeStruct(k.shape, F32), jax.ShapeDtypeStruct(v.shape, F32)),
        grid=(hkv_blocks, lk // bk, nt), in_specs=[qs, ks, ks, ls, qs, ls], out_specs=(ks, ks),
        scratch_shapes=[pltpu.VMEM((bk, kvp * dh), F32), pltpu.VMEM((bk, kvp * dh), F32)],
        compiler_params=pltpu.CompilerParams(dimension_semantics=("parallel", "parallel", "arbitrary"),
                                             vmem_limit_bytes=VMEM_LIMIT),
    )(q, k, v, lse, dob, delta)


@functools.partial(jax.custom_vjp, nondiff_argnums=(3, 4, 5))
def flash(q, k, v, dh, group, scale):
    return _flash_fwd(q.astype(BF16), k.astype(BF16), v.astype(BF16), dh, group, scale)[0]


def _flash_vfwd(q, k, v, dh, group, scale):
    qb, kb, vb = q.astype(BF16), k.astype(BF16), v.astype(BF16)
    o, lse = _flash_fwd(qb, kb, vb, dh, group, scale)
    return o, (qb, kb, vb, o, lse)


def _flash_vbwd(dh, group, scale, res, do):
    q, k, v, o, lse = res
    dq, dob, delta = _flash_dq(q, k, v, o, lse, do, dh, group, scale)
    dk, dv = _flash_dkv(q, k, v, lse, dob, delta, dh, group, scale)
    return dq, dk, dv


flash.defvjp(_flash_vfwd, _flash_vbwd)


def _scan_call(lr, li, br, bi, prev_r, prev_i, reverse, name):
    rows, width = br.shape
    ts = _pick(width, (1024, 512, 256, 128))
    tr = _pick(rows, (256, 128, 64, 32, 16, 8))
    nb = rows // tr
    with_prev = prev_r is not None

    def body(*refs):
        if with_prev:
            lr_ref, li_ref, br_ref, bi_ref, pr_ref, pi_ref, sr_ref, si_ref, dr_ref, di_ref, cr_s, ci_s = refs
        else:
            lr_ref, li_ref, br_ref, bi_ref, sr_ref, si_ref, pr_ref, pi_ref, cr_s, ci_s = refs
        i = pl.program_id(1)

        @pl.when(i == 0)
        def _():
            cr_s[...] = jnp.zeros_like(cr_s)
            ci_s[...] = jnp.zeros_like(ci_s)
            if with_prev:
                dr_ref[...] = jnp.zeros_like(dr_ref)
                di_ref[...] = jnp.zeros_like(di_ref)

        a_re = lr_ref[...]
        a_im = li_ref[...]

        def step(r, carry):
            if with_prev:
                c_re, c_im, d_re, d_im = carry
            else:
                c_re, c_im = carry
            t = tr - 1 - r if reverse else r
            row = pl.ds(t, 1)
            n_re = a_re * c_re - a_im * c_im + br_ref[row, :]
            n_im = a_re * c_im + a_im * c_re + bi_ref[row, :]
            sr_ref[row, :] = n_re
            si_ref[row, :] = n_im
            if with_prev:
                p_re = pr_ref[row, :]
                p_im = pi_ref[row, :]
                return n_re, n_im, d_re + n_re * p_re + n_im * p_im, d_im + n_im * p_re - n_re * p_im
            pr_ref[row, :] = c_re
            pi_ref[row, :] = c_im
            return n_re, n_im

        zero = jnp.zeros((1, ts), F32)
        init = (cr_s[...], ci_s[...]) + ((zero, zero) if with_prev else ())
        out = lax.fori_loop(0, tr, step, init, unroll=8)
        cr_s[...] = out[0]
        ci_s[...] = out[1]
        if with_prev:
            dr_ref[...] += out[2]
            di_ref[...] += out[3]

    blk = (lambda j, i: (nb - 1 - i, j)) if reverse else (lambda j, i: (i, j))
    row_spec = pl.BlockSpec((tr, ts), blk)
    vec_spec = pl.BlockSpec((1, ts), lambda j, i: (0, j))
    full = jax.ShapeDtypeStruct((rows, width), F32)
    vec = jax.ShapeDtypeStruct((1, width), F32)
    ins = [lr, li, br, bi] + ([prev_r, prev_i] if with_prev else [])
    return pl.pallas_call(
        body, name=name, grid=(width // ts, nb),
        in_specs=[vec_spec, vec_spec] + [row_spec] * (len(ins) - 2),
        out_shape=(full, full, vec, vec) if with_prev else (full, full, full, full),
        out_specs=(row_spec, row_spec, vec_spec, vec_spec) if with_prev else (row_spec,) * 4,
        scratch_shapes=[pltpu.VMEM((1, ts), F32), pltpu.VMEM((1, ts), F32)],
        compiler_params=pltpu.CompilerParams(dimension_semantics=("parallel", "arbitrary"), vmem_limit_bytes=VMEM_LIMIT),
    )(*ins)


@functools.partial(jax.custom_vjp, nondiff_argnums=(4,))
def cscan(lr, li, br, bi, reverse):
    out = _scan_call(lr, li, br, bi, None, None, reverse, "s5_scan")
    return out[0], out[1]


def _cscan_fwd(lr, li, br, bi, reverse):
    sr, si, pr, pi = _scan_call(lr, li, br, bi, None, None, reverse, "s5_scan")
    return (sr, si), (lr, li, pr, pi)


def _cscan_bwd(reverse, res, g):
    lr, li, pr, pi = res
    ar, ai, dlr, dli = _scan_call(lr, -li, g[0], g[1], pr, pi, not reverse, "s5_scan_bwd")
    return dlr, dli, ar, ai


cscan.defvjp(_cscan_fwd, _cscan_bwd)


_BDIMS = {"nn": (((2,), (1,)), ((0,), (0,))), "nt": (((2,), (2,)), ((0,), (0,))), "tn": (((1,), (1,)), ((0,), (0,)))}


def _bdot_raw(a, b, mode):
    return lax.dot_general(a.astype(BF16), b.astype(BF16), _BDIMS[mode], preferred_element_type=F32)


@functools.partial(jax.custom_vjp, nondiff_argnums=(2,))
def _bdot(a, b, mode):
    return _bdot_raw(a, b, mode)


def _bdot_fwd(a, b, mode):
    return _bdot_raw(a, b, mode), (a, b)


def _bdot_bwd(mode, res, dc):
    a, b = res
    if mode == "nn":
        return _bdot_raw(dc, b, "nt"), _bdot_raw(a, dc, "tn")
    if mode == "nt":
        return _bdot_raw(dc, b, "nn"), _bdot_raw(dc, a, "tn")
    return _bdot_raw(b, dc, "nt"), _bdot_raw(a, dc, "nn")


_bdot.defvjp(_bdot_fwd, _bdot_bwd)


def _fdot(a, b):
    return lax.dot_general(a, b, _BDIMS["nn"], precision=lax.Precision.HIGHEST, preferred_element_type=F32)


def _dn_chunk(state, q, k, v, bb, gc, gc64, gr, direction):
    c = q.shape[1]
    row = lax.broadcasted_iota(jnp.int32, (c, c), 0)
    col = lax.broadcasted_iota(jnp.int32, (c, c), 1)
    sgn = 1 - 2 * direction
    after = ((row - col) * sgn > 0)[None]
    incl = ((row - col) * sgn >= 0)[None]
    eye = (row == col).astype(F32)[None]
    kb = k * bb
    decay = jnp.where(incl, jnp.exp(jnp.where(incl, gc64 - gr, 0.0)), 0.0)
    lower = jnp.where(after, _bdot(kb, k, "nt") * decay, 0.0)
    inv = eye - lower
    power = _fdot(lower, lower)
    span = 2
    while span < c:
        inv = inv + _fdot(inv, power)
        span *= 2
        if span < c:
            power = _fdot(power, power)
    eg = jnp.exp(gc)
    u = _fdot(inv, v * bb)
    w = _fdot(inv, kb * eg)
    intra = _bdot(q, k, "nt") * decay
    last = (lax.broadcasted_iota(jnp.int32, (c, 1), 0) == (c - 1) * (1 - direction)).astype(F32)[None]
    g_last = jnp.sum(gc * last, axis=1, keepdims=True)
    k_dec = k * jnp.exp(g_last - gc)
    v_new = u - _bdot(w, state, "nn")
    o = _bdot(q * eg, state, "nn") + _bdot(intra, v_new, "nn")
    new_state = state * jnp.exp(g_last) + _bdot(k_dec, v_new, "tn")
    return new_state, o


def _dn_rows(lq):
    return _pick(lq, (256, 128, 64))


def _dn_prep(q, k, v):
    q, k, v = jax.nn.silu(q), jax.nn.silu(k), jax.nn.silu(v)
    q = q * lax.rsqrt(jnp.sum(q * q, axis=2, keepdims=True) + EPS) * (q.shape[2] ** -0.5)
    k = k * lax.rsqrt(jnp.sum(k * k, axis=2, keepdims=True) + EPS)
    return q, k, v


def _dn_step(state, q, k, v, bb, gc, gc64, gr, direction):
    q, k, v = _dn_prep(q, k, v)
    return _dn_chunk(state, q, k, v, bb, gc, gc64, gr, direction)


def _dn_heads(ref, rows, part, h, d):
    return jnp.stack([ref[rows, (part * h + i) * d:(part * h + i + 1) * d] for i in range(h)])


def _dn_fwd_call(qkv, bb, gc, gc64, gr):
    _, h, lq, d = bb.shape
    c = DN_CHUNK
    t = _dn_rows(lq)
    nb, nc = lq // t, t // c

    def body(qkv_ref, bb_ref, gc_ref, g64_ref, gr_ref, o_ref, st_ref, s_s):
        direction = pl.program_id(0)

        @pl.when(pl.program_id(1) == 0)
        def _():
            s_s[...] = jnp.zeros_like(s_s)

        def step(j, state):
            ci = j + direction * (nc - 1 - 2 * j)
            rows = pl.ds(pl.multiple_of(ci * c, c), c)
            st_ref[0, ci] = state
            state, o = _dn_step(state, _dn_heads(qkv_ref, rows, 0, h, d), _dn_heads(qkv_ref, rows, 1, h, d),
                                _dn_heads(qkv_ref, rows, 2, h, d), bb_ref[0, :, rows, :], gc_ref[0, :, rows, :],
                                g64_ref[0, :, rows, :], gr_ref[0, :, ci], direction)
            for i in range(h):
                o_ref[0, rows, i * d:(i + 1) * d] = o[i]
            return state

        s_s[...] = lax.fori_loop(0, nc, step, s_s[...])

    blk = lambda dd, i: i + dd * (nb - 1 - 2 * i)
    ds = pl.BlockSpec((1, h, t, d), lambda dd, i: (dd, 0, blk(dd, i), 0))
    return pl.pallas_call(
        body, name="deltanet_fwd", grid=(2, nb),
        in_specs=[pl.BlockSpec((t, 3 * h * d), lambda dd, i: (blk(dd, i), 0)), ds, ds,
                  pl.BlockSpec((1, h, t, c), lambda dd, i: (dd, 0, blk(dd, i), 0)),
                  pl.BlockSpec((1, h, nc, 1, c), lambda dd, i: (dd, 0, blk(dd, i), 0, 0))],
        out_shape=(jax.ShapeDtypeStruct((2, lq, h * d), F32), jax.ShapeDtypeStruct((2, lq // c, h, d, d), F32)),
        out_specs=(pl.BlockSpec((1, t, h * d), lambda dd, i: (dd, blk(dd, i), 0)),
                   pl.BlockSpec((1, nc, h, d, d), lambda dd, i: (dd, blk(dd, i), 0, 0, 0))),
        scratch_shapes=[pltpu.VMEM((h, d, d), F32)],
        compiler_params=pltpu.CompilerParams(dimension_semantics=("parallel", "arbitrary"), vmem_limit_bytes=VMEM_LIMIT),
    )(qkv, bb, gc, gc64, gr)


def _dn_bwd_call(qkv, bb, gc, gc64, gr, states, do):
    _, h, lq, d = bb.shape
    c = DN_CHUNK
    t = _dn_rows(lq)
    nb, nc = lq // t, t // c

    def body(qkv_ref, bb_ref, gc_ref, g64_ref, gr_ref, st_ref, do_ref, dqkv_ref, dbb_ref, dgc_ref, dg64_ref, dgr_ref, ds_s):
        direction = pl.program_id(0)

        @pl.when(pl.program_id(1) == 0)
        def _():
            ds_s[...] = jnp.zeros_like(ds_s)

        def step(j, dstate):
            ci = (nc - 1 - j) + direction * (2 * j - (nc - 1))
            rows = pl.ds(pl.multiple_of(ci * c, c), c)
            args = (st_ref[0, ci], _dn_heads(qkv_ref, rows, 0, h, d), _dn_heads(qkv_ref, rows, 1, h, d),
                    _dn_heads(qkv_ref, rows, 2, h, d), bb_ref[0, :, rows, :], gc_ref[0, :, rows, :],
                    g64_ref[0, :, rows, :], gr_ref[0, :, ci])
            _, pull = jax.vjp(lambda *a: _dn_step(*a, direction), *args)
            do_c = jnp.stack([do_ref[rows, i * d:(i + 1) * d] for i in range(h)])
            dstate, dq, dk, dv, dbb, dgc, dg64, dgr = pull((dstate, do_c))
            for part, val in enumerate((dq, dk, dv)):
                for i in range(h):
                    dqkv_ref[0, rows, (part * h + i) * d:(part * h + i + 1) * d] = val[i]
            dbb_ref[0, :, rows, :] = dbb
            dgc_ref[0, :, rows, :] = dgc
            dg64_ref[0, :, rows, :] = dg64
            dgr_ref[0, :, ci] = dgr
            return dstate

        ds_s[...] = lax.fori_loop(0, nc, step, ds_s[...])

    blk = lambda dd, i: (nb - 1 - i) + dd * (2 * i - (nb - 1))
    ds = pl.BlockSpec((1, h, t, d), lambda dd, i: (dd, 0, blk(dd, i), 0))
    g64s = pl.BlockSpec((1, h, t, c), lambda dd, i: (dd, 0, blk(dd, i), 0))
    grs = pl.BlockSpec((1, h, nc, 1, c), lambda dd, i: (dd, 0, blk(dd, i), 0, 0))
    big = jax.ShapeDtypeStruct((2, h, lq, d), F32)
    return pl.pallas_call(
        body, name="deltanet_bwd", grid=(2, nb),
        in_specs=[pl.BlockSpec((t, 3 * h * d), lambda dd, i: (blk(dd, i), 0)), ds, ds, g64s, grs,
                  pl.BlockSpec((1, nc, h, d, d), lambda dd, i: (dd, blk(dd, i), 0, 0, 0)),
                  pl.BlockSpec((t, h * d), lambda dd, i: (blk(dd, i), 0))],
        out_shape=(jax.ShapeDtypeStruct((2, lq, 3 * h * d), F32), big, big, jax.ShapeDtypeStruct((2, h, lq, c), F32),
                   jax.ShapeDtypeStruct((2, h, lq // c, 1, c), F32)),
        out_specs=(pl.BlockSpec((1, t, 3 * h * d), lambda dd, i: (dd, blk(dd, i), 0)), ds, ds, g64s, grs),
        scratch_shapes=[pltpu.VMEM((h, d, d), F32)],
        compiler_params=pltpu.CompilerParams(dimension_semantics=("parallel", "arbitrary"), vmem_limit_bytes=VMEM_LIMIT),
    )(qkv, bb, gc, gc64, gr, states, do)


@jax.custom_vjp
def dn_rule(qkv, bb, gc, gc64, gr):
    o = _dn_fwd_call(qkv, bb, gc, gc64, gr)[0]
    return o[0] + o[1]


def _dn_rule_fwd(qkv, bb, gc, gc64, gr):
    o, states = _dn_fwd_call(qkv, bb, gc, gc64, gr)
    return o[0] + o[1], (qkv, bb, gc, gc64, gr, states)


def _dn_rule_bwd(res, do):
    dqkv, dbb, dgc, dg64, dgr = _dn_bwd_call(*res, do)
    return dqkv[0] + dqkv[1], dbb, dgc, dg64, dgr


dn_rule.defvjp(_dn_rule_fwd, _dn_rule_bwd)


def gated_delta_both(qkv, beta2, g2):
    _, h, lq = beta2.shape
    d = qkv.shape[1] // (3 * h)
    c = DN_CHUNK
    gch = g2.reshape(2, h, lq // c, c)
    cum = jnp.stack([jnp.cumsum(gch[0], axis=-1), lax.cumsum(gch[1], axis=2, reverse=True)])
    flat = cum.reshape(2, h, lq)
    return dn_rule(qkv, jnp.broadcast_to(beta2[..., None], (2, h, lq, d)), jnp.broadcast_to(flat[..., None], (2, h, lq, d)),
                   jnp.broadcast_to(flat[..., None], (2, h, lq, c)), cum.reshape(2, h, lq // c, 1, c))


def _s5_mixer(u, a_re, a_im, log_step, b_re, b_im, c_re, c_im, d, w_glu, b_glu):
    per = LANES // SSM_GROUP
    y = u * d
    for direction in range(2):
        are, aim = a_re[direction], a_im[direction]
        step = jnp.exp(log_step[direction])[:, None]
        mag = jnp.exp(are * step)
        lam_re = mag * jnp.cos(aim * step)
        lam_im = mag * jnp.sin(aim * step)
        den = are * are + aim * aim
        nr = lam_re - 1.0
        ni = lam_im
        coef_re = (nr * are + ni * aim) / den
        coef_im = (ni * are - nr * aim) / den
        bb_re = coef_re[..., None] * b_re[direction] - coef_im[..., None] * b_im[direction]
        bb_im = coef_re[..., None] * b_im[direction] + coef_im[..., None] * b_re[direction]
        bu_re = cmm(u, _cluster_diag(jnp.swapaxes(bb_re, 1, 2), per))
        bu_im = cmm(u, _cluster_diag(jnp.swapaxes(bb_im, 1, 2), per))
        width = bu_re.shape[1]
        s_re, s_im = cscan(lam_re.reshape(1, width), lam_im.reshape(1, width), bu_re, bu_im, direction == 1)
        y = y + cmm(s_re, _cluster_diag(jnp.swapaxes(c_re[direction], 1, 2), per))
        y = y - cmm(s_im, _cluster_diag(jnp.swapaxes(c_im[direction], 1, 2), per))
    y = jax.nn.gelu(y)
    return y * jax.nn.sigmoid(pmm(y, w_glu) + b_glu)


def _short_conv(x, w):
    ch = x.shape[-1]
    rhs = jnp.transpose(w)[:, None, :]
    return lax.conv_general_dilated(x[None], rhs, window_strides=(1,), padding=[(DN_CONV // 2, DN_CONV // 2)],
                                    dimension_numbers=("NWC", "WIO", "NWC"), feature_group_count=ch)[0]


def _deltanet_mixer(q, k, v, ab, conv_w, a_log, dt_bias, norm_g):
    seq = q.shape[0]
    qkv = _short_conv(jnp.concatenate([q, k, v], axis=-1), conv_w)
    a4 = ab[:, :2 * DN_HEADS].reshape(seq, 2, DN_HEADS)
    b4 = ab[:, 2 * DN_HEADS:4 * DN_HEADS].reshape(seq, 2, DN_HEADS)
    beta = jax.nn.sigmoid(b4)
    g = -jnp.exp(a_log) * jax.nn.softplus(a4 + dt_bias)
    o = gated_delta_both(qkv, jnp.transpose(beta, (1, 2, 0)), jnp.transpose(g, (1, 2, 0)))
    return prms(o.reshape(seq * DN_HEADS, DN_HEAD_DIM), norm_g).reshape(seq, DN_HEADS * DN_HEAD_DIM)


def _axial_rope(seq):
    rows = seq // GRID_W
    row = jnp.repeat(jnp.arange(rows), GRID_W).astype(F32)
    col = jnp.tile(jnp.arange(GRID_W), rows).astype(F32)
    axis_dim = ATT_HEAD_DIM // 2
    freqs = ROPE_THETA ** (-jnp.arange(0, axis_dim, 2, dtype=F32) / axis_dim)
    ang = jnp.concatenate([row[:, None] * freqs, col[:, None] * freqs], axis=-1)
    return jnp.cos(ang), jnp.sin(ang)


def _apply_rope(x, cos, sin):
    xp = x.reshape(x.shape[:-1] + (x.shape[-1] // 2, 2))
    x0, x1 = xp[..., 0], xp[..., 1]
    c = cos[:, None, :]
    s = sin[:, None, :]
    return jnp.stack([x0 * c - x1 * s, x0 * s + x1 * c], axis=-1).reshape(x.shape)


def _grid_attention(q, k, v, qn_g, kn_g, cos, sin):
    seq = q.shape[0]
    qh = prms(q.reshape(seq * ATT_HEADS, ATT_HEAD_DIM), qn_g).reshape(seq, ATT_HEADS, ATT_HEAD_DIM)
    kh = prms(k.reshape(seq * ATT_KV_HEADS, ATT_HEAD_DIM), kn_g).reshape(seq, ATT_KV_HEADS, ATT_HEAD_DIM)
    qh = _apply_rope(qh, cos, sin) * (ATT_HEAD_DIM ** -0.5)
    kh = _apply_rope(kh, cos, sin)
    return flash(qh.reshape(seq, -1), kh.reshape(seq, -1), v, ATT_HEAD_DIM, ATT_HEADS // ATT_KV_HEADS, 1.0)


def _memory_attention(q, mem_n, w_kv):
    kv = pmm(mem_n, w_kv)
    mem_w = MEM_HEADS * MEM_HEAD_DIM
    return flash(q, kv[:, :mem_w], kv[:, mem_w:], MEM_HEAD_DIM, 1, MEM_HEAD_DIM ** -0.5)


def _middle(h, x, mem, p):
    d_model = x.shape[1]
    lay, _, _ = _layout(d_model)

    def seg(name):
        _, dst, w, _ = lay[name]
        return h[:, dst:dst + w]

    cos, sin = _axial_rope(x.shape[0])
    y_a = _s5_mixer(seg("u"), p["ssm_a_re"], p["ssm_a_im"], p["ssm_log_step"], p["ssm_b_re"], p["ssm_b_im"],
                    p["ssm_c_re"], p["ssm_c_im"], p["ssm_d"], p["ssm_w_glu"], p["ssm_b_glu"]) * jax.nn.silu(seg("z_a"))
    y_b = _deltanet_mixer(seg("dq"), seg("dk"), seg("dv"), seg("dab"), p["dn_conv"], p["dn_a_log"],
                          p["dn_dt_bias"], p["dn_norm_g"]) * jax.nn.silu(seg("z_b"))
    y_c = _grid_attention(seg("aq"), seg("ak"), seg("av"), p["attn_q_norm"], p["attn_k_norm"], cos, sin) * jax.nn.silu(seg("z_c"))
    y_m = _memory_attention(seg("mq"), prms(mem, p["mem_norm_g"]), p["w_mem_kv"]) * jax.nn.silu(seg("z_m"))
    gates = jax.nn.sigmoid(seg("gates"))
    merged = jnp.zeros_like(x)
    off = 0
    for bi, y_br in enumerate((y_a, y_b, y_c, y_m)):
        w = y_br.shape[1]
        merged = merged + gates[:, bi * d_model:(bi + 1) * d_model] * pmm(y_br, p["w_branch"][off:off + w])
        off += w
    return x + pmm(merged, p["w_out"])


def _local_step(x, mem, target, w):
    depth = w["norm_g"].shape[0]
    d_model = x.shape[1]
    mid_names = [n for n in WEIGHTS if n not in ("norm_g", "w_in", "final_norm_g")]
    saved = []
    cur = x
    for layer in range(depth):
        xn = _rms_fwd_call(cur, w["norm_g"][layer], BF16)
        h = _mm(xn, w["w_in"][layer], "nn", "in_proj")
        p = {n: w[n][layer] for n in mid_names}
        nxt, vjp_mid = jax.vjp(lambda h_, x_, p_: _middle(h_, x_, mem, p_), h, cur, p)
        saved.append((cur, xn, vjp_mid))
        cur = nxt
    loss_parts, dx, d_final = _loss_call(cur, w["final_norm_g"], target)
    loss = jnp.sum(loss_parts[:, 0, 0])
    d_final = d_final.reshape(w["final_norm_g"].shape)
    grads = {n: [None] * depth for n in WEIGHTS if n != "final_norm_g"}
    for layer in reversed(range(depth)):
        x_in, xn, vjp_mid = saved[layer]
        dh, dx_skip, dp = vjp_mid(dx)
        grads["w_in"][layer] = _unpad_w_in(_mm(xn, dh, "tn", "in_proj_dw"), d_model)
        dxn = _mm(dh, w["w_in"][layer], "nt", "in_proj_dx")
        dx_n, dg = _rms_bwd_call(x_in, w["norm_g"][layer], dxn)
        dx = dx_skip + dx_n
        grads["norm_g"][layer] = dg.reshape(w["norm_g"][layer].shape)
        for n in mid_names:
            grads[n][layer] = dp[n]
    out = {n: jnp.stack(v) for n, v in grads.items()}
    out["final_norm_g"] = d_final
    return loss, dx, out


_ANY = pl.BlockSpec(memory_space=pl.ANY)
_CHIP_FLIPS = [(1, 0), (0, 1), (1, 1)]


def _remote(src, dst, send_sem, recv_sem, dev):
    return pltpu.make_async_remote_copy(src_ref=src, dst_ref=dst, send_sem=send_sem, recv_sem=recv_sem,
                                        device_id=dev, device_id_type=pl.DeviceIdType.MESH)


def _flip(v, bit):
    return 1 - v if bit else v


def _gather_chips(shards):
    n = len(shards)

    def body(*refs):
        ins, outs = refs[:n], refs[n:2 * n]
        send_sems, recv_sems, local_sems = refs[2 * n:]
        x, y, c = lax.axis_index("x"), lax.axis_index("y"), lax.axis_index("c")
        locals_, sends = [], []
        for a in range(n):
            cp = pltpu.make_async_copy(ins[a], outs[a].at[2 * x + y], local_sems.at[a])
            cp.start()
            locals_.append(cp)

        def half(a, px, py, pc):
            rh = shards[a].shape[1] // 2
            return outs[a].at[2 * px + py, :, pl.ds(pc * rh, rh), :]

        for a in range(n):
            rh = shards[a].shape[1] // 2
            for j, (fx, fy) in enumerate(_CHIP_FLIPS):
                cp = _remote(ins[a].at[:, pl.ds(c * rh, rh), :], half(a, x, y, c), send_sems.at[6 * a + j],
                             recv_sems.at[6 * a + j], (_flip(x, fx), _flip(y, fy), c))
                cp.start()
                sends.append(cp)
        for a in range(n):
            for j, (fx, fy) in enumerate(_CHIP_FLIPS):
                blk = half(a, _flip(x, fx), _flip(y, fy), c)
                _remote(blk, blk, send_sems.at[6 * a + j], recv_sems.at[6 * a + j], (x, y, c)).wait_recv()
                cp = _remote(blk, blk, send_sems.at[6 * a + 3 + j], recv_sems.at[6 * a + 3 + j], (x, y, 1 - c))
                cp.start()
                sends.append(cp)
        for a in range(n):
            for j, (fx, fy) in enumerate(_CHIP_FLIPS):
                blk = half(a, _flip(x, fx), _flip(y, fy), 1 - c)
                _remote(blk, blk, send_sems.at[6 * a + 3 + j], recv_sems.at[6 * a + 3 + j], (x, y, c)).wait_recv()
        for cp in sends:
            cp.wait_send()
        for cp in locals_:
            cp.wait()

    return pl.pallas_call(
        body, name="gather_weights",
        out_shape=tuple(jax.ShapeDtypeStruct((N_CHIPS,) + s.shape, s.dtype) for s in shards),
        in_specs=[_ANY] * n, out_specs=tuple([_ANY] * n),
        scratch_shapes=[pltpu.SemaphoreType.DMA((6 * n,)), pltpu.SemaphoreType.DMA((6 * n,)),
                        pltpu.SemaphoreType.DMA((n,))],
    )(*shards)


def _exchange_pair(parts):
    n = len(parts)

    def body(*refs):
        ins, outs = refs[:n], refs[n:2 * n]
        send_sems, recv_sems = refs[2 * n:]
        x, y, c = lax.axis_index("x"), lax.axis_index("y"), lax.axis_index("c")
        started = []
        for a in range(n):
            rh = parts[a].shape[2] // 2
            cp = _remote(ins[a].at[:, :, pl.ds((1 - c) * rh, rh), :], outs[a], send_sems.at[a], recv_sems.at[a], (x, y, 1 - c))
            cp.start()
            started.append(cp)
        for a in range(n):
            _remote(outs[a], outs[a], send_sems.at[a], recv_sems.at[a], (x, y, c)).wait_recv()
        for cp in started:
            cp.wait_send()

    return pl.pallas_call(
        body, name="exchange_pair",
        out_shape=tuple(jax.ShapeDtypeStruct((p.shape[0], p.shape[1], p.shape[2] // 2, p.shape[3]), p.dtype) for p in parts),
        in_specs=[_ANY] * n, out_specs=tuple([_ANY] * n),
        scratch_shapes=[pltpu.SemaphoreType.DMA((n,)), pltpu.SemaphoreType.DMA((n,))],
    )(*parts)


def _exchange_chips(pair_sums, small):
    n = len(pair_sums)

    def body(*refs):
        ins, small_in = refs[:n], refs[n]
        outs, small_out = refs[n + 1:2 * n + 1], refs[2 * n + 1]
        send_sems, recv_sems, small_send, small_recv, local_sem = refs[2 * n + 2:]
        x, y, c = lax.axis_index("x"), lax.axis_index("y"), lax.axis_index("c")
        me = 4 * x + 2 * y + c
        started = []
        own = pltpu.make_async_copy(small_in, small_out.at[me], local_sem)
        own.start()
        for f, (fx, fy) in enumerate(_CHIP_FLIPS):
            px, py = _flip(x, fx), _flip(y, fy)
            for a in range(n):
                cp = _remote(ins[a].at[2 * px + py], outs[a].at[f], send_sems.at[3 * a + f], recv_sems.at[3 * a + f], (px, py, c))
                cp.start()
                started.append(cp)
        for k in range(1, N_DEV):
            peer = (_flip(x, k >> 2 & 1), _flip(y, k >> 1 & 1), _flip(c, k & 1))
            cp = _remote(small_in, small_out.at[me], small_send.at[k - 1], small_recv.at[k - 1], peer)
            cp.start()
            started.append(cp)
        for f in range(len(_CHIP_FLIPS)):
            for a in range(n):
                _remote(outs[a].at[f], outs[a].at[f], send_sems.at[3 * a + f], recv_sems.at[3 * a + f], (x, y, c)).wait_recv()
        for k in range(1, N_DEV):
            peer = 4 * _flip(x, k >> 2 & 1) + 2 * _flip(y, k >> 1 & 1) + _flip(c, k & 1)
            _remote(small_out.at[peer], small_out.at[peer], small_send.at[k - 1], small_recv.at[k - 1], (x, y, c)).wait_recv()
        for cp in started:
            cp.wait_send()
        own.wait()

    shapes = tuple(jax.ShapeDtypeStruct((3,) + p.shape[1:], p.dtype) for p in pair_sums)
    shapes += (jax.ShapeDtypeStruct((N_DEV,) + small.shape, small.dtype),)
    return pl.pallas_call(
        body, name="exchange_chips", out_shape=shapes, in_specs=[_ANY] * (n + 1), out_specs=tuple([_ANY] * (n + 1)),
        scratch_shapes=[pltpu.SemaphoreType.DMA((3 * n,)), pltpu.SemaphoreType.DMA((3 * n,)),
                        pltpu.SemaphoreType.DMA((N_DEV - 1,)), pltpu.SemaphoreType.DMA((N_DEV - 1,)),
                        pltpu.SemaphoreType.DMA(())],
    )(*pair_sums, small)


def _swap_halves(arrays):
    n = len(arrays)

    def body(*refs):
        outs = refs[n:2 * n]
        send_sems, recv_sems = refs[2 * n:]
        x, y, c = lax.axis_index("x"), lax.axis_index("y"), lax.axis_index("c")
        started = []
        for a in range(n):
            rh = arrays[a].shape[1] // 2
            mine = outs[a].at[:, pl.ds(c * rh, rh), :]
            cp = _remote(mine, mine, send_sems.at[a], recv_sems.at[a], (x, y, 1 - c))
            cp.start()
            started.append(cp)
        for a in range(n):
            rh = arrays[a].shape[1] // 2
            theirs = outs[a].at[:, pl.ds((1 - c) * rh, rh), :]
            _remote(theirs, theirs, send_sems.at[a], recv_sems.at[a], (x, y, c)).wait_recv()
        for cp in started:
            cp.wait_send()

    return pl.pallas_call(
        body, name="swap_halves", out_shape=tuple(jax.ShapeDtypeStruct(t.shape, t.dtype) for t in arrays),
        in_specs=[_ANY] * n, out_specs=tuple([_ANY] * n), input_output_aliases={a: a for a in range(n)},
        scratch_shapes=[pltpu.SemaphoreType.DMA((n,)), pltpu.SemaphoreType.DMA((n,))],
    )(*arrays)


def _adamw_math(g, w, m, v):
    m = ADAM_B1 * m + (1.0 - ADAM_B1) * g
    v = ADAM_B2 * v + (1.0 - ADAM_B2) * jnp.square(g)
    m_hat = m / (1.0 - ADAM_B1 ** ADAM_STEP)
    v_hat = v / (1.0 - ADAM_B2 ** ADAM_STEP)
    delta = -ADAM_LR * (m_hat / (jnp.sqrt(v_hat) + ADAM_EPS) + ADAM_WD * w)
    return delta, m, v


def _row_tile(rows, row_bytes, budget, step=8):
    best = step
    for t in range(step, rows + 1, step):
        if rows % t == 0 and t * row_bytes <= budget:
            best = t
    return best


def _pair_sum(part, other, core):
    nj, a, r, c = part.shape
    rh = r // 2
    tr = _row_tile(rh, 4 * (-(-c // LANES) * LANES), 512 * 1024, 16)

    def body(core_ref, p_ref, o_ref, s32_ref, s16_ref):
        s = p_ref[0, 0, 0] + o_ref[0, 0]
        s32_ref[0, 0] = s
        s16_ref[0, 0] = s.astype(BF16)

    spec = pl.BlockSpec((1, 1, tr, c), lambda j, l, i, core_ref: (j, l, i, 0))
    return pl.pallas_call(
        body, name="pair_sum",
        out_shape=(jax.ShapeDtypeStruct(other.shape, F32), jax.ShapeDtypeStruct(other.shape, BF16)),
        grid_spec=pltpu.PrefetchScalarGridSpec(
            num_scalar_prefetch=1, grid=(nj, a, rh // tr),
            in_specs=[pl.BlockSpec((1, 1, 1, tr, c), lambda j, l, i, core_ref: (j, l, core_ref[0], i, 0)), spec],
            out_specs=(spec, spec)),
        compiler_params=pltpu.CompilerParams(dimension_semantics=("parallel", "parallel", "parallel"),
                                             vmem_limit_bytes=VMEM_LIMIT),
    )(core, part.reshape(nj, a, 2, rh, c), other)


def _sum_adamw_big(own, landed, w, m, v, place):
    _, a, rh, c = own.shape
    tr = _row_tile(rh, 4 * (-(-c // LANES) * LANES), 256 * 1024, 16)
    w4, m4, v4 = (t.reshape(a, 2, rh, c) for t in (w, m, v))

    def body(place_ref, q_ref, p_ref, w_ref, m_ref, v_ref, g_out, d_out, m_out, v_out):
        g = q_ref[0, 0]
        for f in range(3):
            g = g + p_ref[f, 0].astype(F32)
        delta, mm_, vv_ = _adamw_math(g, w_ref[0, 0], m_ref[0, 0], v_ref[0, 0])
        g_out[0, 0] = g
        d_out[0, 0] = delta
        m_out[0, 0] = mm_
        v_out[0, 0] = vv_

    wspec = pl.BlockSpec((1, 1, tr, c), lambda l, i, place_ref: (l, place_ref[0], i, 0))
    out = jax.ShapeDtypeStruct((a, 2, rh, c), F32)
    res = pl.pallas_call(
        body, name="sum_adamw_big", out_shape=(out,) * 4,
        grid_spec=pltpu.PrefetchScalarGridSpec(
            num_scalar_prefetch=1, grid=(a, rh // tr),
            in_specs=[pl.BlockSpec((1, 1, tr, c), lambda l, i, place_ref: (place_ref[1], l, i, 0)),
                      pl.BlockSpec((3, 1, tr, c), lambda l, i, place_ref: (0, l, i, 0)), wspec, wspec, wspec],
            out_specs=(wspec,) * 4),
        compiler_params=pltpu.CompilerParams(dimension_semantics=("parallel", "parallel"), vmem_limit_bytes=VMEM_LIMIT),
    )(place, own, landed, w4, m4, v4)
    return [t.reshape(a, 2 * rh, c) for t in res]


def _sum_small(parts):
    _, rows, _ = parts.shape
    tr = _row_tile(rows, 4 * LANES, 256 * 1024)

    def body(p_ref, o_ref):
        g = p_ref[0]
        for k in range(1, N_DEV):
            g = g + p_ref[k]
        o_ref[...] = g

    return pl.pallas_call(
        body, name="sum_small", out_shape=jax.ShapeDtypeStruct((rows, LANES), F32), grid=(rows // tr,),
        in_specs=[pl.BlockSpec((N_DEV, tr, LANES), lambda i: (0, i, 0))], out_specs=pl.BlockSpec((tr, LANES), lambda i: (i, 0)),
        compiler_params=pltpu.CompilerParams(dimension_semantics=("parallel",)),
    )(parts)


def _adamw_small(g, w, m, v):
    rows, _ = g.shape
    tr = _row_tile(rows, 4 * LANES, 256 * 1024)

    def body(g_ref, w_ref, m_ref, v_ref, d_out, m_out, v_out):
        d_out[...], m_out[...], v_out[...] = _adamw_math(g_ref[...], w_ref[...], m_ref[...], v_ref[...])

    spec = pl.BlockSpec((tr, LANES), lambda i: (i, 0))
    out = jax.ShapeDtypeStruct((rows, LANES), F32)
    return pl.pallas_call(
        body, name="adamw_small", out_shape=(out,) * 3, grid=(rows // tr,), in_specs=[spec] * 4, out_specs=(spec,) * 3,
        compiler_params=pltpu.CompilerParams(dimension_semantics=("parallel",)),
    )(g, w, m, v)


def _pack(arrays):
    flat = jnp.concatenate([a.reshape(-1) for a in arrays])
    rows = -(-flat.shape[0] // (8 * LANES)) * 8
    return jnp.pad(flat, (0, rows * LANES - flat.shape[0])).reshape(rows, LANES)


def _unpack(packed, shapes):
    flat, out, off = packed.reshape(-1), [], 0
    for s in shapes:
        size = math.prod(s)
        out.append(flat[off:off + size].reshape(s))
        off += size
    return out


def kernel(x, mem, norm_g, w_in, ssm_a_re, ssm_a_im, ssm_log_step, ssm_b_re, ssm_b_im, ssm_c_re, ssm_c_im, ssm_d, ssm_w_glu, ssm_b_glu, dn_conv, dn_a_log, dn_dt_bias, dn_norm_g, attn_q_norm, attn_k_norm, mem_norm_g, w_mem_kv, w_branch, w_out, final_norm_g, loss_target, m_norm_g, m_w_in, m_ssm_a_re, m_ssm_a_im, m_ssm_log_step, m_ssm_b_re, m_ssm_b_im, m_ssm_c_re, m_ssm_c_im, m_ssm_d, m_ssm_w_glu, m_ssm_b_glu, m_dn_conv, m_dn_a_log, m_dn_dt_bias, m_dn_norm_g, m_attn_q_norm, m_attn_k_norm, m_mem_norm_g, m_w_mem_kv, m_w_branch, m_w_out, m_final_norm_g, v_norm_g, v_w_in, v_ssm_a_re, v_ssm_a_im, v_ssm_log_step, v_ssm_b_re, v_ssm_b_im, v_ssm_c_re, v_ssm_c_im, v_ssm_d, v_ssm_w_glu, v_ssm_b_glu, v_dn_conv, v_dn_a_log, v_dn_dt_bias, v_dn_norm_g, v_attn_q_norm, v_attn_k_norm, v_mem_norm_g, v_w_mem_kv, v_w_branch, v_w_out, v_final_norm_g):
    args = locals()
    wts = {n: args[n] for n in WEIGHTS}
    mom = {n: args["m_" + n] for n in WEIGHTS}
    var = {n: args["v_" + n] for n in WEIGHTS}
    d_model = x.shape[-1]
    chip = 2 * lax.axis_index("x") + lax.axis_index("y")
    core = lax.axis_index("c")

    conv_rows = dn_conv.shape[1]
    conv_flat = _pack([dn_conv])
    gathered = _gather_chips([wts[n].astype(BF16) for n in BIG] + [conv_flat[None]])
    full = {n: wts[n] for n in SMALL}
    g_in = gathered[0]
    full["w_in"] = _pad_w_in(jnp.concatenate([g_in[j] for j in range(N_CHIPS)], axis=-1), d_model)
    for n, g in zip(BIG[1:], gathered[1:]):
        full[n] = jnp.concatenate([g[j] for j in range(N_CHIPS)], axis=1).astype(F32)
    conv_all = gathered[-1][:, 0].reshape(N_CHIPS, -1)[:, :dn_conv.size].reshape((N_CHIPS,) + dn_conv.shape)
    full["dn_conv"] = jnp.concatenate([conv_all[j] for j in range(N_CHIPS)], axis=1)

    loss, grad_x, grads = _local_step(x[0], mem[0], loss_target[0], full)
    loss = lax.psum(loss, ("x", "y", "c"))

    big_parts = []
    for n in BIG:
        g = grads[n]
        if n == "w_in":
            cols = g.shape[-1] // N_CHIPS
            big_parts.append(jnp.stack([g[..., j * cols:(j + 1) * cols] for j in range(N_CHIPS)]))
        else:
            rows = g.shape[1] // N_CHIPS
            big_parts.append(jnp.stack([g[:, j * rows:(j + 1) * rows] for j in range(N_CHIPS)]))
    core_arr = jnp.reshape(core, (1,)).astype(jnp.int32)
    place = jnp.stack([core, chip]).astype(jnp.int32)
    from_pair = _exchange_pair(big_parts)
    sums = [_pair_sum(p, o, core_arr) for p, o in zip(big_parts, from_pair)]
    small_shapes = [grads[n].shape for n in SMALL]
    landed = _exchange_chips([s[1] for s in sums], _pack([grads[n] for n in SMALL]))

    halves = []
    for n, s, parts in zip(BIG, sums, landed[:-1]):
        halves.extend(_sum_adamw_big(s[0], parts, wts[n], mom[n], var[n], place))
    swapped = _swap_halves(halves)
    res = {}
    for i, n in enumerate(BIG):
        res[n] = swapped[4 * i:4 * i + 4]
    small_g = _unpack(_sum_small(landed[-1]), small_shapes)
    small_g = dict(zip(SMALL, small_g))
    small_g["dn_conv"] = lax.dynamic_slice_in_dim(small_g["dn_conv"], chip * conv_rows, conv_rows, axis=1)
    shapes = [small_g[n].shape for n in SMALL]
    upd = _adamw_small(_pack([small_g[n] for n in SMALL]), _pack([wts[n] for n in SMALL]),
                       _pack([mom[n] for n in SMALL]), _pack([var[n] for n in SMALL]))
    upd = [_unpack(u, shapes) for u in upd]
    for i, n in enumerate(SMALL):
        res[n] = (small_g[n], upd[0][i], upd[1][i], upd[2][i])

    out = [loss, grad_x[None]]
    for kind in range(4):
        out.extend(res[n][kind] for n in WEIGHTS)
    return tuple(out)
```

```python
import functools
import math

import jax
import jax.numpy as jnp
from jax import lax
from jax.experimental import pallas as pl
from jax.experimental.pallas import tpu as pltpu

F32 = jnp.float32
BF16 = jnp.bfloat16

GRID_W = 64
EPS = 1e-6
SSM_GROUP = 16
SSM_STATE = 64
SSM_GROUPS = 48
DN_HEADS = 6
DN_HEAD_DIM = 128
DN_CONV = 5
DN_CHUNK = 64
ATT_HEADS = 8
ATT_KV_HEADS = 2
ATT_HEAD_DIM = 128
ROPE_THETA = 10000.0
MEM_HEADS = 4
MEM_HEAD_DIM = 128
N_BRANCH = 4

ADAM_LR = 0.001
ADAM_B1 = 0.9
ADAM_B2 = 0.999
ADAM_EPS = 1e-08
ADAM_WD = 0.01
ADAM_STEP = 10

N_CHIPS = 4
N_DEV = 8
LANES = 128
VMEM_LIMIT = 48 * 1024 * 1024
IN_PAD_UNIT = 512
MM_FULL_K = 2048
CONV_HALO = 8
ATT_HEADS_PER_STEP = 4

WEIGHTS = ['norm_g', 'w_in', 'ssm_a_re', 'ssm_a_im', 'ssm_log_step', 'ssm_b_re', 'ssm_b_im', 'ssm_c_re',
           'ssm_c_im', 'ssm_d', 'ssm_w_glu', 'ssm_b_glu', 'dn_conv', 'dn_a_log', 'dn_dt_bias', 'dn_norm_g',
           'attn_q_norm', 'attn_k_norm', 'mem_norm_g', 'w_mem_kv', 'w_branch', 'w_out', 'final_norm_g']
BIG = ['w_in', 'ssm_w_glu', 'w_mem_kv', 'w_branch', 'w_out']
SMALL = [n for n in WEIGHTS if n not in BIG]


def _pick(dim, cands):
    for c in cands:
        if dim % c == 0:
            return c
    return dim


def _widths(d_model):
    ssm_w = SSM_GROUPS * SSM_GROUP
    dn_w = DN_HEADS * DN_HEAD_DIM
    att_w = ATT_HEADS * ATT_HEAD_DIM
    kv_w = ATT_KV_HEADS * ATT_HEAD_DIM
    mem_w = MEM_HEADS * MEM_HEAD_DIM
    return [("u", ssm_w), ("z_a", ssm_w), ("dq", dn_w), ("dk", dn_w), ("dv", dn_w), ("dab", 4 * DN_HEADS),
            ("z_b", dn_w), ("aq", att_w), ("ak", kv_w), ("av", kv_w), ("z_c", att_w), ("mq", mem_w),
            ("z_m", mem_w), ("gates", N_BRANCH * d_model)]


def _layout(d_model):
    out, src, dst = {}, 0, 0
    for name, w in _widths(d_model):
        pw = -(-w // IN_PAD_UNIT) * IN_PAD_UNIT if w % LANES else w
        out[name] = (src, dst, w, pw)
        src += w
        dst += pw
    return out, src, dst


def _pad_w_in(full, d_model):
    lay, _, _ = _layout(d_model)
    parts = []
    for name, _ in _widths(d_model):
        s, _, w, pw = lay[name]
        seg = full[..., s:s + w]
        if pw != w:
            seg = jnp.pad(seg, [(0, 0)] * (full.ndim - 1) + [(0, pw - w)])
        parts.append(seg)
    return jnp.concatenate(parts, axis=-1)


def _unpad_w_in(padded, d_model):
    lay, _, _ = _layout(d_model)
    return jnp.concatenate([padded[..., lay[n][1]:lay[n][1] + lay[n][2]] for n, _ in _widths(d_model)], axis=-1)


def _mm(a, b, mode="nn", name="mm"):
    if mode == "nn":
        (m, k), (k2, n) = a.shape, b.shape
    elif mode == "nt":
        (m, k), (n, k2) = a.shape, b.shape
    else:
        (k, m), (k2, n) = a.shape, b.shape
    assert k == k2, (a.shape, b.shape, mode)
    tk = k if k <= MM_FULL_K else _pick(k, (1024, 512, 256, 128))
    tm = _pick(m, (2048, 1024, 512, 256, 128) if mode == "tn" else (1024, 512, 256, 128))
    tn = _pick(n, (512, 384, 256, 128) if tk > 1024 or tm > 1024 else (1024, 512, 384, 256, 128))
    nk = k // tk
    dims = {"nn": (((1,), (0,)), ((), ())), "nt": (((1,), (1,)), ((), ())), "tn": (((0,), (0,)), ((), ()))}[mode]

    def body(a_ref, b_ref, o_ref, *scratch):
        prod = lax.dot_general(a_ref[...].astype(BF16), b_ref[...].astype(BF16), dims, preferred_element_type=F32)
        if nk == 1:
            o_ref[...] = prod
            return
        acc_ref, = scratch
        kk = pl.program_id(2)

        @pl.when(kk == 0)
        def _():
            acc_ref[...] = prod

        @pl.when(kk > 0)
        def _():
            acc_ref[...] += prod

        @pl.when(kk == nk - 1)
        def _():
            o_ref[...] = acc_ref[...]

    a_spec = pl.BlockSpec((tk, tm), lambda i, j, kk: (kk, i)) if mode == "tn" else pl.BlockSpec((tm, tk), lambda i, j, kk: (i, kk))
    b_spec = pl.BlockSpec((tn, tk), lambda i, j, kk: (j, kk)) if mode == "nt" else pl.BlockSpec((tk, tn), lambda i, j, kk: (kk, j))
    return pl.pallas_call(
        body, name=name, out_shape=jax.ShapeDtypeStruct((m, n), F32), grid=(m // tm, n // tn, nk),
        in_specs=[a_spec, b_spec], out_specs=pl.BlockSpec((tm, tn), lambda i, j, kk: (i, j)),
        scratch_shapes=[pltpu.VMEM((tm, tn), F32)] if nk > 1 else [],
        compiler_params=pltpu.CompilerParams(dimension_semantics=("parallel", "parallel", "arbitrary"),
                                             vmem_limit_bytes=VMEM_LIMIT),
    )(a, b)


@jax.custom_vjp
def pmm(a, b):
    return _mm(a, b, "nn", "pmm_fwd")


def _pmm_fwd(a, b):
    return _mm(a, b, "nn", "pmm_fwd"), (a, b)


def _pmm_bwd(res, dc):
    a, b = res
    return _mm(dc, b, "nt", "pmm_da"), _mm(a, dc, "tn", "pmm_db")


pmm.defvjp(_pmm_fwd, _pmm_bwd)


def _cmm_call(a, w):
    m = a.shape[0]
    nc, kc, nn = w.shape
    tm = _pick(m, (1024, 512, 256, 128))

    def body(a_ref, w_ref, o_ref):
        o_ref[...] = lax.dot_general(a_ref[...].astype(BF16), w_ref[0].astype(BF16), (((1,), (0,)), ((), ())),
                                     preferred_element_type=F32)

    return pl.pallas_call(
        body, name="cmm", out_shape=jax.ShapeDtypeStruct((m, nc * nn), F32), grid=(m // tm, nc),
        in_specs=[pl.BlockSpec((tm, kc), lambda i, c: (i, c)), pl.BlockSpec((1, kc, nn), lambda i, c: (c, 0, 0))],
        out_specs=pl.BlockSpec((tm, nn), lambda i, c: (i, c)),
        compiler_params=pltpu.CompilerParams(dimension_semantics=("parallel", "parallel"), vmem_limit_bytes=VMEM_LIMIT),
    )(a, w)


def _cmm_dw_call(a, dc, kc, nn):
    m = a.shape[0]
    nc = a.shape[1] // kc
    tm = _pick(m, (1024, 512, 256, 128))
    nm = m // tm

    def body(a_ref, d_ref, o_ref, acc_s):
        i = pl.program_id(1)

        @pl.when(i == 0)
        def _():
            acc_s[...] = jnp.zeros_like(acc_s)

        acc_s[...] += lax.dot_general(a_ref[...].astype(BF16), d_ref[...].astype(BF16), (((0,), (0,)), ((), ())),
                                      preferred_element_type=F32)

        @pl.when(i == nm - 1)
        def _():
            o_ref[0] = acc_s[...]

    return pl.pallas_call(
        body, name="cmm_dw", out_shape=jax.ShapeDtypeStruct((nc, kc, nn), F32), grid=(nc, nm),
        in_specs=[pl.BlockSpec((tm, kc), lambda c, i: (i, c)), pl.BlockSpec((tm, nn), lambda c, i: (i, c))],
        out_specs=pl.BlockSpec((1, kc, nn), lambda c, i: (c, 0, 0)),
        scratch_shapes=[pltpu.VMEM((kc, nn), F32)],
        compiler_params=pltpu.CompilerParams(dimension_semantics=("parallel", "arbitrary"), vmem_limit_bytes=VMEM_LIMIT),
    )(a, dc)


@jax.custom_vjp
def cmm(a, w):
    return _cmm_call(a, w)


def _cmm_fwd(a, w):
    return _cmm_call(a, w), (a, w)


def _cmm_bwd(res, dc):
    a, w = res
    return _cmm_call(dc, jnp.swapaxes(w, 1, 2)), _cmm_dw_call(a, dc, w.shape[1], w.shape[2])


cmm.defvjp(_cmm_fwd, _cmm_bwd)


def _cluster_diag(w, per):
    g, a, b = w.shape
    eye = jnp.eye(per, dtype=w.dtype)
    wc = w.reshape(g // per, per, a, b)
    return (wc[:, :, :, None, :] * eye[None, :, None, :, None]).reshape(g // per, per * a, per * b)


def _norm_rows(rows, d):
    return _row_tile(rows, 4 * d, 1024 * 1024)


def _rms_fwd_call(x, g, out_dtype=F32):
    rows, d = x.shape
    tr = _norm_rows(rows, d)

    def body(x_ref, g_ref, o_ref):
        xv = x_ref[...]
        o_ref[...] = (xv * lax.rsqrt(jnp.mean(xv * xv, axis=1, keepdims=True) + EPS) * g_ref[...]).astype(out_dtype)

    return pl.pallas_call(
        body, name="rms_fwd", out_shape=jax.ShapeDtypeStruct((rows, d), out_dtype), grid=(rows // tr,),
        in_specs=[pl.BlockSpec((tr, d), lambda i: (i, 0)), pl.BlockSpec((1, d), lambda i: (0, 0))],
        out_specs=pl.BlockSpec((tr, d), lambda i: (i, 0)),
        compiler_params=pltpu.CompilerParams(dimension_semantics=("parallel",), vmem_limit_bytes=VMEM_LIMIT),
    )(x, g.reshape(1, d))


def _rms_bwd_call(x, g, dy):
    rows, d = x.shape
    tr = _norm_rows(rows, d)

    def body(x_ref, g_ref, dy_ref, dx_ref, dg_ref):
        @pl.when(pl.program_id(0) == 0)
        def _():
            dg_ref[...] = jnp.zeros_like(dg_ref)

        xv = x_ref[...]
        r = lax.rsqrt(jnp.mean(xv * xv, axis=1, keepdims=True) + EPS)
        xh = xv * r
        dyv = dy_ref[...]
        dxh = dyv * g_ref[...]
        dx_ref[...] = r * (dxh - xh * jnp.mean(dxh * xh, axis=1, keepdims=True))
        dg_ref[...] += jnp.sum(dyv * xh, axis=0, keepdims=True)

    return pl.pallas_call(
        body, name="rms_bwd", out_shape=(jax.ShapeDtypeStruct((rows, d), F32), jax.ShapeDtypeStruct((1, d), F32)),
        grid=(rows // tr,),
        in_specs=[pl.BlockSpec((tr, d), lambda i: (i, 0)), pl.BlockSpec((1, d), lambda i: (0, 0)),
                  pl.BlockSpec((tr, d), lambda i: (i, 0))],
        out_specs=(pl.BlockSpec((tr, d), lambda i: (i, 0)), pl.BlockSpec((1, d), lambda i: (0, 0))),
        compiler_params=pltpu.CompilerParams(dimension_semantics=("arbitrary",), vmem_limit_bytes=VMEM_LIMIT),
    )(x, g.reshape(1, d), dy)


@jax.custom_vjp
def prms(x, g):
    return _rms_fwd_call(x, g)


def _prms_fwd(x, g):
    return _rms_fwd_call(x, g), (x, g)


def _prms_bwd(res, dy):
    x, g = res
    dx, dg = _rms_bwd_call(x, g, dy)
    return dx, dg.reshape(g.shape)


prms.defvjp(_prms_fwd, _prms_bwd)


def _pair_swap(v):
    lane = lax.broadcasted_iota(jnp.int32, v.shape, 1)
    return jnp.where(lane % 2 == 0, pltpu.roll(v, v.shape[1] - 1, 1), pltpu.roll(v, 1, 1))


def _hnorm_fwd_call(x, g, cos, sin, scale):
    rows, width = x.shape
    d = g.shape[-1]
    h = width // d
    rope = cos is not None
    tr = _row_tile(rows, 4 * width, 1024 * 1024)

    def body(*refs):
        if rope:
            x_ref, g_ref, c_ref, s_ref, o_ref = refs
        else:
            x_ref, g_ref, o_ref = refs
        for i in range(h):
            cols = slice(i * d, (i + 1) * d)
            xv = x_ref[:, cols]
            n = xv * lax.rsqrt(jnp.mean(xv * xv, axis=1, keepdims=True) + EPS) * g_ref[...]
            if rope:
                n = (c_ref[...] * n + s_ref[...] * _pair_swap(n)) * scale
            o_ref[:, cols] = n

    row_spec = pl.BlockSpec((tr, width), lambda i: (i, 0))
    tab_spec = pl.BlockSpec((tr, d), lambda i: (i, 0))
    ins = [x, g.reshape(1, d)] + ([cos, sin] if rope else [])
    return pl.pallas_call(
        body, name="head_norm_fwd", out_shape=jax.ShapeDtypeStruct((rows, width), F32), grid=(rows // tr,),
        in_specs=[row_spec, pl.BlockSpec((1, d), lambda i: (0, 0))] + ([tab_spec, tab_spec] if rope else []),
        out_specs=row_spec,
        compiler_params=pltpu.CompilerParams(dimension_semantics=("parallel",), vmem_limit_bytes=VMEM_LIMIT),
    )(*ins)


def _hnorm_bwd_call(x, g, cos, sin, scale, dy):
    rows, width = x.shape
    d = g.shape[-1]
    h = width // d
    rope = cos is not None
    tr = _row_tile(rows, 4 * width, 1024 * 1024)

    def body(*refs):
        if rope:
            x_ref, g_ref, c_ref, s_ref, dy_ref, dx_ref, dg_ref = refs
        else:
            x_ref, g_ref, dy_ref, dx_ref, dg_ref = refs

        @pl.when(pl.program_id(0) == 0)
        def _():
            dg_ref[...] = jnp.zeros_like(dg_ref)

        dg = jnp.zeros((1, d), F32)
        for i in range(h):
            cols = slice(i * d, (i + 1) * d)
            xv = x_ref[:, cols]
            r = lax.rsqrt(jnp.mean(xv * xv, axis=1, keepdims=True) + EPS)
            xh = xv * r
            dn = dy_ref[:, cols]
            if rope:
                dn = dn * scale
                dn = c_ref[...] * dn + _pair_swap(s_ref[...] * dn)
            dxh = dn * g_ref[...]
            dx_ref[:, cols] = r * (dxh - xh * jnp.mean(dxh * xh, axis=1, keepdims=True))
            dg = dg + jnp.sum(dn * xh, axis=0, keepdims=True)
        dg_ref[...] += dg

    row_spec = pl.BlockSpec((tr, width), lambda i: (i, 0))
    tab_spec = pl.BlockSpec((tr, d), lambda i: (i, 0))
    vec_spec = pl.BlockSpec((1, d), lambda i: (0, 0))
    ins = [x, g.reshape(1, d)] + ([cos, sin] if rope else []) + [dy]
    return pl.pallas_call(
        body, name="head_norm_bwd", out_shape=(jax.ShapeDtypeStruct((rows, width), F32), jax.ShapeDtypeStruct((1, d), F32)),
        grid=(rows // tr,),
        in_specs=[row_spec, vec_spec] + ([tab_spec, tab_spec] if rope else []) + [row_spec],
        out_specs=(row_spec, vec_spec),
        compiler_params=pltpu.CompilerParams(dimension_semantics=("arbitrary",), vmem_limit_bytes=VMEM_LIMIT),
    )(*ins)


@jax.custom_vjp
def head_norm(x, g):
    return _hnorm_fwd_call(x, g, None, None, 1.0)


def _head_norm_fwd(x, g):
    return _hnorm_fwd_call(x, g, None, None, 1.0), (x, g)


def _head_norm_bwd(res, dy):
    x, g = res
    dx, dg = _hnorm_bwd_call(x, g, None, None, 1.0, dy)
    return dx, dg.reshape(g.shape)


head_norm.defvjp(_head_norm_fwd, _head_norm_bwd)


@functools.partial(jax.custom_vjp, nondiff_argnums=(4,))
def head_norm_rope(x, g, cos, sin, scale):
    return _hnorm_fwd_call(x, g, cos, sin, scale)


def _head_norm_rope_fwd(x, g, cos, sin, scale):
    return _hnorm_fwd_call(x, g, cos, sin, scale), (x, g, cos, sin)


def _head_norm_rope_bwd(scale, res, dy):
    x, g, cos, sin = res
    dx, dg = _hnorm_bwd_call(x, g, cos, sin, scale, dy)
    return dx, dg.reshape(g.shape), jnp.zeros_like(cos), jnp.zeros_like(sin)


head_norm_rope.defvjp(_head_norm_rope_fwd, _head_norm_rope_bwd)


def _loss_call(x, g, target):
    rows, d = x.shape
    tr = _norm_rows(rows, d)
    nb = rows // tr

    def body(x_ref, g_ref, t_ref, l_ref, dx_ref, dg_ref):
        @pl.when(pl.program_id(0) == 0)
        def _():
            dg_ref[...] = jnp.zeros_like(dg_ref)

        xv = x_ref[...]
        r = lax.rsqrt(jnp.mean(xv * xv, axis=1, keepdims=True) + EPS)
        xh = xv * r
        err = xh * g_ref[...] - t_ref[...]
        l_ref[...] = jnp.broadcast_to(0.5 * jnp.sum(jnp.mean(err * err, axis=1, keepdims=True), axis=0, keepdims=True)[None],
                                      l_ref.shape)
        dyv = err * (1.0 / d)
        dxh = dyv * g_ref[...]
        dx_ref[...] = r * (dxh - xh * jnp.mean(dxh * xh, axis=1, keepdims=True))
        dg_ref[...] += jnp.sum(dyv * xh, axis=0, keepdims=True)

    return pl.pallas_call(
        body, name="final_loss",
        out_shape=(jax.ShapeDtypeStruct((nb, 1, LANES), F32), jax.ShapeDtypeStruct((rows, d), F32),
                   jax.ShapeDtypeStruct((1, d), F32)),
        grid=(nb,),
        in_specs=[pl.BlockSpec((tr, d), lambda i: (i, 0)), pl.BlockSpec((1, d), lambda i: (0, 0)),
                  pl.BlockSpec((tr, d), lambda i: (i, 0))],
        out_specs=(pl.BlockSpec((1, 1, LANES), lambda i: (i, 0, 0)), pl.BlockSpec((tr, d), lambda i: (i, 0)),
                   pl.BlockSpec((1, d), lambda i: (0, 0))),
        compiler_params=pltpu.CompilerParams(dimension_semantics=("arbitrary",), vmem_limit_bytes=VMEM_LIMIT),
    )(x, g.reshape(1, d), target)


_NT = (((1,), (1,)), ((), ()))
_NN = (((1,), (0,)), ((), ()))
_TN = (((0,), (0,)), ((), ()))


def _att_plan(lq, lk, hq, group):
    hp = min(ATT_HEADS_PER_STEP, hq)
    assert hq % hp == 0 and (hp % group == 0 or group % hp == 0)
    kvp = max(1, hp // group)
    return _pick(lq, (256, 128)), _pick(lk, (1024, 512, 256, 128)), hp, kvp


def _att_specs(bq, bk, dh, hp, kvp, group, q_map, k_map):
    qs = pl.BlockSpec((bq, hp * dh), lambda *g: q_map(*g))
    ls = pl.BlockSpec((bq, hp * LANES), lambda *g: q_map(*g))
    ks = pl.BlockSpec((bk, kvp * dh), lambda *g: (k_map(*g)[0], k_map(*g)[1] * hp // (group * kvp)))
    return qs, ls, ks


def _flash_fwd(q, k, v, dh, group, scale):
    lq, lk = q.shape[0], k.shape[0]
    hq = q.shape[1] // dh
    bq, bk, hp, kvp = _att_plan(lq, lk, hq, group)
    nkv = lk // bk

    def body(q_ref, k_ref, v_ref, o_ref, lse_ref, m_s, l_s, acc_s):
        j = pl.program_id(2)

        @pl.when(j == 0)
        def _():
            m_s[...] = jnp.full_like(m_s, -jnp.inf)
            l_s[...] = jnp.zeros_like(l_s)
            acc_s[...] = jnp.zeros_like(acc_s)

        for h in range(hp):
            hk = h // group if kvp > 1 else 0
            cols, lcols, kcols = slice(h * dh, (h + 1) * dh), slice(h * LANES, (h + 1) * LANES), slice(hk * dh, (hk + 1) * dh)
            s = lax.dot_general(q_ref[:, cols], k_ref[:, kcols], _NT, preferred_element_type=F32)
            if scale != 1.0:
                s = s * scale
            m_prev = m_s[:, lcols]
            m_cur = jnp.maximum(m_prev, jnp.max(s, axis=1, keepdims=True))
            alpha = jnp.exp(m_prev - m_cur)
            p = jnp.exp(s - m_cur[:, :1])
            l_s[:, lcols] = alpha * l_s[:, lcols] + jnp.sum(p, axis=1, keepdims=True)
            acc_s[:, cols] = acc_s[:, cols] * alpha[:, :1] + lax.dot_general(p.astype(BF16), v_ref[:, kcols], _NN,
                                                                           preferred_element_type=F32)
            m_s[:, lcols] = m_cur

        @pl.when(j == nkv - 1)
        def _():
            for h in range(hp):
                cols, lcols = slice(h * dh, (h + 1) * dh), slice(h * LANES, (h + 1) * LANES)
                o_ref[:, cols] = acc_s[:, cols] / l_s[:, lcols][:, :1]
                lse_ref[:, lcols] = m_s[:, lcols] + jnp.log(l_s[:, lcols])

    qs, ls, ks = _att_specs(bq, bk, dh, hp, kvp, group, lambda h, i, j: (i, h), lambda h, i, j: (j, h))
    return pl.pallas_call(
        body, name="flash_fwd",
        out_shape=(jax.ShapeDtypeStruct((lq, hq * dh), F32), jax.ShapeDtypeStruct((lq, hq * LANES), F32)),
        grid=(hq // hp, lq // bq, nkv), in_specs=[qs, ks, ks], out_specs=(qs, ls),
        scratch_shapes=[pltpu.VMEM((bq, hp * LANES), F32), pltpu.VMEM((bq, hp * LANES), F32), pltpu.VMEM((bq, hp * dh), F32)],
        compiler_params=pltpu.CompilerParams(dimension_semantics=("parallel", "parallel", "arbitrary"),
                                             vmem_limit_bytes=VMEM_LIMIT),
    )(q, k, v)


def _flash_dq(q, k, v, o, lse, do, dh, group, scale):
    lq, lk = q.shape[0], k.shape[0]
    hq = q.shape[1] // dh
    bq, bk, hp, kvp = _att_plan(lq, lk, hq, group)
    nkv = lk // bk

    def body(q_ref, k_ref, v_ref, o_ref, lse_ref, do_ref, dq_ref, dob_ref, dl_ref, acc_s):
        j = pl.program_id(2)

        @pl.when(j == 0)
        def _():
            acc_s[...] = jnp.zeros_like(acc_s)
            dob_ref[...] = do_ref[...].astype(BF16)
            for h in range(hp):
                cols, lcols = slice(h * dh, (h + 1) * dh), slice(h * LANES, (h + 1) * LANES)
                dl_ref[:, lcols] = jnp.broadcast_to(jnp.sum(do_ref[:, cols] * o_ref[:, cols], axis=1, keepdims=True),
                                                    (bq, LANES))

        for h in range(hp):
            hk = h // group if kvp > 1 else 0
            cols, lcols, kcols = slice(h * dh, (h + 1) * dh), slice(h * LANES, (h + 1) * LANES), slice(hk * dh, (hk + 1) * dh)
            kb = k_ref[:, kcols]
            s = lax.dot_general(q_ref[:, cols], kb, _NT, preferred_element_type=F32)
            if scale != 1.0:
                s = s * scale
            p = jnp.exp(s - lse_ref[:, lcols][:, :1])
            dp = lax.dot_general(dob_ref[:, cols], v_ref[:, kcols], _NT, preferred_element_type=F32)
            ds = p * (dp - dl_ref[:, lcols][:, :1])
            if scale != 1.0:
                ds = ds * scale
            acc_s[:, cols] += lax.dot_general(ds.astype(BF16), kb, _NN, preferred_element_type=F32)

        @pl.when(j == nkv - 1)
        def _():
            dq_ref[...] = acc_s[...]

    qs, ls, ks = _att_specs(bq, bk, dh, hp, kvp, group, lambda h, i, j: (i, h), lambda h, i, j: (j, h))
    return pl.pallas_call(
        body, name="flash_dq",
        out_shape=(jax.ShapeDtypeStruct((lq, hq * dh), F32), jax.ShapeDtypeStruct((lq, hq * dh), BF16),
                   jax.ShapeDtypeStruct((lq, hq * LANES), F32)),
        grid=(hq // hp, lq // bq, nkv), in_specs=[qs, ks, ks, qs, ls, qs], out_specs=(qs, qs, ls),
        scratch_shapes=[pltpu.VMEM((bq, hp * dh), F32)],
        compiler_params=pltpu.CompilerParams(dimension_semantics=("parallel", "parallel", "arbitrary"),
                                             vmem_limit_bytes=VMEM_LIMIT),
    )(q, k, v, o, lse, do)


def _flash_dkv(q, k, v, lse, dob, delta, dh, group, scale):
    lq, lk = q.shape[0], k.shape[0]
    hq = q.shape[1] // dh
    bq, bk, hp, kvp = _att_plan(lq, lk, hq, group)
    nq = lq // bq
    reps = max(1, group // hp)
    nt = reps * nq

    def body(q_ref, k_ref, v_ref, lse_ref, dob_ref, dl_ref, dk_ref, dv_ref, dk_s, dv_s):
        t = pl.program_id(2)

        @pl.when(t == 0)
        def _():
            dk_s[...] = jnp.zeros_like(dk_s)
            dv_s[...] = jnp.zeros_like(dv_s)

        for h in range(hp):
            hk = h // group if kvp > 1 else 0
            cols, lcols, kcols = slice(h * dh, (h + 1) * dh), slice(h * LANES, (h + 1) * LANES), slice(hk * dh, (hk + 1) * dh)
            qb = q_ref[:, cols]
            dob = dob_ref[:, cols]
            s = lax.dot_general(qb, k_ref[:, kcols], _NT, preferred_element_type=F32)
            if scale != 1.0:
                s = s * scale
            p = jnp.exp(s - lse_ref[:, lcols][:, :1])
            dp = lax.dot_general(dob, v_ref[:, kcols], _NT, preferred_element_type=F32)
            ds = p * (dp - dl_ref[:, lcols][:, :1])
            if scale != 1.0:
                ds = ds * scale
            dv_s[:, kcols] += lax.dot_general(p.astype(BF16), dob, _TN, preferred_element_type=F32)
            dk_s[:, kcols] += lax.dot_general(ds.astype(BF16), qb, _TN, preferred_element_type=F32)

        @pl.when(t == nt - 1)
        def _():
            dk_ref[...] = dk_s[...]
            dv_ref[...] = dv_s[...]

    hkv_blocks = (hq // group) // kvp
    qs = pl.BlockSpec((bq, hp * dh), lambda h, j, t: (t % nq, h * reps + t // nq))
    ls = pl.BlockSpec((bq, hp * LANES), lambda h, j, t: (t % nq, h * reps + t // nq))
    ks = pl.BlockSpec((bk, kvp * dh), lambda h, j, t: (j, h))
    return pl.pallas_call(
        body, name="flash_dkv",
        out_shape=(jax.ShapeDtypeStruct(k.shape, F32), jax.ShapeDtypeStruct(v.shape, F32)),
        grid=(hkv_blocks, lk // bk, nt), in_specs=[qs, ks, ks, ls, qs, ls], out_specs=(ks, ks),
        scratch_shapes=[pltpu.VMEM((bk, kvp * dh), F32), pltpu.VMEM((bk, kvp * dh), F32)],
        compiler_params=pltpu.CompilerParams(dimension_semantics=("parallel", "parallel", "arbitrary"),
                                             vmem_limit_bytes=VMEM_LIMIT),
    )(q, k, v, lse, dob, delta)


@functools.partial(jax.custom_vjp, nondiff_argnums=(3, 4, 5))
def flash(q, k, v, dh, group, scale):
    return _flash_fwd(q.astype(BF16), k.astype(BF16), v.astype(BF16), dh, group, scale)[0]


def _flash_vfwd(q, k, v, dh, group, scale):
    qb, kb, vb = q.astype(BF16), k.astype(BF16), v.astype(BF16)
    o, lse = _flash_fwd(qb, kb, vb, dh, group, scale)
    return o, (qb, kb, vb, o, lse)


def _flash_vbwd(dh, group, scale, res, do):
    q, k, v, o, lse = res
    dq, dob, delta = _flash_dq(q, k, v, o, lse, do, dh, group, scale)
    dk, dv = _flash_dkv(q, k, v, lse, dob, delta, dh, group, scale)
    return dq, dk, dv


flash.defvjp(_flash_vfwd, _flash_vbwd)


def _scan_call(lr, li, br, bi, prev_r, prev_i, reverse, name):
    rows, width = br.shape
    ts = _pick(width, (1024, 512, 256, 128))
    tr = _pick(rows, (256, 128, 64, 32, 16, 8))
    nb = rows // tr
    with_prev = prev_r is not None

    def body(*refs):
        if with_prev:
            lr_ref, li_ref, br_ref, bi_ref, pr_ref, pi_ref, sr_ref, si_ref, dr_ref, di_ref, cr_s, ci_s = refs
        else:
            lr_ref, li_ref, br_ref, bi_ref, sr_ref, si_ref, pr_ref, pi_ref, cr_s, ci_s = refs
        i = pl.program_id(1)

        @pl.when(i == 0)
        def _():
            cr_s[...] = jnp.zeros_like(cr_s)
            ci_s[...] = jnp.zeros_like(ci_s)
            if with_prev:
                dr_ref[...] = jnp.zeros_like(dr_ref)
                di_ref[...] = jnp.zeros_like(di_ref)

        a_re = lr_ref[...]
        a_im = li_ref[...]

        def step(r, carry):
            if with_prev:
                c_re, c_im, d_re, d_im = carry
            else:
                c_re, c_im = carry
            t = tr - 1 - r if reverse else r
            row = pl.ds(t, 1)
            n_re = a_re * c_re - a_im * c_im + br_ref[row, :]
            n_im = a_re * c_im + a_im * c_re + bi_ref[row, :]
            sr_ref[row, :] = n_re
            si_ref[row, :] = n_im
            if with_prev:
                p_re = pr_ref[row, :]
                p_im = pi_ref[row, :]
                return n_re, n_im, d_re + n_re * p_re + n_im * p_im, d_im + n_im * p_re - n_re * p_im
            pr_ref[row, :] = c_re
            pi_ref[row, :] = c_im
            return n_re, n_im

        zero = jnp.zeros((1, ts), F32)
        init = (cr_s[...], ci_s[...]) + ((zero, zero) if with_prev else ())
        out = lax.fori_loop(0, tr, step, init, unroll=8)
        cr_s[...] = out[0]
        ci_s[...] = out[1]
        if with_prev:
            dr_ref[...] += out[2]
            di_ref[...] += out[3]

    blk = (lambda j, i: (nb - 1 - i, j)) if reverse else (lambda j, i: (i, j))
    row_spec = pl.BlockSpec((tr, ts), blk)
    vec_spec = pl.BlockSpec((1, ts), lambda j, i: (0, j))
    full = jax.ShapeDtypeStruct((rows, width), F32)
    vec = jax.ShapeDtypeStruct((1, width), F32)
    ins = [lr, li, br, bi] + ([prev_r, prev_i] if with_prev else [])
    return pl.pallas_call(
        body, name=name, grid=(width // ts, nb),
        in_specs=[vec_spec, vec_spec] + [row_spec] * (len(ins) - 2),
        out_shape=(full, full, vec, vec) if with_prev else (full, full, full, full),
        out_specs=(row_spec, row_spec, vec_spec, vec_spec) if with_prev else (row_spec,) * 4,
        scratch_shapes=[pltpu.VMEM((1, ts), F32), pltpu.VMEM((1, ts), F32)],
        compiler_params=pltpu.CompilerParams(dimension_semantics=("parallel", "arbitrary"), vmem_limit_bytes=VMEM_LIMIT),
    )(*ins)


@functools.partial(jax.custom_vjp, nondiff_argnums=(4,))
def cscan(lr, li, br, bi, reverse):
    out = _scan_call(lr, li, br, bi, None, None, reverse, "s5_scan")
    return out[0], out[1]


def _cscan_fwd(lr, li, br, bi, reverse):
    sr, si, pr, pi = _scan_call(lr, li, br, bi, None, None, reverse, "s5_scan")
    return (sr, si), (lr, li, pr, pi)


def _cscan_bwd(reverse, res, g):
    lr, li, pr, pi = res
    ar, ai, dlr, dli = _scan_call(lr, -li, g[0], g[1], pr, pi, not reverse, "s5_scan_bwd")
    return dlr, dli, ar, ai


cscan.defvjp(_cscan_fwd, _cscan_bwd)


_BDIMS = {"nn": (((2,), (1,)), ((0,), (0,))), "nt": (((2,), (2,)), ((0,), (0,))), "tn": (((1,), (1,)), ((0,), (0,)))}


def _bdot_raw(a, b, mode):
    return lax.dot_general(a.astype(BF16), b.astype(BF16), _BDIMS[mode], preferred_element_type=F32)


@functools.partial(jax.custom_vjp, nondiff_argnums=(2,))
def _bdot(a, b, mode):
    return _bdot_raw(a, b, mode)


def _bdot_fwd(a, b, mode):
    return _bdot_raw(a, b, mode), (a, b)


def _bdot_bwd(mode, res, dc):
    a, b = res
    if mode == "nn":
        return _bdot_raw(dc, b, "nt"), _bdot_raw(a, dc, "tn")
    if mode == "nt":
        return _bdot_raw(dc, b, "nn"), _bdot_raw(dc, a, "tn")
    return _bdot_raw(b, dc, "nt"), _bdot_raw(a, dc, "nn")


_bdot.defvjp(_bdot_fwd, _bdot_bwd)


def _fdot(a, b):
    return lax.dot_general(a, b, _BDIMS["nn"], precision=lax.Precision.HIGHEST, preferred_element_type=F32)


def _dn_chunk(state, q, k, v, bb, gc, gc64, gr, direction):
    c = q.shape[1]
    row = lax.broadcasted_iota(jnp.int32, (c, c), 0)
    col = lax.broadcasted_iota(jnp.int32, (c, c), 1)
    sgn = 1 - 2 * direction
    after = ((row - col) * sgn > 0)[None]
    incl = ((row - col) * sgn >= 0)[None]
    eye = (row == col).astype(F32)[None]
    kb = k * bb
    decay = jnp.where(incl, jnp.exp(jnp.where(incl, gc64 - gr, 0.0)), 0.0)
    lower = jnp.where(after, _bdot(kb, k, "nt") * decay, 0.0)
    inv = eye - lower
    power = _fdot(lower, lower)
    span = 2
    while span < c:
        inv = inv + _fdot(inv, power)
        span *= 2
        if span < c:
            power = _fdot(power, power)
    eg = jnp.exp(gc)
    u = _fdot(inv, v * bb)
    w = _fdot(inv, kb * eg)
    intra = _bdot(q, k, "nt") * decay
    last = (lax.broadcasted_iota(jnp.int32, (c, 1), 0) == (c - 1) * (1 - direction)).astype(F32)[None]
    g_last = jnp.sum(gc * last, axis=1, keepdims=True)
    k_dec = k * jnp.exp(g_last - gc)
    v_new = u - _bdot(w, state, "nn")
    o = _bdot(q * eg, state, "nn") + _bdot(intra, v_new, "nn")
    new_state = state * jnp.exp(g_last) + _bdot(k_dec, v_new, "tn")
    return new_state, o


def _dn_rows(lq):
    return _pick(lq, (256, 128, 64))


def _dn_prep(q, k, v):
    q, k, v = jax.nn.silu(q), jax.nn.silu(k), jax.nn.silu(v)
    q = q * lax.rsqrt(jnp.sum(q * q, axis=2, keepdims=True) + EPS) * (q.shape[2] ** -0.5)
    k = k * lax.rsqrt(jnp.sum(k * k, axis=2, keepdims=True) + EPS)
    return q, k, v


def _dn_step(state, q, k, v, bb, gc, gc64, gr, direction):
    q, k, v = _dn_prep(q, k, v)
    return _dn_chunk(state, q, k, v, bb, gc, gc64, gr, direction)


def _dn_heads(ref, rows, part, h, d):
    return jnp.stack([ref[rows, (part * h + i) * d:(part * h + i + 1) * d] for i in range(h)])


def _dn_fwd_call(qkv, bb, gc, gc64, gr):
    _, h, lq, d = bb.shape
    c = DN_CHUNK
    t = _dn_rows(lq)
    nb, nc = lq // t, t // c

    def body(qkv_ref, bb_ref, gc_ref, g64_ref, gr_ref, o_ref, st_ref, s_s):
        direction = pl.program_id(0)

        @pl.when(pl.program_id(1) == 0)
        def _():
            s_s[...] = jnp.zeros_like(s_s)

        def step(j, state):
            ci = j + direction * (nc - 1 - 2 * j)
            rows = pl.ds(pl.multiple_of(ci * c, c), c)
            st_ref[0, ci] = state
            state, o = _dn_step(state, _dn_heads(qkv_ref, rows, 0, h, d), _dn_heads(qkv_ref, rows, 1, h, d),
                                _dn_heads(qkv_ref, rows, 2, h, d), bb_ref[0, :, rows, :], gc_ref[0, :, rows, :],
                                g64_ref[0, :, rows, :], gr_ref[0, :, ci], direction)
            for i in range(h):
                o_ref[0, rows, i * d:(i + 1) * d] = o[i]
            return state

        s_s[...] = lax.fori_loop(0, nc, step, s_s[...])

    blk = lambda dd, i: i + dd * (nb - 1 - 2 * i)
    ds = pl.BlockSpec((1, h, t, d), lambda dd, i: (dd, 0, blk(dd, i), 0))
    return pl.pallas_call(
        body, name="deltanet_fwd", grid=(2, nb),
        in_specs=[pl.BlockSpec((t, 3 * h * d), lambda dd, i: (blk(dd, i), 0)), ds, ds,
                  pl.BlockSpec((1, h, t, c), lambda dd, i: (dd, 0, blk(dd, i), 0)),
                  pl.BlockSpec((1, h, nc, 1, c), lambda dd, i: (dd, 0, blk(dd, i), 0, 0))],
        out_shape=(jax.ShapeDtypeStruct((2, lq, h * d), F32), jax.ShapeDtypeStruct((2, lq // c, h, d, d), F32)),
        out_specs=(pl.BlockSpec((1, t, h * d), lambda dd, i: (dd, blk(dd, i), 0)),
                   pl.BlockSpec((1, nc, h, d, d), lambda dd, i: (dd, blk(dd, i), 0, 0, 0))),
        scratch_shapes=[pltpu.VMEM((h, d, d), F32)],
        compiler_params=pltpu.CompilerParams(dimension_semantics=("parallel", "arbitrary"), vmem_limit_bytes=VMEM_LIMIT),
    )(qkv, bb, gc, gc64, gr)


def _dn_bwd_call(qkv, bb, gc, gc64, gr, states, do):
    _, h, lq, d = bb.shape
    c = DN_CHUNK
    t = _dn_rows(lq)
    nb, nc = lq // t, t // c

    def body(qkv_ref, bb_ref, gc_ref, g64_ref, gr_ref, st_ref, do_ref, dqkv_ref, dbb_ref, dgc_ref, dg64_ref, dgr_ref, ds_s):
        direction = pl.program_id(0)

        @pl.when(pl.program_id(1) == 0)
        def _():
            ds_s[...] = jnp.zeros_like(ds_s)

        def step(j, dstate):
            ci = (nc - 1 - j) + direction * (2 * j - (nc - 1))
            rows = pl.ds(pl.multiple_of(ci * c, c), c)
            args = (st_ref[0, ci], _dn_heads(qkv_ref, rows, 0, h, d), _dn_heads(qkv_ref, rows, 1, h, d),
                    _dn_heads(qkv_ref, rows, 2, h, d), bb_ref[0, :, rows, :], gc_ref[0, :, rows, :],
                    g64_ref[0, :, rows, :], gr_ref[0, :, ci])
            _, pull = jax.vjp(lambda *a: _dn_step(*a, direction), *args)
            do_c = jnp.stack([do_ref[rows, i * d:(i + 1) * d] for i in range(h)])
            dstate, dq, dk, dv, dbb, dgc, dg64, dgr = pull((dstate, do_c))
            for part, val in enumerate((dq, dk, dv)):
                for i in range(h):
                    dqkv_ref[0, rows, (part * h + i) * d:(part * h + i + 1) * d] = val[i]
            dbb_ref[0, :, rows, :] = dbb
            dgc_ref[0, :, rows, :] = dgc
            dg64_ref[0, :, rows, :] = dg64
            dgr_ref[0, :, ci] = dgr
            return dstate

        ds_s[...] = lax.fori_loop(0, nc, step, ds_s[...])

    blk = lambda dd, i: (nb - 1 - i) + dd * (2 * i - (nb - 1))
    ds = pl.BlockSpec((1, h, t, d), lambda dd, i: (dd, 0, blk(dd, i), 0))
    g64s = pl.BlockSpec((1, h, t, c), lambda dd, i: (dd, 0, blk(dd, i), 0))
    grs = pl.BlockSpec((1, h, nc, 1, c), lambda dd, i: (dd, 0, blk(dd, i), 0, 0))
    big = jax.ShapeDtypeStruct((2, h, lq, d), F32)
    return pl.pallas_call(
        body, name="deltanet_bwd", grid=(2, nb),
        in_specs=[pl.BlockSpec((t, 3 * h * d), lambda dd, i: (blk(dd, i), 0)), ds, ds, g64s, grs,
                  pl.BlockSpec((1, nc, h, d, d), lambda dd, i: (dd, blk(dd, i), 0, 0, 0)),
                  pl.BlockSpec((t, h * d), lambda dd, i: (blk(dd, i), 0))],
        out_shape=(jax.ShapeDtypeStruct((2, lq, 3 * h * d), F32), big, big, jax.ShapeDtypeStruct((2, h, lq, c), F32),
                   jax.ShapeDtypeStruct((2, h, lq // c, 1, c), F32)),
        out_specs=(pl.BlockSpec((1, t, 3 * h * d), lambda dd, i: (dd, blk(dd, i), 0)), ds, ds, g64s, grs),
        scratch_shapes=[pltpu.VMEM((h, d, d), F32)],
        compiler_params=pltpu.CompilerParams(dimension_semantics=("parallel", "arbitrary"), vmem_limit_bytes=VMEM_LIMIT),
    )(qkv, bb, gc, gc64, gr, states, do)


@jax.custom_vjp
def dn_rule(qkv, bb, gc, gc64, gr):
    o = _dn_fwd_call(qkv, bb, gc, gc64, gr)[0]
    return o[0] + o[1]


def _dn_rule_fwd(qkv, bb, gc, gc64, gr):
    o, states = _dn_fwd_call(qkv, bb, gc, gc64, gr)
    return o[0] + o[1], (qkv, bb, gc, gc64, gr, states)


def _dn_rule_bwd(res, do):
    dqkv, dbb, dgc, dg64, dgr = _dn_bwd_call(*res, do)
    return dqkv[0] + dqkv[1], dbb, dgc, dg64, dgr


dn_rule.defvjp(_dn_rule_fwd, _dn_rule_bwd)


def gated_delta_both(qkv, beta2, g2):
    _, h, lq = beta2.shape
    d = qkv.shape[1] // (3 * h)
    c = DN_CHUNK
    gch = g2.reshape(2, h, lq // c, c)
    cum = jnp.stack([jnp.cumsum(gch[0], axis=-1), lax.cumsum(gch[1], axis=2, reverse=True)])
    flat = cum.reshape(2, h, lq)
    return dn_rule(qkv, jnp.broadcast_to(beta2[..., None], (2, h, lq, d)), jnp.broadcast_to(flat[..., None], (2, h, lq, d)),
                   jnp.broadcast_to(flat[..., None], (2, h, lq, c)), cum.reshape(2, h, lq // c, 1, c))


def _conv_rows(rows):
    return _pick(rows, (256, 128, 64, 32, 16, 8))


def _conv_taps(x_ref, prev_ref, next_ref, i, nb, t):
    keep_prev = (i > 0).astype(F32)
    keep_next = (i < nb - 1).astype(F32)
    ext = jnp.concatenate([prev_ref[...] * keep_prev, x_ref[...], next_ref[...] * keep_next], axis=0)
    total = t + 2 * CONV_HALO
    return [pltpu.roll(ext, (DN_CONV // 2 - j) % total, 0)[CONV_HALO:CONV_HALO + t] for j in range(DN_CONV)]


def _conv_specs(t, ch, rows):
    per = t // CONV_HALO
    last = rows // CONV_HALO - 1
    return [pl.BlockSpec((t, ch), lambda i: (i, 0)),
            pl.BlockSpec((CONV_HALO, ch), lambda i: (jnp.maximum(i * per - 1, 0), 0)),
            pl.BlockSpec((CONV_HALO, ch), lambda i: (jnp.minimum((i + 1) * per, last), 0))]


def _conv_call(x, taps):
    rows, ch = x.shape
    t = _conv_rows(rows)
    nb = rows // t

    def body(x_ref, prev_ref, next_ref, w_ref, o_ref):
        views = _conv_taps(x_ref, prev_ref, next_ref, pl.program_id(0), nb, t)
        acc = views[0] * w_ref[0:1, :]
        for j in range(1, DN_CONV):
            acc = acc + views[j] * w_ref[j:j + 1, :]
        o_ref[...] = acc

    return pl.pallas_call(
        body, name="short_conv", out_shape=jax.ShapeDtypeStruct((rows, ch), F32), grid=(nb,),
        in_specs=_conv_specs(t, ch, rows) + [pl.BlockSpec((8, ch), lambda i: (0, 0))],
        out_specs=pl.BlockSpec((t, ch), lambda i: (i, 0)),
        compiler_params=pltpu.CompilerParams(dimension_semantics=("parallel",), vmem_limit_bytes=VMEM_LIMIT),
    )(x, x, x, taps)


def _conv_dw_call(x, dy):
    rows, ch = x.shape
    t = _conv_rows(rows)
    nb = rows // t

    def body(x_ref, prev_ref, next_ref, dy_ref, o_ref):
        @pl.when(pl.program_id(0) == 0)
        def _():
            o_ref[...] = jnp.zeros_like(o_ref)

        views = _conv_taps(x_ref, prev_ref, next_ref, pl.program_id(0), nb, t)
        dyv = dy_ref[...]
        for j in range(DN_CONV):
            o_ref[j:j + 1, :] += jnp.sum(views[j] * dyv, axis=0, keepdims=True)

    return pl.pallas_call(
        body, name="short_conv_dw", out_shape=jax.ShapeDtypeStruct((8, ch), F32), grid=(nb,),
        in_specs=_conv_specs(t, ch, rows) + [pl.BlockSpec((t, ch), lambda i: (i, 0))],
        out_specs=pl.BlockSpec((8, ch), lambda i: (0, 0)),
        compiler_params=pltpu.CompilerParams(dimension_semantics=("arbitrary",), vmem_limit_bytes=VMEM_LIMIT),
    )(x, x, x, dy)


def _taps(w, flip):
    wt = jnp.transpose(w)
    if flip:
        wt = wt[::-1]
    return jnp.pad(wt, ((0, 8 - DN_CONV), (0, 0)))


@jax.custom_vjp
def short_conv(x, w):
    return _conv_call(x, _taps(w, False))


def _short_conv_fwd(x, w):
    return _conv_call(x, _taps(w, False)), (x, w)


def _short_conv_bwd(res, dy):
    x, w = res
    return _conv_call(dy, _taps(w, True)), jnp.transpose(_conv_dw_call(x, dy)[:DN_CONV])


short_conv.defvjp(_short_conv_fwd, _short_conv_bwd)


def _s5_mixer(u, a_re, a_im, log_step, b_re, b_im, c_re, c_im, d, w_glu, b_glu):
    per = LANES // SSM_GROUP
    y = u * d
    for direction in range(2):
        are, aim = a_re[direction], a_im[direction]
        step = jnp.exp(log_step[direction])[:, None]
        mag = jnp.exp(are * step)
        lam_re = mag * jnp.cos(aim * step)
        lam_im = mag * jnp.sin(aim * step)
        den = are * are + aim * aim
        nr = lam_re - 1.0
        ni = lam_im
        coef_re = (nr * are + ni * aim) / den
        coef_im = (ni * are - nr * aim) / den
        bb_re = coef_re[..., None] * b_re[direction] - coef_im[..., None] * b_im[direction]
        bb_im = coef_re[..., None] * b_im[direction] + coef_im[..., None] * b_re[direction]
        bu_re = cmm(u, _cluster_diag(jnp.swapaxes(bb_re, 1, 2), per))
        bu_im = cmm(u, _cluster_diag(jnp.swapaxes(bb_im, 1, 2), per))
        width = bu_re.shape[1]
        s_re, s_im = cscan(lam_re.reshape(1, width), lam_im.reshape(1, width), bu_re, bu_im, direction == 1)
        y = y + cmm(s_re, _cluster_diag(jnp.swapaxes(c_re[direction], 1, 2), per))
        y = y - cmm(s_im, _cluster_diag(jnp.swapaxes(c_im[direction], 1, 2), per))
    y = jax.nn.gelu(y)
    return y * jax.nn.sigmoid(pmm(y, w_glu) + b_glu)


def _deltanet_mixer(q, k, v, ab, conv_w, a_log, dt_bias, norm_g):
    seq = q.shape[0]
    qkv = short_conv(jnp.concatenate([q, k, v], axis=-1), conv_w)
    a4 = ab[:, :2 * DN_HEADS].reshape(seq, 2, DN_HEADS)
    b4 = ab[:, 2 * DN_HEADS:4 * DN_HEADS].reshape(seq, 2, DN_HEADS)
    beta = jax.nn.sigmoid(b4)
    g = -jnp.exp(a_log) * jax.nn.softplus(a4 + dt_bias)
    o = gated_delta_both(qkv, jnp.transpose(beta, (1, 2, 0)), jnp.transpose(g, (1, 2, 0)))
    return head_norm(o, norm_g)


def _axial_rope(seq):
    rows = seq // GRID_W
    row = jnp.repeat(jnp.arange(rows), GRID_W).astype(F32)
    col = jnp.tile(jnp.arange(GRID_W), rows).astype(F32)
    axis_dim = ATT_HEAD_DIM // 2
    freqs = ROPE_THETA ** (-jnp.arange(0, axis_dim, 2, dtype=F32) / axis_dim)
    ang = jnp.concatenate([row[:, None] * freqs, col[:, None] * freqs], axis=-1)
    sign = jnp.tile(jnp.array([-1.0, 1.0], F32), ATT_HEAD_DIM // 2)
    return jnp.repeat(jnp.cos(ang), 2, axis=-1), jnp.repeat(jnp.sin(ang), 2, axis=-1) * sign


def _grid_attention(q, k, v, qn_g, kn_g, cos, sin):
    qh = head_norm_rope(q, qn_g, cos, sin, ATT_HEAD_DIM ** -0.5)
    kh = head_norm_rope(k, kn_g, cos, sin, 1.0)
    return flash(qh, kh, v, ATT_HEAD_DIM, ATT_HEADS // ATT_KV_HEADS, 1.0)


def _memory_attention(q, mem_n, w_kv):
    kv = pmm(mem_n, w_kv)
    mem_w = MEM_HEADS * MEM_HEAD_DIM
    return flash(q, kv[:, :mem_w], kv[:, mem_w:], MEM_HEAD_DIM, 1, MEM_HEAD_DIM ** -0.5)


def _middle(h, x, mem, p):
    d_model = x.shape[1]
    lay, _, _ = _layout(d_model)

    def seg(name):
        _, dst, w, _ = lay[name]
        return h[:, dst:dst + w]

    cos, sin = _axial_rope(x.shape[0])
    y_a = _s5_mixer(seg("u"), p["ssm_a_re"], p["ssm_a_im"], p["ssm_log_step"], p["ssm_b_re"], p["ssm_b_im"],
                    p["ssm_c_re"], p["ssm_c_im"], p["ssm_d"], p["ssm_w_glu"], p["ssm_b_glu"]) * jax.nn.silu(seg("z_a"))
    y_b = _deltanet_mixer(seg("dq"), seg("dk"), seg("dv"), seg("dab"), p["dn_conv"], p["dn_a_log"],
                          p["dn_dt_bias"], p["dn_norm_g"]) * jax.nn.silu(seg("z_b"))
    y_c = _grid_attention(seg("aq"), seg("ak"), seg("av"), p["attn_q_norm"], p["attn_k_norm"], cos, sin) * jax.nn.silu(seg("z_c"))
    y_m = _memory_attention(seg("mq"), prms(mem, p["mem_norm_g"]), p["w_mem_kv"]) * jax.nn.silu(seg("z_m"))
    gates = jax.nn.sigmoid(seg("gates"))
    merged = jnp.zeros_like(x)
    off = 0
    for bi, y_br in enumerate((y_a, y_b, y_c, y_m)):
        w = y_br.shape[1]
        merged = merged + gates[:, bi * d_model:(bi + 1) * d_model] * pmm(y_br, p["w_branch"][off:off + w])
        off += w
    return x + pmm(merged, p["w_out"])


def _local_step(x, mem, target, w):
    depth = w["norm_g"].shape[0]
    d_model = x.shape[1]
    mid_names = [n for n in WEIGHTS if n not in ("norm_g", "w_in", "final_norm_g")]
    saved = []
    cur = x
    for layer in range(depth):
        xn = _rms_fwd_call(cur, w["norm_g"][layer], BF16)
        h = _mm(xn, w["w_in"][layer], "nn", "in_proj")
        p = {n: w[n][layer] for n in mid_names}
        nxt, vjp_mid = jax.vjp(lambda h_, x_, p_: _middle(h_, x_, mem, p_), h, cur, p)
        saved.append((cur, xn, vjp_mid))
        cur = nxt
    loss_parts, dx, d_final = _loss_call(cur, w["final_norm_g"], target)
    loss = jnp.sum(loss_parts[:, 0, 0])
    d_final = d_final.reshape(w["final_norm_g"].shape)
    grads = {n: [None] * depth for n in WEIGHTS if n != "final_norm_g"}
    for layer in reversed(range(depth)):
        x_in, xn, vjp_mid = saved[layer]
        dh, dx_skip, dp = vjp_mid(dx)
        grads["w_in"][layer] = _unpad_w_in(_mm(xn, dh, "tn", "in_proj_dw"), d_model)
        dxn = _mm(dh, w["w_in"][layer], "nt", "in_proj_dx")
        dx_n, dg = _rms_bwd_call(x_in, w["norm_g"][layer], dxn)
        dx = dx_skip + dx_n
        grads["norm_g"][layer] = dg.reshape(w["norm_g"][layer].shape)
        for n in mid_names:
            grads[n][layer] = dp[n]
    out = {n: jnp.stack(v) for n, v in grads.items()}
    out["final_norm_g"] = d_final
    return loss, dx, out


_ANY = pl.BlockSpec(memory_space=pl.ANY)
_CHIP_FLIPS = [(1, 0), (0, 1), (1, 1)]


def _remote(src, dst, send_sem, recv_sem, dev):
    return pltpu.make_async_remote_copy(src_ref=src, dst_ref=dst, send_sem=send_sem, recv_sem=recv_sem,
                                        device_id=dev, device_id_type=pl.DeviceIdType.MESH)


def _flip(v, bit):
    return 1 - v if bit else v


def _gather_chips(shards):
    n = len(shards)

    def body(*refs):
        ins, outs = refs[:n], refs[n:2 * n]
        send_sems, recv_sems, local_sems = refs[2 * n:]
        x, y, c = lax.axis_index("x"), lax.axis_index("y"), lax.axis_index("c")
        locals_, sends = [], []
        for a in range(n):
            cp = pltpu.make_async_copy(ins[a], outs[a].at[2 * x + y], local_sems.at[a])
            cp.start()
            locals_.append(cp)

        def half(a, px, py, pc):
            rh = shards[a].shape[1] // 2
            return outs[a].at[2 * px + py, :, pl.ds(pc * rh, rh), :]

        for a in range(n):
            rh = shards[a].shape[1] // 2
            for j, (fx, fy) in enumerate(_CHIP_FLIPS):
                cp = _remote(ins[a].at[:, pl.ds(c * rh, rh), :], half(a, x, y, c), send_sems.at[6 * a + j],
                             recv_sems.at[6 * a + j], (_flip(x, fx), _flip(y, fy), c))
                cp.start()
                sends.append(cp)
        for a in range(n):
            for j, (fx, fy) in enumerate(_CHIP_FLIPS):
                blk = half(a, _flip(x, fx), _flip(y, fy), c)
                _remote(blk, blk, send_sems.at[6 * a + j], recv_sems.at[6 * a + j], (x, y, c)).wait_recv()
                cp = _remote(blk, blk, send_sems.at[6 * a + 3 + j], recv_sems.at[6 * a + 3 + j], (x, y, 1 - c))
                cp.start()
                sends.append(cp)
        for a in range(n):
            for j, (fx, fy) in enumerate(_CHIP_FLIPS):
                blk = half(a, _flip(x, fx), _flip(y, fy), 1 - c)
                _remote(blk, blk, send_sems.at[6 * a + 3 + j], recv_sems.at[6 * a + 3 + j], (x, y, c)).wait_recv()
        for cp in sends:
            cp.wait_send()
        for cp in locals_:
            cp.wait()

    return pl.pallas_call(
        body, name="gather_weights",
        out_shape=tuple(jax.ShapeDtypeStruct((N_CHIPS,) + s.shape, s.dtype) for s in shards),
        in_specs=[_ANY] * n, out_specs=tuple([_ANY] * n),
        scratch_shapes=[pltpu.SemaphoreType.DMA((6 * n,)), pltpu.SemaphoreType.DMA((6 * n,)),
                        pltpu.SemaphoreType.DMA((n,))],
    )(*shards)


def _exchange_pair(parts):
    n = len(parts)

    def body(*refs):
        ins, outs = refs[:n], refs[n:2 * n]
        send_sems, recv_sems = refs[2 * n:]
        x, y, c = lax.axis_index("x"), lax.axis_index("y"), lax.axis_index("c")
        started = []
        for a in range(n):
            rh = parts[a].shape[2] // 2
            cp = _remote(ins[a].at[:, :, pl.ds((1 - c) * rh, rh), :], outs[a], send_sems.at[a], recv_sems.at[a], (x, y, 1 - c))
            cp.start()
            started.append(cp)
        for a in range(n):
            _remote(outs[a], outs[a], send_sems.at[a], recv_sems.at[a], (x, y, c)).wait_recv()
        for cp in started:
            cp.wait_send()

    return pl.pallas_call(
        body, name="exchange_pair",
        out_shape=tuple(jax.ShapeDtypeStruct((p.shape[0], p.shape[1], p.shape[2] // 2, p.shape[3]), p.dtype) for p in parts),
        in_specs=[_ANY] * n, out_specs=tuple([_ANY] * n),
        scratch_shapes=[pltpu.SemaphoreType.DMA((n,)), pltpu.SemaphoreType.DMA((n,))],
    )(*parts)


def _exchange_chips(pair_sums, small):
    n = len(pair_sums)

    def body(*refs):
        ins, small_in = refs[:n], refs[n]
        outs, small_out = refs[n + 1:2 * n + 1], refs[2 * n + 1]
        send_sems, recv_sems, small_send, small_recv, local_sem = refs[2 * n + 2:]
        x, y, c = lax.axis_index("x"), lax.axis_index("y"), lax.axis_index("c")
        me = 4 * x + 2 * y + c
        started = []
        own = pltpu.make_async_copy(small_in, small_out.at[me], local_sem)
        own.start()
        for f, (fx, fy) in enumerate(_CHIP_FLIPS):
            px, py = _flip(x, fx), _flip(y, fy)
            for a in range(n):
                cp = _remote(ins[a].at[2 * px + py], outs[a].at[f], send_sems.at[3 * a + f], recv_sems.at[3 * a + f], (px, py, c))
                cp.start()
                started.append(cp)
        for k in range(1, N_DEV):
            peer = (_flip(x, k >> 2 & 1), _flip(y, k >> 1 & 1), _flip(c, k & 1))
            cp = _remote(small_in, small_out.at[me], small_send.at[k - 1], small_recv.at[k - 1], peer)
            cp.start()
            started.append(cp)
        for f in range(len(_CHIP_FLIPS)):
            for a in range(n):
                _remote(outs[a].at[f], outs[a].at[f], send_sems.at[3 * a + f], recv_sems.at[3 * a + f], (x, y, c)).wait_recv()
        for k in range(1, N_DEV):
            peer = 4 * _flip(x, k >> 2 & 1) + 2 * _flip(y, k >> 1 & 1) + _flip(c, k & 1)
            _remote(small_out.at[peer], small_out.at[peer], small_send.at[k - 1], small_recv.at[k - 1], (x, y, c)).wait_recv()
        for cp in started:
            cp.wait_send()
        own.wait()

    shapes = tuple(jax.ShapeDtypeStruct((3,) + p.shape[1:], p.dtype) for p in pair_sums)
    shapes += (jax.ShapeDtypeStruct((N_DEV,) + small.shape, small.dtype),)
    return pl.pallas_call(
        body, name="exchange_chips", out_shape=shapes, in_specs=[_ANY] * (n + 1), out_specs=tuple([_ANY] * (n + 1)),
        scratch_shapes=[pltpu.SemaphoreType.DMA((3 * n,)), pltpu.SemaphoreType.DMA((3 * n,)),
                        pltpu.SemaphoreType.DMA((N_DEV - 1,)), pltpu.SemaphoreType.DMA((N_DEV - 1,)),
                        pltpu.SemaphoreType.DMA(())],
    )(*pair_sums, small)


def _swap_halves(arrays):
    n = len(arrays)

    def body(*refs):
        outs = refs[n:2 * n]
        send_sems, recv_sems = refs[2 * n:]
        x, y, c = lax.axis_index("x"), lax.axis_index("y"), lax.axis_index("c")
        started = []
        for a in range(n):
            rh = arrays[a].shape[1] // 2
            mine = outs[a].at[:, pl.ds(c * rh, rh), :]
            cp = _remote(mine, mine, send_sems.at[a], recv_sems.at[a], (x, y, 1 - c))
            cp.start()
            started.append(cp)
        for a in range(n):
            rh = arrays[a].shape[1] // 2
            theirs = outs[a].at[:, pl.ds((1 - c) * rh, rh), :]
            _remote(theirs, theirs, send_sems.at[a], recv_sems.at[a], (x, y, c)).wait_recv()
        for cp in started:
            cp.wait_send()

    return pl.pallas_call(
        body, name="swap_halves", out_shape=tuple(jax.ShapeDtypeStruct(t.shape, t.dtype) for t in arrays),
        in_specs=[_ANY] * n, out_specs=tuple([_ANY] * n), input_output_aliases={a: a for a in range(n)},
        scratch_shapes=[pltpu.SemaphoreType.DMA((n,)), pltpu.SemaphoreType.DMA((n,))],
    )(*arrays)


def _adamw_math(g, w, m, v):
    m = ADAM_B1 * m + (1.0 - ADAM_B1) * g
    v = ADAM_B2 * v + (1.0 - ADAM_B2) * jnp.square(g)
    m_hat = m / (1.0 - ADAM_B1 ** ADAM_STEP)
    v_hat = v / (1.0 - ADAM_B2 ** ADAM_STEP)
    delta = -ADAM_LR * (m_hat / (jnp.sqrt(v_hat) + ADAM_EPS) + ADAM_WD * w)
    return delta, m, v


def _row_tile(rows, row_bytes, budget, step=8):
    best = step
    for t in range(step, rows + 1, step):
        if rows % t == 0 and t * row_bytes <= budget:
            best = t
    return best


def _pair_sum(part, other, core):
    nj, a, r, c = part.shape
    rh = r // 2
    tr = _row_tile(rh, 4 * (-(-c // LANES) * LANES), 512 * 1024, 16)

    def body(core_ref, p_ref, o_ref, s32_ref, s16_ref):
        s = p_ref[0, 0, 0] + o_ref[0, 0]
        s32_ref[0, 0] = s
        s16_ref[0, 0] = s.astype(BF16)

    spec = pl.BlockSpec((1, 1, tr, c), lambda j, l, i, core_ref: (j, l, i, 0))
    return pl.pallas_call(
        body, name="pair_sum",
        out_shape=(jax.ShapeDtypeStruct(other.shape, F32), jax.ShapeDtypeStruct(other.shape, BF16)),
        grid_spec=pltpu.PrefetchScalarGridSpec(
            num_scalar_prefetch=1, grid=(nj, a, rh // tr),
            in_specs=[pl.BlockSpec((1, 1, 1, tr, c), lambda j, l, i, core_ref: (j, l, core_ref[0], i, 0)), spec],
            out_specs=(spec, spec)),
        compiler_params=pltpu.CompilerParams(dimension_semantics=("parallel", "parallel", "parallel"),
                                             vmem_limit_bytes=VMEM_LIMIT),
    )(core, part.reshape(nj, a, 2, rh, c), other)


def _sum_adamw_big(own, landed, w, m, v, place):
    _, a, rh, c = own.shape
    tr = _row_tile(rh, 4 * (-(-c // LANES) * LANES), 256 * 1024, 16)
    w4, m4, v4 = (t.reshape(a, 2, rh, c) for t in (w, m, v))

    def body(place_ref, q_ref, p_ref, w_ref, m_ref, v_ref, g_out, d_out, m_out, v_out):
        g = q_ref[0, 0]
        for f in range(3):
            g = g + p_ref[f, 0].astype(F32)
        delta, mm_, vv_ = _adamw_math(g, w_ref[0, 0], m_ref[0, 0], v_ref[0, 0])
        g_out[0, 0] = g
        d_out[0, 0] = delta
        m_out[0, 0] = mm_
        v_out[0, 0] = vv_

    wspec = pl.BlockSpec((1, 1, tr, c), lambda l, i, place_ref: (l, place_ref[0], i, 0))
    out = jax.ShapeDtypeStruct((a, 2, rh, c), F32)
    res = pl.pallas_call(
        body, name="sum_adamw_big", out_shape=(out,) * 4,
        grid_spec=pltpu.PrefetchScalarGridSpec(
            num_scalar_prefetch=1, grid=(a, rh // tr),
            in_specs=[pl.BlockSpec((1, 1, tr, c), lambda l, i, place_ref: (place_ref[1], l, i, 0)),
                      pl.BlockSpec((3, 1, tr, c), lambda l, i, place_ref: (0, l, i, 0)), wspec, wspec, wspec],
            out_specs=(wspec,) * 4),
        compiler_params=pltpu.CompilerParams(dimension_semantics=("parallel", "parallel"), vmem_limit_bytes=VMEM_LIMIT),
    )(place, own, landed, w4, m4, v4)
    return [t.reshape(a, 2 * rh, c) for t in res]


def _sum_small(parts):
    _, rows, _ = parts.shape
    tr = _row_tile(rows, 4 * LANES, 256 * 1024)

    def body(p_ref, o_ref):
        g = p_ref[0]
        for k in range(1, N_DEV):
            g = g + p_ref[k]
        o_ref[...] = g

    return pl.pallas_call(
        body, name="sum_small", out_shape=jax.ShapeDtypeStruct((rows, LANES), F32), grid=(rows // tr,),
        in_specs=[pl.BlockSpec((N_DEV, tr, LANES), lambda i: (0, i, 0))], out_specs=pl.BlockSpec((tr, LANES), lambda i: (i, 0)),
        compiler_params=pltpu.CompilerParams(dimension_semantics=("parallel",)),
    )(parts)


def _adamw_small(g, w, m, v):
    rows, _ = g.shape
    tr = _row_tile(rows, 4 * LANES, 256 * 1024)

    def body(g_ref, w_ref, m_ref, v_ref, d_out, m_out, v_out):
        d_out[...], m_out[...], v_out[...] = _adamw_math(g_ref[...], w_ref[...], m_ref[...], v_ref[...])

    spec = pl.BlockSpec((tr, LANES), lambda i: (i, 0))
    out = jax.ShapeDtypeStruct((rows, LANES), F32)
    return pl.pallas_call(
        body, name="adamw_small", out_shape=(out,) * 3, grid=(rows // tr,), in_specs=[spec] * 4, out_specs=(spec,) * 3,
        compiler_params=pltpu.CompilerParams(dimension_semantics=("parallel",)),
    )(g, w, m, v)


def _pack(arrays):
    flat = jnp.concatenate([a.reshape(-1) for a in arrays])
    rows = -(-flat.shape[0] // (8 * LANES)) * 8
    return jnp.pad(flat, (0, rows * LANES - flat.shape[0])).reshape(rows, LANES)


def _unpack(packed, shapes):
    flat, out, off = packed.reshape(-1), [], 0
    for s in shapes:
        size = math.prod(s)
        out.append(flat[off:off + size].reshape(s))
        off += size
    return out


def kernel(x, mem, norm_g, w_in, ssm_a_re, ssm_a_im, ssm_log_step, ssm_b_re, ssm_b_im, ssm_c_re, ssm_c_im, ssm_d, ssm_w_glu, ssm_b_glu, dn_conv, dn_a_log, dn_dt_bias, dn_norm_g, attn_q_norm, attn_k_norm, mem_norm_g, w_mem_kv, w_branch, w_out, final_norm_g, loss_target, m_norm_g, m_w_in, m_ssm_a_re, m_ssm_a_im, m_ssm_log_step, m_ssm_b_re, m_ssm_b_im, m_ssm_c_re, m_ssm_c_im, m_ssm_d, m_ssm_w_glu, m_ssm_b_glu, m_dn_conv, m_dn_a_log, m_dn_dt_bias, m_dn_norm_g, m_attn_q_norm, m_attn_k_norm, m_mem_norm_g, m_w_mem_kv, m_w_branch, m_w_out, m_final_norm_g, v_norm_g, v_w_in, v_ssm_a_re, v_ssm_a_im, v_ssm_log_step, v_ssm_b_re, v_ssm_b_im, v_ssm_c_re, v_ssm_c_im, v_ssm_d, v_ssm_w_glu, v_ssm_b_glu, v_dn_conv, v_dn_a_log, v_dn_dt_bias, v_dn_norm_g, v_attn_q_norm, v_attn_k_norm, v_mem_norm_g, v_w_mem_kv, v_w_branch, v_w_out, v_final_norm_g):
    args = locals()
    wts = {n: args[n] for n in WEIGHTS}
    mom = {n: args["m_" + n] for n in WEIGHTS}
    var = {n: args["v_" + n] for n in WEIGHTS}
    d_model = x.shape[-1]
    chip = 2 * lax.axis_index("x") + lax.axis_index("y")
    core = lax.axis_index("c")

    conv_rows = dn_conv.shape[1]
    conv_flat = _pack([dn_conv])
    gathered = _gather_chips([wts[n].astype(BF16) for n in BIG] + [conv_flat[None]])
    full = {n: wts[n] for n in SMALL}
    g_in = gathered[0]
    full["w_in"] = _pad_w_in(jnp.concatenate([g_in[j] for j in range(N_CHIPS)], axis=-1), d_model)
    for n, g in zip(BIG[1:], gathered[1:]):
        full[n] = jnp.concatenate([g[j] for j in range(N_CHIPS)], axis=1).astype(F32)
    conv_all = gathered[-1][:, 0].reshape(N_CHIPS, -1)[:, :dn_conv.size].reshape((N_CHIPS,) + dn_conv.shape)
    full["dn_conv"] = jnp.concatenate([conv_all[j] for j in range(N_CHIPS)], axis=1)

    loss, grad_x, grads = _local_step(x[0], mem[0], loss_target[0], full)
    loss = lax.psum(loss, ("x", "y", "c"))

    big_parts = []
    for n in BIG:
        g = grads[n]
        if n == "w_in":
            cols = g.shape[-1] // N_CHIPS
            big_parts.append(jnp.stack([g[..., j * cols:(j + 1) * cols] for j in range(N_CHIPS)]))
        else:
            rows = g.shape[1] // N_CHIPS
            big_parts.append(jnp.stack([g[:, j * rows:(j + 1) * rows] for j in range(N_CHIPS)]))
    core_arr = jnp.reshape(core, (1,)).astype(jnp.int32)
    place = jnp.stack([core, chip]).astype(jnp.int32)
    from_pair = _exchange_pair(big_parts)
    sums = [_pair_sum(p, o, core_arr) for p, o in zip(big_parts, from_pair)]
    small_shapes = [grads[n].shape for n in SMALL]
    landed = _exchange_chips([s[1] for s in sums], _pack([grads[n] for n in SMALL]))

    halves = []
    for n, s, parts in zip(BIG, sums, landed[:-1]):
        halves.extend(_sum_adamw_big(s[0], parts, wts[n], mom[n], var[n], place))
    swapped = _swap_halves(halves)
    res = {}
    for i, n in enumerate(BIG):
        res[n] = swapped[4 * i:4 * i + 4]
    small_g = _unpack(_sum_small(landed[-1]), small_shapes)
    small_g = dict(zip(SMALL, small_g))
    small_g["dn_conv"] = lax.dynamic_slice_in_dim(small_g["dn_conv"], chip * conv_rows, conv_rows, axis=1)
    shapes = [small_g[n].shape for n in SMALL]
    upd = _adamw_small(_pack([small_g[n] for n in SMALL]), _pack([wts[n] for n in SMALL]),
                       _pack([mom[n] for n in SMALL]), _pack([var[n] for n in SMALL]))
    upd = [_unpack(u, shapes) for u in upd]
    for i, n in enumerate(SMALL):
        res[n] = (small_g[n], upd[0][i], upd[1][i], upd[2][i])

    out = [loss, grad_x[None]]
    for kind in range(4):
        out.extend(res[n][kind] for n in WEIGHTS)
    return tuple(out)
```

```python
import functools
import math

import jax
import jax.numpy as jnp
from jax import lax
from jax.experimental import pallas as pl
from jax.experimental.pallas import tpu as pltpu

F32 = jnp.float32
BF16 = jnp.bfloat16

GRID_W = 64
EPS = 1e-6
SSM_GROUP = 16
SSM_STATE = 64
SSM_GROUPS = 48
DN_HEADS = 6
DN_HEAD_DIM = 128
DN_CONV = 5
DN_CHUNK = 64
ATT_HEADS = 8
ATT_KV_HEADS = 2
ATT_HEAD_DIM = 128
ROPE_THETA = 10000.0
MEM_HEADS = 4
MEM_HEAD_DIM = 128
N_BRANCH = 4

ADAM_LR = 0.001
ADAM_B1 = 0.9
ADAM_B2 = 0.999
ADAM_EPS = 1e-08
ADAM_WD = 0.01
ADAM_STEP = 10

N_CHIPS = 4
N_DEV = 8
LANES = 128
VMEM_LIMIT = 48 * 1024 * 1024
IN_PAD_UNIT = 512
MM_FULL_K = 2048
PACK_ROWS = 512
CONV_HALO = 8
ATT_HEADS_PER_STEP = 4

WEIGHTS = ['norm_g', 'w_in', 'ssm_a_re', 'ssm_a_im', 'ssm_log_step', 'ssm_b_re', 'ssm_b_im', 'ssm_c_re',
           'ssm_c_im', 'ssm_d', 'ssm_w_glu', 'ssm_b_glu', 'dn_conv', 'dn_a_log', 'dn_dt_bias', 'dn_norm_g',
           'attn_q_norm', 'attn_k_norm', 'mem_norm_g', 'w_mem_kv', 'w_branch', 'w_out', 'final_norm_g']
BIG = ['w_in', 'ssm_w_glu', 'w_mem_kv', 'w_branch', 'w_out']
SMALL = [n for n in WEIGHTS if n not in BIG]


def _pick(dim, cands):
    for c in cands:
        if dim % c == 0:
            return c
    return dim


def _widths(d_model):
    ssm_w = SSM_GROUPS * SSM_GROUP
    dn_w = DN_HEADS * DN_HEAD_DIM
    att_w = ATT_HEADS * ATT_HEAD_DIM
    kv_w = ATT_KV_HEADS * ATT_HEAD_DIM
    mem_w = MEM_HEADS * MEM_HEAD_DIM
    return [("u", ssm_w), ("z_a", ssm_w), ("dq", dn_w), ("dk", dn_w), ("dv", dn_w), ("dab", 4 * DN_HEADS),
            ("z_b", dn_w), ("aq", att_w), ("ak", kv_w), ("av", kv_w), ("z_c", att_w), ("mq", mem_w),
            ("z_m", mem_w), ("gates", N_BRANCH * d_model)]


def _layout(d_model):
    out, src, dst = {}, 0, 0
    for name, w in _widths(d_model):
        pw = -(-w // IN_PAD_UNIT) * IN_PAD_UNIT if w % LANES else w
        out[name] = (src, dst, w, pw)
        src += w
        dst += pw
    return out, src, dst


def _pad_w_in(full, d_model):
    lay, _, _ = _layout(d_model)
    parts = []
    for name, _ in _widths(d_model):
        s, _, w, pw = lay[name]
        seg = full[..., s:s + w]
        if pw != w:
            seg = jnp.pad(seg, [(0, 0)] * (full.ndim - 1) + [(0, pw - w)])
        parts.append(seg)
    return jnp.concatenate(parts, axis=-1)


def _unpad_w_in(padded, d_model):
    lay, _, _ = _layout(d_model)
    return jnp.concatenate([padded[..., lay[n][1]:lay[n][1] + lay[n][2]] for n, _ in _widths(d_model)], axis=-1)


def _mm(a, b, mode="nn", name="mm"):
    if mode == "nn":
        (m, k), (k2, n) = a.shape, b.shape
    elif mode == "nt":
        (m, k), (n, k2) = a.shape, b.shape
    else:
        (k, m), (k2, n) = a.shape, b.shape
    assert k == k2, (a.shape, b.shape, mode)
    tk = k if k <= MM_FULL_K else _pick(k, (1024, 512, 256, 128))
    tm = _pick(m, (2048, 1024, 512, 256, 128) if mode == "tn" else (1024, 512, 256, 128))
    tn = _pick(n, (512, 384, 256, 128) if tk > 1024 or tm > 1024 else (1024, 512, 384, 256, 128))
    nk = k // tk
    if mode == "nt" and nk > 1 and n <= 2048:
        tn, tk = n, _pick(k, (512, 256, 128))
        nk = k // tk
    dims = {"nn": (((1,), (0,)), ((), ())), "nt": (((1,), (1,)), ((), ())), "tn": (((0,), (0,)), ((), ()))}[mode]

    def body(a_ref, b_ref, o_ref, *scratch):
        prod = lax.dot_general(a_ref[...].astype(BF16), b_ref[...].astype(BF16), dims, preferred_element_type=F32)
        if nk == 1:
            o_ref[...] = prod
            return
        acc_ref, = scratch
        kk = pl.program_id(2)

        @pl.when(kk == 0)
        def _():
            acc_ref[...] = prod

        @pl.when(kk > 0)
        def _():
            acc_ref[...] += prod

        @pl.when(kk == nk - 1)
        def _():
            o_ref[...] = acc_ref[...]

    a_spec = pl.BlockSpec((tk, tm), lambda i, j, kk: (kk, i)) if mode == "tn" else pl.BlockSpec((tm, tk), lambda i, j, kk: (i, kk))
    b_spec = pl.BlockSpec((tn, tk), lambda i, j, kk: (j, kk)) if mode == "nt" else pl.BlockSpec((tk, tn), lambda i, j, kk: (kk, j))
    return pl.pallas_call(
        body, name=name, out_shape=jax.ShapeDtypeStruct((m, n), F32), grid=(m // tm, n // tn, nk),
        in_specs=[a_spec, b_spec], out_specs=pl.BlockSpec((tm, tn), lambda i, j, kk: (i, j)),
        scratch_shapes=[pltpu.VMEM((tm, tn), F32)] if nk > 1 else [],
        compiler_params=pltpu.CompilerParams(dimension_semantics=("parallel", "parallel", "arbitrary"),
                                             vmem_limit_bytes=VMEM_LIMIT),
    )(a, b)


@jax.custom_vjp
def pmm(a, b):
    return _mm(a, b, "nn", "pmm_fwd")


def _pmm_fwd(a, b):
    return _mm(a, b, "nn", "pmm_fwd"), (a, b)


def _pmm_bwd(res, dc):
    a, b = res
    return _mm(dc, b, "nt", "pmm_da"), _mm(a, dc, "tn", "pmm_db")


pmm.defvjp(_pmm_fwd, _pmm_bwd)


def _cmm_call(a, w):
    m = a.shape[0]
    nc, kc, nn = w.shape
    tm = _pick(m, (1024, 512, 256, 128))

    def body(a_ref, w_ref, o_ref):
        o_ref[...] = lax.dot_general(a_ref[...].astype(BF16), w_ref[0].astype(BF16), (((1,), (0,)), ((), ())),
                                     preferred_element_type=F32)

    return pl.pallas_call(
        body, name="cmm", out_shape=jax.ShapeDtypeStruct((m, nc * nn), F32), grid=(m // tm, nc),
        in_specs=[pl.BlockSpec((tm, kc), lambda i, c: (i, c)), pl.BlockSpec((1, kc, nn), lambda i, c: (c, 0, 0))],
        out_specs=pl.BlockSpec((tm, nn), lambda i, c: (i, c)),
        compiler_params=pltpu.CompilerParams(dimension_semantics=("parallel", "parallel"), vmem_limit_bytes=VMEM_LIMIT),
    )(a, w)


def _cmm_dw_call(a, dc, kc, nn):
    m = a.shape[0]
    nc = a.shape[1] // kc
    tm = _pick(m, (1024, 512, 256, 128))
    nm = m // tm

    def body(a_ref, d_ref, o_ref, acc_s):
        i = pl.program_id(1)

        @pl.when(i == 0)
        def _():
            acc_s[...] = jnp.zeros_like(acc_s)

        acc_s[...] += lax.dot_general(a_ref[...].astype(BF16), d_ref[...].astype(BF16), (((0,), (0,)), ((), ())),
                                      preferred_element_type=F32)

        @pl.when(i == nm - 1)
        def _():
            o_ref[0] = acc_s[...]

    return pl.pallas_call(
        body, name="cmm_dw", out_shape=jax.ShapeDtypeStruct((nc, kc, nn), F32), grid=(nc, nm),
        in_specs=[pl.BlockSpec((tm, kc), lambda c, i: (i, c)), pl.BlockSpec((tm, nn), lambda c, i: (i, c))],
        out_specs=pl.BlockSpec((1, kc, nn), lambda c, i: (c, 0, 0)),
        scratch_shapes=[pltpu.VMEM((kc, nn), F32)],
        compiler_params=pltpu.CompilerParams(dimension_semantics=("parallel", "arbitrary"), vmem_limit_bytes=VMEM_LIMIT),
    )(a, dc)


@jax.custom_vjp
def cmm(a, w):
    return _cmm_call(a, w)


def _cmm_fwd(a, w):
    return _cmm_call(a, w), (a, w)


def _cmm_bwd(res, dc):
    a, w = res
    return _cmm_call(dc, jnp.swapaxes(w, 1, 2)), _cmm_dw_call(a, dc, w.shape[1], w.shape[2])


cmm.defvjp(_cmm_fwd, _cmm_bwd)


def _cluster_diag(w, per):
    g, a, b = w.shape
    eye = jnp.eye(per, dtype=w.dtype)
    wc = w.reshape(g // per, per, a, b)
    return (wc[:, :, :, None, :] * eye[None, :, None, :, None]).reshape(g // per, per * a, per * b)


def _norm_rows(rows, d):
    return _row_tile(rows, 4 * d, 1024 * 1024)


def _rms_fwd_call(x, g, out_dtype=F32):
    rows, d = x.shape
    tr = _norm_rows(rows, d)

    def body(x_ref, g_ref, o_ref):
        xv = x_ref[...]
        o_ref[...] = (xv * lax.rsqrt(jnp.mean(xv * xv, axis=1, keepdims=True) + EPS) * g_ref[...]).astype(out_dtype)

    return pl.pallas_call(
        body, name="rms_fwd", out_shape=jax.ShapeDtypeStruct((rows, d), out_dtype), grid=(rows // tr,),
        in_specs=[pl.BlockSpec((tr, d), lambda i: (i, 0)), pl.BlockSpec((1, d), lambda i: (0, 0))],
        out_specs=pl.BlockSpec((tr, d), lambda i: (i, 0)),
        compiler_params=pltpu.CompilerParams(dimension_semantics=("parallel",), vmem_limit_bytes=VMEM_LIMIT),
    )(x, g.reshape(1, d))


def _rms_bwd_call(x, g, dy):
    rows, d = x.shape
    tr = _norm_rows(rows, d)

    def body(x_ref, g_ref, dy_ref, dx_ref, dg_ref):
        @pl.when(pl.program_id(0) == 0)
        def _():
            dg_ref[...] = jnp.zeros_like(dg_ref)

        xv = x_ref[...]
        r = lax.rsqrt(jnp.mean(xv * xv, axis=1, keepdims=True) + EPS)
        xh = xv * r
        dyv = dy_ref[...]
        dxh = dyv * g_ref[...]
        dx_ref[...] = r * (dxh - xh * jnp.mean(dxh * xh, axis=1, keepdims=True))
        dg_ref[...] += jnp.sum(dyv * xh, axis=0, keepdims=True)

    return pl.pallas_call(
        body, name="rms_bwd", out_shape=(jax.ShapeDtypeStruct((rows, d), F32), jax.ShapeDtypeStruct((1, d), F32)),
        grid=(rows // tr,),
        in_specs=[pl.BlockSpec((tr, d), lambda i: (i, 0)), pl.BlockSpec((1, d), lambda i: (0, 0)),
                  pl.BlockSpec((tr, d), lambda i: (i, 0))],
        out_specs=(pl.BlockSpec((tr, d), lambda i: (i, 0)), pl.BlockSpec((1, d), lambda i: (0, 0))),
        compiler_params=pltpu.CompilerParams(dimension_semantics=("arbitrary",), vmem_limit_bytes=VMEM_LIMIT),
    )(x, g.reshape(1, d), dy)


@jax.custom_vjp
def prms(x, g):
    return _rms_fwd_call(x, g)


def _prms_fwd(x, g):
    return _rms_fwd_call(x, g), (x, g)


def _prms_bwd(res, dy):
    x, g = res
    dx, dg = _rms_bwd_call(x, g, dy)
    return dx, dg.reshape(g.shape)


prms.defvjp(_prms_fwd, _prms_bwd)


def _pair_swap(v):
    lane = lax.broadcasted_iota(jnp.int32, v.shape, 1)
    return jnp.where(lane % 2 == 0, pltpu.roll(v, v.shape[1] - 1, 1), pltpu.roll(v, 1, 1))


def _hnorm_fwd_call(x, g, cos, sin, scale):
    rows, width = x.shape
    d = g.shape[-1]
    h = width // d
    rope = cos is not None
    tr = _row_tile(rows, 4 * width, 1024 * 1024)

    def body(*refs):
        if rope:
            x_ref, g_ref, c_ref, s_ref, o_ref = refs
        else:
            x_ref, g_ref, o_ref = refs
        for i in range(h):
            cols = slice(i * d, (i + 1) * d)
            xv = x_ref[:, cols]
            n = xv * lax.rsqrt(jnp.mean(xv * xv, axis=1, keepdims=True) + EPS) * g_ref[...]
            if rope:
                n = (c_ref[...] * n + s_ref[...] * _pair_swap(n)) * scale
            o_ref[:, cols] = n

    row_spec = pl.BlockSpec((tr, width), lambda i: (i, 0))
    tab_spec = pl.BlockSpec((tr, d), lambda i: (i, 0))
    ins = [x, g.reshape(1, d)] + ([cos, sin] if rope else [])
    return pl.pallas_call(
        body, name="head_norm_fwd", out_shape=jax.ShapeDtypeStruct((rows, width), F32), grid=(rows // tr,),
        in_specs=[row_spec, pl.BlockSpec((1, d), lambda i: (0, 0))] + ([tab_spec, tab_spec] if rope else []),
        out_specs=row_spec,
        compiler_params=pltpu.CompilerParams(dimension_semantics=("parallel",), vmem_limit_bytes=VMEM_LIMIT),
    )(*ins)


def _hnorm_bwd_call(x, g, cos, sin, scale, dy):
    rows, width = x.shape
    d = g.shape[-1]
    h = width // d
    rope = cos is not None
    tr = _row_tile(rows, 4 * width, 1024 * 1024)

    def body(*refs):
        if rope:
            x_ref, g_ref, c_ref, s_ref, dy_ref, dx_ref, dg_ref = refs
        else:
            x_ref, g_ref, dy_ref, dx_ref, dg_ref = refs

        @pl.when(pl.program_id(0) == 0)
        def _():
            dg_ref[...] = jnp.zeros_like(dg_ref)

        dg = jnp.zeros((1, d), F32)
        for i in range(h):
            cols = slice(i * d, (i + 1) * d)
            xv = x_ref[:, cols]
            r = lax.rsqrt(jnp.mean(xv * xv, axis=1, keepdims=True) + EPS)
            xh = xv * r
            dn = dy_ref[:, cols]
            if rope:
                dn = dn * scale
                dn = c_ref[...] * dn + _pair_swap(s_ref[...] * dn)
            dxh = dn * g_ref[...]
            dx_ref[:, cols] = r * (dxh - xh * jnp.mean(dxh * xh, axis=1, keepdims=True))
            dg = dg + jnp.sum(dn * xh, axis=0, keepdims=True)
        dg_ref[...] += dg

    row_spec = pl.BlockSpec((tr, width), lambda i: (i, 0))
    tab_spec = pl.BlockSpec((tr, d), lambda i: (i, 0))
    vec_spec = pl.BlockSpec((1, d), lambda i: (0, 0))
    ins = [x, g.reshape(1, d)] + ([cos, sin] if rope else []) + [dy]
    return pl.pallas_call(
        body, name="head_norm_bwd", out_shape=(jax.ShapeDtypeStruct((rows, width), F32), jax.ShapeDtypeStruct((1, d), F32)),
        grid=(rows // tr,),
        in_specs=[row_spec, vec_spec] + ([tab_spec, tab_spec] if rope else []) + [row_spec],
        out_specs=(row_spec, vec_spec),
        compiler_params=pltpu.CompilerParams(dimension_semantics=("arbitrary",), vmem_limit_bytes=VMEM_LIMIT),
    )(*ins)


@jax.custom_vjp
def head_norm(x, g):
    return _hnorm_fwd_call(x, g, None, None, 1.0)


def _head_norm_fwd(x, g):
    return _hnorm_fwd_call(x, g, None, None, 1.0), (x, g)


def _head_norm_bwd(res, dy):
    x, g = res
    dx, dg = _hnorm_bwd_call(x, g, None, None, 1.0, dy)
    return dx, dg.reshape(g.shape)


head_norm.defvjp(_head_norm_fwd, _head_norm_bwd)


@functools.partial(jax.custom_vjp, nondiff_argnums=(4,))
def head_norm_rope(x, g, cos, sin, scale):
    return _hnorm_fwd_call(x, g, cos, sin, scale)


def _head_norm_rope_fwd(x, g, cos, sin, scale):
    return _hnorm_fwd_call(x, g, cos, sin, scale), (x, g, cos, sin)


def _head_norm_rope_bwd(scale, res, dy):
    x, g, cos, sin = res
    dx, dg = _hnorm_bwd_call(x, g, cos, sin, scale, dy)
    return dx, dg.reshape(g.shape), jnp.zeros_like(cos), jnp.zeros_like(sin)


head_norm_rope.defvjp(_head_norm_rope_fwd, _head_norm_rope_bwd)


def _loss_call(x, g, target):
    rows, d = x.shape
    tr = _norm_rows(rows, d)
    nb = rows // tr

    def body(x_ref, g_ref, t_ref, l_ref, dx_ref, dg_ref):
        @pl.when(pl.program_id(0) == 0)
        def _():
            dg_ref[...] = jnp.zeros_like(dg_ref)

        xv = x_ref[...]
        r = lax.rsqrt(jnp.mean(xv * xv, axis=1, keepdims=True) + EPS)
        xh = xv * r
        err = xh * g_ref[...] - t_ref[...]
        l_ref[...] = jnp.broadcast_to(0.5 * jnp.sum(jnp.mean(err * err, axis=1, keepdims=True), axis=0, keepdims=True)[None],
                                      l_ref.shape)
        dyv = err * (1.0 / d)
        dxh = dyv * g_ref[...]
        dx_ref[...] = r * (dxh - xh * jnp.mean(dxh * xh, axis=1, keepdims=True))
        dg_ref[...] += jnp.sum(dyv * xh, axis=0, keepdims=True)

    return pl.pallas_call(
        body, name="final_loss",
        out_shape=(jax.ShapeDtypeStruct((nb, 1, LANES), F32), jax.ShapeDtypeStruct((rows, d), F32),
                   jax.ShapeDtypeStruct((1, d), F32)),
        grid=(nb,),
        in_specs=[pl.BlockSpec((tr, d), lambda i: (i, 0)), pl.BlockSpec((1, d), lambda i: (0, 0)),
                  pl.BlockSpec((tr, d), lambda i: (i, 0))],
        out_specs=(pl.BlockSpec((1, 1, LANES), lambda i: (i, 0, 0)), pl.BlockSpec((tr, d), lambda i: (i, 0)),
                   pl.BlockSpec((1, d), lambda i: (0, 0))),
        compiler_params=pltpu.CompilerParams(dimension_semantics=("arbitrary",), vmem_limit_bytes=VMEM_LIMIT),
    )(x, g.reshape(1, d), target)


_NT = (((1,), (1,)), ((), ()))
_NN = (((1,), (0,)), ((), ()))
_TN = (((0,), (0,)), ((), ()))


def _att_plan(lq, lk, hq, group):
    hp = min(ATT_HEADS_PER_STEP, hq)
    assert hq % hp == 0 and (hp % group == 0 or group % hp == 0)
    kvp = max(1, hp // group)
    return _pick(lq, (256, 128)), _pick(lk, (1024, 512, 256, 128)), hp, kvp


def _att_specs(bq, bk, dh, hp, kvp, group, q_map, k_map):
    qs = pl.BlockSpec((bq, hp * dh), lambda *g: q_map(*g))
    ls = pl.BlockSpec((bq, hp * LANES), lambda *g: q_map(*g))
    ks = pl.BlockSpec((bk, kvp * dh), lambda *g: (k_map(*g)[0], k_map(*g)[1] * hp // (group * kvp)))
    return qs, ls, ks


def _flash_fwd(q, k, v, dh, group, scale):
    lq, lk = q.shape[0], k.shape[0]
    hq = q.shape[1] // dh
    bq, bk, hp, kvp = _att_plan(lq, lk, hq, group)
    nkv = lk // bk

    def body(q_ref, k_ref, v_ref, o_ref, lse_ref, m_s, l_s, acc_s):
        j = pl.program_id(2)

        @pl.when(j == 0)
        def _():
            m_s[...] = jnp.full_like(m_s, -jnp.inf)
            l_s[...] = jnp.zeros_like(l_s)
            acc_s[...] = jnp.zeros_like(acc_s)

        for h in range(hp):
            hk = h // group if kvp > 1 else 0
            cols, lcols, kcols = slice(h * dh, (h + 1) * dh), slice(h * LANES, (h + 1) * LANES), slice(hk * dh, (hk + 1) * dh)
            s = lax.dot_general(q_ref[:, cols], k_ref[:, kcols], _NT, preferred_element_type=F32)
            if scale != 1.0:
                s = s * scale
            m_prev = m_s[:, lcols]
            m_cur = jnp.maximum(m_prev, jnp.max(s, axis=1, keepdims=True))
            alpha = jnp.exp(m_prev - m_cur)
            p = jnp.exp(s - m_cur[:, :1])
            l_s[:, lcols] = alpha * l_s[:, lcols] + jnp.sum(p, axis=1, keepdims=True)
            acc_s[:, cols] = acc_s[:, cols] * alpha[:, :1] + lax.dot_general(p.astype(BF16), v_ref[:, kcols], _NN,
                                                                           preferred_element_type=F32)
            m_s[:, lcols] = m_cur

        @pl.when(j == nkv - 1)
        def _():
            for h in range(hp):
                cols, lcols = slice(h * dh, (h + 1) * dh), slice(h * LANES, (h + 1) * LANES)
                o_ref[:, cols] = acc_s[:, cols] / l_s[:, lcols][:, :1]
                lse_ref[:, lcols] = m_s[:, lcols] + jnp.log(l_s[:, lcols])

    qs, ls, ks = _att_specs(bq, bk, dh, hp, kvp, group, lambda h, i, j: (i, h), lambda h, i, j: (j, h))
    return pl.pallas_call(
        body, name="flash_fwd",
        out_shape=(jax.ShapeDtypeStruct((lq, hq * dh), F32), jax.ShapeDtypeStruct((lq, hq * LANES), F32)),
        grid=(hq // hp, lq // bq, nkv), in_specs=[qs, ks, ks], out_specs=(qs, ls),
        scratch_shapes=[pltpu.VMEM((bq, hp * LANES), F32), pltpu.VMEM((bq, hp * LANES), F32), pltpu.VMEM((bq, hp * dh), F32)],
        compiler_params=pltpu.CompilerParams(dimension_semantics=("parallel", "parallel", "arbitrary"),
                                             vmem_limit_bytes=VMEM_LIMIT),
    )(q, k, v)


def _flash_dq(q, k, v, o, lse, do, dh, group, scale):
    lq, lk = q.shape[0], k.shape[0]
    hq = q.shape[1] // dh
    bq, bk, hp, kvp = _att_plan(lq, lk, hq, group)
    nkv = lk // bk

    def body(q_ref, k_ref, v_ref, o_ref, lse_ref, do_ref, dq_ref, dob_ref, dl_ref, acc_s):
        j = pl.program_id(2)

        @pl.when(j == 0)
        def _():
            acc_s[...] = jnp.zeros_like(acc_s)
            dob_ref[...] = do_ref[...].astype(BF16)
            for h in range(hp):
                cols, lcols = slice(h * dh, (h + 1) * dh), slice(h * LANES, (h + 1) * LANES)
                dl_ref[:, lcols] = jnp.broadcast_to(jnp.sum(do_ref[:, cols] * o_ref[:, cols], axis=1, keepdims=True),
                                                    (bq, LANES))

        for h in range(hp):
            hk = h // group if kvp > 1 else 0
            cols, lcols, kcols = slice(h * dh, (h + 1) * dh), slice(h * LANES, (h + 1) * LANES), slice(hk * dh, (hk + 1) * dh)
            kb = k_ref[:, kcols]
            s = lax.dot_general(q_ref[:, cols], kb, _NT, preferred_element_type=F32)
            if scale != 1.0:
                s = s * scale
            p = jnp.exp(s - lse_ref[:, lcols][:, :1])
            dp = lax.dot_general(dob_ref[:, cols], v_ref[:, kcols], _NT, preferred_element_type=F32)
            ds = p * (dp - dl_ref[:, lcols][:, :1])
            if scale != 1.0:
                ds = ds * scale
            acc_s[:, cols] += lax.dot_general(ds.astype(BF16), kb, _NN, preferred_element_type=F32)

        @pl.when(j == nkv - 1)
        def _():
            dq_ref[...] = acc_s[...]

    qs, ls, ks = _att_specs(bq, bk, dh, hp, kvp, group, lambda h, i, j: (i, h), lambda h, i, j: (j, h))
    return pl.pallas_call(
        body, name="flash_dq",
        out_shape=(jax.ShapeDtypeStruct((lq, hq * dh), F32), jax.ShapeDtypeStruct((lq, hq * dh), BF16),
                   jax.ShapeDtypeStruct((lq, hq * LANES), F32)),
        grid=(hq // hp, lq // bq, nkv), in_specs=[qs, ks, ks, qs, ls, qs], out_specs=(qs, qs, ls),
        scratch_shapes=[pltpu.VMEM((bq, hp * dh), F32)],
        compiler_params=pltpu.CompilerParams(dimension_semantics=("parallel", "parallel", "arbitrary"),
                                             vmem_limit_bytes=VMEM_LIMIT),
    )(q, k, v, o, lse, do)


def _flash_dkv(q, k, v, lse, dob, delta, dh, group, scale):
    lq, lk = q.shape[0], k.shape[0]
    hq = q.shape[1] // dh
    bq, bk, hp, kvp = _att_plan(lq, lk, hq, group)
    nq = lq // bq
    reps = max(1, group // hp)
    nt = reps * nq

    def body(q_ref, k_ref, v_ref, lse_ref, dob_ref, dl_ref, dk_ref, dv_ref, dk_s, dv_s):
        t = pl.program_id(2)

        @pl.when(t == 0)
        def _():
            dk_s[...] = jnp.zeros_like(dk_s)
            dv_s[...] = jnp.zeros_like(dv_s)

        for h in range(hp):
            hk = h // group if kvp > 1 else 0
            cols, lcols, kcols = slice(h * dh, (h + 1) * dh), slice(h * LANES, (h + 1) * LANES), slice(hk * dh, (hk + 1) * dh)
            qb = q_ref[:, cols]
            dob = dob_ref[:, cols]
            s = lax.dot_general(qb, k_ref[:, kcols], _NT, preferred_element_type=F32)
            if scale != 1.0:
                s = s * scale
            p = jnp.exp(s - lse_ref[:, lcols][:, :1])
            dp = lax.dot_general(dob, v_ref[:, kcols], _NT, preferred_element_type=F32)
            ds = p * (dp - dl_ref[:, lcols][:, :1])
            if scale != 1.0:
                ds = ds * scale
            dv_s[:, kcols] += lax.dot_general(p.astype(BF16), dob, _TN, preferred_element_type=F32)
            dk_s[:, kcols] += lax.dot_general(ds.astype(BF16), qb, _TN, preferred_element_type=F32)

        @pl.when(t == nt - 1)
        def _():
            dk_ref[...] = dk_s[...]
            dv_ref[...] = dv_s[...]

    hkv_blocks = (hq // group) // kvp
    qs = pl.BlockSpec((bq, hp * dh), lambda h, j, t: (t % nq, h * reps + t // nq))
    ls = pl.BlockSpec((bq, hp * LANES), lambda h, j, t: (t % nq, h * reps + t // nq))
    ks = pl.BlockSpec((bk, kvp * dh), lambda h, j, t: (j, h))
    return pl.pallas_call(
        body, name="flash_dkv",
        out_shape=(jax.ShapeDtypeStruct(k.shape, F32), jax.ShapeDtypeStruct(v.shape, F32)),
        grid=(hkv_blocks, lk // bk, nt), in_specs=[qs, ks, ks, ls, qs, ls], out_specs=(ks, ks),
        scratch_shapes=[pltpu.VMEM((bk, kvp * dh), F32), pltpu.VMEM((bk, kvp * dh), F32)],
        compiler_params=pltpu.CompilerParams(dimension_semantics=("parallel", "parallel", "arbitrary"),
                                             vmem_limit_bytes=VMEM_LIMIT),
    )(q, k, v, lse, dob, delta)


@functools.partial(jax.custom_vjp, nondiff_argnums=(3, 4, 5))
def flash(q, k, v, dh, group, scale):
    return _flash_fwd(q.astype(BF16), k.astype(BF16), v.astype(BF16), dh, group, scale)[0]


def _flash_vfwd(q, k, v, dh, group, scale):
    qb, kb, vb = q.astype(BF16), k.astype(BF16), v.astype(BF16)
    o, lse = _flash_fwd(qb, kb, vb, dh, group, scale)
    return o, (qb, kb, vb, o, lse)


def _flash_vbwd(dh, group, scale, res, do):
    q, k, v, o, lse = res
    dq, dob, delta = _flash_dq(q, k, v, o, lse, do, dh, group, scale)
    dk, dv = _flash_dkv(q, k, v, lse, dob, delta, dh, group, scale)
    return dq, dk, dv


flash.defvjp(_flash_vfwd, _flash_vbwd)


def _scan_call(lr, li, br, bi, prev_r, prev_i, reverse, name):
    rows, width = br.shape
    ts = _pick(width, (1024, 512, 256, 128))
    tr = _pick(rows, (256, 128, 64, 32, 16, 8))
    nb = rows // tr
    with_prev = prev_r is not None

    def body(*refs):
        if with_prev:
            lr_ref, li_ref, br_ref, bi_ref, pr_ref, pi_ref, sr_ref, si_ref, dr_ref, di_ref, cr_s, ci_s = refs
        else:
            lr_ref, li_ref, br_ref, bi_ref, sr_ref, si_ref, pr_ref, pi_ref, cr_s, ci_s = refs
        i = pl.program_id(1)

        @pl.when(i == 0)
        def _():
            cr_s[...] = jnp.zeros_like(cr_s)
            ci_s[...] = jnp.zeros_like(ci_s)
            if with_prev:
                dr_ref[...] = jnp.zeros_like(dr_ref)
                di_ref[...] = jnp.zeros_like(di_ref)

        a_re = lr_ref[...]
        a_im = li_ref[...]

        def step(r, carry):
            if with_prev:
                c_re, c_im, d_re, d_im = carry
            else:
                c_re, c_im = carry
            t = tr - 1 - r if reverse else r
            row = pl.ds(t, 1)
            n_re = a_re * c_re - a_im * c_im + br_ref[row, :]
            n_im = a_re * c_im + a_im * c_re + bi_ref[row, :]
            sr_ref[row, :] = n_re
            si_ref[row, :] = n_im
            if with_prev:
                p_re = pr_ref[row, :]
                p_im = pi_ref[row, :]
                return n_re, n_im, d_re + n_re * p_re + n_im * p_im, d_im + n_im * p_re - n_re * p_im
            pr_ref[row, :] = c_re
            pi_ref[row, :] = c_im
            return n_re, n_im

        zero = jnp.zeros((1, ts), F32)
        init = (cr_s[...], ci_s[...]) + ((zero, zero) if with_prev else ())
        out = lax.fori_loop(0, tr, step, init, unroll=8)
        cr_s[...] = out[0]
        ci_s[...] = out[1]
        if with_prev:
            dr_ref[...] += out[2]
            di_ref[...] += out[3]

    blk = (lambda j, i: (nb - 1 - i, j)) if reverse else (lambda j, i: (i, j))
    row_spec = pl.BlockSpec((tr, ts), blk)
    vec_spec = pl.BlockSpec((1, ts), lambda j, i: (0, j))
    full = jax.ShapeDtypeStruct((rows, width), F32)
    vec = jax.ShapeDtypeStruct((1, width), F32)
    ins = [lr, li, br, bi] + ([prev_r, prev_i] if with_prev else [])
    return pl.pallas_call(
        body, name=name, grid=(width // ts, nb),
        in_specs=[vec_spec, vec_spec] + [row_spec] * (len(ins) - 2),
        out_shape=(full, full, vec, vec) if with_prev else (full, full, full, full),
        out_specs=(row_spec, row_spec, vec_spec, vec_spec) if with_prev else (row_spec,) * 4,
        scratch_shapes=[pltpu.VMEM((1, ts), F32), pltpu.VMEM((1, ts), F32)],
        compiler_params=pltpu.CompilerParams(dimension_semantics=("parallel", "arbitrary"), vmem_limit_bytes=VMEM_LIMIT),
    )(*ins)


@functools.partial(jax.custom_vjp, nondiff_argnums=(4,))
def cscan(lr, li, br, bi, reverse):
    out = _scan_call(lr, li, br, bi, None, None, reverse, "s5_scan")
    return out[0], out[1]


def _cscan_fwd(lr, li, br, bi, reverse):
    sr, si, pr, pi = _scan_call(lr, li, br, bi, None, None, reverse, "s5_scan")
    return (sr, si), (lr, li, pr, pi)


def _cscan_bwd(reverse, res, g):
    lr, li, pr, pi = res
    ar, ai, dlr, dli = _scan_call(lr, -li, g[0], g[1], pr, pi, not reverse, "s5_scan_bwd")
    return dlr, dli, ar, ai


cscan.defvjp(_cscan_fwd, _cscan_bwd)


_BDIMS = {"nn": (((2,), (1,)), ((0,), (0,))), "nt": (((2,), (2,)), ((0,), (0,))), "tn": (((1,), (1,)), ((0,), (0,)))}


def _bdot_raw(a, b, mode):
    return lax.dot_general(a.astype(BF16), b.astype(BF16), _BDIMS[mode], preferred_element_type=F32)


@functools.partial(jax.custom_vjp, nondiff_argnums=(2,))
def _bdot(a, b, mode):
    return _bdot_raw(a, b, mode)


def _bdot_fwd(a, b, mode):
    return _bdot_raw(a, b, mode), (a, b)


def _bdot_bwd(mode, res, dc):
    a, b = res
    if mode == "nn":
        return _bdot_raw(dc, b, "nt"), _bdot_raw(a, dc, "tn")
    if mode == "nt":
        return _bdot_raw(dc, b, "nn"), _bdot_raw(dc, a, "tn")
    return _bdot_raw(b, dc, "nt"), _bdot_raw(a, dc, "nn")


_bdot.defvjp(_bdot_fwd, _bdot_bwd)


def _split(a):
    hi = a.astype(BF16)
    return hi, (a - hi.astype(F32)).astype(BF16)


def _fdot_raw(a, b, mode):
    a_hi, a_lo = _split(a)
    b_hi, b_lo = _split(b)
    dot = lambda u, v: lax.dot_general(u, v, _BDIMS[mode], preferred_element_type=F32)
    return dot(a_hi, b_hi) + (dot(a_hi, b_lo) + dot(a_lo, b_hi))


@jax.custom_vjp
def _fdot(a, b):
    return _fdot_raw(a, b, "nn")


def _fdot_fwd(a, b):
    return _fdot_raw(a, b, "nn"), (a, b)


def _fdot_bwd(res, dc):
    a, b = res
    return _fdot_raw(dc, b, "nt"), _fdot_raw(a, dc, "tn")


_fdot.defvjp(_fdot_fwd, _fdot_bwd)


def _dn_chunk(state, q, k, v, bb, gc, gc64, gr, direction):
    c = q.shape[1]
    row = lax.broadcasted_iota(jnp.int32, (c, c), 0)
    col = lax.broadcasted_iota(jnp.int32, (c, c), 1)
    sgn = 1 - 2 * direction
    after = ((row - col) * sgn > 0)[None]
    incl = ((row - col) * sgn >= 0)[None]
    eye = (row == col).astype(F32)[None]
    kb = k * bb
    decay = jnp.where(incl, jnp.exp(jnp.where(incl, gc64 - gr, 0.0)), 0.0)
    lower = jnp.where(after, _bdot(kb, k, "nt") * decay, 0.0)
    inv = eye - lower
    power = _fdot(lower, lower)
    span = 2
    while span < c:
        inv = inv + _fdot(inv, power)
        span *= 2
        if span < c:
            power = _fdot(power, power)
    eg = jnp.exp(gc)
    u = _fdot(inv, v * bb)
    w = _fdot(inv, kb * eg)
    intra = _bdot(q, k, "nt") * decay
    last = (lax.broadcasted_iota(jnp.int32, (c, 1), 0) == (c - 1) * (1 - direction)).astype(F32)[None]
    g_last = jnp.sum(gc * last, axis=1, keepdims=True)
    k_dec = k * jnp.exp(g_last - gc)
    v_new = u - _bdot(w, state, "nn")
    o = _bdot(q * eg, state, "nn") + _bdot(intra, v_new, "nn")
    new_state = state * jnp.exp(g_last) + _bdot(k_dec, v_new, "tn")
    return new_state, o


def _dn_rows(lq):
    return _pick(lq, (256, 128, 64))


def _dn_prep(q, k, v):
    q, k, v = jax.nn.silu(q), jax.nn.silu(k), jax.nn.silu(v)
    q = q * lax.rsqrt(jnp.sum(q * q, axis=2, keepdims=True) + EPS) * (q.shape[2] ** -0.5)
    k = k * lax.rsqrt(jnp.sum(k * k, axis=2, keepdims=True) + EPS)
    return q, k, v


def _dn_step(state, q, k, v, bb, gc, gc64, gr, direction):
    q, k, v = _dn_prep(q, k, v)
    return _dn_chunk(state, q, k, v, bb, gc, gc64, gr, direction)


def _dn_heads(ref, rows, part, h, d):
    return jnp.stack([ref[rows, (part * h + i) * d:(part * h + i + 1) * d] for i in range(h)])


def _dn_fwd_call(qkv, bb, gc, gc64, gr):
    _, h, lq, d = bb.shape
    c = DN_CHUNK
    t = _dn_rows(lq)
    nb, nc = lq // t, t // c

    def body(qkv_ref, bb_ref, gc_ref, g64_ref, gr_ref, o_ref, st_ref, s_s):
        direction = pl.program_id(0)

        @pl.when(pl.program_id(1) == 0)
        def _():
            s_s[...] = jnp.zeros_like(s_s)

        def step(j, state):
            ci = j + direction * (nc - 1 - 2 * j)
            rows = pl.ds(pl.multiple_of(ci * c, c), c)
            st_ref[0, ci] = state
            state, o = _dn_step(state, _dn_heads(qkv_ref, rows, 0, h, d), _dn_heads(qkv_ref, rows, 1, h, d),
                                _dn_heads(qkv_ref, rows, 2, h, d), bb_ref[0, :, rows, :], gc_ref[0, :, rows, :],
                                g64_ref[0, :, rows, :], gr_ref[0, :, ci], direction)
            for i in range(h):
                o_ref[0, rows, i * d:(i + 1) * d] = o[i]
            return state

        s_s[...] = lax.fori_loop(0, nc, step, s_s[...])

    blk = lambda dd, i: i + dd * (nb - 1 - 2 * i)
    ds = pl.BlockSpec((1, h, t, d), lambda dd, i: (dd, 0, blk(dd, i), 0))
    return pl.pallas_call(
        body, name="deltanet_fwd", grid=(2, nb),
        in_specs=[pl.BlockSpec((t, 3 * h * d), lambda dd, i: (blk(dd, i), 0)), ds, ds,
                  pl.BlockSpec((1, h, t, c), lambda dd, i: (dd, 0, blk(dd, i), 0)),
                  pl.BlockSpec((1, h, nc, 1, c), lambda dd, i: (dd, 0, blk(dd, i), 0, 0))],
        out_shape=(jax.ShapeDtypeStruct((2, lq, h * d), F32), jax.ShapeDtypeStruct((2, lq // c, h, d, d), F32)),
        out_specs=(pl.BlockSpec((1, t, h * d), lambda dd, i: (dd, blk(dd, i), 0)),
                   pl.BlockSpec((1, nc, h, d, d), lambda dd, i: (dd, blk(dd, i), 0, 0, 0))),
        scratch_shapes=[pltpu.VMEM((h, d, d), F32)],
        compiler_params=pltpu.CompilerParams(dimension_semantics=("parallel", "arbitrary"), vmem_limit_bytes=VMEM_LIMIT),
    )(qkv, bb, gc, gc64, gr)


def _dn_bwd_call(qkv, bb, gc, gc64, gr, states, do):
    _, h, lq, d = bb.shape
    c = DN_CHUNK
    t = _dn_rows(lq)
    nb, nc = lq // t, t // c

    def body(qkv_ref, bb_ref, gc_ref, g64_ref, gr_ref, st_ref, do_ref, dqkv_ref, dbb_ref, dgc_ref, dg64_ref, dgr_ref, ds_s):
        direction = pl.program_id(0)

        @pl.when(pl.program_id(1) == 0)
        def _():
            ds_s[...] = jnp.zeros_like(ds_s)

        def step(j, dstate):
            ci = (nc - 1 - j) + direction * (2 * j - (nc - 1))
            rows = pl.ds(pl.multiple_of(ci * c, c), c)
            args = (st_ref[0, ci], _dn_heads(qkv_ref, rows, 0, h, d), _dn_heads(qkv_ref, rows, 1, h, d),
                    _dn_heads(qkv_ref, rows, 2, h, d), bb_ref[0, :, rows, :], gc_ref[0, :, rows, :],
                    g64_ref[0, :, rows, :], gr_ref[0, :, ci])
            _, pull = jax.vjp(lambda *a: _dn_step(*a, direction), *args)
            do_c = jnp.stack([do_ref[rows, i * d:(i + 1) * d] for i in range(h)])
            dstate, dq, dk, dv, dbb, dgc, dg64, dgr = pull((dstate, do_c))
            for part, val in enumerate((dq, dk, dv)):
                for i in range(h):
                    dqkv_ref[0, rows, (part * h + i) * d:(part * h + i + 1) * d] = val[i]
            dbb_ref[0, :, rows, :] = dbb
            dgc_ref[0, :, rows, :] = dgc
            dg64_ref[0, :, rows, :] = dg64
            dgr_ref[0, :, ci] = dgr
            return dstate

        ds_s[...] = lax.fori_loop(0, nc, step, ds_s[...])

    blk = lambda dd, i: (nb - 1 - i) + dd * (2 * i - (nb - 1))
    ds = pl.BlockSpec((1, h, t, d), lambda dd, i: (dd, 0, blk(dd, i), 0))
    g64s = pl.BlockSpec((1, h, t, c), lambda dd, i: (dd, 0, blk(dd, i), 0))
    grs = pl.BlockSpec((1, h, nc, 1, c), lambda dd, i: (dd, 0, blk(dd, i), 0, 0))
    big = jax.ShapeDtypeStruct((2, h, lq, d), F32)
    return pl.pallas_call(
        body, name="deltanet_bwd", grid=(2, nb),
        in_specs=[pl.BlockSpec((t, 3 * h * d), lambda dd, i: (blk(dd, i), 0)), ds, ds, g64s, grs,
                  pl.BlockSpec((1, nc, h, d, d), lambda dd, i: (dd, blk(dd, i), 0, 0, 0)),
                  pl.BlockSpec((t, h * d), lambda dd, i: (blk(dd, i), 0))],
        out_shape=(jax.ShapeDtypeStruct((2, lq, 3 * h * d), F32), big, big, jax.ShapeDtypeStruct((2, h, lq, c), F32),
                   jax.ShapeDtypeStruct((2, h, lq // c, 1, c), F32)),
        out_specs=(pl.BlockSpec((1, t, 3 * h * d), lambda dd, i: (dd, blk(dd, i), 0)), ds, ds, g64s, grs),
        scratch_shapes=[pltpu.VMEM((h, d, d), F32)],
        compiler_params=pltpu.CompilerParams(dimension_semantics=("parallel", "arbitrary"), vmem_limit_bytes=VMEM_LIMIT),
    )(qkv, bb, gc, gc64, gr, states, do)


@jax.custom_vjp
def dn_rule(qkv, bb, gc, gc64, gr):
    o = _dn_fwd_call(qkv, bb, gc, gc64, gr)[0]
    return o[0] + o[1]


def _dn_rule_fwd(qkv, bb, gc, gc64, gr):
    o, states = _dn_fwd_call(qkv, bb, gc, gc64, gr)
    return o[0] + o[1], (qkv, bb, gc, gc64, gr, states)


def _dn_rule_bwd(res, do):
    dqkv, dbb, dgc, dg64, dgr = _dn_bwd_call(*res, do)
    return dqkv[0] + dqkv[1], dbb, dgc, dg64, dgr


dn_rule.defvjp(_dn_rule_fwd, _dn_rule_bwd)


def gated_delta_both(qkv, beta2, g2):
    _, h, lq = beta2.shape
    d = qkv.shape[1] // (3 * h)
    c = DN_CHUNK
    gch = g2.reshape(2, h, lq // c, c)
    cum = jnp.stack([jnp.cumsum(gch[0], axis=-1), lax.cumsum(gch[1], axis=2, reverse=True)])
    flat = cum.reshape(2, h, lq)
    return dn_rule(qkv, jnp.broadcast_to(beta2[..., None], (2, h, lq, d)), jnp.broadcast_to(flat[..., None], (2, h, lq, d)),
                   jnp.broadcast_to(flat[..., None], (2, h, lq, c)), cum.reshape(2, h, lq // c, 1, c))


def _merge_rows(rows, d, nb):
    return _row_tile(rows, 4 * nb * d, 2 * 1024 * 1024)


def _merge_fwd_call(logits, ts):
    rows, d = ts[0].shape
    nb = len(ts)
    tr = _merge_rows(rows, d, nb)

    def body(gl_ref, *refs):
        t_refs, o_ref = refs[:nb], refs[nb]
        acc = jax.nn.sigmoid(gl_ref[:, 0:d]) * t_refs[0][...]
        for b in range(1, nb):
            acc = acc + jax.nn.sigmoid(gl_ref[:, b * d:(b + 1) * d]) * t_refs[b][...]
        o_ref[...] = acc

    wide = pl.BlockSpec((tr, nb * d), lambda i: (i, 0))
    one = pl.BlockSpec((tr, d), lambda i: (i, 0))
    return pl.pallas_call(
        body, name="gate_merge", out_shape=jax.ShapeDtypeStruct((rows, d), F32), grid=(rows // tr,),
        in_specs=[wide] + [one] * nb, out_specs=one,
        compiler_params=pltpu.CompilerParams(dimension_semantics=("parallel",), vmem_limit_bytes=VMEM_LIMIT),
    )(logits, *ts)


def _merge_bwd_call(logits, ts, dm):
    rows, d = ts[0].shape
    nb = len(ts)
    tr = _merge_rows(rows, d, nb)

    def body(gl_ref, *refs):
        t_refs, dm_ref, dgl_ref, dt_refs = refs[:nb], refs[nb], refs[nb + 1], refs[nb + 2:]
        dmv = dm_ref[...]
        for b in range(nb):
            s = jax.nn.sigmoid(gl_ref[:, b * d:(b + 1) * d])
            dt_refs[b][...] = dmv * s
            dgl_ref[:, b * d:(b + 1) * d] = dmv * t_refs[b][...] * (s * (1.0 - s))

    wide = pl.BlockSpec((tr, nb * d), lambda i: (i, 0))
    one = pl.BlockSpec((tr, d), lambda i: (i, 0))
    out = jax.ShapeDtypeStruct((rows, d), F32)
    return pl.pallas_call(
        body, name="gate_merge_bwd", out_shape=(jax.ShapeDtypeStruct(logits.shape, F32),) + (out,) * nb, grid=(rows // tr,),
        in_specs=[wide] + [one] * (nb + 1), out_specs=(wide,) + (one,) * nb,
        compiler_params=pltpu.CompilerParams(dimension_semantics=("parallel",), vmem_limit_bytes=VMEM_LIMIT),
    )(logits, *ts, dm)


@jax.custom_vjp
def gate_merge(logits, *ts):
    return _merge_fwd_call(logits, ts)


def _gate_merge_fwd(logits, *ts):
    return _merge_fwd_call(logits, ts), (logits, ts)


def _gate_merge_bwd(res, dm):
    logits, ts = res
    return tuple(_merge_bwd_call(logits, ts, dm))


gate_merge.defvjp(_gate_merge_fwd, _gate_merge_bwd)


def _conv_rows(rows):
    return _pick(rows, (256, 128, 64, 32, 16, 8))


def _conv_taps(x_ref, prev_ref, next_ref, i, nb, t):
    keep_prev = (i > 0).astype(F32)
    keep_next = (i < nb - 1).astype(F32)
    ext = jnp.concatenate([prev_ref[...] * keep_prev, x_ref[...], next_ref[...] * keep_next], axis=0)
    total = t + 2 * CONV_HALO
    return [pltpu.roll(ext, (DN_CONV // 2 - j) % total, 0)[CONV_HALO:CONV_HALO + t] for j in range(DN_CONV)]


def _conv_specs(t, ch, rows):
    per = t // CONV_HALO
    last = rows // CONV_HALO - 1
    return [pl.BlockSpec((t, ch), lambda i: (i, 0)),
            pl.BlockSpec((CONV_HALO, ch), lambda i: (jnp.maximum(i * per - 1, 0), 0)),
            pl.BlockSpec((CONV_HALO, ch), lambda i: (jnp.minimum((i + 1) * per, last), 0))]


def _conv_call(x, taps):
    rows, ch = x.shape
    t = _conv_rows(rows)
    nb = rows // t

    def body(x_ref, prev_ref, next_ref, w_ref, o_ref):
        views = _conv_taps(x_ref, prev_ref, next_ref, pl.program_id(0), nb, t)
        acc = views[0] * w_ref[0:1, :]
        for j in range(1, DN_CONV):
            acc = acc + views[j] * w_ref[j:j + 1, :]
        o_ref[...] = acc

    return pl.pallas_call(
        body, name="short_conv", out_shape=jax.ShapeDtypeStruct((rows, ch), F32), grid=(nb,),
        in_specs=_conv_specs(t, ch, rows) + [pl.BlockSpec((8, ch), lambda i: (0, 0))],
        out_specs=pl.BlockSpec((t, ch), lambda i: (i, 0)),
        compiler_params=pltpu.CompilerParams(dimension_semantics=("parallel",), vmem_limit_bytes=VMEM_LIMIT),
    )(x, x, x, taps)


def _conv_dw_call(x, dy):
    rows, ch = x.shape
    t = _conv_rows(rows)
    nb = rows // t

    def body(x_ref, prev_ref, next_ref, dy_ref, o_ref):
        @pl.when(pl.program_id(0) == 0)
        def _():
            o_ref[...] = jnp.zeros_like(o_ref)

        views = _conv_taps(x_ref, prev_ref, next_ref, pl.program_id(0), nb, t)
        dyv = dy_ref[...]
        for j in range(DN_CONV):
            o_ref[j:j + 1, :] += jnp.sum(views[j] * dyv, axis=0, keepdims=True)

    return pl.pallas_call(
        body, name="short_conv_dw", out_shape=jax.ShapeDtypeStruct((8, ch), F32), grid=(nb,),
        in_specs=_conv_specs(t, ch, rows) + [pl.BlockSpec((t, ch), lambda i: (i, 0))],
        out_specs=pl.BlockSpec((8, ch), lambda i: (0, 0)),
        compiler_params=pltpu.CompilerParams(dimension_semantics=("arbitrary",), vmem_limit_bytes=VMEM_LIMIT),
    )(x, x, x, dy)


def _taps(w, flip):
    wt = jnp.transpose(w)
    if flip:
        wt = wt[::-1]
    return jnp.pad(wt, ((0, 8 - DN_CONV), (0, 0)))


@jax.custom_vjp
def short_conv(x, w):
    return _conv_call(x, _taps(w, False))


def _short_conv_fwd(x, w):
    return _conv_call(x, _taps(w, False)), (x, w)


def _short_conv_bwd(res, dy):
    x, w = res
    return _conv_call(dy, _taps(w, True)), jnp.transpose(_conv_dw_call(x, dy)[:DN_CONV])


short_conv.defvjp(_short_conv_fwd, _short_conv_bwd)


def _s5_mixer(u, a_re, a_im, log_step, b_re, b_im, c_re, c_im, d, w_glu, b_glu):
    per = LANES // SSM_GROUP
    y = u * d
    for direction in range(2):
        are, aim = a_re[direction], a_im[direction]
        step = jnp.exp(log_step[direction])[:, None]
        mag = jnp.exp(are * step)
        lam_re = mag * jnp.cos(aim * step)
        lam_im = mag * jnp.sin(aim * step)
        den = are * are + aim * aim
        nr = lam_re - 1.0
        ni = lam_im
        coef_re = (nr * are + ni * aim) / den
        coef_im = (ni * are - nr * aim) / den
        bb_re = coef_re[..., None] * b_re[direction] - coef_im[..., None] * b_im[direction]
        bb_im = coef_re[..., None] * b_im[direction] + coef_im[..., None] * b_re[direction]
        bu_re = cmm(u, _cluster_diag(jnp.swapaxes(bb_re, 1, 2), per))
        bu_im = cmm(u, _cluster_diag(jnp.swapaxes(bb_im, 1, 2), per))
        width = bu_re.shape[1]
        s_re, s_im = cscan(lam_re.reshape(1, width), lam_im.reshape(1, width), bu_re, bu_im, direction == 1)
        y = y + cmm(s_re, _cluster_diag(jnp.swapaxes(c_re[direction], 1, 2), per))
        y = y - cmm(s_im, _cluster_diag(jnp.swapaxes(c_im[direction], 1, 2), per))
    y = jax.nn.gelu(y)
    return y * jax.nn.sigmoid(pmm(y, w_glu) + b_glu)


def _deltanet_mixer(q, k, v, ab, conv_w, a_log, dt_bias, norm_g):
    seq = q.shape[0]
    qkv = short_conv(jnp.concatenate([q, k, v], axis=-1), conv_w)
    a4 = ab[:, :2 * DN_HEADS].reshape(seq, 2, DN_HEADS)
    b4 = ab[:, 2 * DN_HEADS:4 * DN_HEADS].reshape(seq, 2, DN_HEADS)
    beta = jax.nn.sigmoid(b4)
    g = -jnp.exp(a_log) * jax.nn.softplus(a4 + dt_bias)
    o = gated_delta_both(qkv, jnp.transpose(beta, (1, 2, 0)), jnp.transpose(g, (1, 2, 0)))
    return head_norm(o, norm_g)


def _axial_rope(seq):
    rows = seq // GRID_W
    row = jnp.repeat(jnp.arange(rows), GRID_W).astype(F32)
    col = jnp.tile(jnp.arange(GRID_W), rows).astype(F32)
    axis_dim = ATT_HEAD_DIM // 2
    freqs = ROPE_THETA ** (-jnp.arange(0, axis_dim, 2, dtype=F32) / axis_dim)
    ang = jnp.concatenate([row[:, None] * freqs, col[:, None] * freqs], axis=-1)
    sign = jnp.tile(jnp.array([-1.0, 1.0], F32), ATT_HEAD_DIM // 2)
    return jnp.repeat(jnp.cos(ang), 2, axis=-1), jnp.repeat(jnp.sin(ang), 2, axis=-1) * sign


def _grid_attention(q, k, v, qn_g, kn_g, cos, sin):
    qh = head_norm_rope(q, qn_g, cos, sin, ATT_HEAD_DIM ** -0.5)
    kh = head_norm_rope(k, kn_g, cos, sin, 1.0)
    return flash(qh, kh, v, ATT_HEAD_DIM, ATT_HEADS // ATT_KV_HEADS, 1.0)


def _memory_attention(q, mem_n, w_kv):
    kv = pmm(mem_n, w_kv)
    mem_w = MEM_HEADS * MEM_HEAD_DIM
    return flash(q, kv[:, :mem_w], kv[:, mem_w:], MEM_HEAD_DIM, 1, MEM_HEAD_DIM ** -0.5)


def _middle(h, x, mem, p):
    d_model = x.shape[1]
    lay, _, _ = _layout(d_model)

    def seg(name):
        _, dst, w, _ = lay[name]
        return h[:, dst:dst + w]

    cos, sin = _axial_rope(x.shape[0])
    y_a = _s5_mixer(seg("u"), p["ssm_a_re"], p["ssm_a_im"], p["ssm_log_step"], p["ssm_b_re"], p["ssm_b_im"],
                    p["ssm_c_re"], p["ssm_c_im"], p["ssm_d"], p["ssm_w_glu"], p["ssm_b_glu"]) * jax.nn.silu(seg("z_a"))
    y_b = _deltanet_mixer(seg("dq"), seg("dk"), seg("dv"), seg("dab"), p["dn_conv"], p["dn_a_log"],
                          p["dn_dt_bias"], p["dn_norm_g"]) * jax.nn.silu(seg("z_b"))
    y_c = _grid_attention(seg("aq"), seg("ak"), seg("av"), p["attn_q_norm"], p["attn_k_norm"], cos, sin) * jax.nn.silu(seg("z_c"))
    y_m = _memory_attention(seg("mq"), prms(mem, p["mem_norm_g"]), p["w_mem_kv"]) * jax.nn.silu(seg("z_m"))
    lifted, off = [], 0
    for y_br in (y_a, y_b, y_c, y_m):
        w = y_br.shape[1]
        lifted.append(pmm(y_br, p["w_branch"][off:off + w]))
        off += w
    return x + pmm(gate_merge(seg("gates"), *lifted), p["w_out"])


def _local_step(x, mem, target, w):
    depth = w["norm_g"].shape[0]
    d_model = x.shape[1]
    mid_names = [n for n in WEIGHTS if n not in ("norm_g", "w_in", "final_norm_g")]
    saved = []
    cur = x
    for layer in range(depth):
        xn = _rms_fwd_call(cur, w["norm_g"][layer], BF16)
        h = _mm(xn, w["w_in"][layer], "nn", "in_proj")
        p = {n: w[n][layer] for n in mid_names}
        nxt, vjp_mid = jax.vjp(lambda h_, x_, p_: _middle(h_, x_, mem, p_), h, cur, p)
        saved.append((cur, xn, vjp_mid))
        cur = nxt
    loss_parts, dx, d_final = _loss_call(cur, w["final_norm_g"], target)
    loss = jnp.sum(loss_parts[:, 0, 0])
    d_final = d_final.reshape(w["final_norm_g"].shape)
    grads = {n: [None] * depth for n in WEIGHTS if n != "final_norm_g"}
    for layer in reversed(range(depth)):
        x_in, xn, vjp_mid = saved[layer]
        dh, dx_skip, dp = vjp_mid(dx)
        grads["w_in"][layer] = _unpad_w_in(_mm(xn, dh, "tn", "in_proj_dw"), d_model)
        dxn = _mm(dh, w["w_in"][layer], "nt", "in_proj_dx")
        dx_n, dg = _rms_bwd_call(x_in, w["norm_g"][layer], dxn)
        dx = dx_skip + dx_n
        grads["norm_g"][layer] = dg.reshape(w["norm_g"][layer].shape)
        for n in mid_names:
            grads[n][layer] = dp[n]
    out = {n: jnp.stack(v) for n, v in grads.items()}
    out["final_norm_g"] = d_final
    return loss, dx, out


_ANY = pl.BlockSpec(memory_space=pl.ANY)
_CHIP_FLIPS = [(1, 0), (0, 1), (1, 1)]


def _remote(src, dst, send_sem, recv_sem, dev):
    return pltpu.make_async_remote_copy(src_ref=src, dst_ref=dst, send_sem=send_sem, recv_sem=recv_sem,
                                        device_id=dev, device_id_type=pl.DeviceIdType.MESH)


def _flip(v, bit):
    return 1 - v if bit else v


def _gather_chips(shards):
    n = len(shards)

    def body(*refs):
        ins, outs = refs[:n], refs[n:2 * n]
        send_sems, recv_sems, local_sems = refs[2 * n:]
        x, y, c = lax.axis_index("x"), lax.axis_index("y"), lax.axis_index("c")
        locals_, sends = [], []
        for a in range(n):
            cp = pltpu.make_async_copy(ins[a], outs[a].at[2 * x + y], local_sems.at[a])
            cp.start()
            locals_.append(cp)

        def half(a, px, py, pc):
            rh = shards[a].shape[1] // 2
            return outs[a].at[2 * px + py, :, pl.ds(pc * rh, rh), :]

        for a in range(n):
            rh = shards[a].shape[1] // 2
            for j, (fx, fy) in enumerate(_CHIP_FLIPS):
                cp = _remote(ins[a].at[:, pl.ds(c * rh, rh), :], half(a, x, y, c), send_sems.at[6 * a + j],
                             recv_sems.at[6 * a + j], (_flip(x, fx), _flip(y, fy), c))
                cp.start()
                sends.append(cp)
        for a in range(n):
            for j, (fx, fy) in enumerate(_CHIP_FLIPS):
                blk = half(a, _flip(x, fx), _flip(y, fy), c)
                _remote(blk, blk, send_sems.at[6 * a + j], recv_sems.at[6 * a + j], (x, y, c)).wait_recv()
                cp = _remote(blk, blk, send_sems.at[6 * a + 3 + j], recv_sems.at[6 * a + 3 + j], (x, y, 1 - c))
                cp.start()
                sends.append(cp)
        for a in range(n):
            for j, (fx, fy) in enumerate(_CHIP_FLIPS):
                blk = half(a, _flip(x, fx), _flip(y, fy), 1 - c)
                _remote(blk, blk, send_sems.at[6 * a + 3 + j], recv_sems.at[6 * a + 3 + j], (x, y, c)).wait_recv()
        for cp in sends:
            cp.wait_send()
        for cp in locals_:
            cp.wait()

    return pl.pallas_call(
        body, name="gather_weights",
        out_shape=tuple(jax.ShapeDtypeStruct((N_CHIPS,) + s.shape, s.dtype) for s in shards),
        in_specs=[_ANY] * n, out_specs=tuple([_ANY] * n),
        scratch_shapes=[pltpu.SemaphoreType.DMA((6 * n,)), pltpu.SemaphoreType.DMA((6 * n,)),
                        pltpu.SemaphoreType.DMA((n,))],
    )(*shards)


def _exchange_pair(parts):
    n = len(parts)

    def body(*refs):
        ins, outs = refs[:n], refs[n:2 * n]
        send_sems, recv_sems = refs[2 * n:]
        x, y, c = lax.axis_index("x"), lax.axis_index("y"), lax.axis_index("c")
        started = []
        for a in range(n):
            rh = parts[a].shape[2] // 2
            cp = _remote(ins[a].at[:, :, pl.ds((1 - c) * rh, rh), :], outs[a], send_sems.at[a], recv_sems.at[a], (x, y, 1 - c))
            cp.start()
            started.append(cp)
        for a in range(n):
            _remote(outs[a], outs[a], send_sems.at[a], recv_sems.at[a], (x, y, c)).wait_recv()
        for cp in started:
            cp.wait_send()

    return pl.pallas_call(
        body, name="exchange_pair",
        out_shape=tuple(jax.ShapeDtypeStruct((p.shape[0], p.shape[1], p.shape[2] // 2, p.shape[3]), p.dtype) for p in parts),
        in_specs=[_ANY] * n, out_specs=tuple([_ANY] * n),
        scratch_shapes=[pltpu.SemaphoreType.DMA((n,)), pltpu.SemaphoreType.DMA((n,))],
    )(*parts)


def _exchange_chips(pair_sums, small):
    n = len(pair_sums)

    def body(*refs):
        ins, small_in = refs[:n], refs[n]
        outs, small_out = refs[n + 1:2 * n + 1], refs[2 * n + 1]
        send_sems, recv_sems, small_send, small_recv, local_sem = refs[2 * n + 2:]
        x, y, c = lax.axis_index("x"), lax.axis_index("y"), lax.axis_index("c")
        me = 4 * x + 2 * y + c
        started = []
        own = pltpu.make_async_copy(small_in, small_out.at[me], local_sem)
        own.start()
        for f, (fx, fy) in enumerate(_CHIP_FLIPS):
            px, py = _flip(x, fx), _flip(y, fy)
            for a in range(n):
                cp = _remote(ins[a].at[2 * px + py], outs[a].at[f], send_sems.at[3 * a + f], recv_sems.at[3 * a + f], (px, py, c))
                cp.start()
                started.append(cp)
        for k in range(1, N_DEV):
            peer = (_flip(x, k >> 2 & 1), _flip(y, k >> 1 & 1), _flip(c, k & 1))
            cp = _remote(small_in, small_out.at[me], small_send.at[k - 1], small_recv.at[k - 1], peer)
            cp.start()
            started.append(cp)
        for f in range(len(_CHIP_FLIPS)):
            for a in range(n):
                _remote(outs[a].at[f], outs[a].at[f], send_sems.at[3 * a + f], recv_sems.at[3 * a + f], (x, y, c)).wait_recv()
        for k in range(1, N_DEV):
            peer = 4 * _flip(x, k >> 2 & 1) + 2 * _flip(y, k >> 1 & 1) + _flip(c, k & 1)
            _remote(small_out.at[peer], small_out.at[peer], small_send.at[k - 1], small_recv.at[k - 1], (x, y, c)).wait_recv()
        for cp in started:
            cp.wait_send()
        own.wait()

    shapes = tuple(jax.ShapeDtypeStruct((3,) + p.shape[1:], p.dtype) for p in pair_sums)
    shapes += (jax.ShapeDtypeStruct((N_DEV,) + small.shape, small.dtype),)
    return pl.pallas_call(
        body, name="exchange_chips", out_shape=shapes, in_specs=[_ANY] * (n + 1), out_specs=tuple([_ANY] * (n + 1)),
        scratch_shapes=[pltpu.SemaphoreType.DMA((3 * n,)), pltpu.SemaphoreType.DMA((3 * n,)),
                        pltpu.SemaphoreType.DMA((N_DEV - 1,)), pltpu.SemaphoreType.DMA((N_DEV - 1,)),
                        pltpu.SemaphoreType.DMA(())],
    )(*pair_sums, small)


def _swap_halves(arrays):
    n = len(arrays)

    def body(*refs):
        outs = refs[n:2 * n]
        send_sems, recv_sems = refs[2 * n:]
        x, y, c = lax.axis_index("x"), lax.axis_index("y"), lax.axis_index("c")
        started = []
        for a in range(n):
            rh = arrays[a].shape[1] // 2
            mine = outs[a].at[:, pl.ds(c * rh, rh), :]
            cp = _remote(mine, mine, send_sems.at[a], recv_sems.at[a], (x, y, 1 - c))
            cp.start()
            started.append(cp)
        for a in range(n):
            rh = arrays[a].shape[1] // 2
            theirs = outs[a].at[:, pl.ds((1 - c) * rh, rh), :]
            _remote(theirs, theirs, send_sems.at[a], recv_sems.at[a], (x, y, c)).wait_recv()
        for cp in started:
            cp.wait_send()

    return pl.pallas_call(
        body, name="swap_halves", out_shape=tuple(jax.ShapeDtypeStruct(t.shape, t.dtype) for t in arrays),
        in_specs=[_ANY] * n, out_specs=tuple([_ANY] * n), input_output_aliases={a: a for a in range(n)},
        scratch_shapes=[pltpu.SemaphoreType.DMA((n,)), pltpu.SemaphoreType.DMA((n,))],
    )(*arrays)


def _adamw_math(g, w, m, v):
    m = ADAM_B1 * m + (1.0 - ADAM_B1) * g
    v = ADAM_B2 * v + (1.0 - ADAM_B2) * jnp.square(g)
    m_hat = m / (1.0 - ADAM_B1 ** ADAM_STEP)
    v_hat = v / (1.0 - ADAM_B2 ** ADAM_STEP)
    delta = -ADAM_LR * (m_hat / (jnp.sqrt(v_hat) + ADAM_EPS) + ADAM_WD * w)
    return delta, m, v


def _row_tile(rows, row_bytes, budget, step=8):
    best = step
    for t in range(step, rows + 1, step):
        if rows % t == 0 and t * row_bytes <= budget:
            best = t
    return best


def _pair_sum(part, other, core):
    nj, a, r, c = part.shape
    rh = r // 2
    tr = _row_tile(rh, 4 * (-(-c // LANES) * LANES), 512 * 1024, 16)

    def body(core_ref, p_ref, o_ref, s32_ref, s16_ref):
        s = p_ref[0, 0, 0] + o_ref[0, 0]
        s32_ref[0, 0] = s
        s16_ref[0, 0] = s.astype(BF16)

    spec = pl.BlockSpec((1, 1, tr, c), lambda j, l, i, core_ref: (j, l, i, 0))
    return pl.pallas_call(
        body, name="pair_sum",
        out_shape=(jax.ShapeDtypeStruct(other.shape, F32), jax.ShapeDtypeStruct(other.shape, BF16)),
        grid_spec=pltpu.PrefetchScalarGridSpec(
            num_scalar_prefetch=1, grid=(nj, a, rh // tr),
            in_specs=[pl.BlockSpec((1, 1, 1, tr, c), lambda j, l, i, core_ref: (j, l, core_ref[0], i, 0)), spec],
            out_specs=(spec, spec)),
        compiler_params=pltpu.CompilerParams(dimension_semantics=("parallel", "parallel", "parallel"),
                                             vmem_limit_bytes=VMEM_LIMIT),
    )(core, part.reshape(nj, a, 2, rh, c), other)


def _sum_adamw_big(own, landed, w, m, v, place):
    _, a, rh, c = own.shape
    tr = _row_tile(rh, 4 * (-(-c // LANES) * LANES), 256 * 1024, 16)
    w4, m4, v4 = (t.reshape(a, 2, rh, c) for t in (w, m, v))

    def body(place_ref, q_ref, p_ref, w_ref, m_ref, v_ref, g_out, d_out, m_out, v_out):
        g = q_ref[0, 0]
        for f in range(3):
            g = g + p_ref[f, 0].astype(F32)
        delta, mm_, vv_ = _adamw_math(g, w_ref[0, 0], m_ref[0, 0], v_ref[0, 0])
        g_out[0, 0] = g
        d_out[0, 0] = delta
        m_out[0, 0] = mm_
        v_out[0, 0] = vv_

    wspec = pl.BlockSpec((1, 1, tr, c), lambda l, i, place_ref: (l, place_ref[0], i, 0))
    out = jax.ShapeDtypeStruct((a, 2, rh, c), F32)
    res = pl.pallas_call(
        body, name="sum_adamw_big", out_shape=(out,) * 4,
        grid_spec=pltpu.PrefetchScalarGridSpec(
            num_scalar_prefetch=1, grid=(a, rh // tr),
            in_specs=[pl.BlockSpec((1, 1, tr, c), lambda l, i, place_ref: (place_ref[1], l, i, 0)),
                      pl.BlockSpec((3, 1, tr, c), lambda l, i, place_ref: (0, l, i, 0)), wspec, wspec, wspec],
            out_specs=(wspec,) * 4),
        compiler_params=pltpu.CompilerParams(dimension_semantics=("parallel", "parallel"), vmem_limit_bytes=VMEM_LIMIT),
    )(place, own, landed, w4, m4, v4)
    return [t.reshape(a, 2 * rh, c) for t in res]


def _sum_small(parts):
    _, rows, _ = parts.shape
    tr = _row_tile(rows, 4 * LANES, 256 * 1024)

    def body(p_ref, o_ref):
        g = p_ref[0]
        for k in range(1, N_DEV):
            g = g + p_ref[k]
        o_ref[...] = g

    return pl.pallas_call(
        body, name="sum_small", out_shape=jax.ShapeDtypeStruct((rows, LANES), F32), grid=(rows // tr,),
        in_specs=[pl.BlockSpec((N_DEV, tr, LANES), lambda i: (0, i, 0))], out_specs=pl.BlockSpec((tr, LANES), lambda i: (i, 0)),
        compiler_params=pltpu.CompilerParams(dimension_semantics=("parallel",)),
    )(parts)


def _adamw_small(g, w, m, v):
    rows, _ = g.shape
    tr = _row_tile(rows, 4 * LANES, 256 * 1024)

    def body(g_ref, w_ref, m_ref, v_ref, d_out, m_out, v_out):
        d_out[...], m_out[...], v_out[...] = _adamw_math(g_ref[...], w_ref[...], m_ref[...], v_ref[...])

    spec = pl.BlockSpec((tr, LANES), lambda i: (i, 0))
    out = jax.ShapeDtypeStruct((rows, LANES), F32)
    return pl.pallas_call(
        body, name="adamw_small", out_shape=(out,) * 3, grid=(rows // tr,), in_specs=[spec] * 4, out_specs=(spec,) * 3,
        compiler_params=pltpu.CompilerParams(dimension_semantics=("parallel",)),
    )(g, w, m, v)


def _pack(arrays):
    flat = jnp.concatenate([a.reshape(-1) for a in arrays])
    rows = -(-flat.shape[0] // (PACK_ROWS * LANES)) * PACK_ROWS
    return jnp.pad(flat, (0, rows * LANES - flat.shape[0])).reshape(rows, LANES)


def _unpack(packed, shapes):
    flat, out, off = packed.reshape(-1), [], 0
    for s in shapes:
        size = math.prod(s)
        out.append(flat[off:off + size].reshape(s))
        off += size
    return out


def kernel(x, mem, norm_g, w_in, ssm_a_re, ssm_a_im, ssm_log_step, ssm_b_re, ssm_b_im, ssm_c_re, ssm_c_im, ssm_d, ssm_w_glu, ssm_b_glu, dn_conv, dn_a_log, dn_dt_bias, dn_norm_g, attn_q_norm, attn_k_norm, mem_norm_g, w_mem_kv, w_branch, w_out, final_norm_g, loss_target, m_norm_g, m_w_in, m_ssm_a_re, m_ssm_a_im, m_ssm_log_step, m_ssm_b_re, m_ssm_b_im, m_ssm_c_re, m_ssm_c_im, m_ssm_d, m_ssm_w_glu, m_ssm_b_glu, m_dn_conv, m_dn_a_log, m_dn_dt_bias, m_dn_norm_g, m_attn_q_norm, m_attn_k_norm, m_mem_norm_g, m_w_mem_kv, m_w_branch, m_w_out, m_final_norm_g, v_norm_g, v_w_in, v_ssm_a_re, v_ssm_a_im, v_ssm_log_step, v_ssm_b_re, v_ssm_b_im, v_ssm_c_re, v_ssm_c_im, v_ssm_d, v_ssm_w_glu, v_ssm_b_glu, v_dn_conv, v_dn_a_log, v_dn_dt_bias, v_dn_norm_g, v_attn_q_norm, v_attn_k_norm, v_mem_norm_g, v_w_mem_kv, v_w_branch, v_w_out, v_final_norm_g):
    args = locals()
    wts = {n: args[n] for n in WEIGHTS}
    mom = {n: args["m_" + n] for n in WEIGHTS}
    var = {n: args["v_" + n] for n in WEIGHTS}
    d_model = x.shape[-1]
    chip = 2 * lax.axis_index("x") + lax.axis_index("y")
    core = lax.axis_index("c")

    conv_rows = dn_conv.shape[1]
    conv_flat = _pack([dn_conv])
    gathered = _gather_chips([wts[n].astype(BF16) for n in BIG] + [conv_flat[None]])
    full = {n: wts[n] for n in SMALL}
    g_in = gathered[0]
    full["w_in"] = _pad_w_in(jnp.concatenate([g_in[j] for j in range(N_CHIPS)], axis=-1), d_model)
    for n, g in zip(BIG[1:], gathered[1:]):
        full[n] = jnp.concatenate([g[j] for j in range(N_CHIPS)], axis=1).astype(F32)
    conv_all = gathered[-1][:, 0].reshape(N_CHIPS, -1)[:, :dn_conv.size].reshape((N_CHIPS,) + dn_conv.shape)
    full["dn_conv"] = jnp.concatenate([conv_all[j] for j in range(N_CHIPS)], axis=1)

    loss, grad_x, grads = _local_step(x[0], mem[0], loss_target[0], full)
    loss = lax.psum(loss, ("x", "y", "c"))

    big_parts = []
    for n in BIG:
        g = grads[n]
        if n == "w_in":
            cols = g.shape[-1] // N_CHIPS
            big_parts.append(jnp.stack([g[..., j * cols:(j + 1) * cols] for j in range(N_CHIPS)]))
        else:
            rows = g.shape[1] // N_CHIPS
            big_parts.append(jnp.stack([g[:, j * rows:(j + 1) * rows] for j in range(N_CHIPS)]))
    core_arr = jnp.reshape(core, (1,)).astype(jnp.int32)
    place = jnp.stack([core, chip]).astype(jnp.int32)
    from_pair = _exchange_pair(big_parts)
    sums = [_pair_sum(p, o, core_arr) for p, o in zip(big_parts, from_pair)]
    small_shapes = [grads[n].shape for n in SMALL]
    landed = _exchange_chips([s[1] for s in sums], _pack([grads[n] for n in SMALL]))

    halves = []
    for n, s, parts in zip(BIG, sums, landed[:-1]):
        halves.extend(_sum_adamw_big(s[0], parts, wts[n], mom[n], var[n], place))
    swapped = _swap_halves(halves)
    res = {}
    for i, n in enumerate(BIG):
        res[n] = swapped[4 * i:4 * i + 4]
    small_g = _unpack(_sum_small(landed[-1]), small_shapes)
    small_g = dict(zip(SMALL, small_g))
    small_g["dn_conv"] = lax.dynamic_slice_in_dim(small_g["dn_conv"], chip * conv_rows, conv_rows, axis=1)
    shapes = [small_g[n].shape for n in SMALL]
    upd = _adamw_small(_pack([small_g[n] for n in SMALL]), _pack([wts[n] for n in SMALL]),
                       _pack([mom[n] for n in SMALL]), _pack([var[n] for n in SMALL]))
    upd = [_unpack(u, shapes) for u in upd]
    for i, n in enumerate(SMALL):
        res[n] = (small_g[n], upd[0][i], upd[1][i], upd[2][i])

    out = [loss, grad_x[None]]
    for kind in range(4):
        out.extend(res[n][kind] for n in WEIGHTS)
    return tuple(out)
```

```python
import functools
import math

import jax
import jax.numpy as jnp
from jax import lax
from jax.experimental import pallas as pl
from jax.experimental.pallas import tpu as pltpu

F32 = jnp.float32
BF16 = jnp.bfloat16

GRID_W = 64
EPS = 1e-6
SSM_GROUP = 16
SSM_STATE = 64
SSM_GROUPS = 48
DN_HEADS = 6
DN_HEAD_DIM = 128
DN_CONV = 5
DN_CHUNK = 64
ATT_HEADS = 8
ATT_KV_HEADS = 2
ATT_HEAD_DIM = 128
ROPE_THETA = 10000.0
MEM_HEADS = 4
MEM_HEAD_DIM = 128
N_BRANCH = 4

ADAM_LR = 0.001
ADAM_B1 = 0.9
ADAM_B2 = 0.999
ADAM_EPS = 1e-08
ADAM_WD = 0.01
ADAM_STEP = 10

N_CHIPS = 4
N_DEV = 8
LANES = 128
VMEM_LIMIT = 48 * 1024 * 1024
IN_PAD_UNIT = 512
MM_FULL_K = 2048
PACK_ROWS = 512
CONV_HALO = 8
ATT_HEADS_PER_STEP = 4

WEIGHTS = ['norm_g', 'w_in', 'ssm_a_re', 'ssm_a_im', 'ssm_log_step', 'ssm_b_re', 'ssm_b_im', 'ssm_c_re',
           'ssm_c_im', 'ssm_d', 'ssm_w_glu', 'ssm_b_glu', 'dn_conv', 'dn_a_log', 'dn_dt_bias', 'dn_norm_g',
           'attn_q_norm', 'attn_k_norm', 'mem_norm_g', 'w_mem_kv', 'w_branch', 'w_out', 'final_norm_g']
BIG = ['w_in', 'ssm_w_glu', 'w_mem_kv', 'w_branch', 'w_out']
SMALL = [n for n in WEIGHTS if n not in BIG]


def _pick(dim, cands):
    for c in cands:
        if dim % c == 0:
            return c
    return dim


def _widths(d_model):
    ssm_w = SSM_GROUPS * SSM_GROUP
    dn_w = DN_HEADS * DN_HEAD_DIM
    att_w = ATT_HEADS * ATT_HEAD_DIM
    kv_w = ATT_KV_HEADS * ATT_HEAD_DIM
    mem_w = MEM_HEADS * MEM_HEAD_DIM
    return [("u", ssm_w), ("z_a", ssm_w), ("dq", dn_w), ("dk", dn_w), ("dv", dn_w), ("dab", 4 * DN_HEADS),
            ("z_b", dn_w), ("aq", att_w), ("ak", kv_w), ("av", kv_w), ("z_c", att_w), ("mq", mem_w),
            ("z_m", mem_w), ("gates", N_BRANCH * d_model)]


def _layout(d_model):
    out, src, dst = {}, 0, 0
    for name, w in _widths(d_model):
        pw = -(-w // IN_PAD_UNIT) * IN_PAD_UNIT if w % LANES else w
        out[name] = (src, dst, w, pw)
        src += w
        dst += pw
    return out, src, dst


def _pad_w_in(full, d_model):
    lay, _, _ = _layout(d_model)
    parts = []
    for name, _ in _widths(d_model):
        s, _, w, pw = lay[name]
        seg = full[..., s:s + w]
        if pw != w:
            seg = jnp.pad(seg, [(0, 0)] * (full.ndim - 1) + [(0, pw - w)])
        parts.append(seg)
    return jnp.concatenate(parts, axis=-1)


def _unpad_w_in(padded, d_model):
    lay, _, _ = _layout(d_model)
    return jnp.concatenate([padded[..., lay[n][1]:lay[n][1] + lay[n][2]] for n, _ in _widths(d_model)], axis=-1)


def _mm(a, b, mode="nn", name="mm"):
    if mode == "nn":
        (m, k), (k2, n) = a.shape, b.shape
    elif mode == "nt":
        (m, k), (n, k2) = a.shape, b.shape
    else:
        (k, m), (k2, n) = a.shape, b.shape
    assert k == k2, (a.shape, b.shape, mode)
    tk = k if k <= MM_FULL_K else _pick(k, (1024, 512, 256, 128))
    tall = mode == "tn" or (mode == "nn" and a.dtype == BF16 and tk == k)
    tm = _pick(m, (2048, 1024, 512, 256, 128) if tall else (1024, 512, 256, 128))
    tn = _pick(n, (512, 384, 256, 128) if tk > 1024 or tm > 1024 else (1024, 512, 384, 256, 128))
    if mode == "nt" and k > MM_FULL_K:
        tk = _pick(k, (1536, 1024, 512, 256, 128))
    nk = k // tk
    dims = {"nn": (((1,), (0,)), ((), ())), "nt": (((1,), (1,)), ((), ())), "tn": (((0,), (0,)), ((), ()))}[mode]

    def body(a_ref, b_ref, o_ref, *scratch):
        prod = lax.dot_general(a_ref[...].astype(BF16), b_ref[...].astype(BF16), dims, preferred_element_type=F32)
        if nk == 1:
            o_ref[...] = prod
            return
        acc_ref, = scratch
        kk = pl.program_id(2)

        @pl.when(kk == 0)
        def _():
            acc_ref[...] = prod

        @pl.when(kk > 0)
        def _():
            acc_ref[...] += prod

        @pl.when(kk == nk - 1)
        def _():
            o_ref[...] = acc_ref[...]

    a_spec = pl.BlockSpec((tk, tm), lambda i, j, kk: (kk, i)) if mode == "tn" else pl.BlockSpec((tm, tk), lambda i, j, kk: (i, kk))
    b_spec = pl.BlockSpec((tn, tk), lambda i, j, kk: (j, kk)) if mode == "nt" else pl.BlockSpec((tk, tn), lambda i, j, kk: (kk, j))
    return pl.pallas_call(
        body, name=name, out_shape=jax.ShapeDtypeStruct((m, n), F32), grid=(m // tm, n // tn, nk),
        in_specs=[a_spec, b_spec], out_specs=pl.BlockSpec((tm, tn), lambda i, j, kk: (i, j)),
        scratch_shapes=[pltpu.VMEM((tm, tn), F32)] if nk > 1 else [],
        compiler_params=pltpu.CompilerParams(dimension_semantics=("parallel", "parallel", "arbitrary"),
                                             vmem_limit_bytes=VMEM_LIMIT),
    )(a, b)


@jax.custom_vjp
def pmm(a, b):
    return _mm(a, b, "nn", "pmm_fwd")


def _pmm_fwd(a, b):
    return _mm(a, b, "nn", "pmm_fwd"), (a, b)


def _pmm_bwd(res, dc):
    a, b = res
    return _mm(dc, b, "nt", "pmm_da"), _mm(a, dc, "tn", "pmm_db")


pmm.defvjp(_pmm_fwd, _pmm_bwd)


def _cmm_call(a, w):
    m = a.shape[0]
    nc, kc, nn = w.shape
    tm = _pick(m, (1024, 512, 256, 128))

    def body(a_ref, w_ref, o_ref):
        o_ref[...] = lax.dot_general(a_ref[...].astype(BF16), w_ref[0].astype(BF16), (((1,), (0,)), ((), ())),
                                     preferred_element_type=F32)

    return pl.pallas_call(
        body, name="cmm", out_shape=jax.ShapeDtypeStruct((m, nc * nn), F32), grid=(m // tm, nc),
        in_specs=[pl.BlockSpec((tm, kc), lambda i, c: (i, c)), pl.BlockSpec((1, kc, nn), lambda i, c: (c, 0, 0))],
        out_specs=pl.BlockSpec((tm, nn), lambda i, c: (i, c)),
        compiler_params=pltpu.CompilerParams(dimension_semantics=("parallel", "parallel"), vmem_limit_bytes=VMEM_LIMIT),
    )(a, w)


def _cmm_dw_call(a, dc, kc, nn):
    m = a.shape[0]
    nc = a.shape[1] // kc
    tm = _pick(m, (1024, 512, 256, 128))
    nm = m // tm

    def body(a_ref, d_ref, o_ref, acc_s):
        i = pl.program_id(1)

        @pl.when(i == 0)
        def _():
            acc_s[...] = jnp.zeros_like(acc_s)

        acc_s[...] += lax.dot_general(a_ref[...].astype(BF16), d_ref[...].astype(BF16), (((0,), (0,)), ((), ())),
                                      preferred_element_type=F32)

        @pl.when(i == nm - 1)
        def _():
            o_ref[0] = acc_s[...]

    return pl.pallas_call(
        body, name="cmm_dw", out_shape=jax.ShapeDtypeStruct((nc, kc, nn), F32), grid=(nc, nm),
        in_specs=[pl.BlockSpec((tm, kc), lambda c, i: (i, c)), pl.BlockSpec((tm, nn), lambda c, i: (i, c))],
        out_specs=pl.BlockSpec((1, kc, nn), lambda c, i: (c, 0, 0)),
        scratch_shapes=[pltpu.VMEM((kc, nn), F32)],
        compiler_params=pltpu.CompilerParams(dimension_semantics=("parallel", "arbitrary"), vmem_limit_bytes=VMEM_LIMIT),
    )(a, dc)


@jax.custom_vjp
def cmm(a, w):
    return _cmm_call(a, w)


def _cmm_fwd(a, w):
    return _cmm_call(a, w), (a, w)


def _cmm_bwd(res, dc):
    a, w = res
    return _cmm_call(dc, jnp.swapaxes(w, 1, 2)), _cmm_dw_call(a, dc, w.shape[1], w.shape[2])


cmm.defvjp(_cmm_fwd, _cmm_bwd)


def _cluster_diag(w, per):
    g, a, b = w.shape
    eye = jnp.eye(per, dtype=w.dtype)
    wc = w.reshape(g // per, per, a, b)
    return (wc[:, :, :, None, :] * eye[None, :, None, :, None]).reshape(g // per, per * a, per * b)


def _norm_rows(rows, d):
    return _row_tile(rows, 4 * d, 1024 * 1024)


def _rms_fwd_call(x, g, out_dtype=F32):
    rows, d = x.shape
    tr = _norm_rows(rows, d)

    def body(x_ref, g_ref, o_ref):
        xv = x_ref[...]
        o_ref[...] = (xv * lax.rsqrt(jnp.mean(xv * xv, axis=1, keepdims=True) + EPS) * g_ref[...]).astype(out_dtype)

    return pl.pallas_call(
        body, name="rms_fwd", out_shape=jax.ShapeDtypeStruct((rows, d), out_dtype), grid=(rows // tr,),
        in_specs=[pl.BlockSpec((tr, d), lambda i: (i, 0)), pl.BlockSpec((1, d), lambda i: (0, 0))],
        out_specs=pl.BlockSpec((tr, d), lambda i: (i, 0)),
        compiler_params=pltpu.CompilerParams(dimension_semantics=("parallel",), vmem_limit_bytes=VMEM_LIMIT),
    )(x, g.reshape(1, d))


def _rms_bwd_call(x, g, dy):
    rows, d = x.shape
    tr = _norm_rows(rows, d)

    def body(x_ref, g_ref, dy_ref, dx_ref, dg_ref):
        @pl.when(pl.program_id(0) == 0)
        def _():
            dg_ref[...] = jnp.zeros_like(dg_ref)

        xv = x_ref[...]
        r = lax.rsqrt(jnp.mean(xv * xv, axis=1, keepdims=True) + EPS)
        xh = xv * r
        dyv = dy_ref[...]
        dxh = dyv * g_ref[...]
        dx_ref[...] = r * (dxh - xh * jnp.mean(dxh * xh, axis=1, keepdims=True))
        dg_ref[...] += jnp.sum(dyv * xh, axis=0, keepdims=True)

    return pl.pallas_call(
        body, name="rms_bwd", out_shape=(jax.ShapeDtypeStruct((rows, d), F32), jax.ShapeDtypeStruct((1, d), F32)),
        grid=(rows // tr,),
        in_specs=[pl.BlockSpec((tr, d), lambda i: (i, 0)), pl.BlockSpec((1, d), lambda i: (0, 0)),
                  pl.BlockSpec((tr, d), lambda i: (i, 0))],
        out_specs=(pl.BlockSpec((tr, d), lambda i: (i, 0)), pl.BlockSpec((1, d), lambda i: (0, 0))),
        compiler_params=pltpu.CompilerParams(dimension_semantics=("arbitrary",), vmem_limit_bytes=VMEM_LIMIT),
    )(x, g.reshape(1, d), dy)


@jax.custom_vjp
def prms(x, g):
    return _rms_fwd_call(x, g)


def _prms_fwd(x, g):
    return _rms_fwd_call(x, g), (x, g)


def _prms_bwd(res, dy):
    x, g = res
    dx, dg = _rms_bwd_call(x, g, dy)
    return dx, dg.reshape(g.shape)


prms.defvjp(_prms_fwd, _prms_bwd)


def _pair_swap(v):
    lane = lax.broadcasted_iota(jnp.int32, v.shape, 1)
    return jnp.where(lane % 2 == 0, pltpu.roll(v, v.shape[1] - 1, 1), pltpu.roll(v, 1, 1))


def _hnorm_fwd_call(x, g, cos, sin, scale):
    rows, width = x.shape
    d = g.shape[-1]
    h = width // d
    rope = cos is not None
    tr = _row_tile(rows, 4 * width, 1024 * 1024)

    def body(*refs):
        if rope:
            x_ref, g_ref, c_ref, s_ref, o_ref = refs
        else:
            x_ref, g_ref, o_ref = refs
        for i in range(h):
            cols = slice(i * d, (i + 1) * d)
            xv = x_ref[:, cols]
            n = xv * lax.rsqrt(jnp.mean(xv * xv, axis=1, keepdims=True) + EPS) * g_ref[...]
            if rope:
                n = (c_ref[...] * n + s_ref[...] * _pair_swap(n)) * scale
            o_ref[:, cols] = n

    row_spec = pl.BlockSpec((tr, width), lambda i: (i, 0))
    tab_spec = pl.BlockSpec((tr, d), lambda i: (i, 0))
    ins = [x, g.reshape(1, d)] + ([cos, sin] if rope else [])
    return pl.pallas_call(
        body, name="head_norm_fwd", out_shape=jax.ShapeDtypeStruct((rows, width), F32), grid=(rows // tr,),
        in_specs=[row_spec, pl.BlockSpec((1, d), lambda i: (0, 0))] + ([tab_spec, tab_spec] if rope else []),
        out_specs=row_spec,
        compiler_params=pltpu.CompilerParams(dimension_semantics=("parallel",), vmem_limit_bytes=VMEM_LIMIT),
    )(*ins)


def _hnorm_bwd_call(x, g, cos, sin, scale, dy):
    rows, width = x.shape
    d = g.shape[-1]
    h = width // d
    rope = cos is not None
    tr = _row_tile(rows, 4 * width, 1024 * 1024)

    def body(*refs):
        if rope:
            x_ref, g_ref, c_ref, s_ref, dy_ref, dx_ref, dg_ref = refs
        else:
            x_ref, g_ref, dy_ref, dx_ref, dg_ref = refs

        @pl.when(pl.program_id(0) == 0)
        def _():
            dg_ref[...] = jnp.zeros_like(dg_ref)

        dg = jnp.zeros((1, d), F32)
        for i in range(h):
            cols = slice(i * d, (i + 1) * d)
            xv = x_ref[:, cols]
            r = lax.rsqrt(jnp.mean(xv * xv, axis=1, keepdims=True) + EPS)
            xh = xv * r
            dn = dy_ref[:, cols]
            if rope:
                dn = dn * scale
                dn = c_ref[...] * dn + _pair_swap(s_ref[...] * dn)
            dxh = dn * g_ref[...]
            dx_ref[:, cols] = r * (dxh - xh * jnp.mean(dxh * xh, axis=1, keepdims=True))
            dg = dg + jnp.sum(dn * xh, axis=0, keepdims=True)
        dg_ref[...] += dg

    row_spec = pl.BlockSpec((tr, width), lambda i: (i, 0))
    tab_spec = pl.BlockSpec((tr, d), lambda i: (i, 0))
    vec_spec = pl.BlockSpec((1, d), lambda i: (0, 0))
    ins = [x, g.reshape(1, d)] + ([cos, sin] if rope else []) + [dy]
    return pl.pallas_call(
        body, name="head_norm_bwd", out_shape=(jax.ShapeDtypeStruct((rows, width), F32), jax.ShapeDtypeStruct((1, d), F32)),
        grid=(rows // tr,),
        in_specs=[row_spec, vec_spec] + ([tab_spec, tab_spec] if rope else []) + [row_spec],
        out_specs=(row_spec, vec_spec),
        compiler_params=pltpu.CompilerParams(dimension_semantics=("arbitrary",), vmem_limit_bytes=VMEM_LIMIT),
    )(*ins)


@jax.custom_vjp
def head_norm(x, g):
    return _hnorm_fwd_call(x, g, None, None, 1.0)


def _head_norm_fwd(x, g):
    return _hnorm_fwd_call(x, g, None, None, 1.0), (x, g)


def _head_norm_bwd(res, dy):
    x, g = res
    dx, dg = _hnorm_bwd_call(x, g, None, None, 1.0, dy)
    return dx, dg.reshape(g.shape)


head_norm.defvjp(_head_norm_fwd, _head_norm_bwd)


@functools.partial(jax.custom_vjp, nondiff_argnums=(4,))
def head_norm_rope(x, g, cos, sin, scale):
    return _hnorm_fwd_call(x, g, cos, sin, scale)


def _head_norm_rope_fwd(x, g, cos, sin, scale):
    return _hnorm_fwd_call(x, g, cos, sin, scale), (x, g, cos, sin)


def _head_norm_rope_bwd(scale, res, dy):
    x, g, cos, sin = res
    dx, dg = _hnorm_bwd_call(x, g, cos, sin, scale, dy)
    return dx, dg.reshape(g.shape), jnp.zeros_like(cos), jnp.zeros_like(sin)


head_norm_rope.defvjp(_head_norm_rope_fwd, _head_norm_rope_bwd)


def _loss_call(x, g, target):
    rows, d = x.shape
    tr = _norm_rows(rows, d)
    nb = rows // tr

    def body(x_ref, g_ref, t_ref, l_ref, dx_ref, dg_ref):
        @pl.when(pl.program_id(0) == 0)
        def _():
            dg_ref[...] = jnp.zeros_like(dg_ref)

        xv = x_ref[...]
        r = lax.rsqrt(jnp.mean(xv * xv, axis=1, keepdims=True) + EPS)
        xh = xv * r
        err = xh * g_ref[...] - t_ref[...]
        l_ref[...] = jnp.broadcast_to(0.5 * jnp.sum(jnp.mean(err * err, axis=1, keepdims=True), axis=0, keepdims=True)[None],
                                      l_ref.shape)
        dyv = err * (1.0 / d)
        dxh = dyv * g_ref[...]
        dx_ref[...] = r * (dxh - xh * jnp.mean(dxh * xh, axis=1, keepdims=True))
        dg_ref[...] += jnp.sum(dyv * xh, axis=0, keepdims=True)

    return pl.pallas_call(
        body, name="final_loss",
        out_shape=(jax.ShapeDtypeStruct((nb, 1, LANES), F32), jax.ShapeDtypeStruct((rows, d), F32),
                   jax.ShapeDtypeStruct((1, d), F32)),
        grid=(nb,),
        in_specs=[pl.BlockSpec((tr, d), lambda i: (i, 0)), pl.BlockSpec((1, d), lambda i: (0, 0)),
                  pl.BlockSpec((tr, d), lambda i: (i, 0))],
        out_specs=(pl.BlockSpec((1, 1, LANES), lambda i: (i, 0, 0)), pl.BlockSpec((tr, d), lambda i: (i, 0)),
                   pl.BlockSpec((1, d), lambda i: (0, 0))),
        compiler_params=pltpu.CompilerParams(dimension_semantics=("arbitrary",), vmem_limit_bytes=VMEM_LIMIT),
    )(x, g.reshape(1, d), target)


_NT = (((1,), (1,)), ((), ()))
_NN = (((1,), (0,)), ((), ()))
_TN = (((0,), (0,)), ((), ()))


def _att_plan(lq, lk, hq, group):
    hp = min(ATT_HEADS_PER_STEP, hq)
    assert hq % hp == 0 and (hp % group == 0 or group % hp == 0)
    kvp = max(1, hp // group)
    return _pick(lq, (256, 128)), _pick(lk, (1024, 512, 256, 128)), hp, kvp


def _att_specs(bq, bk, dh, hp, kvp, group, q_map, k_map):
    qs = pl.BlockSpec((bq, hp * dh), lambda *g: q_map(*g))
    ls = pl.BlockSpec((bq, hp * LANES), lambda *g: q_map(*g))
    ks = pl.BlockSpec((bk, kvp * dh), lambda *g: (k_map(*g)[0], k_map(*g)[1] * hp // (group * kvp)))
    return qs, ls, ks


def _flash_fwd(q, k, v, dh, group, scale):
    lq, lk = q.shape[0], k.shape[0]
    hq = q.shape[1] // dh
    bq, bk, hp, kvp = _att_plan(lq, lk, hq, group)
    nkv = lk // bk

    def body(q_ref, k_ref, v_ref, o_ref, lse_ref, m_s, l_s, acc_s):
        j = pl.program_id(2)

        @pl.when(j == 0)
        def _():
            m_s[...] = jnp.full_like(m_s, -jnp.inf)
            l_s[...] = jnp.zeros_like(l_s)
            acc_s[...] = jnp.zeros_like(acc_s)

        for h in range(hp):
            hk = h // group if kvp > 1 else 0
            cols, lcols, kcols = slice(h * dh, (h + 1) * dh), slice(h * LANES, (h + 1) * LANES), slice(hk * dh, (hk + 1) * dh)
            s = lax.dot_general(q_ref[:, cols], k_ref[:, kcols], _NT, preferred_element_type=F32)
            if scale != 1.0:
                s = s * scale
            m_prev = m_s[:, lcols]
            m_cur = jnp.maximum(m_prev, jnp.max(s, axis=1, keepdims=True))
            alpha = jnp.exp(m_prev - m_cur)
            p = jnp.exp(s - m_cur[:, :1])
            l_s[:, lcols] = alpha * l_s[:, lcols] + jnp.sum(p, axis=1, keepdims=True)
            acc_s[:, cols] = acc_s[:, cols] * alpha[:, :1] + lax.dot_general(p.astype(BF16), v_ref[:, kcols], _NN,
                                                                           preferred_element_type=F32)
            m_s[:, lcols] = m_cur

        @pl.when(j == nkv - 1)
        def _():
            for h in range(hp):
                cols, lcols = slice(h * dh, (h + 1) * dh), slice(h * LANES, (h + 1) * LANES)
                o_ref[:, cols] = acc_s[:, cols] / l_s[:, lcols][:, :1]
                lse_ref[:, lcols] = m_s[:, lcols] + jnp.log(l_s[:, lcols])

    qs, ls, ks = _att_specs(bq, bk, dh, hp, kvp, group, lambda h, i, j: (i, h), lambda h, i, j: (j, h))
    return pl.pallas_call(
        body, name="flash_fwd",
        out_shape=(jax.ShapeDtypeStruct((lq, hq * dh), F32), jax.ShapeDtypeStruct((lq, hq * LANES), F32)),
        grid=(hq // hp, lq // bq, nkv), in_specs=[qs, ks, ks], out_specs=(qs, ls),
        scratch_shapes=[pltpu.VMEM((bq, hp * LANES), F32), pltpu.VMEM((bq, hp * LANES), F32), pltpu.VMEM((bq, hp * dh), F32)],
        compiler_params=pltpu.CompilerParams(dimension_semantics=("parallel", "parallel", "arbitrary"),
                                             vmem_limit_bytes=VMEM_LIMIT),
    )(q, k, v)


def _flash_dq(q, k, v, o, lse, do, dh, group, scale):
    lq, lk = q.shape[0], k.shape[0]
    hq = q.shape[1] // dh
    bq, bk, hp, kvp = _att_plan(lq, lk, hq, group)
    nkv = lk // bk

    def body(q_ref, k_ref, v_ref, o_ref, lse_ref, do_ref, dq_ref, dob_ref, dl_ref, acc_s):
        j = pl.program_id(2)

        @pl.when(j == 0)
        def _():
            acc_s[...] = jnp.zeros_like(acc_s)
            dob_ref[...] = do_ref[...].astype(BF16)
            for h in range(hp):
                cols, lcols = slice(h * dh, (h + 1) * dh), slice(h * LANES, (h + 1) * LANES)
                dl_ref[:, lcols] = jnp.broadcast_to(jnp.sum(do_ref[:, cols] * o_ref[:, cols], axis=1, keepdims=True),
                                                    (bq, LANES))

        for h in range(hp):
            hk = h // group if kvp > 1 else 0
            cols, lcols, kcols = slice(h * dh, (h + 1) * dh), slice(h * LANES, (h + 1) * LANES), slice(hk * dh, (hk + 1) * dh)
            kb = k_ref[:, kcols]
            s = lax.dot_general(q_ref[:, cols], kb, _NT, preferred_element_type=F32)
            if scale != 1.0:
                s = s * scale
            p = jnp.exp(s - lse_ref[:, lcols][:, :1])
            dp = lax.dot_general(dob_ref[:, cols], v_ref[:, kcols], _NT, preferred_element_type=F32)
            ds = p * (dp - dl_ref[:, lcols][:, :1])
            if scale != 1.0:
                ds = ds * scale
            acc_s[:, cols] += lax.dot_general(ds.astype(BF16), kb, _NN, preferred_element_type=F32)

        @pl.when(j == nkv - 1)
        def _():
            dq_ref[...] = acc_s[...]

    qs, ls, ks = _att_specs(bq, bk, dh, hp, kvp, group, lambda h, i, j: (i, h), lambda h, i, j: (j, h))
    return pl.pallas_call(
        body, name="flash_dq",
        out_shape=(jax.ShapeDtypeStruct((lq, hq * dh), F32), jax.ShapeDtypeStruct((lq, hq * dh), BF16),
                   jax.ShapeDtypeStruct((lq, hq * LANES), F32)),
        grid=(hq // hp, lq // bq, nkv), in_specs=[qs, ks, ks, qs, ls, qs], out_specs=(qs, qs, ls),
        scratch_shapes=[pltpu.VMEM((bq, hp * dh), F32)],
        compiler_params=pltpu.CompilerParams(dimension_semantics=("parallel", "parallel", "arbitrary"),
                                             vmem_limit_bytes=VMEM_LIMIT),
    )(q, k, v, o, lse, do)


def _flash_dkv(q, k, v, lse, dob, delta, dh, group, scale):
    lq, lk = q.shape[0], k.shape[0]
    hq = q.shape[1] // dh
    bq, bk, hp, kvp = _att_plan(lq, lk, hq, group)
    nq = lq // bq
    reps = max(1, group // hp)
    nt = reps * nq

    def body(q_ref, k_ref, v_ref, lse_ref, dob_ref, dl_ref, dk_ref, dv_ref, dk_s, dv_s):
        t = pl.program_id(2)

        @pl.when(t == 0)
        def _():
            dk_s[...] = jnp.zeros_like(dk_s)
            dv_s[...] = jnp.zeros_like(dv_s)

        for h in range(hp):
            hk = h // group if kvp > 1 else 0
            cols, lcols, kcols = slice(h * dh, (h + 1) * dh), slice(h * LANES, (h + 1) * LANES), slice(hk * dh, (hk + 1) * dh)
            qb = q_ref[:, cols]
            dob = dob_ref[:, cols]
            s = lax.dot_general(qb, k_ref[:, kcols], _NT, preferred_element_type=F32)
            if scale != 1.0:
                s = s * scale
            p = jnp.exp(s - lse_ref[:, lcols][:, :1])
            dp = lax.dot_general(dob, v_ref[:, kcols], _NT, preferred_element_type=F32)
            ds = p * (dp - dl_ref[:, lcols][:, :1])
            if scale != 1.0:
                ds = ds * scale
            dv_s[:, kcols] += lax.dot_general(p.astype(BF16), dob, _TN, preferred_element_type=F32)
            dk_s[:, kcols] += lax.dot_general(ds.astype(BF16), qb, _TN, preferred_element_type=F32)

        @pl.when(t == nt - 1)
        def _():
            dk_ref[...] = dk_s[...]
            dv_ref[...] = dv_s[...]

    hkv_blocks = (hq // group) // kvp
    qs = pl.BlockSpec((bq, hp * dh), lambda h, j, t: (t % nq, h * reps + t // nq))
    ls = pl.BlockSpec((bq, hp * LANES), lambda h, j, t: (t % nq, h * reps + t // nq))
    ks = pl.BlockSpec((bk, kvp * dh), lambda h, j, t: (j, h))
    return pl.pallas_call(
        body, name="flash_dkv",
        out_shape=(jax.ShapeDtypeStruct(k.shape, F32), jax.ShapeDtypeStruct(v.shape, F32)),
        grid=(hkv_blocks, lk // bk, nt), in_specs=[qs, ks, ks, ls, qs, ls], out_specs=(ks, ks),
        scratch_shapes=[pltpu.VMEM((bk, kvp * dh), F32), pltpu.VMEM((bk, kvp * dh), F32)],
        compiler_params=pltpu.CompilerParams(dimension_semantics=("parallel", "parallel", "arbitrary"),
                                             vmem_limit_bytes=VMEM_LIMIT),
    )(q, k, v, lse, dob, delta)


@functools.partial(jax.custom_vjp, nondiff_argnums=(3, 4, 5))
def flash(q, k, v, dh, group, scale):
    return _flash_fwd(q.astype(BF16), k.astype(BF16), v.astype(BF16), dh, group, scale)[0]


def _flash_vfwd(q, k, v, dh, group, scale):
    qb, kb, vb = q.astype(BF16), k.astype(BF16), v.astype(BF16)
    o, lse = _flash_fwd(qb, kb, vb, dh, group, scale)
    return o, (qb, kb, vb, o, lse)


def _flash_vbwd(dh, group, scale, res, do):
    q, k, v, o, lse = res
    dq, dob, delta = _flash_dq(q, k, v, o, lse, do, dh, group, scale)
    dk, dv = _flash_dkv(q, k, v, lse, dob, delta, dh, group, scale)
    return dq, dk, dv


flash.defvjp(_flash_vfwd, _flash_vbwd)


def _scan_call(lr, li, br, bi, prev_r, prev_i, reverse, name):
    rows, width = br.shape
    ts = _pick(width, (1024, 512, 256, 128))
    tr = _pick(rows, (256, 128, 64, 32, 16, 8))
    nb = rows // tr
    with_prev = prev_r is not None

    def body(*refs):
        if with_prev:
            lr_ref, li_ref, br_ref, bi_ref, pr_ref, pi_ref, sr_ref, si_ref, dr_ref, di_ref, cr_s, ci_s = refs
        else:
            lr_ref, li_ref, br_ref, bi_ref, sr_ref, si_ref, pr_ref, pi_ref, cr_s, ci_s = refs
        i = pl.program_id(1)

        @pl.when(i == 0)
        def _():
            cr_s[...] = jnp.zeros_like(cr_s)
            ci_s[...] = jnp.zeros_like(ci_s)
            if with_prev:
                dr_ref[...] = jnp.zeros_like(dr_ref)
                di_ref[...] = jnp.zeros_like(di_ref)

        a_re = lr_ref[...]
        a_im = li_ref[...]

        def step(r, carry):
            if with_prev:
                c_re, c_im, d_re, d_im = carry
            else:
                c_re, c_im = carry
            t = tr - 1 - r if reverse else r
            row = pl.ds(t, 1)
            n_re = a_re * c_re - a_im * c_im + br_ref[row, :]
            n_im = a_re * c_im + a_im * c_re + bi_ref[row, :]
            sr_ref[row, :] = n_re
            si_ref[row, :] = n_im
            if with_prev:
                p_re = pr_ref[row, :]
                p_im = pi_ref[row, :]
                return n_re, n_im, d_re + n_re * p_re + n_im * p_im, d_im + n_im * p_re - n_re * p_im
            pr_ref[row, :] = c_re
            pi_ref[row, :] = c_im
            return n_re, n_im

        zero = jnp.zeros((1, ts), F32)
        init = (cr_s[...], ci_s[...]) + ((zero, zero) if with_prev else ())
        out = lax.fori_loop(0, tr, step, init, unroll=8)
        cr_s[...] = out[0]
        ci_s[...] = out[1]
        if with_prev:
            dr_ref[...] += out[2]
            di_ref[...] += out[3]

    blk = (lambda j, i: (nb - 1 - i, j)) if reverse else (lambda j, i: (i, j))
    row_spec = pl.BlockSpec((tr, ts), blk)
    vec_spec = pl.BlockSpec((1, ts), lambda j, i: (0, j))
    full = jax.ShapeDtypeStruct((rows, width), F32)
    vec = jax.ShapeDtypeStruct((1, width), F32)
    ins = [lr, li, br, bi] + ([prev_r, prev_i] if with_prev else [])
    return pl.pallas_call(
        body, name=name, grid=(width // ts, nb),
        in_specs=[vec_spec, vec_spec] + [row_spec] * (len(ins) - 2),
        out_shape=(full, full, vec, vec) if with_prev else (full, full, full, full),
        out_specs=(row_spec, row_spec, vec_spec, vec_spec) if with_prev else (row_spec,) * 4,
        scratch_shapes=[pltpu.VMEM((1, ts), F32), pltpu.VMEM((1, ts), F32)],
        compiler_params=pltpu.CompilerParams(dimension_semantics=("parallel", "arbitrary"), vmem_limit_bytes=VMEM_LIMIT),
    )(*ins)


@functools.partial(jax.custom_vjp, nondiff_argnums=(4,))
def cscan(lr, li, br, bi, reverse):
    out = _scan_call(lr, li, br, bi, None, None, reverse, "s5_scan")
    return out[0], out[1]


def _cscan_fwd(lr, li, br, bi, reverse):
    sr, si, pr, pi = _scan_call(lr, li, br, bi, None, None, reverse, "s5_scan")
    return (sr, si), (lr, li, pr, pi)


def _cscan_bwd(reverse, res, g):
    lr, li, pr, pi = res
    ar, ai, dlr, dli = _scan_call(lr, -li, g[0], g[1], pr, pi, not reverse, "s5_scan_bwd")
    return dlr, dli, ar, ai


cscan.defvjp(_cscan_fwd, _cscan_bwd)


_BDIMS = {"nn": (((2,), (1,)), ((0,), (0,))), "nt": (((2,), (2,)), ((0,), (0,))), "tn": (((1,), (1,)), ((0,), (0,)))}


def _bdot_raw(a, b, mode):
    return lax.dot_general(a.astype(BF16), b.astype(BF16), _BDIMS[mode], preferred_element_type=F32)


@functools.partial(jax.custom_vjp, nondiff_argnums=(2,))
def _bdot(a, b, mode):
    return _bdot_raw(a, b, mode)


def _bdot_fwd(a, b, mode):
    return _bdot_raw(a, b, mode), (a, b)


def _bdot_bwd(mode, res, dc):
    a, b = res
    if mode == "nn":
        return _bdot_raw(dc, b, "nt"), _bdot_raw(a, dc, "tn")
    if mode == "nt":
        return _bdot_raw(dc, b, "nn"), _bdot_raw(dc, a, "tn")
    return _bdot_raw(b, dc, "nt"), _bdot_raw(a, dc, "nn")


_bdot.defvjp(_bdot_fwd, _bdot_bwd)


def _split(a):
    hi = a.astype(BF16)
    return hi, (a - hi.astype(F32)).astype(BF16)


def _fdot_raw(a, b, mode):
    a_hi, a_lo = _split(a)
    b_hi, b_lo = _split(b)
    dot = lambda u, v: lax.dot_general(u, v, _BDIMS[mode], preferred_element_type=F32)
    return dot(a_hi, b_hi) + (dot(a_hi, b_lo) + dot(a_lo, b_hi))


@jax.custom_vjp
def _fdot(a, b):
    return _fdot_raw(a, b, "nn")


def _fdot_fwd(a, b):
    return _fdot_raw(a, b, "nn"), (a, b)


def _fdot_bwd(res, dc):
    a, b = res
    return _fdot_raw(dc, b, "nt"), _fdot_raw(a, dc, "tn")


_fdot.defvjp(_fdot_fwd, _fdot_bwd)


def _dn_chunk(state, q, k, v, bb, gc, gc64, gr, direction):
    c = q.shape[1]
    row = lax.broadcasted_iota(jnp.int32, (c, c), 0)
    col = lax.broadcasted_iota(jnp.int32, (c, c), 1)
    sgn = 1 - 2 * direction
    after = ((row - col) * sgn > 0)[None]
    incl = ((row - col) * sgn >= 0)[None]
    eye = (row == col).astype(F32)[None]
    kb = k * bb
    decay = jnp.where(incl, jnp.exp(jnp.where(incl, gc64 - gr, 0.0)), 0.0)
    lower = jnp.where(after, _bdot(kb, k, "nt") * decay, 0.0)
    inv = eye - lower
    power = _fdot(lower, lower)
    span = 2
    while span < c:
        inv = inv + _fdot(inv, power)
        span *= 2
        if span < c:
            power = _fdot(power, power)
    eg = jnp.exp(gc)
    u = _fdot(inv, v * bb)
    w = _fdot(inv, kb * eg)
    intra = _bdot(q, k, "nt") * decay
    last = (lax.broadcasted_iota(jnp.int32, (c, 1), 0) == (c - 1) * (1 - direction)).astype(F32)[None]
    g_last = jnp.sum(gc * last, axis=1, keepdims=True)
    k_dec = k * jnp.exp(g_last - gc)
    v_new = u - _bdot(w, state, "nn")
    o = _bdot(q * eg, state, "nn") + _bdot(intra, v_new, "nn")
    new_state = state * jnp.exp(g_last) + _bdot(k_dec, v_new, "tn")
    return new_state, o


def _dn_rows(lq):
    return _pick(lq, (256, 128, 64))


def _dn_prep(q, k, v):
    q, k, v = jax.nn.silu(q), jax.nn.silu(k), jax.nn.silu(v)
    q = q * lax.rsqrt(jnp.sum(q * q, axis=2, keepdims=True) + EPS) * (q.shape[2] ** -0.5)
    k = k * lax.rsqrt(jnp.sum(k * k, axis=2, keepdims=True) + EPS)
    return q, k, v


def _dn_step(state, q, k, v, bb, gc, gc64, gr, direction):
    q, k, v = _dn_prep(q, k, v)
    return _dn_chunk(state, q, k, v, bb, gc, gc64, gr, direction)


def _dn_heads(ref, rows, part, h, d):
    return jnp.stack([ref[rows, (part * h + i) * d:(part * h + i + 1) * d] for i in range(h)])


def _dn_fwd_call(qkv, bb, gc, gc64, gr):
    _, h, lq, d = bb.shape
    c = DN_CHUNK
    t = _dn_rows(lq)
    nb, nc = lq // t, t // c

    def body(qkv_ref, bb_ref, gc_ref, g64_ref, gr_ref, o_ref, st_ref, s_s):
        direction = pl.program_id(0)

        @pl.when(pl.program_id(1) == 0)
        def _():
            s_s[...] = jnp.zeros_like(s_s)

        def step(j, state):
            ci = j + direction * (nc - 1 - 2 * j)
            rows = pl.ds(pl.multiple_of(ci * c, c), c)
            st_ref[0, ci] = state
            state, o = _dn_step(state, _dn_heads(qkv_ref, rows, 0, h, d), _dn_heads(qkv_ref, rows, 1, h, d),
                                _dn_heads(qkv_ref, rows, 2, h, d), bb_ref[0, :, rows, :], gc_ref[0, :, rows, :],
                                g64_ref[0, :, rows, :], gr_ref[0, :, ci], direction)
            for i in range(h):
                o_ref[0, rows, i * d:(i + 1) * d] = o[i]
            return state

        s_s[...] = lax.fori_loop(0, nc, step, s_s[...])

    blk = lambda dd, i: i + dd * (nb - 1 - 2 * i)
    ds = pl.BlockSpec((1, h, t, d), lambda dd, i: (dd, 0, blk(dd, i), 0))
    return pl.pallas_call(
        body, name="deltanet_fwd", grid=(2, nb),
        in_specs=[pl.BlockSpec((t, 3 * h * d), lambda dd, i: (blk(dd, i), 0)), ds, ds,
                  pl.BlockSpec((1, h, t, c), lambda dd, i: (dd, 0, blk(dd, i), 0)),
                  pl.BlockSpec((1, h, nc, 1, c), lambda dd, i: (dd, 0, blk(dd, i), 0, 0))],
        out_shape=(jax.ShapeDtypeStruct((2, lq, h * d), F32), jax.ShapeDtypeStruct((2, lq // c, h, d, d), F32)),
        out_specs=(pl.BlockSpec((1, t, h * d), lambda dd, i: (dd, blk(dd, i), 0)),
                   pl.BlockSpec((1, nc, h, d, d), lambda dd, i: (dd, blk(dd, i), 0, 0, 0))),
        scratch_shapes=[pltpu.VMEM((h, d, d), F32)],
        compiler_params=pltpu.CompilerParams(dimension_semantics=("parallel", "arbitrary"), vmem_limit_bytes=VMEM_LIMIT),
    )(qkv, bb, gc, gc64, gr)


def _dn_bwd_call(qkv, bb, gc, gc64, gr, states, do):
    _, h, lq, d = bb.shape
    c = DN_CHUNK
    t = _dn_rows(lq)
    nb, nc = lq // t, t // c

    def body(qkv_ref, bb_ref, gc_ref, g64_ref, gr_ref, st_ref, do_ref, dqkv_ref, dbb_ref, dgc_ref, dg64_ref, dgr_ref, ds_s):
        direction = pl.program_id(0)

        @pl.when(pl.program_id(1) == 0)
        def _():
            ds_s[...] = jnp.zeros_like(ds_s)

        def step(j, dstate):
            ci = (nc - 1 - j) + direction * (2 * j - (nc - 1))
            rows = pl.ds(pl.multiple_of(ci * c, c), c)
            args = (st_ref[0, ci], _dn_heads(qkv_ref, rows, 0, h, d), _dn_heads(qkv_ref, rows, 1, h, d),
                    _dn_heads(qkv_ref, rows, 2, h, d), bb_ref[0, :, rows, :], gc_ref[0, :, rows, :],
                    g64_ref[0, :, rows, :], gr_ref[0, :, ci])
            _, pull = jax.vjp(lambda *a: _dn_step(*a, direction), *args)
            do_c = jnp.stack([do_ref[rows, i * d:(i + 1) * d] for i in range(h)])
            dstate, dq, dk, dv, dbb, dgc, dg64, dgr = pull((dstate, do_c))
            for part, val in enumerate((dq, dk, dv)):
                for i in range(h):
                    dqkv_ref[0, rows, (part * h + i) * d:(part * h + i + 1) * d] = val[i]
            dbb_ref[0, :, rows, :] = dbb
            dgc_ref[0, :, rows, :] = dgc
            dg64_ref[0, :, rows, :] = dg64
            dgr_ref[0, :, ci] = dgr
            return dstate

        ds_s[...] = lax.fori_loop(0, nc, step, ds_s[...])

    blk = lambda dd, i: (nb - 1 - i) + dd * (2 * i - (nb - 1))
    ds = pl.BlockSpec((1, h, t, d), lambda dd, i: (dd, 0, blk(dd, i), 0))
    g64s = pl.BlockSpec((1, h, t, c), lambda dd, i: (dd, 0, blk(dd, i), 0))
    grs = pl.BlockSpec((1, h, nc, 1, c), lambda dd, i: (dd, 0, blk(dd, i), 0, 0))
    big = jax.ShapeDtypeStruct((2, h, lq, d), F32)
    return pl.pallas_call(
        body, name="deltanet_bwd", grid=(2, nb),
        in_specs=[pl.BlockSpec((t, 3 * h * d), lambda dd, i: (blk(dd, i), 0)), ds, ds, g64s, grs,
                  pl.BlockSpec((1, nc, h, d, d), lambda dd, i: (dd, blk(dd, i), 0, 0, 0)),
                  pl.BlockSpec((t, h * d), lambda dd, i: (blk(dd, i), 0))],
        out_shape=(jax.ShapeDtypeStruct((2, lq, 3 * h * d), F32), big, big, jax.ShapeDtypeStruct((2, h, lq, c), F32),
                   jax.ShapeDtypeStruct((2, h, lq // c, 1, c), F32)),
        out_specs=(pl.BlockSpec((1, t, 3 * h * d), lambda dd, i: (dd, blk(dd, i), 0)), ds, ds, g64s, grs),
        scratch_shapes=[pltpu.VMEM((h, d, d), F32)],
        compiler_params=pltpu.CompilerParams(dimension_semantics=("parallel", "arbitrary"), vmem_limit_bytes=VMEM_LIMIT),
    )(qkv, bb, gc, gc64, gr, states, do)


@jax.custom_vjp
def dn_rule(qkv, bb, gc, gc64, gr):
    o = _dn_fwd_call(qkv, bb, gc, gc64, gr)[0]
    return o[0] + o[1]


def _dn_rule_fwd(qkv, bb, gc, gc64, gr):
    o, states = _dn_fwd_call(qkv, bb, gc, gc64, gr)
    return o[0] + o[1], (qkv, bb, gc, gc64, gr, states)


def _dn_rule_bwd(res, do):
    dqkv, dbb, dgc, dg64, dgr = _dn_bwd_call(*res, do)
    return dqkv[0] + dqkv[1], dbb, dgc, dg64, dgr


dn_rule.defvjp(_dn_rule_fwd, _dn_rule_bwd)


def gated_delta_both(qkv, beta2, g2):
    _, h, lq = beta2.shape
    d = qkv.shape[1] // (3 * h)
    c = DN_CHUNK
    gch = g2.reshape(2, h, lq // c, c)
    cum = jnp.stack([jnp.cumsum(gch[0], axis=-1), lax.cumsum(gch[1], axis=2, reverse=True)])
    flat = cum.reshape(2, h, lq)
    return dn_rule(qkv, jnp.broadcast_to(beta2[..., None], (2, h, lq, d)), jnp.broadcast_to(flat[..., None], (2, h, lq, d)),
                   jnp.broadcast_to(flat[..., None], (2, h, lq, c)), cum.reshape(2, h, lq // c, 1, c))


def _merge_rows(rows, d, nb):
    return _row_tile(rows, 4 * nb * d, 2 * 1024 * 1024)


def _merge_fwd_call(logits, ts):
    rows, d = ts[0].shape
    nb = len(ts)
    tr = _merge_rows(rows, d, nb)

    def body(gl_ref, *refs):
        t_refs, o_ref = refs[:nb], refs[nb]
        acc = jax.nn.sigmoid(gl_ref[:, 0:d]) * t_refs[0][...]
        for b in range(1, nb):
            acc = acc + jax.nn.sigmoid(gl_ref[:, b * d:(b + 1) * d]) * t_refs[b][...]
        o_ref[...] = acc

    wide = pl.BlockSpec((tr, nb * d), lambda i: (i, 0))
    one = pl.BlockSpec((tr, d), lambda i: (i, 0))
    return pl.pallas_call(
        body, name="gate_merge", out_shape=jax.ShapeDtypeStruct((rows, d), F32), grid=(rows // tr,),
        in_specs=[wide] + [one] * nb, out_specs=one,
        compiler_params=pltpu.CompilerParams(dimension_semantics=("parallel",), vmem_limit_bytes=VMEM_LIMIT),
    )(logits, *ts)


def _merge_bwd_call(logits, ts, dm):
    rows, d = ts[0].shape
    nb = len(ts)
    tr = _merge_rows(rows, d, nb)

    def body(gl_ref, *refs):
        t_refs, dm_ref, dgl_ref, dt_refs = refs[:nb], refs[nb], refs[nb + 1], refs[nb + 2:]
        dmv = dm_ref[...]
        for b in range(nb):
            s = jax.nn.sigmoid(gl_ref[:, b * d:(b + 1) * d])
            dt_refs[b][...] = dmv * s
            dgl_ref[:, b * d:(b + 1) * d] = dmv * t_refs[b][...] * (s * (1.0 - s))

    wide = pl.BlockSpec((tr, nb * d), lambda i: (i, 0))
    one = pl.BlockSpec((tr, d), lambda i: (i, 0))
    out = jax.ShapeDtypeStruct((rows, d), F32)
    return pl.pallas_call(
        body, name="gate_merge_bwd", out_shape=(jax.ShapeDtypeStruct(logits.shape, F32),) + (out,) * nb, grid=(rows // tr,),
        in_specs=[wide] + [one] * (nb + 1), out_specs=(wide,) + (one,) * nb,
        compiler_params=pltpu.CompilerParams(dimension_semantics=("parallel",), vmem_limit_bytes=VMEM_LIMIT),
    )(logits, *ts, dm)


@jax.custom_vjp
def gate_merge(logits, *ts):
    return _merge_fwd_call(logits, ts)


def _gate_merge_fwd(logits, *ts):
    return _merge_fwd_call(logits, ts), (logits, ts)


def _gate_merge_bwd(res, dm):
    logits, ts = res
    return tuple(_merge_bwd_call(logits, ts, dm))


gate_merge.defvjp(_gate_merge_fwd, _gate_merge_bwd)


def _conv_rows(rows):
    return _pick(rows, (256, 128, 64, 32, 16, 8))


def _conv_taps(x_ref, prev_ref, next_ref, i, nb, t):
    keep_prev = (i > 0).astype(F32)
    keep_next = (i < nb - 1).astype(F32)
    ext = jnp.concatenate([prev_ref[...] * keep_prev, x_ref[...], next_ref[...] * keep_next], axis=0)
    total = t + 2 * CONV_HALO
    return [pltpu.roll(ext, (DN_CONV // 2 - j) % total, 0)[CONV_HALO:CONV_HALO + t] for j in range(DN_CONV)]


def _conv_specs(t, ch, rows):
    per = t // CONV_HALO
    last = rows // CONV_HALO - 1
    return [pl.BlockSpec((t, ch), lambda i: (i, 0)),
            pl.BlockSpec((CONV_HALO, ch), lambda i: (jnp.maximum(i * per - 1, 0), 0)),
            pl.BlockSpec((CONV_HALO, ch), lambda i: (jnp.minimum((i + 1) * per, last), 0))]


def _conv_call(x, taps):
    rows, ch = x.shape
    t = _conv_rows(rows)
    nb = rows // t

    def body(x_ref, prev_ref, next_ref, w_ref, o_ref):
        views = _conv_taps(x_ref, prev_ref, next_ref, pl.program_id(0), nb, t)
        acc = views[0] * w_ref[0:1, :]
        for j in range(1, DN_CONV):
            acc = acc + views[j] * w_ref[j:j + 1, :]
        o_ref[...] = acc

    return pl.pallas_call(
        body, name="short_conv", out_shape=jax.ShapeDtypeStruct((rows, ch), F32), grid=(nb,),
        in_specs=_conv_specs(t, ch, rows) + [pl.BlockSpec((8, ch), lambda i: (0, 0))],
        out_specs=pl.BlockSpec((t, ch), lambda i: (i, 0)),
        compiler_params=pltpu.CompilerParams(dimension_semantics=("parallel",), vmem_limit_bytes=VMEM_LIMIT),
    )(x, x, x, taps)


def _conv_dw_call(x, dy):
    rows, ch = x.shape
    t = _conv_rows(rows)
    nb = rows // t

    def body(x_ref, prev_ref, next_ref, dy_ref, o_ref):
        @pl.when(pl.program_id(0) == 0)
        def _():
            o_ref[...] = jnp.zeros_like(o_ref)

        views = _conv_taps(x_ref, prev_ref, next_ref, pl.program_id(0), nb, t)
        dyv = dy_ref[...]
        for j in range(DN_CONV):
            o_ref[j:j + 1, :] += jnp.sum(views[j] * dyv, axis=0, keepdims=True)

    return pl.pallas_call(
        body, name="short_conv_dw", out_shape=jax.ShapeDtypeStruct((8, ch), F32), grid=(nb,),
        in_specs=_conv_specs(t, ch, rows) + [pl.BlockSpec((t, ch), lambda i: (i, 0))],
        out_specs=pl.BlockSpec((8, ch), lambda i: (0, 0)),
        compiler_params=pltpu.CompilerParams(dimension_semantics=("arbitrary",), vmem_limit_bytes=VMEM_LIMIT),
    )(x, x, x, dy)


def _taps(w, flip):
    wt = jnp.transpose(w)
    if flip:
        wt = wt[::-1]
    return jnp.pad(wt, ((0, 8 - DN_CONV), (0, 0)))


@jax.custom_vjp
def short_conv(x, w):
    return _conv_call(x, _taps(w, False))


def _short_conv_fwd(x, w):
    return _conv_call(x, _taps(w, False)), (x, w)


def _short_conv_bwd(res, dy):
    x, w = res
    return _conv_call(dy, _taps(w, True)), jnp.transpose(_conv_dw_call(x, dy)[:DN_CONV])


short_conv.defvjp(_short_conv_fwd, _short_conv_bwd)


def _s5_mixer(u, a_re, a_im, log_step, b_re, b_im, c_re, c_im, d, w_glu, b_glu):
    per = LANES // SSM_GROUP
    y = u * d
    for direction in range(2):
        are, aim = a_re[direction], a_im[direction]
        step = jnp.exp(log_step[direction])[:, None]
        mag = jnp.exp(are * step)
        lam_re = mag * jnp.cos(aim * step)
        lam_im = mag * jnp.sin(aim * step)
        den = are * are + aim * aim
        nr = lam_re - 1.0
        ni = lam_im
        coef_re = (nr * are + ni * aim) / den
        coef_im = (ni * are - nr * aim) / den
        bb_re = coef_re[..., None] * b_re[direction] - coef_im[..., None] * b_im[direction]
        bb_im = coef_re[..., None] * b_im[direction] + coef_im[..., None] * b_re[direction]
        bu_re = cmm(u, _cluster_diag(jnp.swapaxes(bb_re, 1, 2), per))
        bu_im = cmm(u, _cluster_diag(jnp.swapaxes(bb_im, 1, 2), per))
        width = bu_re.shape[1]
        s_re, s_im = cscan(lam_re.reshape(1, width), lam_im.reshape(1, width), bu_re, bu_im, direction == 1)
        y = y + cmm(s_re, _cluster_diag(jnp.swapaxes(c_re[direction], 1, 2), per))
        y = y - cmm(s_im, _cluster_diag(jnp.swapaxes(c_im[direction], 1, 2), per))
    y = jax.nn.gelu(y)
    return y * jax.nn.sigmoid(pmm(y, w_glu) + b_glu)


def _deltanet_mixer(q, k, v, ab, conv_w, a_log, dt_bias, norm_g):
    seq = q.shape[0]
    qkv = short_conv(jnp.concatenate([q, k, v], axis=-1), conv_w)
    a4 = ab[:, :2 * DN_HEADS].reshape(seq, 2, DN_HEADS)
    b4 = ab[:, 2 * DN_HEADS:4 * DN_HEADS].reshape(seq, 2, DN_HEADS)
    beta = jax.nn.sigmoid(b4)
    g = -jnp.exp(a_log) * jax.nn.softplus(a4 + dt_bias)
    o = gated_delta_both(qkv, jnp.transpose(beta, (1, 2, 0)), jnp.transpose(g, (1, 2, 0)))
    return head_norm(o, norm_g)


def _axial_rope(seq):
    rows = seq // GRID_W
    row = jnp.repeat(jnp.arange(rows), GRID_W).astype(F32)
    col = jnp.tile(jnp.arange(GRID_W), rows).astype(F32)
    axis_dim = ATT_HEAD_DIM // 2
    freqs = ROPE_THETA ** (-jnp.arange(0, axis_dim, 2, dtype=F32) / axis_dim)
    ang = jnp.concatenate([row[:, None] * freqs, col[:, None] * freqs], axis=-1)
    sign = jnp.tile(jnp.array([-1.0, 1.0], F32), ATT_HEAD_DIM // 2)
    return jnp.repeat(jnp.cos(ang), 2, axis=-1), jnp.repeat(jnp.sin(ang), 2, axis=-1) * sign


def _grid_attention(q, k, v, qn_g, kn_g, cos, sin):
    qh = head_norm_rope(q, qn_g, cos, sin, ATT_HEAD_DIM ** -0.5)
    kh = head_norm_rope(k, kn_g, cos, sin, 1.0)
    return flash(qh, kh, v, ATT_HEAD_DIM, ATT_HEADS // ATT_KV_HEADS, 1.0)


def _memory_attention(q, mem_n, w_kv):
    kv = pmm(mem_n, w_kv)
    mem_w = MEM_HEADS * MEM_HEAD_DIM
    return flash(q, kv[:, :mem_w], kv[:, mem_w:], MEM_HEAD_DIM, 1, MEM_HEAD_DIM ** -0.5)


def _middle(h, x, mem, p):
    d_model = x.shape[1]
    lay, _, _ = _layout(d_model)

    def seg(name):
        _, dst, w, _ = lay[name]
        return h[:, dst:dst + w]

    cos, sin = _axial_rope(x.shape[0])
    y_a = _s5_mixer(seg("u"), p["ssm_a_re"], p["ssm_a_im"], p["ssm_log_step"], p["ssm_b_re"], p["ssm_b_im"],
                    p["ssm_c_re"], p["ssm_c_im"], p["ssm_d"], p["ssm_w_glu"], p["ssm_b_glu"]) * jax.nn.silu(seg("z_a"))
    y_b = _deltanet_mixer(seg("dq"), seg("dk"), seg("dv"), seg("dab"), p["dn_conv"], p["dn_a_log"],
                          p["dn_dt_bias"], p["dn_norm_g"]) * jax.nn.silu(seg("z_b"))
    y_c = _grid_attention(seg("aq"), seg("ak"), seg("av"), p["attn_q_norm"], p["attn_k_norm"], cos, sin) * jax.nn.silu(seg("z_c"))
    y_m = _memory_attention(seg("mq"), prms(mem, p["mem_norm_g"]), p["w_mem_kv"]) * jax.nn.silu(seg("z_m"))
    lifted, off = [], 0
    for y_br in (y_a, y_b, y_c, y_m):
        w = y_br.shape[1]
        lifted.append(pmm(y_br, p["w_branch"][off:off + w]))
        off += w
    return x + pmm(gate_merge(seg("gates"), *lifted), p["w_out"])


def _local_step(x, mem, target, w):
    depth = w["norm_g"].shape[0]
    d_model = x.shape[1]
    mid_names = [n for n in WEIGHTS if n not in ("norm_g", "w_in", "final_norm_g")]
    saved = []
    cur = x
    for layer in range(depth):
        xn = _rms_fwd_call(cur, w["norm_g"][layer], BF16)
        h = _mm(xn, w["w_in"][layer], "nn", "in_proj")
        p = {n: w[n][layer] for n in mid_names}
        nxt, vjp_mid = jax.vjp(lambda h_, x_, p_: _middle(h_, x_, mem, p_), h, cur, p)
        saved.append((cur, xn, vjp_mid))
        cur = nxt
    loss_parts, dx, d_final = _loss_call(cur, w["final_norm_g"], target)
    loss = jnp.sum(loss_parts[:, 0, 0])
    d_final = d_final.reshape(w["final_norm_g"].shape)
    grads = {n: [None] * depth for n in WEIGHTS if n != "final_norm_g"}
    for layer in reversed(range(depth)):
        x_in, xn, vjp_mid = saved[layer]
        dh, dx_skip, dp = vjp_mid(dx)
        dh = dh.astype(BF16)
        grads["w_in"][layer] = _unpad_w_in(_mm(xn, dh, "tn", "in_proj_dw"), d_model)
        dxn = _mm(dh, w["w_in"][layer], "nt", "in_proj_dx")
        dx_n, dg = _rms_bwd_call(x_in, w["norm_g"][layer], dxn)
        dx = dx_skip + dx_n
        grads["norm_g"][layer] = dg.reshape(w["norm_g"][layer].shape)
        for n in mid_names:
            grads[n][layer] = dp[n]
    out = {n: jnp.stack(v) for n, v in grads.items()}
    out["final_norm_g"] = d_final
    return loss, dx, out


_ANY = pl.BlockSpec(memory_space=pl.ANY)
_CHIP_FLIPS = [(1, 0), (0, 1), (1, 1)]


def _remote(src, dst, send_sem, recv_sem, dev):
    return pltpu.make_async_remote_copy(src_ref=src, dst_ref=dst, send_sem=send_sem, recv_sem=recv_sem,
                                        device_id=dev, device_id_type=pl.DeviceIdType.MESH)


def _flip(v, bit):
    return 1 - v if bit else v


def _gather_chips(shards):
    n = len(shards)

    def body(*refs):
        ins, outs = refs[:n], refs[n:2 * n]
        send_sems, recv_sems, local_sems = refs[2 * n:]
        x, y, c = lax.axis_index("x"), lax.axis_index("y"), lax.axis_index("c")
        locals_, sends = [], []
        for a in range(n):
            cp = pltpu.make_async_copy(ins[a], outs[a].at[2 * x + y], local_sems.at[a])
            cp.start()
            locals_.append(cp)

        def half(a, px, py, pc):
            rh = shards[a].shape[1] // 2
            return outs[a].at[2 * px + py, :, pl.ds(pc * rh, rh), :]

        for a in range(n):
            rh = shards[a].shape[1] // 2
            for j, (fx, fy) in enumerate(_CHIP_FLIPS):
                cp = _remote(ins[a].at[:, pl.ds(c * rh, rh), :], half(a, x, y, c), send_sems.at[6 * a + j],
                             recv_sems.at[6 * a + j], (_flip(x, fx), _flip(y, fy), c))
                cp.start()
                sends.append(cp)
        for a in range(n):
            for j, (fx, fy) in enumerate(_CHIP_FLIPS):
                blk = half(a, _flip(x, fx), _flip(y, fy), c)
                _remote(blk, blk, send_sems.at[6 * a + j], recv_sems.at[6 * a + j], (x, y, c)).wait_recv()
                cp = _remote(blk, blk, send_sems.at[6 * a + 3 + j], recv_sems.at[6 * a + 3 + j], (x, y, 1 - c))
                cp.start()
                sends.append(cp)
        for a in range(n):
            for j, (fx, fy) in enumerate(_CHIP_FLIPS):
                blk = half(a, _flip(x, fx), _flip(y, fy), 1 - c)
                _remote(blk, blk, send_sems.at[6 * a + 3 + j], recv_sems.at[6 * a + 3 + j], (x, y, c)).wait_recv()
        for cp in sends:
            cp.wait_send()
        for cp in locals_:
            cp.wait()

    return pl.pallas_call(
        body, name="gather_weights",
        out_shape=tuple(jax.ShapeDtypeStruct((N_CHIPS,) + s.shape, s.dtype) for s in shards),
        in_specs=[_ANY] * n, out_specs=tuple([_ANY] * n),
        scratch_shapes=[pltpu.SemaphoreType.DMA((6 * n,)), pltpu.SemaphoreType.DMA((6 * n,)),
                        pltpu.SemaphoreType.DMA((n,))],
    )(*shards)


def _exchange_pair(parts):
    n = len(parts)

    def body(*refs):
        ins, outs = refs[:n], refs[n:2 * n]
        send_sems, recv_sems = refs[2 * n:]
        x, y, c = lax.axis_index("x"), lax.axis_index("y"), lax.axis_index("c")
        started = []
        for a in range(n):
            rh = parts[a].shape[2] // 2
            cp = _remote(ins[a].at[:, :, pl.ds((1 - c) * rh, rh), :], outs[a], send_sems.at[a], recv_sems.at[a], (x, y, 1 - c))
            cp.start()
            started.append(cp)
        for a in range(n):
            _remote(outs[a], outs[a], send_sems.at[a], recv_sems.at[a], (x, y, c)).wait_recv()
        for cp in started:
            cp.wait_send()

    return pl.pallas_call(
        body, name="exchange_pair",
        out_shape=tuple(jax.ShapeDtypeStruct((p.shape[0], p.shape[1], p.shape[2] // 2, p.shape[3]), p.dtype) for p in parts),
        in_specs=[_ANY] * n, out_specs=tuple([_ANY] * n),
        scratch_shapes=[pltpu.SemaphoreType.DMA((n,)), pltpu.SemaphoreType.DMA((n,))],
    )(*parts)


def _exchange_chips(pair_sums, small):
    n = len(pair_sums)

    def body(*refs):
        ins, small_in = refs[:n], refs[n]
        outs, small_out = refs[n + 1:2 * n + 1], refs[2 * n + 1]
        send_sems, recv_sems, small_send, small_recv, local_sem = refs[2 * n + 2:]
        x, y, c = lax.axis_index("x"), lax.axis_index("y"), lax.axis_index("c")
        me = 4 * x + 2 * y + c
        started = []
        own = pltpu.make_async_copy(small_in, small_out.at[me], local_sem)
        own.start()
        for f, (fx, fy) in enumerate(_CHIP_FLIPS):
            px, py = _flip(x, fx), _flip(y, fy)
            for a in range(n):
                cp = _remote(ins[a].at[2 * px + py], outs[a].at[f], send_sems.at[3 * a + f], recv_sems.at[3 * a + f], (px, py, c))
                cp.start()
                started.append(cp)
        for k in range(1, N_DEV):
            peer = (_flip(x, k >> 2 & 1), _flip(y, k >> 1 & 1), _flip(c, k & 1))
            cp = _remote(small_in, small_out.at[me], small_send.at[k - 1], small_recv.at[k - 1], peer)
            cp.start()
            started.append(cp)
        for f in range(len(_CHIP_FLIPS)):
            for a in range(n):
                _remote(outs[a].at[f], outs[a].at[f], send_sems.at[3 * a + f], recv_sems.at[3 * a + f], (x, y, c)).wait_recv()
        for k in range(1, N_DEV):
            peer = 4 * _flip(x, k >> 2 & 1) + 2 * _flip(y, k >> 1 & 1) + _flip(c, k & 1)
            _remote(small_out.at[peer], small_out.at[peer], small_send.at[k - 1], small_recv.at[k - 1], (x, y, c)).wait_recv()
        for cp in started:
            cp.wait_send()
        own.wait()

    shapes = tuple(jax.ShapeDtypeStruct((3,) + p.shape[1:], p.dtype) for p in pair_sums)
    shapes += (jax.ShapeDtypeStruct((N_DEV,) + small.shape, small.dtype),)
    return pl.pallas_call(
        body, name="exchange_chips", out_shape=shapes, in_specs=[_ANY] * (n + 1), out_specs=tuple([_ANY] * (n + 1)),
        scratch_shapes=[pltpu.SemaphoreType.DMA((3 * n,)), pltpu.SemaphoreType.DMA((3 * n,)),
                        pltpu.SemaphoreType.DMA((N_DEV - 1,)), pltpu.SemaphoreType.DMA((N_DEV - 1,)),
                        pltpu.SemaphoreType.DMA(())],
    )(*pair_sums, small)


def _swap_halves(arrays):
    n = len(arrays)

    def body(*refs):
        outs = refs[n:2 * n]
        send_sems, recv_sems = refs[2 * n:]
        x, y, c = lax.axis_index("x"), lax.axis_index("y"), lax.axis_index("c")
        started = []
        for a in range(n):
            rh = arrays[a].shape[1] // 2
            mine = outs[a].at[:, pl.ds(c * rh, rh), :]
            cp = _remote(mine, mine, send_sems.at[a], recv_sems.at[a], (x, y, 1 - c))
            cp.start()
            started.append(cp)
        for a in range(n):
            rh = arrays[a].shape[1] // 2
            theirs = outs[a].at[:, pl.ds((1 - c) * rh, rh), :]
            _remote(theirs, theirs, send_sems.at[a], recv_sems.at[a], (x, y, c)).wait_recv()
        for cp in started:
            cp.wait_send()

    return pl.pallas_call(
        body, name="swap_halves", out_shape=tuple(jax.ShapeDtypeStruct(t.shape, t.dtype) for t in arrays),
        in_specs=[_ANY] * n, out_specs=tuple([_ANY] * n), input_output_aliases={a: a for a in range(n)},
        scratch_shapes=[pltpu.SemaphoreType.DMA((n,)), pltpu.SemaphoreType.DMA((n,))],
    )(*arrays)


def _adamw_math(g, w, m, v):
    m = ADAM_B1 * m + (1.0 - ADAM_B1) * g
    v = ADAM_B2 * v + (1.0 - ADAM_B2) * jnp.square(g)
    m_hat = m / (1.0 - ADAM_B1 ** ADAM_STEP)
    v_hat = v / (1.0 - ADAM_B2 ** ADAM_STEP)
    delta = -ADAM_LR * (m_hat / (jnp.sqrt(v_hat) + ADAM_EPS) + ADAM_WD * w)
    return delta, m, v


def _row_tile(rows, row_bytes, budget, step=8):
    best = step
    for t in range(step, rows + 1, step):
        if rows % t == 0 and t * row_bytes <= budget:
            best = t
    return best


def _pair_sum(part, other, core):
    nj, a, r, c = part.shape
    rh = r // 2
    tr = _row_tile(rh, 4 * (-(-c // LANES) * LANES), 512 * 1024, 16)

    def body(core_ref, p_ref, o_ref, s32_ref, s16_ref):
        s = p_ref[0, 0, 0] + o_ref[0, 0]
        s32_ref[0, 0] = s
        s16_ref[0, 0] = s.astype(BF16)

    spec = pl.BlockSpec((1, 1, tr, c), lambda j, l, i, core_ref: (j, l, i, 0))
    return pl.pallas_call(
        body, name="pair_sum",
        out_shape=(jax.ShapeDtypeStruct(other.shape, F32), jax.ShapeDtypeStruct(other.shape, BF16)),
        grid_spec=pltpu.PrefetchScalarGridSpec(
            num_scalar_prefetch=1, grid=(nj, a, rh // tr),
            in_specs=[pl.BlockSpec((1, 1, 1, tr, c), lambda j, l, i, core_ref: (j, l, core_ref[0], i, 0)), spec],
            out_specs=(spec, spec)),
        compiler_params=pltpu.CompilerParams(dimension_semantics=("parallel", "parallel", "parallel"),
                                             vmem_limit_bytes=VMEM_LIMIT),
    )(core, part.reshape(nj, a, 2, rh, c), other)


def _sum_adamw_big(own, landed, w, m, v, place):
    _, a, rh, c = own.shape
    tr = _row_tile(rh, 4 * (-(-c // LANES) * LANES), 256 * 1024, 16)
    w4, m4, v4 = (t.reshape(a, 2, rh, c) for t in (w, m, v))

    def body(place_ref, q_ref, p_ref, w_ref, m_ref, v_ref, g_out, d_out, m_out, v_out):
        g = q_ref[0, 0]
        for f in range(3):
            g = g + p_ref[f, 0].astype(F32)
        delta, mm_, vv_ = _adamw_math(g, w_ref[0, 0], m_ref[0, 0], v_ref[0, 0])
        g_out[0, 0] = g
        d_out[0, 0] = delta
        m_out[0, 0] = mm_
        v_out[0, 0] = vv_

    wspec = pl.BlockSpec((1, 1, tr, c), lambda l, i, place_ref: (l, place_ref[0], i, 0))
    out = jax.ShapeDtypeStruct((a, 2, rh, c), F32)
    res = pl.pallas_call(
        body, name="sum_adamw_big", out_shape=(out,) * 4,
        grid_spec=pltpu.PrefetchScalarGridSpec(
            num_scalar_prefetch=1, grid=(a, rh // tr),
            in_specs=[pl.BlockSpec((1, 1, tr, c), lambda l, i, place_ref: (place_ref[1], l, i, 0)),
                      pl.BlockSpec((3, 1, tr, c), lambda l, i, place_ref: (0, l, i, 0)), wspec, wspec, wspec],
            out_specs=(wspec,) * 4),
        compiler_params=pltpu.CompilerParams(dimension_semantics=("parallel", "parallel"), vmem_limit_bytes=VMEM_LIMIT),
    )(place, own, landed, w4, m4, v4)
    return [t.reshape(a, 2 * rh, c) for t in res]


def _sum_small(parts):
    _, rows, _ = parts.shape
    tr = _row_tile(rows, 4 * LANES, 256 * 1024)

    def body(p_ref, o_ref):
        g = p_ref[0]
        for k in range(1, N_DEV):
            g = g + p_ref[k]
        o_ref[...] = g

    return pl.pallas_call(
        body, name="sum_small", out_shape=jax.ShapeDtypeStruct((rows, LANES), F32), grid=(rows // tr,),
        in_specs=[pl.BlockSpec((N_DEV, tr, LANES), lambda i: (0, i, 0))], out_specs=pl.BlockSpec((tr, LANES), lambda i: (i, 0)),
        compiler_params=pltpu.CompilerParams(dimension_semantics=("parallel",)),
    )(parts)


def _adamw_small(g, w, m, v):
    rows, _ = g.shape
    tr = _row_tile(rows, 4 * LANES, 256 * 1024)

    def body(g_ref, w_ref, m_ref, v_ref, d_out, m_out, v_out):
        d_out[...], m_out[...], v_out[...] = _adamw_math(g_ref[...], w_ref[...], m_ref[...], v_ref[...])

    spec = pl.BlockSpec((tr, LANES), lambda i: (i, 0))
    out = jax.ShapeDtypeStruct((rows, LANES), F32)
    return pl.pallas_call(
        body, name="adamw_small", out_shape=(out,) * 3, grid=(rows // tr,), in_specs=[spec] * 4, out_specs=(spec,) * 3,
        compiler_params=pltpu.CompilerParams(dimension_semantics=("parallel",)),
    )(g, w, m, v)


def _pack(arrays):
    flat = jnp.concatenate([a.reshape(-1) for a in arrays])
    rows = -(-flat.shape[0] // (PACK_ROWS * LANES)) * PACK_ROWS
    return jnp.pad(flat, (0, rows * LANES - flat.shape[0])).reshape(rows, LANES)


def _unpack(packed, shapes):
    flat, out, off = packed.reshape(-1), [], 0
    for s in shapes:
        size = math.prod(s)
        out.append(flat[off:off + size].reshape(s))
        off += size
    return out


def kernel(x, mem, norm_g, w_in, ssm_a_re, ssm_a_im, ssm_log_step, ssm_b_re, ssm_b_im, ssm_c_re, ssm_c_im, ssm_d, ssm_w_glu, ssm_b_glu, dn_conv, dn_a_log, dn_dt_bias, dn_norm_g, attn_q_norm, attn_k_norm, mem_norm_g, w_mem_kv, w_branch, w_out, final_norm_g, loss_target, m_norm_g, m_w_in, m_ssm_a_re, m_ssm_a_im, m_ssm_log_step, m_ssm_b_re, m_ssm_b_im, m_ssm_c_re, m_ssm_c_im, m_ssm_d, m_ssm_w_glu, m_ssm_b_glu, m_dn_conv, m_dn_a_log, m_dn_dt_bias, m_dn_norm_g, m_attn_q_norm, m_attn_k_norm, m_mem_norm_g, m_w_mem_kv, m_w_branch, m_w_out, m_final_norm_g, v_norm_g, v_w_in, v_ssm_a_re, v_ssm_a_im, v_ssm_log_step, v_ssm_b_re, v_ssm_b_im, v_ssm_c_re, v_ssm_c_im, v_ssm_d, v_ssm_w_glu, v_ssm_b_glu, v_dn_conv, v_dn_a_log, v_dn_dt_bias, v_dn_norm_g, v_attn_q_norm, v_attn_k_norm, v_mem_norm_g, v_w_mem_kv, v_w_branch, v_w_out, v_final_norm_g):
    args = locals()
    wts = {n: args[n] for n in WEIGHTS}
    mom = {n: args["m_" + n] for n in WEIGHTS}
    var = {n: args["v_" + n] for n in WEIGHTS}
    d_model = x.shape[-1]
    chip = 2 * lax.axis_index("x") + lax.axis_index("y")
    core = lax.axis_index("c")

    conv_rows = dn_conv.shape[1]
    conv_flat = _pack([dn_conv])
    gathered = _gather_chips([wts[n].astype(BF16) for n in BIG] + [conv_flat[None]])
    full = {n: wts[n] for n in SMALL}
    g_in = gathered[0]
    full["w_in"] = _pad_w_in(jnp.concatenate([g_in[j] for j in range(N_CHIPS)], axis=-1), d_model)
    for n, g in zip(BIG[1:], gathered[1:]):
        full[n] = jnp.concatenate([g[j] for j in range(N_CHIPS)], axis=1).astype(F32)
    conv_all = gathered[-1][:, 0].reshape(N_CHIPS, -1)[:, :dn_conv.size].reshape((N_CHIPS,) + dn_conv.shape)
    full["dn_conv"] = jnp.concatenate([conv_all[j] for j in range(N_CHIPS)], axis=1)

    loss, grad_x, grads = _local_step(x[0], mem[0], loss_target[0], full)
    loss = lax.psum(loss, ("x", "y", "c"))

    big_parts = []
    for n in BIG:
        g = grads[n]
        if n == "w_in":
            cols = g.shape[-1] // N_CHIPS
            big_parts.append(jnp.stack([g[..., j * cols:(j + 1) * cols] for j in range(N_CHIPS)]))
        else:
            rows = g.shape[1] // N_CHIPS
            big_parts.append(jnp.stack([g[:, j * rows:(j + 1) * rows] for j in range(N_CHIPS)]))
    core_arr = jnp.reshape(core, (1,)).astype(jnp.int32)
    place = jnp.stack([core, chip]).astype(jnp.int32)
    from_pair = _exchange_pair(big_parts)
    sums = [_pair_sum(p, o, core_arr) for p, o in zip(big_parts, from_pair)]
    small_shapes = [grads[n].shape for n in SMALL]
    landed = _exchange_chips([s[1] for s in sums], _pack([grads[n] for n in SMALL]))

    halves = []
    for n, s, parts in zip(BIG, sums, landed[:-1]):
        halves.extend(_sum_adamw_big(s[0], parts, wts[n], mom[n], var[n], place))
    swapped = _swap_halves(halves)
    res = {}
    for i, n in enumerate(BIG):
        res[n] = swapped[4 * i:4 * i + 4]
    small_g = _unpack(_sum_small(landed[-1]), small_shapes)
    small_g = dict(zip(SMALL, small_g))
    small_g["dn_conv"] = lax.dynamic_slice_in_dim(small_g["dn_conv"], chip * conv_rows, conv_rows, axis=1)
    shapes = [small_g[n].shape for n in SMALL]
    upd = _adamw_small(_pack([small_g[n] for n in SMALL]), _pack([wts[n] for n in SMALL]),
                       _pack([mom[n] for n in SMALL]), _pack([var[n] for n in SMALL]))
    upd = [_unpack(u, shapes) for u in upd]
    for i, n in enumerate(SMALL):
        res[n] = (small_g[n], upd[0][i], upd[1][i], upd[2][i])

    out = [loss, grad_x[None]]
    for kind in range(4):
        out.extend(res[n][kind] for n in WEIGHTS)
    return tuple(out)
```

```python
import functools
import math

import jax
import jax.numpy as jnp
from jax import lax
from jax.experimental import pallas as pl
from jax.experimental.pallas import tpu as pltpu

F32 = jnp.float32
BF16 = jnp.bfloat16

GRID_W = 64
EPS = 1e-6
SSM_GROUP = 16
SSM_STATE = 64
SSM_GROUPS = 48
DN_HEADS = 6
DN_HEAD_DIM = 128
DN_CONV = 5
DN_CHUNK = 64
ATT_HEADS = 8
ATT_KV_HEADS = 2
ATT_HEAD_DIM = 128
ROPE_THETA = 10000.0
MEM_HEADS = 4
MEM_HEAD_DIM = 128
N_BRANCH = 4

ADAM_LR = 0.001
ADAM_B1 = 0.9
ADAM_B2 = 0.999
ADAM_EPS = 1e-08
ADAM_WD = 0.01
ADAM_STEP = 10

N_CHIPS = 4
N_DEV = 8
LANES = 128
VMEM_LIMIT = 48 * 1024 * 1024
IN_PAD_UNIT = 512
MM_FULL_K = 2048
PACK_ROWS = 512
CONV_HALO = 8
ATT_HEADS_PER_STEP = 4

WEIGHTS = ['norm_g', 'w_in', 'ssm_a_re', 'ssm_a_im', 'ssm_log_step', 'ssm_b_re', 'ssm_b_im', 'ssm_c_re',
           'ssm_c_im', 'ssm_d', 'ssm_w_glu', 'ssm_b_glu', 'dn_conv', 'dn_a_log', 'dn_dt_bias', 'dn_norm_g',
           'attn_q_norm', 'attn_k_norm', 'mem_norm_g', 'w_mem_kv', 'w_branch', 'w_out', 'final_norm_g']
BIG = ['w_in', 'ssm_w_glu', 'w_mem_kv', 'w_branch', 'w_out']
SMALL = [n for n in WEIGHTS if n not in BIG]


def _pick(dim, cands):
    for c in cands:
        if dim % c == 0:
            return c
    return dim


def _widths(d_model):
    ssm_w = SSM_GROUPS * SSM_GROUP
    dn_w = DN_HEADS * DN_HEAD_DIM
    att_w = ATT_HEADS * ATT_HEAD_DIM
    kv_w = ATT_KV_HEADS * ATT_HEAD_DIM
    mem_w = MEM_HEADS * MEM_HEAD_DIM
    return [("u", ssm_w), ("z_a", ssm_w), ("dq", dn_w), ("dk", dn_w), ("dv", dn_w), ("dab", 4 * DN_HEADS),
            ("z_b", dn_w), ("aq", att_w), ("ak", kv_w), ("av", kv_w), ("z_c", att_w), ("mq", mem_w),
            ("z_m", mem_w), ("gates", N_BRANCH * d_model)]


def _layout(d_model):
    out, src, dst = {}, 0, 0
    for name, w in _widths(d_model):
        pw = -(-w // IN_PAD_UNIT) * IN_PAD_UNIT if w % LANES else w
        out[name] = (src, dst, w, pw)
        src += w
        dst += pw
    return out, src, dst


def _pad_w_in(full, d_model):
    lay, _, _ = _layout(d_model)
    parts = []
    for name, _ in _widths(d_model):
        s, _, w, pw = lay[name]
        seg = full[..., s:s + w]
        if pw != w:
            seg = jnp.pad(seg, [(0, 0)] * (full.ndim - 1) + [(0, pw - w)])
        parts.append(seg)
    return jnp.concatenate(parts, axis=-1)


def _unpad_w_in(padded, d_model):
    lay, _, _ = _layout(d_model)
    return jnp.concatenate([padded[..., lay[n][1]:lay[n][1] + lay[n][2]] for n, _ in _widths(d_model)], axis=-1)


def _mm(a, b, mode="nn", name="mm"):
    if mode == "nn":
        (m, k), (k2, n) = a.shape, b.shape
    elif mode == "nt":
        (m, k), (n, k2) = a.shape, b.shape
    else:
        (k, m), (k2, n) = a.shape, b.shape
    assert k == k2, (a.shape, b.shape, mode)
    tk = k if k <= MM_FULL_K else _pick(k, (1024, 512, 256, 128))
    tall = mode == "tn" or (mode == "nn" and a.dtype == BF16 and tk == k)
    tm = _pick(m, (2048, 1024, 512, 256, 128) if tall else (1024, 512, 256, 128))
    tn = _pick(n, (512, 384, 256, 128) if tk > 1024 or tm > 1024 else (1024, 512, 384, 256, 128))
    if mode == "nt" and k > MM_FULL_K:
        tk = _pick(k, (1536, 1024, 512, 256, 128))
    nk = k // tk
    dims = {"nn": (((1,), (0,)), ((), ())), "nt": (((1,), (1,)), ((), ())), "tn": (((0,), (0,)), ((), ()))}[mode]

    def body(a_ref, b_ref, o_ref, *scratch):
        prod = lax.dot_general(a_ref[...].astype(BF16), b_ref[...].astype(BF16), dims, preferred_element_type=F32)
        if nk == 1:
            o_ref[...] = prod
            return
        acc_ref, = scratch
        kk = pl.program_id(2)

        @pl.when(kk == 0)
        def _():
            acc_ref[...] = prod

        @pl.when(kk > 0)
        def _():
            acc_ref[...] += prod

        @pl.when(kk == nk - 1)
        def _():
            o_ref[...] = acc_ref[...]

    a_spec = pl.BlockSpec((tk, tm), lambda i, j, kk: (kk, i)) if mode == "tn" else pl.BlockSpec((tm, tk), lambda i, j, kk: (i, kk))
    b_spec = pl.BlockSpec((tn, tk), lambda i, j, kk: (j, kk)) if mode == "nt" else pl.BlockSpec((tk, tn), lambda i, j, kk: (kk, j))
    return pl.pallas_call(
        body, name=name, out_shape=jax.ShapeDtypeStruct((m, n), F32), grid=(m // tm, n // tn, nk),
        in_specs=[a_spec, b_spec], out_specs=pl.BlockSpec((tm, tn), lambda i, j, kk: (i, j)),
        scratch_shapes=[pltpu.VMEM((tm, tn), F32)] if nk > 1 else [],
        compiler_params=pltpu.CompilerParams(dimension_semantics=("parallel", "parallel", "arbitrary"),
                                             vmem_limit_bytes=VMEM_LIMIT),
    )(a, b)


@jax.custom_vjp
def pmm(a, b):
    return _mm(a, b, "nn", "pmm_fwd")


def _pmm_fwd(a, b):
    return _mm(a, b, "nn", "pmm_fwd"), (a, b)


def _pmm_bwd(res, dc):
    a, b = res
    return _mm(dc, b, "nt", "pmm_da"), _mm(a, dc, "tn", "pmm_db")


pmm.defvjp(_pmm_fwd, _pmm_bwd)


def _cmm_call(a, w):
    m = a.shape[0]
    nc, kc, nn = w.shape
    tm = _pick(m, (1024, 512, 256, 128))

    def body(a_ref, w_ref, o_ref):
        o_ref[...] = lax.dot_general(a_ref[...].astype(BF16), w_ref[0].astype(BF16), (((1,), (0,)), ((), ())),
                                     preferred_element_type=F32)

    return pl.pallas_call(
        body, name="cmm", out_shape=jax.ShapeDtypeStruct((m, nc * nn), F32), grid=(m // tm, nc),
        in_specs=[pl.BlockSpec((tm, kc), lambda i, c: (i, c)), pl.BlockSpec((1, kc, nn), lambda i, c: (c, 0, 0))],
        out_specs=pl.BlockSpec((tm, nn), lambda i, c: (i, c)),
        compiler_params=pltpu.CompilerParams(dimension_semantics=("parallel", "parallel"), vmem_limit_bytes=VMEM_LIMIT),
    )(a, w)


def _cmm_dw_call(a, dc, kc, nn):
    m = a.shape[0]
    nc = a.shape[1] // kc
    tm = _pick(m, (1024, 512, 256, 128))
    nm = m // tm

    def body(a_ref, d_ref, o_ref, acc_s):
        i = pl.program_id(1)

        @pl.when(i == 0)
        def _():
            acc_s[...] = jnp.zeros_like(acc_s)

        acc_s[...] += lax.dot_general(a_ref[...].astype(BF16), d_ref[...].astype(BF16), (((0,), (0,)), ((), ())),
                                      preferred_element_type=F32)

        @pl.when(i == nm - 1)
        def _():
            o_ref[0] = acc_s[...]

    return pl.pallas_call(
        body, name="cmm_dw", out_shape=jax.ShapeDtypeStruct((nc, kc, nn), F32), grid=(nc, nm),
        in_specs=[pl.BlockSpec((tm, kc), lambda c, i: (i, c)), pl.BlockSpec((tm, nn), lambda c, i: (i, c))],
        out_specs=pl.BlockSpec((1, kc, nn), lambda c, i: (c, 0, 0)),
        scratch_shapes=[pltpu.VMEM((kc, nn), F32)],
        compiler_params=pltpu.CompilerParams(dimension_semantics=("parallel", "arbitrary"), vmem_limit_bytes=VMEM_LIMIT),
    )(a, dc)


@jax.custom_vjp
def cmm(a, w):
    return _cmm_call(a, w)


def _cmm_fwd(a, w):
    return _cmm_call(a, w), (a, w)


def _cmm_bwd(res, dc):
    a, w = res
    return _cmm_call(dc, jnp.swapaxes(w, 1, 2)), _cmm_dw_call(a, dc, w.shape[1], w.shape[2])


cmm.defvjp(_cmm_fwd, _cmm_bwd)


def _cluster_diag(w, per):
    g, a, b = w.shape
    eye = jnp.eye(per, dtype=w.dtype)
    wc = w.reshape(g // per, per, a, b)
    return (wc[:, :, :, None, :] * eye[None, :, None, :, None]).reshape(g // per, per * a, per * b)


def _norm_rows(rows, d):
    return _row_tile(rows, 4 * d, 1024 * 1024)


def _rms_fwd_call(x, g, out_dtype=F32):
    rows, d = x.shape
    tr = _norm_rows(rows, d)

    def body(x_ref, g_ref, o_ref):
        xv = x_ref[...]
        o_ref[...] = (xv * lax.rsqrt(jnp.mean(xv * xv, axis=1, keepdims=True) + EPS) * g_ref[...]).astype(out_dtype)

    return pl.pallas_call(
        body, name="rms_fwd", out_shape=jax.ShapeDtypeStruct((rows, d), out_dtype), grid=(rows // tr,),
        in_specs=[pl.BlockSpec((tr, d), lambda i: (i, 0)), pl.BlockSpec((1, d), lambda i: (0, 0))],
        out_specs=pl.BlockSpec((tr, d), lambda i: (i, 0)),
        compiler_params=pltpu.CompilerParams(dimension_semantics=("parallel",), vmem_limit_bytes=VMEM_LIMIT),
    )(x, g.reshape(1, d))


def _rms_bwd_call(x, g, dy):
    rows, d = x.shape
    tr = _norm_rows(rows, d)

    def body(x_ref, g_ref, dy_ref, dx_ref, dg_ref):
        @pl.when(pl.program_id(0) == 0)
        def _():
            dg_ref[...] = jnp.zeros_like(dg_ref)

        xv = x_ref[...]
        r = lax.rsqrt(jnp.mean(xv * xv, axis=1, keepdims=True) + EPS)
        xh = xv * r
        dyv = dy_ref[...]
        dxh = dyv * g_ref[...]
        dx_ref[...] = r * (dxh - xh * jnp.mean(dxh * xh, axis=1, keepdims=True))
        dg_ref[...] += jnp.sum(dyv * xh, axis=0, keepdims=True)

    return pl.pallas_call(
        body, name="rms_bwd", out_shape=(jax.ShapeDtypeStruct((rows, d), F32), jax.ShapeDtypeStruct((1, d), F32)),
        grid=(rows // tr,),
        in_specs=[pl.BlockSpec((tr, d), lambda i: (i, 0)), pl.BlockSpec((1, d), lambda i: (0, 0)),
                  pl.BlockSpec((tr, d), lambda i: (i, 0))],
        out_specs=(pl.BlockSpec((tr, d), lambda i: (i, 0)), pl.BlockSpec((1, d), lambda i: (0, 0))),
        compiler_params=pltpu.CompilerParams(dimension_semantics=("arbitrary",), vmem_limit_bytes=VMEM_LIMIT),
    )(x, g.reshape(1, d), dy)


@jax.custom_vjp
def prms(x, g):
    return _rms_fwd_call(x, g)


def _prms_fwd(x, g):
    return _rms_fwd_call(x, g), (x, g)


def _prms_bwd(res, dy):
    x, g = res
    dx, dg = _rms_bwd_call(x, g, dy)
    return dx, dg.reshape(g.shape)


prms.defvjp(_prms_fwd, _prms_bwd)


def _pair_swap(v):
    lane = lax.broadcasted_iota(jnp.int32, v.shape, 1)
    return jnp.where(lane % 2 == 0, pltpu.roll(v, v.shape[1] - 1, 1), pltpu.roll(v, 1, 1))


def _hnorm_fwd_call(x, g, cos, sin, scale):
    rows, width = x.shape
    d = g.shape[-1]
    h = width // d
    rope = cos is not None
    tr = _row_tile(rows, 4 * width, 1024 * 1024)

    def body(*refs):
        if rope:
            x_ref, g_ref, c_ref, s_ref, o_ref = refs
        else:
            x_ref, g_ref, o_ref = refs
        for i in range(h):
            cols = slice(i * d, (i + 1) * d)
            xv = x_ref[:, cols]
            n = xv * lax.rsqrt(jnp.mean(xv * xv, axis=1, keepdims=True) + EPS) * g_ref[...]
            if rope:
                n = (c_ref[...] * n + s_ref[...] * _pair_swap(n)) * scale
            o_ref[:, cols] = n

    row_spec = pl.BlockSpec((tr, width), lambda i: (i, 0))
    tab_spec = pl.BlockSpec((tr, d), lambda i: (i, 0))
    ins = [x, g.reshape(1, d)] + ([cos, sin] if rope else [])
    return pl.pallas_call(
        body, name="head_norm_fwd", out_shape=jax.ShapeDtypeStruct((rows, width), F32), grid=(rows // tr,),
        in_specs=[row_spec, pl.BlockSpec((1, d), lambda i: (0, 0))] + ([tab_spec, tab_spec] if rope else []),
        out_specs=row_spec,
        compiler_params=pltpu.CompilerParams(dimension_semantics=("parallel",), vmem_limit_bytes=VMEM_LIMIT),
    )(*ins)


def _hnorm_bwd_call(x, g, cos, sin, scale, dy):
    rows, width = x.shape
    d = g.shape[-1]
    h = width // d
    rope = cos is not None
    tr = _row_tile(rows, 4 * width, 1024 * 1024)

    def body(*refs):
        if rope:
            x_ref, g_ref, c_ref, s_ref, dy_ref, dx_ref, dg_ref = refs
        else:
            x_ref, g_ref, dy_ref, dx_ref, dg_ref = refs

        @pl.when(pl.program_id(0) == 0)
        def _():
            dg_ref[...] = jnp.zeros_like(dg_ref)

        dg = jnp.zeros((1, d), F32)
        for i in range(h):
            cols = slice(i * d, (i + 1) * d)
            xv = x_ref[:, cols]
            r = lax.rsqrt(jnp.mean(xv * xv, axis=1, keepdims=True) + EPS)
            xh = xv * r
            dn = dy_ref[:, cols]
            if rope:
                dn = dn * scale
                dn = c_ref[...] * dn + _pair_swap(s_ref[...] * dn)
            dxh = dn * g_ref[...]
            dx_ref[:, cols] = r * (dxh - xh * jnp.mean(dxh * xh, axis=1, keepdims=True))
            dg = dg + jnp.sum(dn * xh, axis=0, keepdims=True)
        dg_ref[...] += dg

    row_spec = pl.BlockSpec((tr, width), lambda i: (i, 0))
    tab_spec = pl.BlockSpec((tr, d), lambda i: (i, 0))
    vec_spec = pl.BlockSpec((1, d), lambda i: (0, 0))
    ins = [x, g.reshape(1, d)] + ([cos, sin] if rope else []) + [dy]
    return pl.pallas_call(
        body, name="head_norm_bwd", out_shape=(jax.ShapeDtypeStruct((rows, width), F32), jax.ShapeDtypeStruct((1, d), F32)),
        grid=(rows // tr,),
        in_specs=[row_spec, vec_spec] + ([tab_spec, tab_spec] if rope else []) + [row_spec],
        out_specs=(row_spec, vec_spec),
        compiler_params=pltpu.CompilerParams(dimension_semantics=("arbitrary",), vmem_limit_bytes=VMEM_LIMIT),
    )(*ins)


@jax.custom_vjp
def head_norm(x, g):
    return _hnorm_fwd_call(x, g, None, None, 1.0)


def _head_norm_fwd(x, g):
    return _hnorm_fwd_call(x, g, None, None, 1.0), (x, g)


def _head_norm_bwd(res, dy):
    x, g = res
    dx, dg = _hnorm_bwd_call(x, g, None, None, 1.0, dy)
    return dx, dg.reshape(g.shape)


head_norm.defvjp(_head_norm_fwd, _head_norm_bwd)


@functools.partial(jax.custom_vjp, nondiff_argnums=(4,))
def head_norm_rope(x, g, cos, sin, scale):
    return _hnorm_fwd_call(x, g, cos, sin, scale)


def _head_norm_rope_fwd(x, g, cos, sin, scale):
    return _hnorm_fwd_call(x, g, cos, sin, scale), (x, g, cos, sin)


def _head_norm_rope_bwd(scale, res, dy):
    x, g, cos, sin = res
    dx, dg = _hnorm_bwd_call(x, g, cos, sin, scale, dy)
    return dx, dg.reshape(g.shape), jnp.zeros_like(cos), jnp.zeros_like(sin)


head_norm_rope.defvjp(_head_norm_rope_fwd, _head_norm_rope_bwd)


def _loss_call(x, g, target):
    rows, d = x.shape
    tr = _norm_rows(rows, d)
    nb = rows // tr

    def body(x_ref, g_ref, t_ref, l_ref, dx_ref, dg_ref):
        @pl.when(pl.program_id(0) == 0)
        def _():
            dg_ref[...] = jnp.zeros_like(dg_ref)

        xv = x_ref[...]
        r = lax.rsqrt(jnp.mean(xv * xv, axis=1, keepdims=True) + EPS)
        xh = xv * r
        err = xh * g_ref[...] - t_ref[...]
        l_ref[...] = jnp.broadcast_to(0.5 * jnp.sum(jnp.mean(err * err, axis=1, keepdims=True), axis=0, keepdims=True)[None],
                                      l_ref.shape)
        dyv = err * (1.0 / d)
        dxh = dyv * g_ref[...]
        dx_ref[...] = r * (dxh - xh * jnp.mean(dxh * xh, axis=1, keepdims=True))
        dg_ref[...] += jnp.sum(dyv * xh, axis=0, keepdims=True)

    return pl.pallas_call(
        body, name="final_loss",
        out_shape=(jax.ShapeDtypeStruct((nb, 1, LANES), F32), jax.ShapeDtypeStruct((rows, d), F32),
                   jax.ShapeDtypeStruct((1, d), F32)),
        grid=(nb,),
        in_specs=[pl.BlockSpec((tr, d), lambda i: (i, 0)), pl.BlockSpec((1, d), lambda i: (0, 0)),
                  pl.BlockSpec((tr, d), lambda i: (i, 0))],
        out_specs=(pl.BlockSpec((1, 1, LANES), lambda i: (i, 0, 0)), pl.BlockSpec((tr, d), lambda i: (i, 0)),
                   pl.BlockSpec((1, d), lambda i: (0, 0))),
        compiler_params=pltpu.CompilerParams(dimension_semantics=("arbitrary",), vmem_limit_bytes=VMEM_LIMIT),
    )(x, g.reshape(1, d), target)


_NT = (((1,), (1,)), ((), ()))
_NN = (((1,), (0,)), ((), ()))
_TN = (((0,), (0,)), ((), ()))


def _att_plan(lq, lk, hq, group):
    hp = min(ATT_HEADS_PER_STEP, hq)
    assert hq % hp == 0 and (hp % group == 0 or group % hp == 0)
    kvp = max(1, hp // group)
    return _pick(lq, (512, 256, 128)), _pick(lk, (1024, 512, 256, 128)), hp, kvp


def _att_specs(bq, bk, dh, hp, kvp, group, q_map, k_map):
    qs = pl.BlockSpec((bq, hp * dh), lambda *g: q_map(*g))
    ls = pl.BlockSpec((bq, hp * LANES), lambda *g: q_map(*g))
    ks = pl.BlockSpec((bk, kvp * dh), lambda *g: (k_map(*g)[0], k_map(*g)[1] * hp // (group * kvp)))
    return qs, ls, ks


def _flash_fwd(q, k, v, dh, group, scale):
    lq, lk = q.shape[0], k.shape[0]
    hq = q.shape[1] // dh
    bq, bk, hp, kvp = _att_plan(lq, lk, hq, group)
    nkv = lk // bk

    def body(q_ref, k_ref, v_ref, o_ref, lse_ref, *scratch):
        m_s, l_s, acc_s = scratch[:hp], scratch[hp:2 * hp], scratch[2 * hp:]
        j = pl.program_id(2)

        @pl.when(j == 0)
        def _():
            for h in range(hp):
                m_s[h][...] = jnp.full_like(m_s[h], -jnp.inf)
                l_s[h][...] = jnp.zeros_like(l_s[h])
                acc_s[h][...] = jnp.zeros_like(acc_s[h])

        for h in range(hp):
            hk = h // group if kvp > 1 else 0
            cols, kcols = slice(h * dh, (h + 1) * dh), slice(hk * dh, (hk + 1) * dh)
            s = lax.dot_general(q_ref[:, cols], k_ref[:, kcols], _NT, preferred_element_type=F32)
            if scale != 1.0:
                s = s * scale
            m_prev = m_s[h][...]
            m_cur = jnp.maximum(m_prev, jnp.max(s, axis=1, keepdims=True))
            alpha = jnp.exp(m_prev - m_cur)
            p = jnp.exp(s - m_cur[:, :1])
            l_s[h][...] = alpha * l_s[h][...] + jnp.sum(p, axis=1, keepdims=True)
            acc_s[h][...] = acc_s[h][...] * alpha[:, :1] + lax.dot_general(p.astype(BF16), v_ref[:, kcols], _NN,
                                                                         preferred_element_type=F32)
            m_s[h][...] = m_cur

        @pl.when(j == nkv - 1)
        def _():
            for h in range(hp):
                cols, lcols = slice(h * dh, (h + 1) * dh), slice(h * LANES, (h + 1) * LANES)
                o_ref[:, cols] = acc_s[h][...] / l_s[h][...][:, :1]
                lse_ref[:, lcols] = m_s[h][...] + jnp.log(l_s[h][...])

    qs, ls, ks = _att_specs(bq, bk, dh, hp, kvp, group, lambda h, i, j: (i, h), lambda h, i, j: (j, h))
    return pl.pallas_call(
        body, name="flash_fwd",
        out_shape=(jax.ShapeDtypeStruct((lq, hq * dh), F32), jax.ShapeDtypeStruct((lq, hq * LANES), F32)),
        grid=(hq // hp, lq // bq, nkv), in_specs=[qs, ks, ks], out_specs=(qs, ls),
        scratch_shapes=[pltpu.VMEM((bq, LANES), F32)] * (2 * hp) + [pltpu.VMEM((bq, dh), F32)] * hp,
        compiler_params=pltpu.CompilerParams(dimension_semantics=("parallel", "parallel", "arbitrary"),
                                             vmem_limit_bytes=VMEM_LIMIT),
    )(q, k, v)


def _flash_dq(q, k, v, o, lse, do, dh, group, scale):
    lq, lk = q.shape[0], k.shape[0]
    hq = q.shape[1] // dh
    bq, bk, hp, kvp = _att_plan(lq, lk, hq, group)
    nkv = lk // bk

    def body(q_ref, k_ref, v_ref, o_ref, lse_ref, do_ref, dq_ref, dob_ref, dl_ref, *acc_s):
        j = pl.program_id(2)

        @pl.when(j == 0)
        def _():
            for h in range(hp):
                acc_s[h][...] = jnp.zeros_like(acc_s[h])
            dob_ref[...] = do_ref[...].astype(BF16)
            for h in range(hp):
                cols, lcols = slice(h * dh, (h + 1) * dh), slice(h * LANES, (h + 1) * LANES)
                dl_ref[:, lcols] = jnp.broadcast_to(jnp.sum(do_ref[:, cols] * o_ref[:, cols], axis=1, keepdims=True),
                                                    (bq, LANES))

        for h in range(hp):
            hk = h // group if kvp > 1 else 0
            cols, lcols, kcols = slice(h * dh, (h + 1) * dh), slice(h * LANES, (h + 1) * LANES), slice(hk * dh, (hk + 1) * dh)
            kb = k_ref[:, kcols]
            s = lax.dot_general(q_ref[:, cols], kb, _NT, preferred_element_type=F32)
            if scale != 1.0:
                s = s * scale
            p = jnp.exp(s - lse_ref[:, lcols][:, :1])
            dp = lax.dot_general(dob_ref[:, cols], v_ref[:, kcols], _NT, preferred_element_type=F32)
            ds = p * (dp - dl_ref[:, lcols][:, :1])
            if scale != 1.0:
                ds = ds * scale
            acc_s[h][...] += lax.dot_general(ds.astype(BF16), kb, _NN, preferred_element_type=F32)

        @pl.when(j == nkv - 1)
        def _():
            for h in range(hp):
                dq_ref[:, h * dh:(h + 1) * dh] = acc_s[h][...]

    qs, ls, ks = _att_specs(bq, bk, dh, hp, kvp, group, lambda h, i, j: (i, h), lambda h, i, j: (j, h))
    return pl.pallas_call(
        body, name="flash_dq",
        out_shape=(jax.ShapeDtypeStruct((lq, hq * dh), F32), jax.ShapeDtypeStruct((lq, hq * dh), BF16),
                   jax.ShapeDtypeStruct((lq, hq * LANES), F32)),
        grid=(hq // hp, lq // bq, nkv), in_specs=[qs, ks, ks, qs, ls, qs], out_specs=(qs, qs, ls),
        scratch_shapes=[pltpu.VMEM((bq, dh), F32)] * hp,
        compiler_params=pltpu.CompilerParams(dimension_semantics=("parallel", "parallel", "arbitrary"),
                                             vmem_limit_bytes=VMEM_LIMIT),
    )(q, k, v, o, lse, do)


def _flash_dkv(q, k, v, lse, dob, delta, dh, group, scale):
    lq, lk = q.shape[0], k.shape[0]
    hq = q.shape[1] // dh
    bq, bk, hp, kvp = _att_plan(lq, lk, hq, group)
    nq = lq // bq
    reps = max(1, group // hp)
    nt = reps * nq

    def body(q_ref, k_ref, v_ref, lse_ref, dob_ref, dl_ref, dk_ref, dv_ref, dk_s, dv_s):
        t = pl.program_id(2)

        @pl.when(t == 0)
        def _():
            dk_s[...] = jnp.zeros_like(dk_s)
            dv_s[...] = jnp.zeros_like(dv_s)

        for h in range(hp):
            hk = h // group if kvp > 1 else 0
            cols, lcols, kcols = slice(h * dh, (h + 1) * dh), slice(h * LANES, (h + 1) * LANES), slice(hk * dh, (hk + 1) * dh)
            qb = q_ref[:, cols]
            dob = dob_ref[:, cols]
            s = lax.dot_general(qb, k_ref[:, kcols], _NT, preferred_element_type=F32)
            if scale != 1.0:
                s = s * scale
            p = jnp.exp(s - lse_ref[:, lcols][:, :1])
            dp = lax.dot_general(dob, v_ref[:, kcols], _NT, preferred_element_type=F32)
            ds = p * (dp - dl_ref[:, lcols][:, :1])
            if scale != 1.0:
                ds = ds * scale
            dv_s[:, kcols] += lax.dot_general(p.astype(BF16), dob, _TN, preferred_element_type=F32)
            dk_s[:, kcols] += lax.dot_general(ds.astype(BF16), qb, _TN, preferred_element_type=F32)

        @pl.when(t == nt - 1)
        def _():
            dk_ref[...] = dk_s[...]
            dv_ref[...] = dv_s[...]

    hkv_blocks = (hq // group) // kvp
    qs = pl.BlockSpec((bq, hp * dh), lambda h, j, t: (t % nq, h * reps + t // nq))
    ls = pl.BlockSpec((bq, hp * LANES), lambda h, j, t: (t % nq, h * reps + t // nq))
    ks = pl.BlockSpec((bk, kvp * dh), lambda h, j, t: (j, h))
    return pl.pallas_call(
        body, name="flash_dkv",
        out_shape=(jax.ShapeDtypeStruct(k.shape, F32), jax.ShapeDtypeStruct(v.shape, F32)),
        grid=(hkv_blocks, lk // bk, nt), in_specs=[qs, ks, ks, ls, qs, ls], out_specs=(ks, ks),
        scratch_shapes=[pltpu.VMEM((bk, kvp * dh), F32), pltpu.VMEM((bk, kvp * dh), F32)],
        compiler_params=pltpu.CompilerParams(dimension_semantics=("parallel", "parallel", "arbitrary"),
                                             vmem_limit_bytes=VMEM_LIMIT),
    )(q, k, v, lse, dob, delta)


@functools.partial(jax.custom_vjp, nondiff_argnums=(3, 4, 5))
def flash(q, k, v, dh, group, scale):
    return _flash_fwd(q.astype(BF16), k.astype(BF16), v.astype(BF16), dh, group, scale)[0]


def _flash_vfwd(q, k, v, dh, group, scale):
    qb, kb, vb = q.astype(BF16), k.astype(BF16), v.astype(BF16)
    o, lse = _flash_fwd(qb, kb, vb, dh, group, scale)
    return o, (qb, kb, vb, o, lse)


def _flash_vbwd(dh, group, scale, res, do):
    q, k, v, o, lse = res
    dq, dob, delta = _flash_dq(q, k, v, o, lse, do, dh, group, scale)
    dk, dv = _flash_dkv(q, k, v, lse, dob, delta, dh, group, scale)
    return dq, dk, dv


flash.defvjp(_flash_vfwd, _flash_vbwd)


def _scan_call(lr, li, br, bi, prev_r, prev_i, reverse, name):
    rows, width = br.shape
    ts = _pick(width, (1024, 512, 256, 128))
    tr = _pick(rows, (256, 128, 64, 32, 16, 8))
    nb = rows // tr
    with_prev = prev_r is not None

    def body(*refs):
        if with_prev:
            lr_ref, li_ref, br_ref, bi_ref, pr_ref, pi_ref, sr_ref, si_ref, dr_ref, di_ref, cr_s, ci_s = refs
        else:
            lr_ref, li_ref, br_ref, bi_ref, sr_ref, si_ref, pr_ref, pi_ref, cr_s, ci_s = refs
        i = pl.program_id(1)

        @pl.when(i == 0)
        def _():
            cr_s[...] = jnp.zeros_like(cr_s)
            ci_s[...] = jnp.zeros_like(ci_s)
            if with_prev:
                dr_ref[...] = jnp.zeros_like(dr_ref)
                di_ref[...] = jnp.zeros_like(di_ref)

        a_re = lr_ref[...]
        a_im = li_ref[...]

        def step(r, carry):
            if with_prev:
                c_re, c_im, d_re, d_im = carry
            else:
                c_re, c_im = carry
            t = tr - 1 - r if reverse else r
            row = pl.ds(t, 1)
            n_re = a_re * c_re - a_im * c_im + br_ref[row, :]
            n_im = a_re * c_im + a_im * c_re + bi_ref[row, :]
            sr_ref[row, :] = n_re
            si_ref[row, :] = n_im
            if with_prev:
                p_re = pr_ref[row, :]
                p_im = pi_ref[row, :]
                return n_re, n_im, d_re + n_re * p_re + n_im * p_im, d_im + n_im * p_re - n_re * p_im
            pr_ref[row, :] = c_re
            pi_ref[row, :] = c_im
            return n_re, n_im

        zero = jnp.zeros((1, ts), F32)
        init = (cr_s[...], ci_s[...]) + ((zero, zero) if with_prev else ())
        out = lax.fori_loop(0, tr, step, init, unroll=8)
        cr_s[...] = out[0]
        ci_s[...] = out[1]
        if with_prev:
            dr_ref[...] += out[2]
            di_ref[...] += out[3]

    blk = (lambda j, i: (nb - 1 - i, j)) if reverse else (lambda j, i: (i, j))
    row_spec = pl.BlockSpec((tr, ts), blk)
    vec_spec = pl.BlockSpec((1, ts), lambda j, i: (0, j))
    full = jax.ShapeDtypeStruct((rows, width), F32)
    vec = jax.ShapeDtypeStruct((1, width), F32)
    ins = [lr, li, br, bi] + ([prev_r, prev_i] if with_prev else [])
    return pl.pallas_call(
        body, name=name, grid=(width // ts, nb),
        in_specs=[vec_spec, vec_spec] + [row_spec] * (len(ins) - 2),
        out_shape=(full, full, vec, vec) if with_prev else (full, full, full, full),
        out_specs=(row_spec, row_spec, vec_spec, vec_spec) if with_prev else (row_spec,) * 4,
        scratch_shapes=[pltpu.VMEM((1, ts), F32), pltpu.VMEM((1, ts), F32)],
        compiler_params=pltpu.CompilerParams(dimension_semantics=("parallel", "arbitrary"), vmem_limit_bytes=VMEM_LIMIT),
    )(*ins)


@functools.partial(jax.custom_vjp, nondiff_argnums=(4,))
def cscan(lr, li, br, bi, reverse):
    out = _scan_call(lr, li, br, bi, None, None, reverse, "s5_scan")
    return out[0], out[1]


def _cscan_fwd(lr, li, br, bi, reverse):
    sr, si, pr, pi = _scan_call(lr, li, br, bi, None, None, reverse, "s5_scan")
    return (sr, si), (lr, li, pr, pi)


def _cscan_bwd(reverse, res, g):
    lr, li, pr, pi = res
    ar, ai, dlr, dli = _scan_call(lr, -li, g[0], g[1], pr, pi, not reverse, "s5_scan_bwd")
    return dlr, dli, ar, ai


cscan.defvjp(_cscan_fwd, _cscan_bwd)


_BDIMS = {"nn": (((2,), (1,)), ((0,), (0,))), "nt": (((2,), (2,)), ((0,), (0,))), "tn": (((1,), (1,)), ((0,), (0,)))}


def _bdot_raw(a, b, mode):
    return lax.dot_general(a.astype(BF16), b.astype(BF16), _BDIMS[mode], preferred_element_type=F32)


@functools.partial(jax.custom_vjp, nondiff_argnums=(2,))
def _bdot(a, b, mode):
    return _bdot_raw(a, b, mode)


def _bdot_fwd(a, b, mode):
    return _bdot_raw(a, b, mode), (a, b)


def _bdot_bwd(mode, res, dc):
    a, b = res
    if mode == "nn":
        return _bdot_raw(dc, b, "nt"), _bdot_raw(a, dc, "tn")
    if mode == "nt":
        return _bdot_raw(dc, b, "nn"), _bdot_raw(dc, a, "tn")
    return _bdot_raw(b, dc, "nt"), _bdot_raw(a, dc, "nn")


_bdot.defvjp(_bdot_fwd, _bdot_bwd)


def _split(a):
    hi = a.astype(BF16)
    return hi, (a - hi.astype(F32)).astype(BF16)


def _fdot_raw(a, b, mode):
    a_hi, a_lo = _split(a)
    b_hi, b_lo = _split(b)
    dot = lambda u, v: lax.dot_general(u, v, _BDIMS[mode], preferred_element_type=F32)
    return dot(a_hi, b_hi) + (dot(a_hi, b_lo) + dot(a_lo, b_hi))


@jax.custom_vjp
def _fdot(a, b):
    return _fdot_raw(a, b, "nn")


def _fdot_fwd(a, b):
    return _fdot_raw(a, b, "nn"), (a, b)


def _fdot_bwd(res, dc):
    a, b = res
    return _fdot_raw(dc, b, "nt"), _fdot_raw(a, dc, "tn")


_fdot.defvjp(_fdot_fwd, _fdot_bwd)


def _dn_chunk(state, q, k, v, bb, gc, gc64, gr, direction):
    c = q.shape[1]
    row = lax.broadcasted_iota(jnp.int32, (c, c), 0)
    col = lax.broadcasted_iota(jnp.int32, (c, c), 1)
    sgn = 1 - 2 * direction
    after = ((row - col) * sgn > 0)[None]
    incl = ((row - col) * sgn >= 0)[None]
    eye = (row == col).astype(F32)[None]
    kb = k * bb
    decay = jnp.where(incl, jnp.exp(jnp.where(incl, gc64 - gr, 0.0)), 0.0)
    lower = jnp.where(after, _bdot(kb, k, "nt") * decay, 0.0)
    inv = eye - lower
    power = _fdot(lower, lower)
    span = 2
    while span < c:
        inv = inv + _fdot(inv, power)
        span *= 2
        if span < c:
            power = _fdot(power, power)
    eg = jnp.exp(gc)
    u = _fdot(inv, v * bb)
    w = _fdot(inv, kb * eg)
    intra = _bdot(q, k, "nt") * decay
    last = (lax.broadcasted_iota(jnp.int32, (c, 1), 0) == (c - 1) * (1 - direction)).astype(F32)[None]
    g_last = jnp.sum(gc * last, axis=1, keepdims=True)
    k_dec = k * jnp.exp(g_last - gc)
    v_new = u - _bdot(w, state, "nn")
    o = _bdot(q * eg, state, "nn") + _bdot(intra, v_new, "nn")
    new_state = state * jnp.exp(g_last) + _bdot(k_dec, v_new, "tn")
    return new_state, o


def _dn_rows(lq):
    return _pick(lq, (128, 64))


def _dn_prep(q, k, v):
    q, k, v = jax.nn.silu(q), jax.nn.silu(k), jax.nn.silu(v)
    q = q * lax.rsqrt(jnp.sum(q * q, axis=2, keepdims=True) + EPS) * (q.shape[2] ** -0.5)
    k = k * lax.rsqrt(jnp.sum(k * k, axis=2, keepdims=True) + EPS)
    return q, k, v


def _dn_step(state, q, k, v, bb, gc, gc64, gr, direction):
    q, k, v = _dn_prep(q, k, v)
    return _dn_chunk(state, q, k, v, bb, gc, gc64, gr, direction)


def _dn_heads(ref, rows, part, h, d):
    return jnp.stack([ref[rows, (part * h + i) * d:(part * h + i + 1) * d] for i in range(h)])


def _dn_specs(h, t, d, c, nc, blk):
    return [pl.BlockSpec((h, t, d), lambda i: (0, blk(i), 0)), pl.BlockSpec((h, t, d), lambda i: (0, blk(i), 0)),
            pl.BlockSpec((h, t, c), lambda i: (0, blk(i), 0)), pl.BlockSpec((h, nc, 1, c), lambda i: (0, blk(i), 0, 0))]


def _dn_fwd_call(qkv, dirs):
    h, lq, d = dirs[0][0].shape
    c = DN_CHUNK
    t = _dn_rows(lq)
    nb, nc = lq // t, t // c
    blks = [lambda i: i, lambda i: nb - 1 - i]

    def body(*refs):
        qkv_refs = refs[0:2]
        op_refs = [refs[2:6], refs[6:10]]
        o_refs, st_refs = refs[10:12], refs[12:14]
        s_refs = refs[14:16]

        @pl.when(pl.program_id(0) == 0)
        def _():
            for s in s_refs:
                s[...] = jnp.zeros_like(s)

        def step(j, states):
            out = []
            for direction in range(2):
                ci = j if direction == 0 else nc - 1 - j
                rows = pl.ds(pl.multiple_of(ci * c, c), c)
                bb_ref, gc_ref, g64_ref, gr_ref = op_refs[direction]
                st_refs[direction][ci] = states[direction]
                state, o = _dn_step(states[direction], _dn_heads(qkv_refs[direction], rows, 0, h, d),
                                    _dn_heads(qkv_refs[direction], rows, 1, h, d), _dn_heads(qkv_refs[direction], rows, 2, h, d),
                                    bb_ref[:, rows, :], gc_ref[:, rows, :], g64_ref[:, rows, :], gr_ref[:, ci], direction)
                for i in range(h):
                    o_refs[direction][rows, i * d:(i + 1) * d] = o[i]
                out.append(state)
            return tuple(out)

        final = lax.fori_loop(0, nc, step, (s_refs[0][...], s_refs[1][...]))
        s_refs[0][...] = final[0]
        s_refs[1][...] = final[1]

    in_specs = [pl.BlockSpec((t, 3 * h * d), lambda i, b=b: (b(i), 0)) for b in blks]
    for b in blks:
        in_specs += _dn_specs(h, t, d, c, nc, b)
    out_specs = [pl.BlockSpec((t, h * d), lambda i, b=b: (b(i), 0)) for b in blks]
    out_specs += [pl.BlockSpec((nc, h, d, d), lambda i, b=b: (b(i), 0, 0, 0)) for b in blks]
    o_shape = jax.ShapeDtypeStruct((lq, h * d), F32)
    st_shape = jax.ShapeDtypeStruct((lq // c, h, d, d), F32)
    res = pl.pallas_call(
        body, name="deltanet_fwd", grid=(nb,), in_specs=in_specs, out_shape=(o_shape, o_shape, st_shape, st_shape),
        out_specs=tuple(out_specs), scratch_shapes=[pltpu.VMEM((h, d, d), F32)] * 2,
        compiler_params=pltpu.CompilerParams(dimension_semantics=("arbitrary",), vmem_limit_bytes=VMEM_LIMIT),
    )(qkv, qkv, *dirs[0], *dirs[1])
    return res[0:2], res[2:4]


def _dn_bwd_call(qkv, dirs, states, do):
    h, lq, d = dirs[0][0].shape
    c = DN_CHUNK
    t = _dn_rows(lq)
    nb, nc = lq // t, t // c
    blks = [lambda i: nb - 1 - i, lambda i: i]

    def body(*refs):
        qkv_refs = refs[0:2]
        op_refs = [refs[2:6], refs[6:10]]
        st_refs, do_refs = refs[10:12], refs[12:14]
        out_refs = [refs[14:19], refs[19:24]]
        ds_refs = refs[24:26]

        @pl.when(pl.program_id(0) == 0)
        def _():
            for s in ds_refs:
                s[...] = jnp.zeros_like(s)

        def step(j, dstates):
            out = []
            for direction in range(2):
                ci = nc - 1 - j if direction == 0 else j
                rows = pl.ds(pl.multiple_of(ci * c, c), c)
                bb_ref, gc_ref, g64_ref, gr_ref = op_refs[direction]
                args = (st_refs[direction][ci], _dn_heads(qkv_refs[direction], rows, 0, h, d),
                        _dn_heads(qkv_refs[direction], rows, 1, h, d), _dn_heads(qkv_refs[direction], rows, 2, h, d),
                        bb_ref[:, rows, :], gc_ref[:, rows, :], g64_ref[:, rows, :], gr_ref[:, ci])
                _, pull = jax.vjp(lambda *a, direction=direction: _dn_step(*a, direction), *args)
                do_c = jnp.stack([do_refs[direction][rows, i * d:(i + 1) * d] for i in range(h)])
                dstate, dq, dk, dv, dbb, dgc, dg64, dgr = pull((dstates[direction], do_c))
                dqkv_ref, dbb_ref, dgc_ref, dg64_ref, dgr_ref = out_refs[direction]
                for part, val in enumerate((dq, dk, dv)):
                    for i in range(h):
                        dqkv_ref[rows, (part * h + i) * d:(part * h + i + 1) * d] = val[i]
                dbb_ref[:, rows, :] = dbb
                dgc_ref[:, rows, :] = dgc
                dg64_ref[:, rows, :] = dg64
                dgr_ref[:, ci] = dgr
                out.append(dstate)
            return tuple(out)

        final = lax.fori_loop(0, nc, step, (ds_refs[0][...], ds_refs[1][...]))
        ds_refs[0][...] = final[0]
        ds_refs[1][...] = final[1]

    in_specs = [pl.BlockSpec((t, 3 * h * d), lambda i, b=b: (b(i), 0)) for b in blks]
    for b in blks:
        in_specs += _dn_specs(h, t, d, c, nc, b)
    in_specs += [pl.BlockSpec((nc, h, d, d), lambda i, b=b: (b(i), 0, 0, 0)) for b in blks]
    in_specs += [pl.BlockSpec((t, h * d), lambda i, b=b: (b(i), 0)) for b in blks]
    out_specs, out_shape = [], []
    for b in blks:
        out_specs += [pl.BlockSpec((t, 3 * h * d), lambda i, b=b: (b(i), 0))] + _dn_specs(h, t, d, c, nc, b)
        out_shape += [jax.ShapeDtypeStruct((lq, 3 * h * d), F32), jax.ShapeDtypeStruct((h, lq, d), F32),
                      jax.ShapeDtypeStruct((h, lq, d), F32), jax.ShapeDtypeStruct((h, lq, c), F32),
                      jax.ShapeDtypeStruct((h, lq // c, 1, c), F32)]
    res = pl.pallas_call(
        body, name="deltanet_bwd", grid=(nb,), in_specs=in_specs, out_shape=tuple(out_shape), out_specs=tuple(out_specs),
        scratch_shapes=[pltpu.VMEM((h, d, d), F32)] * 2,
        compiler_params=pltpu.CompilerParams(dimension_semantics=("arbitrary",), vmem_limit_bytes=VMEM_LIMIT),
    )(qkv, qkv, *dirs[0], *dirs[1], *states, do, do)
    return res[0:5], res[5:10]


@jax.custom_vjp
def dn_rule(qkv, dir0, dir1):
    o, _ = _dn_fwd_call(qkv, (dir0, dir1))
    return o[0] + o[1]


def _dn_rule_fwd(qkv, dir0, dir1):
    o, states = _dn_fwd_call(qkv, (dir0, dir1))
    return o[0] + o[1], (qkv, dir0, dir1, states)


def _dn_rule_bwd(res, do):
    qkv, dir0, dir1, states = res
    g0, g1 = _dn_bwd_call(qkv, (dir0, dir1), states, do)
    return g0[0] + g1[0], tuple(g0[1:]), tuple(g1[1:])


dn_rule.defvjp(_dn_rule_fwd, _dn_rule_bwd)


def gated_delta_both(qkv, beta2, g2):
    _, h, lq = beta2.shape
    d = qkv.shape[1] // (3 * h)
    c = DN_CHUNK
    dirs = []
    for direction in range(2):
        cum = lax.cumsum(g2[direction].reshape(h, lq // c, c), axis=2, reverse=direction == 1)
        flat = cum.reshape(h, lq)
        dirs.append((jnp.broadcast_to(beta2[direction][..., None], (h, lq, d)), jnp.broadcast_to(flat[..., None], (h, lq, d)),
                     jnp.broadcast_to(flat[..., None], (h, lq, c)), cum.reshape(h, lq // c, 1, c)))
    return dn_rule(qkv, dirs[0], dirs[1])


def _merge_rows(rows, d, nb):
    return _row_tile(rows, 4 * nb * d, 2 * 1024 * 1024)


def _merge_fwd_call(logits, ts):
    rows, d = ts[0].shape
    nb = len(ts)
    tr = _merge_rows(rows, d, nb)

    def body(gl_ref, *refs):
        t_refs, o_ref = refs[:nb], refs[nb]
        acc = jax.nn.sigmoid(gl_ref[:, 0:d]) * t_refs[0][...]
        for b in range(1, nb):
            acc = acc + jax.nn.sigmoid(gl_ref[:, b * d:(b + 1) * d]) * t_refs[b][...]
        o_ref[...] = acc

    wide = pl.BlockSpec((tr, nb * d), lambda i: (i, 0))
    one = pl.BlockSpec((tr, d), lambda i: (i, 0))
    return pl.pallas_call(
        body, name="gate_merge", out_shape=jax.ShapeDtypeStruct((rows, d), F32), grid=(rows // tr,),
        in_specs=[wide] + [one] * nb, out_specs=one,
        compiler_params=pltpu.CompilerParams(dimension_semantics=("parallel",), vmem_limit_bytes=VMEM_LIMIT),
    )(logits, *ts)


def _merge_bwd_call(logits, ts, dm):
    rows, d = ts[0].shape
    nb = len(ts)
    tr = _merge_rows(rows, d, nb)

    def body(gl_ref, *refs):
        t_refs, dm_ref, dgl_ref, dt_refs = refs[:nb], refs[nb], refs[nb + 1], refs[nb + 2:]
        dmv = dm_ref[...]
        for b in range(nb):
            s = jax.nn.sigmoid(gl_ref[:, b * d:(b + 1) * d])
            dt_refs[b][...] = dmv * s
            dgl_ref[:, b * d:(b + 1) * d] = dmv * t_refs[b][...] * (s * (1.0 - s))

    wide = pl.BlockSpec((tr, nb * d), lambda i: (i, 0))
    one = pl.BlockSpec((tr, d), lambda i: (i, 0))
    out = jax.ShapeDtypeStruct((rows, d), F32)
    return pl.pallas_call(
        body, name="gate_merge_bwd", out_shape=(jax.ShapeDtypeStruct(logits.shape, F32),) + (out,) * nb, grid=(rows // tr,),
        in_specs=[wide] + [one] * (nb + 1), out_specs=(wide,) + (one,) * nb,
        compiler_params=pltpu.CompilerParams(dimension_semantics=("parallel",), vmem_limit_bytes=VMEM_LIMIT),
    )(logits, *ts, dm)


@jax.custom_vjp
def gate_merge(logits, *ts):
    return _merge_fwd_call(logits, ts)


def _gate_merge_fwd(logits, *ts):
    return _merge_fwd_call(logits, ts), (logits, ts)


def _gate_merge_bwd(res, dm):
    logits, ts = res
    return tuple(_merge_bwd_call(logits, ts, dm))


gate_merge.defvjp(_gate_merge_fwd, _gate_merge_bwd)


def _conv_rows(rows):
    return _pick(rows, (256, 128, 64, 32, 16, 8))


def _conv_taps(x_ref, prev_ref, next_ref, i, nb, t):
    keep_prev = (i > 0).astype(F32)
    keep_next = (i < nb - 1).astype(F32)
    ext = jnp.concatenate([prev_ref[...] * keep_prev, x_ref[...], next_ref[...] * keep_next], axis=0)
    total = t + 2 * CONV_HALO
    return [pltpu.roll(ext, (DN_CONV // 2 - j) % total, 0)[CONV_HALO:CONV_HALO + t] for j in range(DN_CONV)]


def _conv_specs(t, ch, rows):
    per = t // CONV_HALO
    last = rows // CONV_HALO - 1
    return [pl.BlockSpec((t, ch), lambda i: (i, 0)),
            pl.BlockSpec((CONV_HALO, ch), lambda i: (jnp.maximum(i * per - 1, 0), 0)),
            pl.BlockSpec((CONV_HALO, ch), lambda i: (jnp.minimum((i + 1) * per, last), 0))]


def _conv_call(x, taps):
    rows, ch = x.shape
    t = _conv_rows(rows)
    nb = rows // t

    def body(x_ref, prev_ref, next_ref, w_ref, o_ref):
        views = _conv_taps(x_ref, prev_ref, next_ref, pl.program_id(0), nb, t)
        acc = views[0] * w_ref[0:1, :]
        for j in range(1, DN_CONV):
            acc = acc + views[j] * w_ref[j:j + 1, :]
        o_ref[...] = acc

    return pl.pallas_call(
        body, name="short_conv", out_shape=jax.ShapeDtypeStruct((rows, ch), F32), grid=(nb,),
        in_specs=_conv_specs(t, ch, rows) + [pl.BlockSpec((8, ch), lambda i: (0, 0))],
        out_specs=pl.BlockSpec((t, ch), lambda i: (i, 0)),
        compiler_params=pltpu.CompilerParams(dimension_semantics=("parallel",), vmem_limit_bytes=VMEM_LIMIT),
    )(x, x, x, taps)


def _conv_dw_call(x, dy):
    rows, ch = x.shape
    t = _conv_rows(rows)
    nb = rows // t

    def body(x_ref, prev_ref, next_ref, dy_ref, o_ref):
        @pl.when(pl.program_id(0) == 0)
        def _():
            o_ref[...] = jnp.zeros_like(o_ref)

        views = _conv_taps(x_ref, prev_ref, next_ref, pl.program_id(0), nb, t)
        dyv = dy_ref[...]
        for j in range(DN_CONV):
            o_ref[j:j + 1, :] += jnp.sum(views[j] * dyv, axis=0, keepdims=True)

    return pl.pallas_call(
        body, name="short_conv_dw", out_shape=jax.ShapeDtypeStruct((8, ch), F32), grid=(nb,),
        in_specs=_conv_specs(t, ch, rows) + [pl.BlockSpec((t, ch), lambda i: (i, 0))],
        out_specs=pl.BlockSpec((8, ch), lambda i: (0, 0)),
        compiler_params=pltpu.CompilerParams(dimension_semantics=("arbitrary",), vmem_limit_bytes=VMEM_LIMIT),
    )(x, x, x, dy)


def _taps(w, flip):
    wt = jnp.transpose(w)
    if flip:
        wt = wt[::-1]
    return jnp.pad(wt, ((0, 8 - DN_CONV), (0, 0)))


@jax.custom_vjp
def short_conv(x, w):
    return _conv_call(x, _taps(w, False))


def _short_conv_fwd(x, w):
    return _conv_call(x, _taps(w, False)), (x, w)


def _short_conv_bwd(res, dy):
    x, w = res
    return _conv_call(dy, _taps(w, True)), jnp.transpose(_conv_dw_call(x, dy)[:DN_CONV])


short_conv.defvjp(_short_conv_fwd, _short_conv_bwd)


def _s5_mixer(u, a_re, a_im, log_step, b_re, b_im, c_re, c_im, d, w_glu, b_glu):
    per = LANES // SSM_GROUP
    y = u * d
    for direction in range(2):
        are, aim = a_re[direction], a_im[direction]
        step = jnp.exp(log_step[direction])[:, None]
        mag = jnp.exp(are * step)
        lam_re = mag * jnp.cos(aim * step)
        lam_im = mag * jnp.sin(aim * step)
        den = are * are + aim * aim
        nr = lam_re - 1.0
        ni = lam_im
        coef_re = (nr * are + ni * aim) / den
        coef_im = (ni * are - nr * aim) / den
        bb_re = coef_re[..., None] * b_re[direction] - coef_im[..., None] * b_im[direction]
        bb_im = coef_re[..., None] * b_im[direction] + coef_im[..., None] * b_re[direction]
        bu_re = cmm(u, _cluster_diag(jnp.swapaxes(bb_re, 1, 2), per))
        bu_im = cmm(u, _cluster_diag(jnp.swapaxes(bb_im, 1, 2), per))
        width = bu_re.shape[1]
        s_re, s_im = cscan(lam_re.reshape(1, width), lam_im.reshape(1, width), bu_re, bu_im, direction == 1)
        y = y + cmm(s_re, _cluster_diag(jnp.swapaxes(c_re[direction], 1, 2), per))
        y = y - cmm(s_im, _cluster_diag(jnp.swapaxes(c_im[direction], 1, 2), per))
    y = jax.nn.gelu(y)
    return y * jax.nn.sigmoid(pmm(y, w_glu) + b_glu)


def _deltanet_mixer(q, k, v, ab, conv_w, a_log, dt_bias, norm_g):
    seq = q.shape[0]
    qkv = short_conv(jnp.concatenate([q, k, v], axis=-1), conv_w)
    a4 = ab[:, :2 * DN_HEADS].reshape(seq, 2, DN_HEADS)
    b4 = ab[:, 2 * DN_HEADS:4 * DN_HEADS].reshape(seq, 2, DN_HEADS)
    beta = jax.nn.sigmoid(b4)
    g = -jnp.exp(a_log) * jax.nn.softplus(a4 + dt_bias)
    o = gated_delta_both(qkv, jnp.transpose(beta, (1, 2, 0)), jnp.transpose(g, (1, 2, 0)))
    return head_norm(o, norm_g)


def _axial_rope(seq):
    rows = seq // GRID_W
    row = jnp.repeat(jnp.arange(rows), GRID_W).astype(F32)
    col = jnp.tile(jnp.arange(GRID_W), rows).astype(F32)
    axis_dim = ATT_HEAD_DIM // 2
    freqs = ROPE_THETA ** (-jnp.arange(0, axis_dim, 2, dtype=F32) / axis_dim)
    ang = jnp.concatenate([row[:, None] * freqs, col[:, None] * freqs], axis=-1)
    sign = jnp.tile(jnp.array([-1.0, 1.0], F32), ATT_HEAD_DIM // 2)
    return jnp.repeat(jnp.cos(ang), 2, axis=-1), jnp.repeat(jnp.sin(ang), 2, axis=-1) * sign


def _grid_attention(q, k, v, qn_g, kn_g, cos, sin):
    qh = head_norm_rope(q, qn_g, cos, sin, ATT_HEAD_DIM ** -0.5)
    kh = head_norm_rope(k, kn_g, cos, sin, 1.0)
    return flash(qh, kh, v, ATT_HEAD_DIM, ATT_HEADS // ATT_KV_HEADS, 1.0)


def _memory_attention(q, mem_n, w_kv):
    kv = pmm(mem_n, w_kv)
    mem_w = MEM_HEADS * MEM_HEAD_DIM
    return flash(q, kv[:, :mem_w], kv[:, mem_w:], MEM_HEAD_DIM, 1, MEM_HEAD_DIM ** -0.5)


def _middle(h, x, mem, p):
    d_model = x.shape[1]
    lay, _, _ = _layout(d_model)

    def seg(name):
        _, dst, w, _ = lay[name]
        return h[:, dst:dst + w]

    cos, sin = _axial_rope(x.shape[0])
    y_a = _s5_mixer(seg("u"), p["ssm_a_re"], p["ssm_a_im"], p["ssm_log_step"], p["ssm_b_re"], p["ssm_b_im"],
                    p["ssm_c_re"], p["ssm_c_im"], p["ssm_d"], p["ssm_w_glu"], p["ssm_b_glu"]) * jax.nn.silu(seg("z_a"))
    y_b = _deltanet_mixer(seg("dq"), seg("dk"), seg("dv"), seg("dab"), p["dn_conv"], p["dn_a_log"],
                          p["dn_dt_bias"], p["dn_norm_g"]) * jax.nn.silu(seg("z_b"))
    y_c = _grid_attention(seg("aq"), seg("ak"), seg("av"), p["attn_q_norm"], p["attn_k_norm"], cos, sin) * jax.nn.silu(seg("z_c"))
    y_m = _memory_attention(seg("mq"), prms(mem, p["mem_norm_g"]), p["w_mem_kv"]) * jax.nn.silu(seg("z_m"))
    lifted, off = [], 0
    for y_br in (y_a, y_b, y_c, y_m):
        w = y_br.shape[1]
        lifted.append(pmm(y_br, p["w_branch"][off:off + w]))
        off += w
    return x + pmm(gate_merge(seg("gates"), *lifted), p["w_out"])


def _local_step(x, mem, target, w):
    depth = w["norm_g"].shape[0]
    d_model = x.shape[1]
    mid_names = [n for n in WEIGHTS if n not in ("norm_g", "w_in", "final_norm_g")]
    saved = []
    cur = x
    for layer in range(depth):
        xn = _rms_fwd_call(cur, w["norm_g"][layer], BF16)
        h = _mm(xn, w["w_in"][layer], "nn", "in_proj")
        p = {n: w[n][layer] for n in mid_names}
        nxt, vjp_mid = jax.vjp(lambda h_, x_, p_: _middle(h_, x_, mem, p_), h, cur, p)
        saved.append((cur, xn, vjp_mid))
        cur = nxt
    loss_parts, dx, d_final = _loss_call(cur, w["final_norm_g"], target)
    loss = jnp.sum(loss_parts[:, 0, 0])
    d_final = d_final.reshape(w["final_norm_g"].shape)
    grads = {n: [None] * depth for n in WEIGHTS if n != "final_norm_g"}
    for layer in reversed(range(depth)):
        x_in, xn, vjp_mid = saved[layer]
        dh, dx_skip, dp = vjp_mid(dx)
        dh = dh.astype(BF16)
        grads["w_in"][layer] = _unpad_w_in(_mm(xn, dh, "tn", "in_proj_dw"), d_model)
        dxn = _mm(dh, w["w_in"][layer], "nt", "in_proj_dx")
        dx_n, dg = _rms_bwd_call(x_in, w["norm_g"][layer], dxn)
        dx = dx_skip + dx_n
        grads["norm_g"][layer] = dg.reshape(w["norm_g"][layer].shape)
        for n in mid_names:
            grads[n][layer] = dp[n]
    out = {n: jnp.stack(v) for n, v in grads.items()}
    out["final_norm_g"] = d_final
    return loss, dx, out


_ANY = pl.BlockSpec(memory_space=pl.ANY)
_CHIP_FLIPS = [(1, 0), (0, 1), (1, 1)]


def _remote(src, dst, send_sem, recv_sem, dev):
    return pltpu.make_async_remote_copy(src_ref=src, dst_ref=dst, send_sem=send_sem, recv_sem=recv_sem,
                                        device_id=dev, device_id_type=pl.DeviceIdType.MESH)


def _flip(v, bit):
    return 1 - v if bit else v


def _gather_chips(shards):
    n = len(shards)

    def body(*refs):
        ins, outs = refs[:n], refs[n:2 * n]
        send_sems, recv_sems, local_sems = refs[2 * n:]
        x, y, c = lax.axis_index("x"), lax.axis_index("y"), lax.axis_index("c")
        locals_, sends = [], []
        for a in range(n):
            cp = pltpu.make_async_copy(ins[a], outs[a].at[2 * x + y], local_sems.at[a])
            cp.start()
            locals_.append(cp)

        def half(a, px, py, pc):
            rh = shards[a].shape[1] // 2
            return outs[a].at[2 * px + py, :, pl.ds(pc * rh, rh), :]

        for a in range(n):
            rh = shards[a].shape[1] // 2
            for j, (fx, fy) in enumerate(_CHIP_FLIPS):
                cp = _remote(ins[a].at[:, pl.ds(c * rh, rh), :], half(a, x, y, c), send_sems.at[6 * a + j],
                             recv_sems.at[6 * a + j], (_flip(x, fx), _flip(y, fy), c))
                cp.start()
                sends.append(cp)
        for a in range(n):
            for j, (fx, fy) in enumerate(_CHIP_FLIPS):
                blk = half(a, _flip(x, fx), _flip(y, fy), c)
                _remote(blk, blk, send_sems.at[6 * a + j], recv_sems.at[6 * a + j], (x, y, c)).wait_recv()
                cp = _remote(blk, blk, send_sems.at[6 * a + 3 + j], recv_sems.at[6 * a + 3 + j], (x, y, 1 - c))
                cp.start()
                sends.append(cp)
        for a in range(n):
            for j, (fx, fy) in enumerate(_CHIP_FLIPS):
                blk = half(a, _flip(x, fx), _flip(y, fy), 1 - c)
                _remote(blk, blk, send_sems.at[6 * a + 3 + j], recv_sems.at[6 * a + 3 + j], (x, y, c)).wait_recv()
        for cp in sends:
            cp.wait_send()
        for cp in locals_:
            cp.wait()

    return pl.pallas_call(
        body, name="gather_weights",
        out_shape=tuple(jax.ShapeDtypeStruct((N_CHIPS,) + s.shape, s.dtype) for s in shards),
        in_specs=[_ANY] * n, out_specs=tuple([_ANY] * n),
        scratch_shapes=[pltpu.SemaphoreType.DMA((6 * n,)), pltpu.SemaphoreType.DMA((6 * n,)),
                        pltpu.SemaphoreType.DMA((n,))],
    )(*shards)


def _exchange_pair(parts):
    n = len(parts)

    def body(*refs):
        ins, outs = refs[:n], refs[n:2 * n]
        send_sems, recv_sems = refs[2 * n:]
        x, y, c = lax.axis_index("x"), lax.axis_index("y"), lax.axis_index("c")
        started = []
        for a in range(n):
            rh = parts[a].shape[2] // 2
            cp = _remote(ins[a].at[:, :, pl.ds((1 - c) * rh, rh), :], outs[a], send_sems.at[a], recv_sems.at[a], (x, y, 1 - c))
            cp.start()
            started.append(cp)
        for a in range(n):
            _remote(outs[a], outs[a], send_sems.at[a], recv_sems.at[a], (x, y, c)).wait_recv()
        for cp in started:
            cp.wait_send()

    return pl.pallas_call(
        body, name="exchange_pair",
        out_shape=tuple(jax.ShapeDtypeStruct((p.shape[0], p.shape[1], p.shape[2] // 2, p.shape[3]), p.dtype) for p in parts),
        in_specs=[_ANY] * n, out_specs=tuple([_ANY] * n),
        scratch_shapes=[pltpu.SemaphoreType.DMA((n,)), pltpu.SemaphoreType.DMA((n,))],
    )(*parts)


def _exchange_chips(pair_sums, small):
    n = len(pair_sums)

    def body(*refs):
        ins, small_in = refs[:n], refs[n]
        outs, small_out = refs[n + 1:2 * n + 1], refs[2 * n + 1]
        send_sems, recv_sems, small_send, small_recv, local_sem = refs[2 * n + 2:]
        x, y, c = lax.axis_index("x"), lax.axis_index("y"), lax.axis_index("c")
        me = 4 * x + 2 * y + c
        started = []
        own = pltpu.make_async_copy(small_in, small_out.at[me], local_sem)
        own.start()
        for f, (fx, fy) in enumerate(_CHIP_FLIPS):
            px, py = _flip(x, fx), _flip(y, fy)
            for a in range(n):
                cp = _remote(ins[a].at[2 * px + py], outs[a].at[f], send_sems.at[3 * a + f], recv_sems.at[3 * a + f], (px, py, c))
                cp.start()
                started.append(cp)
        for k in range(1, N_DEV):
            peer = (_flip(x, k >> 2 & 1), _flip(y, k >> 1 & 1), _flip(c, k & 1))
            cp = _remote(small_in, small_out.at[me], small_send.at[k - 1], small_recv.at[k - 1], peer)
            cp.start()
            started.append(cp)
        for f in range(len(_CHIP_FLIPS)):
            for a in range(n):
                _remote(outs[a].at[f], outs[a].at[f], send_sems.at[3 * a + f], recv_sems.at[3 * a + f], (x, y, c)).wait_recv()
        for k in range(1, N_DEV):
            peer = 4 * _flip(x, k >> 2 & 1) + 2 * _flip(y, k >> 1 & 1) + _flip(c, k & 1)
            _remote(small_out.at[peer], small_out.at[peer], small_send.at[k - 1], small_recv.at[k - 1], (x, y, c)).wait_recv()
        for cp in started:
            cp.wait_send()
        own.wait()

    shapes = tuple(jax.ShapeDtypeStruct((3,) + p.shape[1:], p.dtype) for p in pair_sums)
    shapes += (jax.ShapeDtypeStruct((N_DEV,) + small.shape, small.dtype),)
    return pl.pallas_call(
        body, name="exchange_chips", out_shape=shapes, in_specs=[_ANY] * (n + 1), out_specs=tuple([_ANY] * (n + 1)),
        scratch_shapes=[pltpu.SemaphoreType.DMA((3 * n,)), pltpu.SemaphoreType.DMA((3 * n,)),
                        pltpu.SemaphoreType.DMA((N_DEV - 1,)), pltpu.SemaphoreType.DMA((N_DEV - 1,)),
                        pltpu.SemaphoreType.DMA(())],
    )(*pair_sums, small)


def _swap_halves(arrays):
    n = len(arrays)

    def body(*refs):
        outs = refs[n:2 * n]
        send_sems, recv_sems = refs[2 * n:]
        x, y, c = lax.axis_index("x"), lax.axis_index("y"), lax.axis_index("c")
        started = []
        for a in range(n):
            rh = arrays[a].shape[1] // 2
            mine = outs[a].at[:, pl.ds(c * rh, rh), :]
            cp = _remote(mine, mine, send_sems.at[a], recv_sems.at[a], (x, y, 1 - c))
            cp.start()
            started.append(cp)
        for a in range(n):
            rh = arrays[a].shape[1] // 2
            theirs = outs[a].at[:, pl.ds((1 - c) * rh, rh), :]
            _remote(theirs, theirs, send_sems.at[a], recv_sems.at[a], (x, y, c)).wait_recv()
        for cp in started:
            cp.wait_send()

    return pl.pallas_call(
        body, name="swap_halves", out_shape=tuple(jax.ShapeDtypeStruct(t.shape, t.dtype) for t in arrays),
        in_specs=[_ANY] * n, out_specs=tuple([_ANY] * n), input_output_aliases={a: a for a in range(n)},
        scratch_shapes=[pltpu.SemaphoreType.DMA((n,)), pltpu.SemaphoreType.DMA((n,))],
    )(*arrays)


def _adamw_math(g, w, m, v):
    m = ADAM_B1 * m + (1.0 - ADAM_B1) * g
    v = ADAM_B2 * v + (1.0 - ADAM_B2) * jnp.square(g)
    m_hat = m / (1.0 - ADAM_B1 ** ADAM_STEP)
    v_hat = v / (1.0 - ADAM_B2 ** ADAM_STEP)
    delta = -ADAM_LR * (m_hat / (jnp.sqrt(v_hat) + ADAM_EPS) + ADAM_WD * w)
    return delta, m, v


def _row_tile(rows, row_bytes, budget, step=8):
    best = step
    for t in range(step, rows + 1, step):
        if rows % t == 0 and t * row_bytes <= budget:
            best = t
    return best


def _pair_sum(part, other, core):
    nj, a, r, c = part.shape
    rh = r // 2
    tr = _row_tile(rh, 4 * (-(-c // LANES) * LANES), 512 * 1024, 16)

    def body(core_ref, p_ref, o_ref, s32_ref, s16_ref):
        s = p_ref[0, 0, 0] + o_ref[0, 0]
        s32_ref[0, 0] = s
        s16_ref[0, 0] = s.astype(BF16)

    spec = pl.BlockSpec((1, 1, tr, c), lambda j, l, i, core_ref: (j, l, i, 0))
    return pl.pallas_call(
        body, name="pair_sum",
        out_shape=(jax.ShapeDtypeStruct(other.shape, F32), jax.ShapeDtypeStruct(other.shape, BF16)),
        grid_spec=pltpu.PrefetchScalarGridSpec(
            num_scalar_prefetch=1, grid=(nj, a, rh // tr),
            in_specs=[pl.BlockSpec((1, 1, 1, tr, c), lambda j, l, i, core_ref: (j, l, core_ref[0], i, 0)), spec],
            out_specs=(spec, spec)),
        compiler_params=pltpu.CompilerParams(dimension_semantics=("parallel", "parallel", "parallel"),
                                             vmem_limit_bytes=VMEM_LIMIT),
    )(core, part.reshape(nj, a, 2, rh, c), other)


def _sum_adamw_big(own, landed, w, m, v, place):
    _, a, rh, c = own.shape
    tr = _row_tile(rh, 4 * (-(-c // LANES) * LANES), 256 * 1024, 16)
    w4, m4, v4 = (t.reshape(a, 2, rh, c) for t in (w, m, v))

    def body(place_ref, q_ref, p_ref, w_ref, m_ref, v_ref, g_out, d_out, m_out, v_out):
        g = q_ref[0, 0]
        for f in range(3):
            g = g + p_ref[f, 0].astype(F32)
        delta, mm_, vv_ = _adamw_math(g, w_ref[0, 0], m_ref[0, 0], v_ref[0, 0])
        g_out[0, 0] = g
        d_out[0, 0] = delta
        m_out[0, 0] = mm_
        v_out[0, 0] = vv_

    wspec = pl.BlockSpec((1, 1, tr, c), lambda l, i, place_ref: (l, place_ref[0], i, 0))
    out = jax.ShapeDtypeStruct((a, 2, rh, c), F32)
    res = pl.pallas_call(
        body, name="sum_adamw_big", out_shape=(out,) * 4,
        grid_spec=pltpu.PrefetchScalarGridSpec(
            num_scalar_prefetch=1, grid=(a, rh // tr),
            in_specs=[pl.BlockSpec((1, 1, tr, c), lambda l, i, place_ref: (place_ref[1], l, i, 0)),
                      pl.BlockSpec((3, 1, tr, c), lambda l, i, place_ref: (0, l, i, 0)), wspec, wspec, wspec],
            out_specs=(wspec,) * 4),
        compiler_params=pltpu.CompilerParams(dimension_semantics=("parallel", "parallel"), vmem_limit_bytes=VMEM_LIMIT),
    )(place, own, landed, w4, m4, v4)
    return [t.reshape(a, 2 * rh, c) for t in res]


def _sum_small(parts):
    _, rows, _ = parts.shape
    tr = _row_tile(rows, 4 * LANES, 256 * 1024)

    def body(p_ref, o_ref):
        g = p_ref[0]
        for k in range(1, N_DEV):
            g = g + p_ref[k]
        o_ref[...] = g

    return pl.pallas_call(
        body, name="sum_small", out_shape=jax.ShapeDtypeStruct((rows, LANES), F32), grid=(rows // tr,),
        in_specs=[pl.BlockSpec((N_DEV, tr, LANES), lambda i: (0, i, 0))], out_specs=pl.BlockSpec((tr, LANES), lambda i: (i, 0)),
        compiler_params=pltpu.CompilerParams(dimension_semantics=("parallel",)),
    )(parts)


def _adamw_small(g, w, m, v):
    rows, _ = g.shape
    tr = _row_tile(rows, 4 * LANES, 256 * 1024)

    def body(g_ref, w_ref, m_ref, v_ref, d_out, m_out, v_out):
        d_out[...], m_out[...], v_out[...] = _adamw_math(g_ref[...], w_ref[...], m_ref[...], v_ref[...])

    spec = pl.BlockSpec((tr, LANES), lambda i: (i, 0))
    out = jax.ShapeDtypeStruct((rows, LANES), F32)
    return pl.pallas_call(
        body, name="adamw_small", out_shape=(out,) * 3, grid=(rows // tr,), in_specs=[spec] * 4, out_specs=(spec,) * 3,
        compiler_params=pltpu.CompilerParams(dimension_semantics=("parallel",)),
    )(g, w, m, v)


def _pack(arrays):
    flat = jnp.concatenate([a.reshape(-1) for a in arrays])
    rows = -(-flat.shape[0] // (PACK_ROWS * LANES)) * PACK_ROWS
    return jnp.pad(flat, (0, rows * LANES - flat.shape[0])).reshape(rows, LANES)


def _unpack(packed, shapes):
    flat, out, off = packed.reshape(-1), [], 0
    for s in shapes:
        size = math.prod(s)
        out.append(flat[off:off + size].reshape(s))
        off += size
    return out


def kernel(x, mem, norm_g, w_in, ssm_a_re, ssm_a_im, ssm_log_step, ssm_b_re, ssm_b_im, ssm_c_re, ssm_c_im, ssm_d, ssm_w_glu, ssm_b_glu, dn_conv, dn_a_log, dn_dt_bias, dn_norm_g, attn_q_norm, attn_k_norm, mem_norm_g, w_mem_kv, w_branch, w_out, final_norm_g, loss_target, m_norm_g, m_w_in, m_ssm_a_re, m_ssm_a_im, m_ssm_log_step, m_ssm_b_re, m_ssm_b_im, m_ssm_c_re, m_ssm_c_im, m_ssm_d, m_ssm_w_glu, m_ssm_b_glu, m_dn_conv, m_dn_a_log, m_dn_dt_bias, m_dn_norm_g, m_attn_q_norm, m_attn_k_norm, m_mem_norm_g, m_w_mem_kv, m_w_branch, m_w_out, m_final_norm_g, v_norm_g, v_w_in, v_ssm_a_re, v_ssm_a_im, v_ssm_log_step, v_ssm_b_re, v_ssm_b_im, v_ssm_c_re, v_ssm_c_im, v_ssm_d, v_ssm_w_glu, v_ssm_b_glu, v_dn_conv, v_dn_a_log, v_dn_dt_bias, v_dn_norm_g, v_attn_q_norm, v_attn_k_norm, v_mem_norm_g, v_w_mem_kv, v_w_branch, v_w_out, v_final_norm_g):
    args = locals()
    wts = {n: args[n] for n in WEIGHTS}
    mom = {n: args["m_" + n] for n in WEIGHTS}
    var = {n: args["v_" + n] for n in WEIGHTS}
    d_model = x.shape[-1]
    chip = 2 * lax.axis_index("x") + lax.axis_index("y")
    core = lax.axis_index("c")

    conv_rows = dn_conv.shape[1]
    conv_flat = _pack([dn_conv])
    gathered = _gather_chips([wts[n].astype(BF16) for n in BIG] + [conv_flat[None]])
    full = {n: wts[n] for n in SMALL}
    g_in = gathered[0]
    full["w_in"] = _pad_w_in(jnp.concatenate([g_in[j] for j in range(N_CHIPS)], axis=-1), d_model)
    for n, g in zip(BIG[1:], gathered[1:]):
        full[n] = jnp.concatenate([g[j] for j in range(N_CHIPS)], axis=1).astype(F32)
    conv_all = gathered[-1][:, 0].reshape(N_CHIPS, -1)[:, :dn_conv.size].reshape((N_CHIPS,) + dn_conv.shape)
    full["dn_conv"] = jnp.concatenate([conv_all[j] for j in range(N_CHIPS)], axis=1)

    loss, grad_x, grads = _local_step(x[0], mem[0], loss_target[0], full)
    loss = lax.psum(loss, ("x", "y", "c"))

    big_parts = []
    for n in BIG:
        g = grads[n]
        if n == "w_in":
            cols = g.shape[-1] // N_CHIPS
            big_parts.append(jnp.stack([g[..., j * cols:(j + 1) * cols] for j in range(N_CHIPS)]))
        else:
            rows = g.shape[1] // N_CHIPS
            big_parts.append(jnp.stack([g[:, j * rows:(j + 1) * rows] for j in range(N_CHIPS)]))
    core_arr = jnp.reshape(core, (1,)).astype(jnp.int32)
    place = jnp.stack([core, chip]).astype(jnp.int32)
    from_pair = _exchange_pair(big_parts)
    sums = [_pair_sum(p, o, core_arr) for p, o in zip(big_parts, from_pair)]
    small_shapes = [grads[n].shape for n in SMALL]
    landed = _exchange_chips([s[1] for s in sums], _pack([grads[n] for n in SMALL]))

    halves = []
    for n, s, parts in zip(BIG, sums, landed[:-1]):
        halves.extend(_sum_adamw_big(s[0], parts, wts[n], mom[n], var[n], place))
    swapped = _swap_halves(halves)
    res = {}
    for i, n in enumerate(BIG):
        res[n] = swapped[4 * i:4 * i + 4]
    small_g = _unpack(_sum_small(landed[-1]), small_shapes)
    small_g = dict(zip(SMALL, small_g))
    small_g["dn_conv"] = lax.dynamic_slice_in_dim(small_g["dn_conv"], chip * conv_rows, conv_rows, axis=1)
    shapes = [small_g[n].shape for n in SMALL]
    upd = _adamw_small(_pack([small_g[n] for n in SMALL]), _pack([wts[n] for n in SMALL]),
                       _pack([mom[n] for n in SMALL]), _pack([var[n] for n in SMALL]))
    upd = [_unpack(u, shapes) for u in upd]
    for i, n in enumerate(SMALL):
        res[n] = (small_g[n], upd[0][i], upd[1][i], upd[2][i])

    out = [loss, grad_x[None]]
    for kind in range(4):
        out.extend(res[n][kind] for n in WEIGHTS)
    return tuple(out)
```

```python
import functools
import math

import jax
import jax.numpy as jnp
from jax import lax
from jax.experimental import pallas as pl
from jax.experimental.pallas import tpu as pltpu

F32 = jnp.float32
BF16 = jnp.bfloat16

GRID_W = 64
EPS = 1e-6
SSM_GROUP = 16
SSM_STATE = 64
SSM_GROUPS = 48
DN_HEADS = 6
DN_HEAD_DIM = 128
DN_CONV = 5
DN_CHUNK = 64
ATT_HEADS = 8
ATT_KV_HEADS = 2
ATT_HEAD_DIM = 128
ROPE_THETA = 10000.0
MEM_HEADS = 4
MEM_HEAD_DIM = 128
N_BRANCH = 4

ADAM_LR = 0.001
ADAM_B1 = 0.9
ADAM_B2 = 0.999
ADAM_EPS = 1e-08
ADAM_WD = 0.01
ADAM_STEP = 10

N_CHIPS = 4
N_DEV = 8
LANES = 128
VMEM_LIMIT = 48 * 1024 * 1024
IN_PAD_UNIT = 512
MM_FULL_K = 2048
PACK_ROWS = 512
CONV_HALO = 8
ATT_HEADS_PER_STEP = 4

WEIGHTS = ['norm_g', 'w_in', 'ssm_a_re', 'ssm_a_im', 'ssm_log_step', 'ssm_b_re', 'ssm_b_im', 'ssm_c_re',
           'ssm_c_im', 'ssm_d', 'ssm_w_glu', 'ssm_b_glu', 'dn_conv', 'dn_a_log', 'dn_dt_bias', 'dn_norm_g',
           'attn_q_norm', 'attn_k_norm', 'mem_norm_g', 'w_mem_kv', 'w_branch', 'w_out', 'final_norm_g']
BIG = ['w_in', 'ssm_w_glu', 'w_mem_kv', 'w_branch', 'w_out']
SMALL = [n for n in WEIGHTS if n not in BIG]


def _pick(dim, cands):
    for c in cands:
        if dim % c == 0:
            return c
    return dim


def _widths(d_model):
    ssm_w = SSM_GROUPS * SSM_GROUP
    dn_w = DN_HEADS * DN_HEAD_DIM
    att_w = ATT_HEADS * ATT_HEAD_DIM
    kv_w = ATT_KV_HEADS * ATT_HEAD_DIM
    mem_w = MEM_HEADS * MEM_HEAD_DIM
    return [("u", ssm_w), ("z_a", ssm_w), ("dq", dn_w), ("dk", dn_w), ("dv", dn_w), ("dab", 4 * DN_HEADS),
            ("z_b", dn_w), ("aq", att_w), ("ak", kv_w), ("av", kv_w), ("z_c", att_w), ("mq", mem_w),
            ("z_m", mem_w), ("gates", N_BRANCH * d_model)]


def _layout(d_model):
    out, src, dst = {}, 0, 0
    for name, w in _widths(d_model):
        pw = -(-w // IN_PAD_UNIT) * IN_PAD_UNIT if w % LANES else w
        out[name] = (src, dst, w, pw)
        src += w
        dst += pw
    return out, src, dst


def _pad_w_in(full, d_model):
    lay, _, _ = _layout(d_model)
    parts = []
    for name, _ in _widths(d_model):
        s, _, w, pw = lay[name]
        seg = full[..., s:s + w]
        if pw != w:
            seg = jnp.pad(seg, [(0, 0)] * (full.ndim - 1) + [(0, pw - w)])
        parts.append(seg)
    return jnp.concatenate(parts, axis=-1)


def _unpad_w_in(padded, d_model):
    lay, _, _ = _layout(d_model)
    return jnp.concatenate([padded[..., lay[n][1]:lay[n][1] + lay[n][2]] for n, _ in _widths(d_model)], axis=-1)


def _mm(a, b, mode="nn", name="mm"):
    if mode == "nn":
        (m, k), (k2, n) = a.shape, b.shape
    elif mode == "nt":
        (m, k), (n, k2) = a.shape, b.shape
    else:
        (k, m), (k2, n) = a.shape, b.shape
    assert k == k2, (a.shape, b.shape, mode)
    tk = k if k <= MM_FULL_K else _pick(k, (1024, 512, 256, 128))
    tall = mode == "tn" or (mode == "nn" and a.dtype == BF16 and tk == k)
    tm = _pick(m, (2048, 1024, 512, 256, 128) if tall else (1024, 512, 256, 128))
    if mode == "tn" and m <= 1024:
        tm = m
    tn = _pick(n, (512, 384, 256, 128) if tk > 1024 or tm > 1024 else (1024, 512, 384, 256, 128))
    if mode == "nt" and k > MM_FULL_K:
        tk = _pick(k, (1536, 1024, 512, 256, 128))
    nk = k // tk
    dims = {"nn": (((1,), (0,)), ((), ())), "nt": (((1,), (1,)), ((), ())), "tn": (((0,), (0,)), ((), ()))}[mode]

    def body(a_ref, b_ref, o_ref, *scratch):
        prod = lax.dot_general(a_ref[...].astype(BF16), b_ref[...].astype(BF16), dims, preferred_element_type=F32)
        if nk == 1:
            o_ref[...] = prod
            return
        acc_ref, = scratch
        kk = pl.program_id(2)

        @pl.when(kk == 0)
        def _():
            acc_ref[...] = prod

        @pl.when(kk > 0)
        def _():
            acc_ref[...] += prod

        @pl.when(kk == nk - 1)
        def _():
            o_ref[...] = acc_ref[...]

    a_spec = pl.BlockSpec((tk, tm), lambda i, j, kk: (kk, i)) if mode == "tn" else pl.BlockSpec((tm, tk), lambda i, j, kk: (i, kk))
    b_spec = pl.BlockSpec((tn, tk), lambda i, j, kk: (j, kk)) if mode == "nt" else pl.BlockSpec((tk, tn), lambda i, j, kk: (kk, j))
    return pl.pallas_call(
        body, name=name, out_shape=jax.ShapeDtypeStruct((m, n), F32), grid=(m // tm, n // tn, nk),
        in_specs=[a_spec, b_spec], out_specs=pl.BlockSpec((tm, tn), lambda i, j, kk: (i, j)),
        scratch_shapes=[pltpu.VMEM((tm, tn), F32)] if nk > 1 else [],
        compiler_params=pltpu.CompilerParams(dimension_semantics=("parallel", "parallel", "arbitrary"),
                                             vmem_limit_bytes=VMEM_LIMIT),
    )(a, b)


@jax.custom_vjp
def pmm(a, b):
    return _mm(a, b, "nn", "pmm_fwd")


def _pmm_fwd(a, b):
    return _mm(a, b, "nn", "pmm_fwd"), (a, b)


def _pmm_bwd(res, dc):
    a, b = res
    return _mm(dc, b, "nt", "pmm_da"), _mm(a, dc, "tn", "pmm_db")


pmm.defvjp(_pmm_fwd, _pmm_bwd)


def _cmm_call(a, w):
    m = a.shape[0]
    nc, kc, nn = w.shape
    tm = _pick(m, (2048, 1024, 512, 256, 128))

    def body(a_ref, w_ref, o_ref):
        o_ref[...] = lax.dot_general(a_ref[...].astype(BF16), w_ref[0].astype(BF16), (((1,), (0,)), ((), ())),
                                     preferred_element_type=F32)

    return pl.pallas_call(
        body, name="cmm", out_shape=jax.ShapeDtypeStruct((m, nc * nn), F32), grid=(m // tm, nc),
        in_specs=[pl.BlockSpec((tm, kc), lambda i, c: (i, c)), pl.BlockSpec((1, kc, nn), lambda i, c: (c, 0, 0))],
        out_specs=pl.BlockSpec((tm, nn), lambda i, c: (i, c)),
        compiler_params=pltpu.CompilerParams(dimension_semantics=("parallel", "parallel"), vmem_limit_bytes=VMEM_LIMIT),
    )(a, w)


def _cmm_dw_call(a, dc, kc, nn):
    m = a.shape[0]
    nc = a.shape[1] // kc
    tm = _pick(m, (4096, 2048, 1024, 512, 256, 128))
    nm = m // tm

    def body(a_ref, d_ref, o_ref, acc_s):
        i = pl.program_id(1)

        @pl.when(i == 0)
        def _():
            acc_s[...] = jnp.zeros_like(acc_s)

        acc_s[...] += lax.dot_general(a_ref[...].astype(BF16), d_ref[...].astype(BF16), (((0,), (0,)), ((), ())),
                                      preferred_element_type=F32)

        @pl.when(i == nm - 1)
        def _():
            o_ref[0] = acc_s[...]

    return pl.pallas_call(
        body, name="cmm_dw", out_shape=jax.ShapeDtypeStruct((nc, kc, nn), F32), grid=(nc, nm),
        in_specs=[pl.BlockSpec((tm, kc), lambda c, i: (i, c)), pl.BlockSpec((tm, nn), lambda c, i: (i, c))],
        out_specs=pl.BlockSpec((1, kc, nn), lambda c, i: (c, 0, 0)),
        scratch_shapes=[pltpu.VMEM((kc, nn), F32)],
        compiler_params=pltpu.CompilerParams(dimension_semantics=("parallel", "arbitrary"), vmem_limit_bytes=VMEM_LIMIT),
    )(a, dc)


@jax.custom_vjp
def cmm(a, w):
    return _cmm_call(a, w)


def _cmm_fwd(a, w):
    return _cmm_call(a, w), (a, w)


def _cmm_bwd(res, dc):
    a, w = res
    return _cmm_call(dc, jnp.swapaxes(w, 1, 2)), _cmm_dw_call(a, dc, w.shape[1], w.shape[2])


cmm.defvjp(_cmm_fwd, _cmm_bwd)


def _cluster_diag(w, per):
    g, a, b = w.shape
    eye = jnp.eye(per, dtype=w.dtype)
    wc = w.reshape(g // per, per, a, b)
    return (wc[:, :, :, None, :] * eye[None, :, None, :, None]).reshape(g // per, per * a, per * b)


def _norm_rows(rows, d):
    return _row_tile(rows, 4 * d, 1024 * 1024)


def _rms_fwd_call(x, g, out_dtype=F32):
    rows, d = x.shape
    tr = _norm_rows(rows, d)

    def body(x_ref, g_ref, o_ref):
        xv = x_ref[...]
        o_ref[...] = (xv * lax.rsqrt(jnp.mean(xv * xv, axis=1, keepdims=True) + EPS) * g_ref[...]).astype(out_dtype)

    return pl.pallas_call(
        body, name="rms_fwd", out_shape=jax.ShapeDtypeStruct((rows, d), out_dtype), grid=(rows // tr,),
        in_specs=[pl.BlockSpec((tr, d), lambda i: (i, 0)), pl.BlockSpec((1, d), lambda i: (0, 0))],
        out_specs=pl.BlockSpec((tr, d), lambda i: (i, 0)),
        compiler_params=pltpu.CompilerParams(dimension_semantics=("parallel",), vmem_limit_bytes=VMEM_LIMIT),
    )(x, g.reshape(1, d))


def _rms_bwd_call(x, g, dy):
    rows, d = x.shape
    tr = _norm_rows(rows, d)

    def body(x_ref, g_ref, dy_ref, dx_ref, dg_ref):
        @pl.when(pl.program_id(0) == 0)
        def _():
            dg_ref[...] = jnp.zeros_like(dg_ref)

        xv = x_ref[...]
        r = lax.rsqrt(jnp.mean(xv * xv, axis=1, keepdims=True) + EPS)
        xh = xv * r
        dyv = dy_ref[...]
        dxh = dyv * g_ref[...]
        dx_ref[...] = r * (dxh - xh * jnp.mean(dxh * xh, axis=1, keepdims=True))
        dg_ref[...] += jnp.sum(dyv * xh, axis=0, keepdims=True)

    return pl.pallas_call(
        body, name="rms_bwd", out_shape=(jax.ShapeDtypeStruct((rows, d), F32), jax.ShapeDtypeStruct((1, d), F32)),
        grid=(rows // tr,),
        in_specs=[pl.BlockSpec((tr, d), lambda i: (i, 0)), pl.BlockSpec((1, d), lambda i: (0, 0)),
                  pl.BlockSpec((tr, d), lambda i: (i, 0))],
        out_specs=(pl.BlockSpec((tr, d), lambda i: (i, 0)), pl.BlockSpec((1, d), lambda i: (0, 0))),
        compiler_params=pltpu.CompilerParams(dimension_semantics=("arbitrary",), vmem_limit_bytes=VMEM_LIMIT),
    )(x, g.reshape(1, d), dy)


@jax.custom_vjp
def prms(x, g):
    return _rms_fwd_call(x, g)


def _prms_fwd(x, g):
    return _rms_fwd_call(x, g), (x, g)


def _prms_bwd(res, dy):
    x, g = res
    dx, dg = _rms_bwd_call(x, g, dy)
    return dx, dg.reshape(g.shape)


prms.defvjp(_prms_fwd, _prms_bwd)


def _pair_swap(v):
    lane = lax.broadcasted_iota(jnp.int32, v.shape, 1)
    return jnp.where(lane % 2 == 0, pltpu.roll(v, v.shape[1] - 1, 1), pltpu.roll(v, 1, 1))


def _hnorm_fwd_call(x, g, cos, sin, scale):
    rows, width = x.shape
    d = g.shape[-1]
    h = width // d
    rope = cos is not None
    tr = _row_tile(rows, 4 * width, 1024 * 1024)

    def body(*refs):
        if rope:
            x_ref, g_ref, c_ref, s_ref, o_ref = refs
        else:
            x_ref, g_ref, o_ref = refs
        for i in range(h):
            cols = slice(i * d, (i + 1) * d)
            xv = x_ref[:, cols]
            n = xv * lax.rsqrt(jnp.mean(xv * xv, axis=1, keepdims=True) + EPS) * g_ref[...]
            if rope:
                n = (c_ref[...] * n + s_ref[...] * _pair_swap(n)) * scale
            o_ref[:, cols] = n

    row_spec = pl.BlockSpec((tr, width), lambda i: (i, 0))
    tab_spec = pl.BlockSpec((tr, d), lambda i: (i, 0))
    ins = [x, g.reshape(1, d)] + ([cos, sin] if rope else [])
    return pl.pallas_call(
        body, name="head_norm_fwd", out_shape=jax.ShapeDtypeStruct((rows, width), F32), grid=(rows // tr,),
        in_specs=[row_spec, pl.BlockSpec((1, d), lambda i: (0, 0))] + ([tab_spec, tab_spec] if rope else []),
        out_specs=row_spec,
        compiler_params=pltpu.CompilerParams(dimension_semantics=("parallel",), vmem_limit_bytes=VMEM_LIMIT),
    )(*ins)


def _hnorm_bwd_call(x, g, cos, sin, scale, dy):
    rows, width = x.shape
    d = g.shape[-1]
    h = width // d
    rope = cos is not None
    tr = _row_tile(rows, 4 * width, 1024 * 1024)

    def body(*refs):
        if rope:
            x_ref, g_ref, c_ref, s_ref, dy_ref, dx_ref, dg_ref = refs
        else:
            x_ref, g_ref, dy_ref, dx_ref, dg_ref = refs

        @pl.when(pl.program_id(0) == 0)
        def _():
            dg_ref[...] = jnp.zeros_like(dg_ref)

        dg = jnp.zeros((1, d), F32)
        for i in range(h):
            cols = slice(i * d, (i + 1) * d)
            xv = x_ref[:, cols]
            r = lax.rsqrt(jnp.mean(xv * xv, axis=1, keepdims=True) + EPS)
            xh = xv * r
            dn = dy_ref[:, cols]
            if rope:
                dn = dn * scale
                dn = c_ref[...] * dn + _pair_swap(s_ref[...] * dn)
            dxh = dn * g_ref[...]
            dx_ref[:, cols] = r * (dxh - xh * jnp.mean(dxh * xh, axis=1, keepdims=True))
            dg = dg + jnp.sum(dn * xh, axis=0, keepdims=True)
        dg_ref[...] += dg

    row_spec = pl.BlockSpec((tr, width), lambda i: (i, 0))
    tab_spec = pl.BlockSpec((tr, d), lambda i: (i, 0))
    vec_spec = pl.BlockSpec((1, d), lambda i: (0, 0))
    ins = [x, g.reshape(1, d)] + ([cos, sin] if rope else []) + [dy]
    return pl.pallas_call(
        body, name="head_norm_bwd", out_shape=(jax.ShapeDtypeStruct((rows, width), F32), jax.ShapeDtypeStruct((1, d), F32)),
        grid=(rows // tr,),
        in_specs=[row_spec, vec_spec] + ([tab_spec, tab_spec] if rope else []) + [row_spec],
        out_specs=(row_spec, vec_spec),
        compiler_params=pltpu.CompilerParams(dimension_semantics=("arbitrary",), vmem_limit_bytes=VMEM_LIMIT),
    )(*ins)


@jax.custom_vjp
def head_norm(x, g):
    return _hnorm_fwd_call(x, g, None, None, 1.0)


def _head_norm_fwd(x, g):
    return _hnorm_fwd_call(x, g, None, None, 1.0), (x, g)


def _head_norm_bwd(res, dy):
    x, g = res
    dx, dg = _hnorm_bwd_call(x, g, None, None, 1.0, dy)
    return dx, dg.reshape(g.shape)


head_norm.defvjp(_head_norm_fwd, _head_norm_bwd)


@functools.partial(jax.custom_vjp, nondiff_argnums=(4,))
def head_norm_rope(x, g, cos, sin, scale):
    return _hnorm_fwd_call(x, g, cos, sin, scale)


def _head_norm_rope_fwd(x, g, cos, sin, scale):
    return _hnorm_fwd_call(x, g, cos, sin, scale), (x, g, cos, sin)


def _head_norm_rope_bwd(scale, res, dy):
    x, g, cos, sin = res
    dx, dg = _hnorm_bwd_call(x, g, cos, sin, scale, dy)
    return dx, dg.reshape(g.shape), jnp.zeros_like(cos), jnp.zeros_like(sin)


head_norm_rope.defvjp(_head_norm_rope_fwd, _head_norm_rope_bwd)


def _loss_call(x, g, target):
    rows, d = x.shape
    tr = _norm_rows(rows, d)
    nb = rows // tr

    def body(x_ref, g_ref, t_ref, l_ref, dx_ref, dg_ref):
        @pl.when(pl.program_id(0) == 0)
        def _():
            dg_ref[...] = jnp.zeros_like(dg_ref)

        xv = x_ref[...]
        r = lax.rsqrt(jnp.mean(xv * xv, axis=1, keepdims=True) + EPS)
        xh = xv * r
        err = xh * g_ref[...] - t_ref[...]
        l_ref[...] = jnp.broadcast_to(0.5 * jnp.sum(jnp.mean(err * err, axis=1, keepdims=True), axis=0, keepdims=True)[None],
                                      l_ref.shape)
        dyv = err * (1.0 / d)
        dxh = dyv * g_ref[...]
        dx_ref[...] = r * (dxh - xh * jnp.mean(dxh * xh, axis=1, keepdims=True))
        dg_ref[...] += jnp.sum(dyv * xh, axis=0, keepdims=True)

    return pl.pallas_call(
        body, name="final_loss",
        out_shape=(jax.ShapeDtypeStruct((nb, 1, LANES), F32), jax.ShapeDtypeStruct((rows, d), F32),
                   jax.ShapeDtypeStruct((1, d), F32)),
        grid=(nb,),
        in_specs=[pl.BlockSpec((tr, d), lambda i: (i, 0)), pl.BlockSpec((1, d), lambda i: (0, 0)),
                  pl.BlockSpec((tr, d), lambda i: (i, 0))],
        out_specs=(pl.BlockSpec((1, 1, LANES), lambda i: (i, 0, 0)), pl.BlockSpec((tr, d), lambda i: (i, 0)),
                   pl.BlockSpec((1, d), lambda i: (0, 0))),
        compiler_params=pltpu.CompilerParams(dimension_semantics=("arbitrary",), vmem_limit_bytes=VMEM_LIMIT),
    )(x, g.reshape(1, d), target)


_NT = (((1,), (1,)), ((), ()))
_NN = (((1,), (0,)), ((), ()))
_TN = (((0,), (0,)), ((), ()))


def _att_plan(lq, lk, hq, group):
    hp = min(ATT_HEADS_PER_STEP, hq)
    assert hq % hp == 0 and (hp % group == 0 or group % hp == 0)
    kvp = max(1, hp // group)
    return _pick(lq, (512, 256, 128)), _pick(lk, (1024, 512, 256, 128)), hp, kvp


def _att_specs(bq, bk, dh, hp, kvp, group, q_map, k_map):
    qs = pl.BlockSpec((bq, hp * dh), lambda *g: q_map(*g))
    ls = pl.BlockSpec((bq, hp * LANES), lambda *g: q_map(*g))
    ks = pl.BlockSpec((bk, kvp * dh), lambda *g: (k_map(*g)[0], k_map(*g)[1] * hp // (group * kvp)))
    return qs, ls, ks


def _flash_fwd(q, k, v, dh, group, scale):
    lq, lk = q.shape[0], k.shape[0]
    hq = q.shape[1] // dh
    bq, bk, hp, kvp = _att_plan(lq, lk, hq, group)
    nkv = lk // bk

    def body(q_ref, k_ref, v_ref, o_ref, lse_ref, *scratch):
        m_s, l_s, acc_s = scratch[:hp], scratch[hp:2 * hp], scratch[2 * hp:]
        j = pl.program_id(2)

        @pl.when(j == 0)
        def _():
            for h in range(hp):
                m_s[h][...] = jnp.full_like(m_s[h], -jnp.inf)
                l_s[h][...] = jnp.zeros_like(l_s[h])
                acc_s[h][...] = jnp.zeros_like(acc_s[h])

        for h in range(hp):
            hk = h // group if kvp > 1 else 0
            cols, kcols = slice(h * dh, (h + 1) * dh), slice(hk * dh, (hk + 1) * dh)
            s = lax.dot_general(q_ref[:, cols], k_ref[:, kcols], _NT, preferred_element_type=F32)
            if scale != 1.0:
                s = s * scale
            m_prev = m_s[h][...]
            m_cur = jnp.maximum(m_prev, jnp.max(s, axis=1, keepdims=True))
            alpha = jnp.exp(m_prev - m_cur)
            p = jnp.exp(s - m_cur[:, :1])
            l_s[h][...] = alpha * l_s[h][...] + jnp.sum(p, axis=1, keepdims=True)
            acc_s[h][...] = acc_s[h][...] * alpha[:, :1] + lax.dot_general(p.astype(BF16), v_ref[:, kcols], _NN,
                                                                         preferred_element_type=F32)
            m_s[h][...] = m_cur

        @pl.when(j == nkv - 1)
        def _():
            for h in range(hp):
                cols, lcols = slice(h * dh, (h + 1) * dh), slice(h * LANES, (h + 1) * LANES)
                o_ref[:, cols] = acc_s[h][...] / l_s[h][...][:, :1]
                lse_ref[:, lcols] = m_s[h][...] + jnp.log(l_s[h][...])

    qs, ls, ks = _att_specs(bq, bk, dh, hp, kvp, group, lambda h, i, j: (i, h), lambda h, i, j: (j, h))
    return pl.pallas_call(
        body, name="flash_fwd",
        out_shape=(jax.ShapeDtypeStruct((lq, hq * dh), F32), jax.ShapeDtypeStruct((lq, hq * LANES), F32)),
        grid=(hq // hp, lq // bq, nkv), in_specs=[qs, ks, ks], out_specs=(qs, ls),
        scratch_shapes=[pltpu.VMEM((bq, LANES), F32)] * (2 * hp) + [pltpu.VMEM((bq, dh), F32)] * hp,
        compiler_params=pltpu.CompilerParams(dimension_semantics=("parallel", "parallel", "arbitrary"),
                                             vmem_limit_bytes=VMEM_LIMIT),
    )(q, k, v)


def _flash_dq(q, k, v, o, lse, do, dh, group, scale):
    lq, lk = q.shape[0], k.shape[0]
    hq = q.shape[1] // dh
    bq, bk, hp, kvp = _att_plan(lq, lk, hq, group)
    nkv = lk // bk

    def body(q_ref, k_ref, v_ref, o_ref, lse_ref, do_ref, dq_ref, dob_ref, dl_ref, *acc_s):
        j = pl.program_id(2)

        @pl.when(j == 0)
        def _():
            for h in range(hp):
                acc_s[h][...] = jnp.zeros_like(acc_s[h])
            dob_ref[...] = do_ref[...].astype(BF16)
            for h in range(hp):
                cols, lcols = slice(h * dh, (h + 1) * dh), slice(h * LANES, (h + 1) * LANES)
                dl_ref[:, lcols] = jnp.broadcast_to(jnp.sum(do_ref[:, cols] * o_ref[:, cols], axis=1, keepdims=True),
                                                    (bq, LANES))

        for h in range(hp):
            hk = h // group if kvp > 1 else 0
            cols, lcols, kcols = slice(h * dh, (h + 1) * dh), slice(h * LANES, (h + 1) * LANES), slice(hk * dh, (hk + 1) * dh)
            kb = k_ref[:, kcols]
            s = lax.dot_general(q_ref[:, cols], kb, _NT, preferred_element_type=F32)
            if scale != 1.0:
                s = s * scale
            p = jnp.exp(s - lse_ref[:, lcols][:, :1])
            dp = lax.dot_general(dob_ref[:, cols], v_ref[:, kcols], _NT, preferred_element_type=F32)
            ds = p * (dp - dl_ref[:, lcols][:, :1])
            if scale != 1.0:
                ds = ds * scale
            acc_s[h][...] += lax.dot_general(ds.astype(BF16), kb, _NN, preferred_element_type=F32)

        @pl.when(j == nkv - 1)
        def _():
            for h in range(hp):
                dq_ref[:, h * dh:(h + 1) * dh] = acc_s[h][...]

    qs, ls, ks = _att_specs(bq, bk, dh, hp, kvp, group, lambda h, i, j: (i, h), lambda h, i, j: (j, h))
    return pl.pallas_call(
        body, name="flash_dq",
        out_shape=(jax.ShapeDtypeStruct((lq, hq * dh), F32), jax.ShapeDtypeStruct((lq, hq * dh), BF16),
                   jax.ShapeDtypeStruct((lq, hq * LANES), F32)),
        grid=(hq // hp, lq // bq, nkv), in_specs=[qs, ks, ks, qs, ls, qs], out_specs=(qs, qs, ls),
        scratch_shapes=[pltpu.VMEM((bq, dh), F32)] * hp,
        compiler_params=pltpu.CompilerParams(dimension_semantics=("parallel", "parallel", "arbitrary"),
                                             vmem_limit_bytes=VMEM_LIMIT),
    )(q, k, v, o, lse, do)


def _flash_dkv(q, k, v, lse, dob, delta, dh, group, scale):
    lq, lk = q.shape[0], k.shape[0]
    hq = q.shape[1] // dh
    bq, bk, hp, kvp = _att_plan(lq, lk, hq, group)
    nq = lq // bq
    reps = max(1, group // hp)
    nt = reps * nq

    def body(q_ref, k_ref, v_ref, lse_ref, dob_ref, dl_ref, dk_ref, dv_ref, dk_s, dv_s):
        t = pl.program_id(2)

        @pl.when(t == 0)
        def _():
            dk_s[...] = jnp.zeros_like(dk_s)
            dv_s[...] = jnp.zeros_like(dv_s)

        for h in range(hp):
            hk = h // group if kvp > 1 else 0
            cols, lcols, kcols = slice(h * dh, (h + 1) * dh), slice(h * LANES, (h + 1) * LANES), slice(hk * dh, (hk + 1) * dh)
            qb = q_ref[:, cols]
            dob = dob_ref[:, cols]
            s = lax.dot_general(qb, k_ref[:, kcols], _NT, preferred_element_type=F32)
            if scale != 1.0:
                s = s * scale
            p = jnp.exp(s - lse_ref[:, lcols][:, :1])
            dp = lax.dot_general(dob, v_ref[:, kcols], _NT, preferred_element_type=F32)
            ds = p * (dp - dl_ref[:, lcols][:, :1])
            if scale != 1.0:
                ds = ds * scale
            dv_s[:, kcols] += lax.dot_general(p.astype(BF16), dob, _TN, preferred_element_type=F32)
            dk_s[:, kcols] += lax.dot_general(ds.astype(BF16), qb, _TN, preferred_element_type=F32)

        @pl.when(t == nt - 1)
        def _():
            dk_ref[...] = dk_s[...]
            dv_ref[...] = dv_s[...]

    hkv_blocks = (hq // group) // kvp
    qs = pl.BlockSpec((bq, hp * dh), lambda h, j, t: (t % nq, h * reps + t // nq))
    ls = pl.BlockSpec((bq, hp * LANES), lambda h, j, t: (t % nq, h * reps + t // nq))
    ks = pl.BlockSpec((bk, kvp * dh), lambda h, j, t: (j, h))
    return pl.pallas_call(
        body, name="flash_dkv",
        out_shape=(jax.ShapeDtypeStruct(k.shape, F32), jax.ShapeDtypeStruct(v.shape, F32)),
        grid=(hkv_blocks, lk // bk, nt), in_specs=[qs, ks, ks, ls, qs, ls], out_specs=(ks, ks),
        scratch_shapes=[pltpu.VMEM((bk, kvp * dh), F32), pltpu.VMEM((bk, kvp * dh), F32)],
        compiler_params=pltpu.CompilerParams(dimension_semantics=("parallel", "parallel", "arbitrary"),
                                             vmem_limit_bytes=VMEM_LIMIT),
    )(q, k, v, lse, dob, delta)


@functools.partial(jax.custom_vjp, nondiff_argnums=(3, 4, 5))
def flash(q, k, v, dh, group, scale):
    return _flash_fwd(q.astype(BF16), k.astype(BF16), v.astype(BF16), dh, group, scale)[0]


def _flash_vfwd(q, k, v, dh, group, scale):
    qb, kb, vb = q.astype(BF16), k.astype(BF16), v.astype(BF16)
    o, lse = _flash_fwd(qb, kb, vb, dh, group, scale)
    return o, (qb, kb, vb, o, lse)


def _flash_vbwd(dh, group, scale, res, do):
    q, k, v, o, lse = res
    dq, dob, delta = _flash_dq(q, k, v, o, lse, do, dh, group, scale)
    dk, dv = _flash_dkv(q, k, v, lse, dob, delta, dh, group, scale)
    return dq, dk, dv


flash.defvjp(_flash_vfwd, _flash_vbwd)


def _scan_call(lr, li, br, bi, prev_r, prev_i, reverse, name):
    rows, width = br.shape
    ts = _pick(width, (1024, 512, 256, 128))
    tr = _pick(rows, (256, 128, 64, 32, 16, 8))
    nb = rows // tr
    with_prev = prev_r is not None

    def body(*refs):
        if with_prev:
            lr_ref, li_ref, br_ref, bi_ref, pr_ref, pi_ref, sr_ref, si_ref, dr_ref, di_ref, cr_s, ci_s = refs
        else:
            lr_ref, li_ref, br_ref, bi_ref, sr_ref, si_ref, pr_ref, pi_ref, cr_s, ci_s = refs
        i = pl.program_id(1)

        @pl.when(i == 0)
        def _():
            cr_s[...] = jnp.zeros_like(cr_s)
            ci_s[...] = jnp.zeros_like(ci_s)
            if with_prev:
                dr_ref[...] = jnp.zeros_like(dr_ref)
                di_ref[...] = jnp.zeros_like(di_ref)

        a_re = lr_ref[...]
        a_im = li_ref[...]

        def step(r, carry):
            if with_prev:
                c_re, c_im, d_re, d_im = carry
            else:
                c_re, c_im = carry
            t = tr - 1 - r if reverse else r
            row = pl.ds(t, 1)
            n_re = a_re * c_re - a_im * c_im + br_ref[row, :]
            n_im = a_re * c_im + a_im * c_re + bi_ref[row, :]
            sr_ref[row, :] = n_re
            si_ref[row, :] = n_im
            if with_prev:
                p_re = pr_ref[row, :]
                p_im = pi_ref[row, :]
                return n_re, n_im, d_re + n_re * p_re + n_im * p_im, d_im + n_im * p_re - n_re * p_im
            pr_ref[row, :] = c_re
            pi_ref[row, :] = c_im
            return n_re, n_im

        zero = jnp.zeros((1, ts), F32)
        init = (cr_s[...], ci_s[...]) + ((zero, zero) if with_prev else ())
        out = lax.fori_loop(0, tr, step, init, unroll=8)
        cr_s[...] = out[0]
        ci_s[...] = out[1]
        if with_prev:
            dr_ref[...] += out[2]
            di_ref[...] += out[3]

    blk = (lambda j, i: (nb - 1 - i, j)) if reverse else (lambda j, i: (i, j))
    row_spec = pl.BlockSpec((tr, ts), blk)
    vec_spec = pl.BlockSpec((1, ts), lambda j, i: (0, j))
    full = jax.ShapeDtypeStruct((rows, width), F32)
    vec = jax.ShapeDtypeStruct((1, width), F32)
    ins = [lr, li, br, bi] + ([prev_r, prev_i] if with_prev else [])
    return pl.pallas_call(
        body, name=name, grid=(width // ts, nb),
        in_specs=[vec_spec, vec_spec] + [row_spec] * (len(ins) - 2),
        out_shape=(full, full, vec, vec) if with_prev else (full, full, full, full),
        out_specs=(row_spec, row_spec, vec_spec, vec_spec) if with_prev else (row_spec,) * 4,
        scratch_shapes=[pltpu.VMEM((1, ts), F32), pltpu.VMEM((1, ts), F32)],
        compiler_params=pltpu.CompilerParams(dimension_semantics=("parallel", "arbitrary"), vmem_limit_bytes=VMEM_LIMIT),
    )(*ins)


@functools.partial(jax.custom_vjp, nondiff_argnums=(4,))
def cscan(lr, li, br, bi, reverse):
    out = _scan_call(lr, li, br, bi, None, None, reverse, "s5_scan")
    return out[0], out[1]


def _cscan_fwd(lr, li, br, bi, reverse):
    sr, si, pr, pi = _scan_call(lr, li, br, bi, None, None, reverse, "s5_scan")
    return (sr, si), (lr, li, pr, pi)


def _cscan_bwd(reverse, res, g):
    lr, li, pr, pi = res
    ar, ai, dlr, dli = _scan_call(lr, -li, g[0], g[1], pr, pi, not reverse, "s5_scan_bwd")
    return dlr, dli, ar, ai


cscan.defvjp(_cscan_fwd, _cscan_bwd)


_BDIMS = {"nn": (((2,), (1,)), ((0,), (0,))), "nt": (((2,), (2,)), ((0,), (0,))), "tn": (((1,), (1,)), ((0,), (0,)))}


def _bdot_raw(a, b, mode):
    return lax.dot_general(a.astype(BF16), b.astype(BF16), _BDIMS[mode], preferred_element_type=F32)


@functools.partial(jax.custom_vjp, nondiff_argnums=(2,))
def _bdot(a, b, mode):
    return _bdot_raw(a, b, mode)


def _bdot_fwd(a, b, mode):
    return _bdot_raw(a, b, mode), (a, b)


def _bdot_bwd(mode, res, dc):
    a, b = res
    if mode == "nn":
        return _bdot_raw(dc, b, "nt"), _bdot_raw(a, dc, "tn")
    if mode == "nt":
        return _bdot_raw(dc, b, "nn"), _bdot_raw(dc, a, "tn")
    return _bdot_raw(b, dc, "nt"), _bdot_raw(a, dc, "nn")


_bdot.defvjp(_bdot_fwd, _bdot_bwd)


def _split(a):
    hi = a.astype(BF16)
    return hi, (a - hi.astype(F32)).astype(BF16)


def _fdot_raw(a, b, mode):
    a_hi, a_lo = _split(a)
    b_hi, b_lo = _split(b)
    dot = lambda u, v: lax.dot_general(u, v, _BDIMS[mode], preferred_element_type=F32)
    return dot(a_hi, b_hi) + (dot(a_hi, b_lo) + dot(a_lo, b_hi))


@jax.custom_vjp
def _fdot(a, b):
    return _fdot_raw(a, b, "nn")


def _fdot_fwd(a, b):
    return _fdot_raw(a, b, "nn"), (a, b)


def _fdot_bwd(res, dc):
    a, b = res
    return _fdot_raw(dc, b, "nt"), _fdot_raw(a, dc, "tn")


_fdot.defvjp(_fdot_fwd, _fdot_bwd)


def _dn_chunk(state, q, k, v, bb, gc, gc64, gr, direction):
    c = q.shape[1]
    row = lax.broadcasted_iota(jnp.int32, (c, c), 0)
    col = lax.broadcasted_iota(jnp.int32, (c, c), 1)
    sgn = 1 - 2 * direction
    after = ((row - col) * sgn > 0)[None]
    incl = ((row - col) * sgn >= 0)[None]
    eye = (row == col).astype(F32)[None]
    kb = k * bb
    decay = jnp.where(incl, jnp.exp(jnp.where(incl, gc64 - gr, 0.0)), 0.0)
    lower = jnp.where(after, _bdot(kb, k, "nt") * decay, 0.0)
    inv = eye - lower
    power = _fdot(lower, lower)
    span = 2
    while span < c:
        inv = inv + _fdot(inv, power)
        span *= 2
        if span < c:
            power = _fdot(power, power)
    eg = jnp.exp(gc)
    u = _fdot(inv, v * bb)
    w = _fdot(inv, kb * eg)
    intra = _bdot(q, k, "nt") * decay
    last = (lax.broadcasted_iota(jnp.int32, (c, 1), 0) == (c - 1) * (1 - direction)).astype(F32)[None]
    g_last = jnp.sum(gc * last, axis=1, keepdims=True)
    k_dec = k * jnp.exp(g_last - gc)
    v_new = u - _bdot(w, state, "nn")
    o = _bdot(q * eg, state, "nn") + _bdot(intra, v_new, "nn")
    new_state = state * jnp.exp(g_last) + _bdot(k_dec, v_new, "tn")
    return new_state, o


def _dn_rows(lq):
    return _pick(lq, (128, 64))


def _dn_prep(q, k, v):
    q, k, v = jax.nn.silu(q), jax.nn.silu(k), jax.nn.silu(v)
    q = q * lax.rsqrt(jnp.sum(q * q, axis=2, keepdims=True) + EPS) * (q.shape[2] ** -0.5)
    k = k * lax.rsqrt(jnp.sum(k * k, axis=2, keepdims=True) + EPS)
    return q, k, v


def _dn_step(state, q, k, v, bb, gc, gc64, gr, direction):
    q, k, v = _dn_prep(q, k, v)
    return _dn_chunk(state, q, k, v, bb, gc, gc64, gr, direction)


def _dn_heads(ref, rows, part, h, d):
    return jnp.stack([ref[rows, (part * h + i) * d:(part * h + i + 1) * d] for i in range(h)])


def _dn_specs(h, t, d, c, nc, blk):
    return [pl.BlockSpec((h, t, d), lambda i: (0, blk(i), 0)), pl.BlockSpec((h, t, d), lambda i: (0, blk(i), 0)),
            pl.BlockSpec((h, t, c), lambda i: (0, blk(i), 0)), pl.BlockSpec((h, nc, 1, c), lambda i: (0, blk(i), 0, 0))]


def _dn_fwd_call(qkv, dirs):
    h, lq, d = dirs[0][0].shape
    c = DN_CHUNK
    t = _dn_rows(lq)
    nb, nc = lq // t, t // c
    blks = [lambda i: i, lambda i: nb - 1 - i]

    def body(*refs):
        qkv_refs = refs[0:2]
        op_refs = [refs[2:6], refs[6:10]]
        o_refs, st_refs = refs[10:12], refs[12:14]
        s_refs = refs[14:16]

        @pl.when(pl.program_id(0) == 0)
        def _():
            for s in s_refs:
                s[...] = jnp.zeros_like(s)

        def step(j, states):
            out = []
            for direction in range(2):
                ci = j if direction == 0 else nc - 1 - j
                rows = pl.ds(pl.multiple_of(ci * c, c), c)
                bb_ref, gc_ref, g64_ref, gr_ref = op_refs[direction]
                st_refs[direction][ci] = states[direction]
                state, o = _dn_step(states[direction], _dn_heads(qkv_refs[direction], rows, 0, h, d),
                                    _dn_heads(qkv_refs[direction], rows, 1, h, d), _dn_heads(qkv_refs[direction], rows, 2, h, d),
                                    bb_ref[:, rows, :], gc_ref[:, rows, :], g64_ref[:, rows, :], gr_ref[:, ci], direction)
                for i in range(h):
                    o_refs[direction][rows, i * d:(i + 1) * d] = o[i]
                out.append(state)
            return tuple(out)

        final = lax.fori_loop(0, nc, step, (s_refs[0][...], s_refs[1][...]))
        s_refs[0][...] = final[0]
        s_refs[1][...] = final[1]

    in_specs = [pl.BlockSpec((t, 3 * h * d), lambda i, b=b: (b(i), 0)) for b in blks]
    for b in blks:
        in_specs += _dn_specs(h, t, d, c, nc, b)
    out_specs = [pl.BlockSpec((t, h * d), lambda i, b=b: (b(i), 0)) for b in blks]
    out_specs += [pl.BlockSpec((nc, h, d, d), lambda i, b=b: (b(i), 0, 0, 0)) for b in blks]
    o_shape = jax.ShapeDtypeStruct((lq, h * d), F32)
    st_shape = jax.ShapeDtypeStruct((lq // c, h, d, d), F32)
    res = pl.pallas_call(
        body, name="deltanet_fwd", grid=(nb,), in_specs=in_specs, out_shape=(o_shape, o_shape, st_shape, st_shape),
        out_specs=tuple(out_specs), scratch_shapes=[pltpu.VMEM((h, d, d), F32)] * 2,
        compiler_params=pltpu.CompilerParams(dimension_semantics=("arbitrary",), vmem_limit_bytes=VMEM_LIMIT),
    )(qkv, qkv, *dirs[0], *dirs[1])
    return res[0:2], res[2:4]


def _dn_bwd_call(qkv, dirs, states, do):
    h, lq, d = dirs[0][0].shape
    c = DN_CHUNK
    t = _dn_rows(lq)
    nb, nc = lq // t, t // c
    blks = [lambda i: nb - 1 - i, lambda i: i]

    def body(*refs):
        qkv_refs = refs[0:2]
        op_refs = [refs[2:6], refs[6:10]]
        st_refs, do_refs = refs[10:12], refs[12:14]
        out_refs = [refs[14:19], refs[19:24]]
        ds_refs = refs[24:26]

        @pl.when(pl.program_id(0) == 0)
        def _():
            for s in ds_refs:
                s[...] = jnp.zeros_like(s)

        def step(j, dstates):
            out = []
            for direction in range(2):
                ci = nc - 1 - j if direction == 0 else j
                rows = pl.ds(pl.multiple_of(ci * c, c), c)
                bb_ref, gc_ref, g64_ref, gr_ref = op_refs[direction]
                args = (st_refs[direction][ci], _dn_heads(qkv_refs[direction], rows, 0, h, d),
                        _dn_heads(qkv_refs[direction], rows, 1, h, d), _dn_heads(qkv_refs[direction], rows, 2, h, d),
                        bb_ref[:, rows, :], gc_ref[:, rows, :], g64_ref[:, rows, :], gr_ref[:, ci])
                _, pull = jax.vjp(lambda *a, direction=direction: _dn_step(*a, direction), *args)
                do_c = jnp.stack([do_refs[direction][rows, i * d:(i + 1) * d] for i in range(h)])
                dstate, dq, dk, dv, dbb, dgc, dg64, dgr = pull((dstates[direction], do_c))
                dqkv_ref, dbb_ref, dgc_ref, dg64_ref, dgr_ref = out_refs[direction]
                for part, val in enumerate((dq, dk, dv)):
                    for i in range(h):
                        dqkv_ref[rows, (part * h + i) * d:(part * h + i + 1) * d] = val[i]
                dbb_ref[:, rows, :] = dbb
                dgc_ref[:, rows, :] = dgc
                dg64_ref[:, rows, :] = dg64
                dgr_ref[:, ci] = dgr
                out.append(dstate)
            return tuple(out)

        final = lax.fori_loop(0, nc, step, (ds_refs[0][...], ds_refs[1][...]))
        ds_refs[0][...] = final[0]
        ds_refs[1][...] = final[1]

    in_specs = [pl.BlockSpec((t, 3 * h * d), lambda i, b=b: (b(i), 0)) for b in blks]
    for b in blks:
        in_specs += _dn_specs(h, t, d, c, nc, b)
    in_specs += [pl.BlockSpec((nc, h, d, d), lambda i, b=b: (b(i), 0, 0, 0)) for b in blks]
    in_specs += [pl.BlockSpec((t, h * d), lambda i, b=b: (b(i), 0)) for b in blks]
    out_specs, out_shape = [], []
    for b in blks:
        out_specs += [pl.BlockSpec((t, 3 * h * d), lambda i, b=b: (b(i), 0))] + _dn_specs(h, t, d, c, nc, b)
        out_shape += [jax.ShapeDtypeStruct((lq, 3 * h * d), F32), jax.ShapeDtypeStruct((h, lq, d), F32),
                      jax.ShapeDtypeStruct((h, lq, d), F32), jax.ShapeDtypeStruct((h, lq, c), F32),
                      jax.ShapeDtypeStruct((h, lq // c, 1, c), F32)]
    res = pl.pallas_call(
        body, name="deltanet_bwd", grid=(nb,), in_specs=in_specs, out_shape=tuple(out_shape), out_specs=tuple(out_specs),
        scratch_shapes=[pltpu.VMEM((h, d, d), F32)] * 2,
        compiler_params=pltpu.CompilerParams(dimension_semantics=("arbitrary",), vmem_limit_bytes=VMEM_LIMIT),
    )(qkv, qkv, *dirs[0], *dirs[1], *states, do, do)
    return res[0:5], res[5:10]


@jax.custom_vjp
def dn_rule(qkv, dir0, dir1):
    o, _ = _dn_fwd_call(qkv, (dir0, dir1))
    return o[0] + o[1]


def _dn_rule_fwd(qkv, dir0, dir1):
    o, states = _dn_fwd_call(qkv, (dir0, dir1))
    return o[0] + o[1], (qkv, dir0, dir1, states)


def _dn_rule_bwd(res, do):
    qkv, dir0, dir1, states = res
    g0, g1 = _dn_bwd_call(qkv, (dir0, dir1), states, do)
    return g0[0] + g1[0], tuple(g0[1:]), tuple(g1[1:])


dn_rule.defvjp(_dn_rule_fwd, _dn_rule_bwd)


def gated_delta_both(qkv, beta2, g2):
    _, h, lq = beta2.shape
    d = qkv.shape[1] // (3 * h)
    c = DN_CHUNK
    dirs = []
    for direction in range(2):
        cum = lax.cumsum(g2[direction].reshape(h, lq // c, c), axis=2, reverse=direction == 1)
        flat = cum.reshape(h, lq)
        dirs.append((jnp.broadcast_to(beta2[direction][..., None], (h, lq, d)), jnp.broadcast_to(flat[..., None], (h, lq, d)),
                     jnp.broadcast_to(flat[..., None], (h, lq, c)), cum.reshape(h, lq // c, 1, c)))
    return dn_rule(qkv, dirs[0], dirs[1])


def _merge_rows(rows, d, nb):
    return _row_tile(rows, 4 * nb * d, 2 * 1024 * 1024)


def _merge_fwd_call(logits, ts):
    rows, d = ts[0].shape
    nb = len(ts)
    tr = _merge_rows(rows, d, nb)

    def body(gl_ref, *refs):
        t_refs, o_ref = refs[:nb], refs[nb]
        acc = jax.nn.sigmoid(gl_ref[:, 0:d]) * t_refs[0][...]
        for b in range(1, nb):
            acc = acc + jax.nn.sigmoid(gl_ref[:, b * d:(b + 1) * d]) * t_refs[b][...]
        o_ref[...] = acc

    wide = pl.BlockSpec((tr, nb * d), lambda i: (i, 0))
    one = pl.BlockSpec((tr, d), lambda i: (i, 0))
    return pl.pallas_call(
        body, name="gate_merge", out_shape=jax.ShapeDtypeStruct((rows, d), F32), grid=(rows // tr,),
        in_specs=[wide] + [one] * nb, out_specs=one,
        compiler_params=pltpu.CompilerParams(dimension_semantics=("parallel",), vmem_limit_bytes=VMEM_LIMIT),
    )(logits, *ts)


def _merge_bwd_call(logits, ts, dm):
    rows, d = ts[0].shape
    nb = len(ts)
    tr = _merge_rows(rows, d, nb)

    def body(gl_ref, *refs):
        t_refs, dm_ref, dgl_ref, dt_refs = refs[:nb], refs[nb], refs[nb + 1], refs[nb + 2:]
        dmv = dm_ref[...]
        for b in range(nb):
            s = jax.nn.sigmoid(gl_ref[:, b * d:(b + 1) * d])
            dt_refs[b][...] = dmv * s
            dgl_ref[:, b * d:(b + 1) * d] = dmv * t_refs[b][...] * (s * (1.0 - s))

    wide = pl.BlockSpec((tr, nb * d), lambda i: (i, 0))
    one = pl.BlockSpec((tr, d), lambda i: (i, 0))
    out = jax.ShapeDtypeStruct((rows, d), F32)
    return pl.pallas_call(
        body, name="gate_merge_bwd", out_shape=(jax.ShapeDtypeStruct(logits.shape, F32),) + (out,) * nb, grid=(rows // tr,),
        in_specs=[wide] + [one] * (nb + 1), out_specs=(wide,) + (one,) * nb,
        compiler_params=pltpu.CompilerParams(dimension_semantics=("parallel",), vmem_limit_bytes=VMEM_LIMIT),
    )(logits, *ts, dm)


@jax.custom_vjp
def gate_merge(logits, *ts):
    return _merge_fwd_call(logits, ts)


def _gate_merge_fwd(logits, *ts):
    return _merge_fwd_call(logits, ts), (logits, ts)


def _gate_merge_bwd(res, dm):
    logits, ts = res
    return tuple(_merge_bwd_call(logits, ts, dm))


gate_merge.defvjp(_gate_merge_fwd, _gate_merge_bwd)


def _conv_rows(rows):
    return _pick(rows, (256, 128, 64, 32, 16, 8))


def _conv_taps(x_ref, prev_ref, next_ref, i, nb, t):
    keep_prev = (i > 0).astype(F32)
    keep_next = (i < nb - 1).astype(F32)
    ext = jnp.concatenate([prev_ref[...] * keep_prev, x_ref[...], next_ref[...] * keep_next], axis=0)
    total = t + 2 * CONV_HALO
    return [pltpu.roll(ext, (DN_CONV // 2 - j) % total, 0)[CONV_HALO:CONV_HALO + t] for j in range(DN_CONV)]


def _conv_specs(t, ch, rows):
    per = t // CONV_HALO
    last = rows // CONV_HALO - 1
    return [pl.BlockSpec((t, ch), lambda i: (i, 0)),
            pl.BlockSpec((CONV_HALO, ch), lambda i: (jnp.maximum(i * per - 1, 0), 0)),
            pl.BlockSpec((CONV_HALO, ch), lambda i: (jnp.minimum((i + 1) * per, last), 0))]


def _conv_call(x, taps):
    rows, ch = x.shape
    t = _conv_rows(rows)
    nb = rows // t

    def body(x_ref, prev_ref, next_ref, w_ref, o_ref):
        views = _conv_taps(x_ref, prev_ref, next_ref, pl.program_id(0), nb, t)
        acc = views[0] * w_ref[0:1, :]
        for j in range(1, DN_CONV):
            acc = acc + views[j] * w_ref[j:j + 1, :]
        o_ref[...] = acc

    return pl.pallas_call(
        body, name="short_conv", out_shape=jax.ShapeDtypeStruct((rows, ch), F32), grid=(nb,),
        in_specs=_conv_specs(t, ch, rows) + [pl.BlockSpec((8, ch), lambda i: (0, 0))],
        out_specs=pl.BlockSpec((t, ch), lambda i: (i, 0)),
        compiler_params=pltpu.CompilerParams(dimension_semantics=("parallel",), vmem_limit_bytes=VMEM_LIMIT),
    )(x, x, x, taps)


def _conv_dw_call(x, dy):
    rows, ch = x.shape
    t = _conv_rows(rows)
    nb = rows // t

    def body(x_ref, prev_ref, next_ref, dy_ref, o_ref):
        @pl.when(pl.program_id(0) == 0)
        def _():
            o_ref[...] = jnp.zeros_like(o_ref)

        views = _conv_taps(x_ref, prev_ref, next_ref, pl.program_id(0), nb, t)
        dyv = dy_ref[...]
        for j in range(DN_CONV):
            o_ref[j:j + 1, :] += jnp.sum(views[j] * dyv, axis=0, keepdims=True)

    return pl.pallas_call(
        body, name="short_conv_dw", out_shape=jax.ShapeDtypeStruct((8, ch), F32), grid=(nb,),
        in_specs=_conv_specs(t, ch, rows) + [pl.BlockSpec((t, ch), lambda i: (i, 0))],
        out_specs=pl.BlockSpec((8, ch), lambda i: (0, 0)),
        compiler_params=pltpu.CompilerParams(dimension_semantics=("arbitrary",), vmem_limit_bytes=VMEM_LIMIT),
    )(x, x, x, dy)


def _taps(w, flip):
    wt = jnp.transpose(w)
    if flip:
        wt = wt[::-1]
    return jnp.pad(wt, ((0, 8 - DN_CONV), (0, 0)))


@jax.custom_vjp
def short_conv(x, w):
    return _conv_call(x, _taps(w, False))


def _short_conv_fwd(x, w):
    return _conv_call(x, _taps(w, False)), (x, w)


def _short_conv_bwd(res, dy):
    x, w = res
    return _conv_call(dy, _taps(w, True)), jnp.transpose(_conv_dw_call(x, dy)[:DN_CONV])


short_conv.defvjp(_short_conv_fwd, _short_conv_bwd)


def _s5_mixer(u, a_re, a_im, log_step, b_re, b_im, c_re, c_im, d, w_glu, b_glu):
    per = LANES // SSM_GROUP
    y = u * d
    for direction in range(2):
        are, aim = a_re[direction], a_im[direction]
        step = jnp.exp(log_step[direction])[:, None]
        mag = jnp.exp(are * step)
        lam_re = mag * jnp.cos(aim * step)
        lam_im = mag * jnp.sin(aim * step)
        den = are * are + aim * aim
        nr = lam_re - 1.0
        ni = lam_im
        coef_re = (nr * are + ni * aim) / den
        coef_im = (ni * are - nr * aim) / den
        bb_re = coef_re[..., None] * b_re[direction] - coef_im[..., None] * b_im[direction]
        bb_im = coef_re[..., None] * b_im[direction] + coef_im[..., None] * b_re[direction]
        bu_re = cmm(u, _cluster_diag(jnp.swapaxes(bb_re, 1, 2), per))
        bu_im = cmm(u, _cluster_diag(jnp.swapaxes(bb_im, 1, 2), per))
        width = bu_re.shape[1]
        s_re, s_im = cscan(lam_re.reshape(1, width), lam_im.reshape(1, width), bu_re, bu_im, direction == 1)
        y = y + cmm(s_re, _cluster_diag(jnp.swapaxes(c_re[direction], 1, 2), per))
        y = y - cmm(s_im, _cluster_diag(jnp.swapaxes(c_im[direction], 1, 2), per))
    y = jax.nn.gelu(y)
    return y * jax.nn.sigmoid(pmm(y, w_glu) + b_glu)


def _deltanet_mixer(q, k, v, ab, conv_w, a_log, dt_bias, norm_g):
    seq = q.shape[0]
    qkv = short_conv(jnp.concatenate([q, k, v], axis=-1), conv_w)
    a4 = ab[:, :2 * DN_HEADS].reshape(seq, 2, DN_HEADS)
    b4 = ab[:, 2 * DN_HEADS:4 * DN_HEADS].reshape(seq, 2, DN_HEADS)
    beta = jax.nn.sigmoid(b4)
    g = -jnp.exp(a_log) * jax.nn.softplus(a4 + dt_bias)
    o = gated_delta_both(qkv, jnp.transpose(beta, (1, 2, 0)), jnp.transpose(g, (1, 2, 0)))
    return head_norm(o, norm_g)


def _axial_rope(seq):
    rows = seq // GRID_W
    row = jnp.repeat(jnp.arange(rows), GRID_W).astype(F32)
    col = jnp.tile(jnp.arange(GRID_W), rows).astype(F32)
    axis_dim = ATT_HEAD_DIM // 2
    freqs = ROPE_THETA ** (-jnp.arange(0, axis_dim, 2, dtype=F32) / axis_dim)
    ang = jnp.concatenate([row[:, None] * freqs, col[:, None] * freqs], axis=-1)
    sign = jnp.tile(jnp.array([-1.0, 1.0], F32), ATT_HEAD_DIM // 2)
    return jnp.repeat(jnp.cos(ang), 2, axis=-1), jnp.repeat(jnp.sin(ang), 2, axis=-1) * sign


def _grid_attention(q, k, v, qn_g, kn_g, cos, sin):
    qh = head_norm_rope(q, qn_g, cos, sin, ATT_HEAD_DIM ** -0.5)
    kh = head_norm_rope(k, kn_g, cos, sin, 1.0)
    return flash(qh, kh, v, ATT_HEAD_DIM, ATT_HEADS // ATT_KV_HEADS, 1.0)


def _memory_attention(q, mem_n, w_kv):
    kv = pmm(mem_n, w_kv)
    mem_w = MEM_HEADS * MEM_HEAD_DIM
    return flash(q, kv[:, :mem_w], kv[:, mem_w:], MEM_HEAD_DIM, 1, MEM_HEAD_DIM ** -0.5)


def _middle(h, x, mem, p):
    d_model = x.shape[1]
    lay, _, _ = _layout(d_model)

    def seg(name):
        _, dst, w, _ = lay[name]
        return h[:, dst:dst + w]

    cos, sin = _axial_rope(x.shape[0])
    y_a = _s5_mixer(seg("u"), p["ssm_a_re"], p["ssm_a_im"], p["ssm_log_step"], p["ssm_b_re"], p["ssm_b_im"],
                    p["ssm_c_re"], p["ssm_c_im"], p["ssm_d"], p["ssm_w_glu"], p["ssm_b_glu"]) * jax.nn.silu(seg("z_a"))
    y_b = _deltanet_mixer(seg("dq"), seg("dk"), seg("dv"), seg("dab"), p["dn_conv"], p["dn_a_log"],
                          p["dn_dt_bias"], p["dn_norm_g"]) * jax.nn.silu(seg("z_b"))
    y_c = _grid_attention(seg("aq"), seg("ak"), seg("av"), p["attn_q_norm"], p["attn_k_norm"], cos, sin) * jax.nn.silu(seg("z_c"))
    y_m = _memory_attention(seg("mq"), prms(mem, p["mem_norm_g"]), p["w_mem_kv"]) * jax.nn.silu(seg("z_m"))
    lifted, off = [], 0
    for y_br in (y_a, y_b, y_c, y_m):
        w = y_br.shape[1]
        lifted.append(pmm(y_br, p["w_branch"][off:off + w]))
        off += w
    return x + pmm(gate_merge(seg("gates"), *lifted), p["w_out"])


def _local_step(x, mem, target, w):
    depth = w["norm_g"].shape[0]
    d_model = x.shape[1]
    mid_names = [n for n in WEIGHTS if n not in ("norm_g", "w_in", "final_norm_g")]
    saved = []
    cur = x
    for layer in range(depth):
        xn = _rms_fwd_call(cur, w["norm_g"][layer], BF16)
        h = _mm(xn, w["w_in"][layer], "nn", "in_proj")
        p = {n: w[n][layer] for n in mid_names}
        nxt, vjp_mid = jax.vjp(lambda h_, x_, p_: _middle(h_, x_, mem, p_), h, cur, p)
        saved.append((cur, xn, vjp_mid))
        cur = nxt
    loss_parts, dx, d_final = _loss_call(cur, w["final_norm_g"], target)
    loss = jnp.sum(loss_parts[:, 0, 0])
    d_final = d_final.reshape(w["final_norm_g"].shape)
    grads = {n: [None] * depth for n in WEIGHTS if n != "final_norm_g"}
    for layer in reversed(range(depth)):
        x_in, xn, vjp_mid = saved[layer]
        dh, dx_skip, dp = vjp_mid(dx)
        dh = dh.astype(BF16)
        grads["w_in"][layer] = _unpad_w_in(_mm(xn, dh, "tn", "in_proj_dw"), d_model)
        dxn = _mm(dh, w["w_in"][layer], "nt", "in_proj_dx")
        dx_n, dg = _rms_bwd_call(x_in, w["norm_g"][layer], dxn)
        dx = dx_skip + dx_n
        grads["norm_g"][layer] = dg.reshape(w["norm_g"][layer].shape)
        for n in mid_names:
            grads[n][layer] = dp[n]
    out = {n: jnp.stack(v) for n, v in grads.items()}
    out["final_norm_g"] = d_final
    return loss, dx, out


_ANY = pl.BlockSpec(memory_space=pl.ANY)
_CHIP_FLIPS = [(1, 0), (0, 1), (1, 1)]


def _remote(src, dst, send_sem, recv_sem, dev):
    return pltpu.make_async_remote_copy(src_ref=src, dst_ref=dst, send_sem=send_sem, recv_sem=recv_sem,
                                        device_id=dev, device_id_type=pl.DeviceIdType.MESH)


def _flip(v, bit):
    return 1 - v if bit else v


def _gather_chips(shards):
    n = len(shards)

    def body(*refs):
        ins, outs = refs[:n], refs[n:2 * n]
        send_sems, recv_sems, local_sems = refs[2 * n:]
        x, y, c = lax.axis_index("x"), lax.axis_index("y"), lax.axis_index("c")
        locals_, sends = [], []
        for a in range(n):
            cp = pltpu.make_async_copy(ins[a], outs[a].at[2 * x + y], local_sems.at[a])
            cp.start()
            locals_.append(cp)

        def half(a, px, py, pc):
            rh = shards[a].shape[1] // 2
            return outs[a].at[2 * px + py, :, pl.ds(pc * rh, rh), :]

        for a in range(n):
            rh = shards[a].shape[1] // 2
            for j, (fx, fy) in enumerate(_CHIP_FLIPS):
                cp = _remote(ins[a].at[:, pl.ds(c * rh, rh), :], half(a, x, y, c), send_sems.at[6 * a + j],
                             recv_sems.at[6 * a + j], (_flip(x, fx), _flip(y, fy), c))
                cp.start()
                sends.append(cp)
        for a in range(n):
            for j, (fx, fy) in enumerate(_CHIP_FLIPS):
                blk = half(a, _flip(x, fx), _flip(y, fy), c)
                _remote(blk, blk, send_sems.at[6 * a + j], recv_sems.at[6 * a + j], (x, y, c)).wait_recv()
                cp = _remote(blk, blk, send_sems.at[6 * a + 3 + j], recv_sems.at[6 * a + 3 + j], (x, y, 1 - c))
                cp.start()
                sends.append(cp)
        for a in range(n):
            for j, (fx, fy) in enumerate(_CHIP_FLIPS):
                blk = half(a, _flip(x, fx), _flip(y, fy), 1 - c)
                _remote(blk, blk, send_sems.at[6 * a + 3 + j], recv_sems.at[6 * a + 3 + j], (x, y, c)).wait_recv()
        for cp in sends:
            cp.wait_send()
        for cp in locals_:
            cp.wait()

    return pl.pallas_call(
        body, name="gather_weights",
        out_shape=tuple(jax.ShapeDtypeStruct((N_CHIPS,) + s.shape, s.dtype) for s in shards),
        in_specs=[_ANY] * n, out_specs=tuple([_ANY] * n),
        scratch_shapes=[pltpu.SemaphoreType.DMA((6 * n,)), pltpu.SemaphoreType.DMA((6 * n,)),
                        pltpu.SemaphoreType.DMA((n,))],
    )(*shards)


def _exchange_pair(parts):
    n = len(parts)

    def body(*refs):
        ins, outs = refs[:n], refs[n:2 * n]
        send_sems, recv_sems = refs[2 * n:]
        x, y, c = lax.axis_index("x"), lax.axis_index("y"), lax.axis_index("c")
        started = []
        for a in range(n):
            rh = parts[a].shape[2] // 2
            cp = _remote(ins[a].at[:, :, pl.ds((1 - c) * rh, rh), :], outs[a], send_sems.at[a], recv_sems.at[a], (x, y, 1 - c))
            cp.start()
            started.append(cp)
        for a in range(n):
            _remote(outs[a], outs[a], send_sems.at[a], recv_sems.at[a], (x, y, c)).wait_recv()
        for cp in started:
            cp.wait_send()

    return pl.pallas_call(
        body, name="exchange_pair",
        out_shape=tuple(jax.ShapeDtypeStruct((p.shape[0], p.shape[1], p.shape[2] // 2, p.shape[3]), p.dtype) for p in parts),
        in_specs=[_ANY] * n, out_specs=tuple([_ANY] * n),
        scratch_shapes=[pltpu.SemaphoreType.DMA((n,)), pltpu.SemaphoreType.DMA((n,))],
    )(*parts)


def _exchange_chips(pair_sums, small):
    n = len(pair_sums)

    def body(*refs):
        ins, small_in = refs[:n], refs[n]
        outs, small_out = refs[n + 1:2 * n + 1], refs[2 * n + 1]
        send_sems, recv_sems, small_send, small_recv, local_sem = refs[2 * n + 2:]
        x, y, c = lax.axis_index("x"), lax.axis_index("y"), lax.axis_index("c")
        me = 4 * x + 2 * y + c
        started = []
        own = pltpu.make_async_copy(small_in, small_out.at[me], local_sem)
        own.start()
        for f, (fx, fy) in enumerate(_CHIP_FLIPS):
            px, py = _flip(x, fx), _flip(y, fy)
            for a in range(n):
                cp = _remote(ins[a].at[2 * px + py], outs[a].at[f], send_sems.at[3 * a + f], recv_sems.at[3 * a + f], (px, py, c))
                cp.start()
                started.append(cp)
        for k in range(1, N_DEV):
            peer = (_flip(x, k >> 2 & 1), _flip(y, k >> 1 & 1), _flip(c, k & 1))
            cp = _remote(small_in, small_out.at[me], small_send.at[k - 1], small_recv.at[k - 1], peer)
            cp.start()
            started.append(cp)
        for f in range(len(_CHIP_FLIPS)):
            for a in range(n):
                _remote(outs[a].at[f], outs[a].at[f], send_sems.at[3 * a + f], recv_sems.at[3 * a + f], (x, y, c)).wait_recv()
        for k in range(1, N_DEV):
            peer = 4 * _flip(x, k >> 2 & 1) + 2 * _flip(y, k >> 1 & 1) + _flip(c, k & 1)
            _remote(small_out.at[peer], small_out.at[peer], small_send.at[k - 1], small_recv.at[k - 1], (x, y, c)).wait_recv()
        for cp in started:
            cp.wait_send()
        own.wait()

    shapes = tuple(jax.ShapeDtypeStruct((3,) + p.shape[1:], p.dtype) for p in pair_sums)
    shapes += (jax.ShapeDtypeStruct((N_DEV,) + small.shape, small.dtype),)
    return pl.pallas_call(
        body, name="exchange_chips", out_shape=shapes, in_specs=[_ANY] * (n + 1), out_specs=tuple([_ANY] * (n + 1)),
        scratch_shapes=[pltpu.SemaphoreType.DMA((3 * n,)), pltpu.SemaphoreType.DMA((3 * n,)),
                        pltpu.SemaphoreType.DMA((N_DEV - 1,)), pltpu.SemaphoreType.DMA((N_DEV - 1,)),
                        pltpu.SemaphoreType.DMA(())],
    )(*pair_sums, small)


def _swap_halves(arrays):
    n = len(arrays)

    def body(*refs):
        outs = refs[n:2 * n]
        send_sems, recv_sems = refs[2 * n:]
        x, y, c = lax.axis_index("x"), lax.axis_index("y"), lax.axis_index("c")
        started = []
        for a in range(n):
            rh = arrays[a].shape[1] // 2
            mine = outs[a].at[:, pl.ds(c * rh, rh), :]
            cp = _remote(mine, mine, send_sems.at[a], recv_sems.at[a], (x, y, 1 - c))
            cp.start()
            started.append(cp)
        for a in range(n):
            rh = arrays[a].shape[1] // 2
            theirs = outs[a].at[:, pl.ds((1 - c) * rh, rh), :]
            _remote(theirs, theirs, send_sems.at[a], recv_sems.at[a], (x, y, c)).wait_recv()
        for cp in started:
            cp.wait_send()

    return pl.pallas_call(
        body, name="swap_halves", out_shape=tuple(jax.ShapeDtypeStruct(t.shape, t.dtype) for t in arrays),
        in_specs=[_ANY] * n, out_specs=tuple([_ANY] * n), input_output_aliases={a: a for a in range(n)},
        scratch_shapes=[pltpu.SemaphoreType.DMA((n,)), pltpu.SemaphoreType.DMA((n,))],
    )(*arrays)


def _adamw_math(g, w, m, v):
    m = ADAM_B1 * m + (1.0 - ADAM_B1) * g
    v = ADAM_B2 * v + (1.0 - ADAM_B2) * jnp.square(g)
    m_hat = m / (1.0 - ADAM_B1 ** ADAM_STEP)
    v_hat = v / (1.0 - ADAM_B2 ** ADAM_STEP)
    delta = -ADAM_LR * (m_hat / (jnp.sqrt(v_hat) + ADAM_EPS) + ADAM_WD * w)
    return delta, m, v


def _row_tile(rows, row_bytes, budget, step=8):
    best = step
    for t in range(step, rows + 1, step):
        if rows % t == 0 and t * row_bytes <= budget:
            best = t
    return best


def _pair_sum(part, other, core):
    nj, a, r, c = part.shape
    rh = r // 2
    tr = _row_tile(rh, 4 * (-(-c // LANES) * LANES), 512 * 1024, 16)

    def body(core_ref, p_ref, o_ref, s32_ref, s16_ref):
        s = p_ref[0, 0, 0] + o_ref[0, 0]
        s32_ref[0, 0] = s
        s16_ref[0, 0] = s.astype(BF16)

    spec = pl.BlockSpec((1, 1, tr, c), lambda j, l, i, core_ref: (j, l, i, 0))
    return pl.pallas_call(
        body, name="pair_sum",
        out_shape=(jax.ShapeDtypeStruct(other.shape, F32), jax.ShapeDtypeStruct(other.shape, BF16)),
        grid_spec=pltpu.PrefetchScalarGridSpec(
            num_scalar_prefetch=1, grid=(nj, a, rh // tr),
            in_specs=[pl.BlockSpec((1, 1, 1, tr, c), lambda j, l, i, core_ref: (j, l, core_ref[0], i, 0)), spec],
            out_specs=(spec, spec)),
        compiler_params=pltpu.CompilerParams(dimension_semantics=("parallel", "parallel", "parallel"),
                                             vmem_limit_bytes=VMEM_LIMIT),
    )(core, part.reshape(nj, a, 2, rh, c), other)


def _sum_adamw_big(own, landed, w, m, v, place):
    _, a, rh, c = own.shape
    tr = _row_tile(rh, 4 * (-(-c // LANES) * LANES), 256 * 1024, 16)
    w4, m4, v4 = (t.reshape(a, 2, rh, c) for t in (w, m, v))

    def body(place_ref, q_ref, p_ref, w_ref, m_ref, v_ref, g_out, d_out, m_out, v_out):
        g = q_ref[0, 0]
        for f in range(3):
            g = g + p_ref[f, 0].astype(F32)
        delta, mm_, vv_ = _adamw_math(g, w_ref[0, 0], m_ref[0, 0], v_ref[0, 0])
        g_out[0, 0] = g
        d_out[0, 0] = delta
        m_out[0, 0] = mm_
        v_out[0, 0] = vv_

    wspec = pl.BlockSpec((1, 1, tr, c), lambda l, i, place_ref: (l, place_ref[0], i, 0))
    out = jax.ShapeDtypeStruct((a, 2, rh, c), F32)
    res = pl.pallas_call(
        body, name="sum_adamw_big", out_shape=(out,) * 4,
        grid_spec=pltpu.PrefetchScalarGridSpec(
            num_scalar_prefetch=1, grid=(a, rh // tr),
            in_specs=[pl.BlockSpec((1, 1, tr, c), lambda l, i, place_ref: (place_ref[1], l, i, 0)),
                      pl.BlockSpec((3, 1, tr, c), lambda l, i, place_ref: (0, l, i, 0)), wspec, wspec, wspec],
            out_specs=(wspec,) * 4),
        compiler_params=pltpu.CompilerParams(dimension_semantics=("parallel", "parallel"), vmem_limit_bytes=VMEM_LIMIT),
    )(place, own, landed, w4, m4, v4)
    return [t.reshape(a, 2 * rh, c) for t in res]


def _sum_small(parts):
    _, rows, _ = parts.shape
    tr = _row_tile(rows, 4 * LANES, 256 * 1024)

    def body(p_ref, o_ref):
        g = p_ref[0]
        for k in range(1, N_DEV):
            g = g + p_ref[k]
        o_ref[...] = g

    return pl.pallas_call(
        body, name="sum_small", out_shape=jax.ShapeDtypeStruct((rows, LANES), F32), grid=(rows // tr,),
        in_specs=[pl.BlockSpec((N_DEV, tr, LANES), lambda i: (0, i, 0))], out_specs=pl.BlockSpec((tr, LANES), lambda i: (i, 0)),
        compiler_params=pltpu.CompilerParams(dimension_semantics=("parallel",)),
    )(parts)


def _adamw_small(g, w, m, v):
    rows, _ = g.shape
    tr = _row_tile(rows, 4 * LANES, 256 * 1024)

    def body(g_ref, w_ref, m_ref, v_ref, d_out, m_out, v_out):
        d_out[...], m_out[...], v_out[...] = _adamw_math(g_ref[...], w_ref[...], m_ref[...], v_ref[...])

    spec = pl.BlockSpec((tr, LANES), lambda i: (i, 0))
    out = jax.ShapeDtypeStruct((rows, LANES), F32)
    return pl.pallas_call(
        body, name="adamw_small", out_shape=(out,) * 3, grid=(rows // tr,), in_specs=[spec] * 4, out_specs=(spec,) * 3,
        compiler_params=pltpu.CompilerParams(dimension_semantics=("parallel",)),
    )(g, w, m, v)


def _pack(arrays):
    flat = jnp.concatenate([a.reshape(-1) for a in arrays])
    rows = -(-flat.shape[0] // (PACK_ROWS * LANES)) * PACK_ROWS
    return jnp.pad(flat, (0, rows * LANES - flat.shape[0])).reshape(rows, LANES)


def _unpack(packed, shapes):
    flat, out, off = packed.reshape(-1), [], 0
    for s in shapes:
        size = math.prod(s)
        out.append(flat[off:off + size].reshape(s))
        off += size
    return out


def kernel(x, mem, norm_g, w_in, ssm_a_re, ssm_a_im, ssm_log_step, ssm_b_re, ssm_b_im, ssm_c_re, ssm_c_im, ssm_d, ssm_w_glu, ssm_b_glu, dn_conv, dn_a_log, dn_dt_bias, dn_norm_g, attn_q_norm, attn_k_norm, mem_norm_g, w_mem_kv, w_branch, w_out, final_norm_g, loss_target, m_norm_g, m_w_in, m_ssm_a_re, m_ssm_a_im, m_ssm_log_step, m_ssm_b_re, m_ssm_b_im, m_ssm_c_re, m_ssm_c_im, m_ssm_d, m_ssm_w_glu, m_ssm_b_glu, m_dn_conv, m_dn_a_log, m_dn_dt_bias, m_dn_norm_g, m_attn_q_norm, m_attn_k_norm, m_mem_norm_g, m_w_mem_kv, m_w_branch, m_w_out, m_final_norm_g, v_norm_g, v_w_in, v_ssm_a_re, v_ssm_a_im, v_ssm_log_step, v_ssm_b_re, v_ssm_b_im, v_ssm_c_re, v_ssm_c_im, v_ssm_d, v_ssm_w_glu, v_ssm_b_glu, v_dn_conv, v_dn_a_log, v_dn_dt_bias, v_dn_norm_g, v_attn_q_norm, v_attn_k_norm, v_mem_norm_g, v_w_mem_kv, v_w_branch, v_w_out, v_final_norm_g):
    args = locals()
    wts = {n: args[n] for n in WEIGHTS}
    mom = {n: args["m_" + n] for n in WEIGHTS}
    var = {n: args["v_" + n] for n in WEIGHTS}
    d_model = x.shape[-1]
    chip = 2 * lax.axis_index("x") + lax.axis_index("y")
    core = lax.axis_index("c")

    conv_rows = dn_conv.shape[1]
    conv_flat = _pack([dn_conv])
    gathered = _gather_chips([wts[n].astype(BF16) for n in BIG] + [conv_flat[None]])
    full = {n: wts[n] for n in SMALL}
    g_in = gathered[0]
    full["w_in"] = _pad_w_in(jnp.concatenate([g_in[j] for j in range(N_CHIPS)], axis=-1), d_model)
    for n, g in zip(BIG[1:], gathered[1:]):
        full[n] = jnp.concatenate([g[j] for j in range(N_CHIPS)], axis=1).astype(F32)
    conv_all = gathered[-1][:, 0].reshape(N_CHIPS, -1)[:, :dn_conv.size].reshape((N_CHIPS,) + dn_conv.shape)
    full["dn_conv"] = jnp.concatenate([conv_all[j] for j in range(N_CHIPS)], axis=1)

    loss, grad_x, grads = _local_step(x[0], mem[0], loss_target[0], full)
    loss = lax.psum(loss, ("x", "y", "c"))

    big_parts = []
    for n in BIG:
        g = grads[n]
        if n == "w_in":
            cols = g.shape[-1] // N_CHIPS
            big_parts.append(jnp.stack([g[..., j * cols:(j + 1) * cols] for j in range(N_CHIPS)]))
        else:
            rows = g.shape[1] // N_CHIPS
            big_parts.append(jnp.stack([g[:, j * rows:(j + 1) * rows] for j in range(N_CHIPS)]))
    core_arr = jnp.reshape(core, (1,)).astype(jnp.int32)
    place = jnp.stack([core, chip]).astype(jnp.int32)
    from_pair = _exchange_pair(big_parts)
    sums = [_pair_sum(p, o, core_arr) for p, o in zip(big_parts, from_pair)]
    small_shapes = [grads[n].shape for n in SMALL]
    landed = _exchange_chips([s[1] for s in sums], _pack([grads[n] for n in SMALL]))

    halves = []
    for n, s, parts in zip(BIG, sums, landed[:-1]):
        halves.extend(_sum_adamw_big(s[0], parts, wts[n], mom[n], var[n], place))
    swapped = _swap_halves(halves)
    res = {}
    for i, n in enumerate(BIG):
        res[n] = swapped[4 * i:4 * i + 4]
    small_g = _unpack(_sum_small(landed[-1]), small_shapes)
    small_g = dict(zip(SMALL, small_g))
    small_g["dn_conv"] = lax.dynamic_slice_in_dim(small_g["dn_conv"], chip * conv_rows, conv_rows, axis=1)
    shapes = [small_g[n].shape for n in SMALL]
    upd = _adamw_small(_pack([small_g[n] for n in SMALL]), _pack([wts[n] for n in SMALL]),
                       _pack([mom[n] for n in SMALL]), _pack([var[n] for n in SMALL]))
    upd = [_unpack(u, shapes) for u in upd]
    for i, n in enumerate(SMALL):
        res[n] = (small_g[n], upd[0][i], upd[1][i], upd[2][i])

    out = [loss, grad_x[None]]
    for kind in range(4):
        out.extend(res[n][kind] for n in WEIGHTS)
    return tuple(out)
```
